```python
import math
import numpy as np
import jax
import jax.numpy as jnp
from jax import lax

D_MODEL = 1024
BATCH = 16
SEQ = 2048
DEPTH = 2

MIX_WIDTH = D_MODEL
SSM_WIDTH = D_MODEL // 4
NSA_WIDTH = D_MODEL // 2
LRU_WIDTH = D_MODEL // 4

S5_GROUP = 16
S5_GROUPS = SSM_WIDTH // S5_GROUP
S5_STATE = 64

HEAD_DIM = 64
NSA_Q_HEADS = NSA_WIDTH // HEAD_DIM
NSA_KV_HEADS = 2
NSA_GQA = NSA_Q_HEADS // NSA_KV_HEADS
NSA_KV_WIDTH = NSA_KV_HEADS * HEAD_DIM
CMP_LEN = 32
CMP_STRIDE = 16
SEL_LEN = 64
SEL_TOPK = 8
WINDOW = 256
Q_BLOCK = 64

LRU_HEADS = 4
LRU_HEAD_DIM = LRU_WIDTH // LRU_HEADS
CONV_WIDTH = 4
LRU_C = 8.0

REL_BUCKETS = 32
REL_MAX_DIST = 128

D_FF = ((8 * D_MODEL // 3 + 255) // 256) * 256

IN_WIDTH = SSM_WIDTH + NSA_WIDTH + 6 * NSA_KV_WIDTH + 3 * NSA_Q_HEADS + 2 * LRU_WIDTH

NEG_INF = -1e9
FORCE_BONUS = 1e4
RMS_EPS = 1e-6

kernel_name = 'hymba_s5_nsa_rglru_trunk'


def in_proj_splits():
    o1 = SSM_WIDTH
    o2 = o1 + NSA_WIDTH
    o3 = o2 + 6 * NSA_KV_WIDTH
    o4 = o3 + 3 * NSA_Q_HEADS
    o5 = o4 + LRU_WIDTH
    return [o1, o2, o3, o4, o5]


def rmsnorm(x, g):
    xf = x.astype(jnp.float32)
    y = xf * lax.rsqrt(jnp.mean(xf * xf, axis=-1, keepdims=True) + RMS_EPS)
    return (y * g.astype(jnp.float32)).astype(x.dtype)


def masked_softmax(s, mask):
    p = jax.nn.softmax(jnp.where(mask, s, NEG_INF), axis=-1)
    return jnp.where(mask, p, 0.0)


def t5_bucket(dist):
    n = jnp.maximum(dist, 0)
    max_exact = REL_BUCKETS // 2
    nf = jnp.maximum(n, 1).astype(jnp.float32)
    large = max_exact + (jnp.log(nf / max_exact) / math.log(REL_MAX_DIST / max_exact)
                         * (REL_BUCKETS - max_exact)).astype(jnp.int32)
    large = jnp.minimum(large, REL_BUCKETS - 1)
    return jnp.where(n < max_exact, n, large)


def rel_bias_dense(table, dist):
    b = table.astype(jnp.float32)[t5_bucket(dist)]
    b = jnp.moveaxis(b, -1, 0)
    return b.reshape(NSA_KV_HEADS, NSA_GQA, *dist.shape)


def rel_bias_selected(table, dist):
    tbl = table.astype(jnp.float32).reshape(REL_BUCKETS, NSA_KV_HEADS, NSA_GQA).transpose(1, 0, 2)
    b = jax.vmap(lambda tb, bk: tb[bk], in_axes=(0, 1), out_axes=1)(tbl, t5_bucket(dist))
    return jnp.moveaxis(b, -1, 2)


def s5_mixer(u, lam_re, lam_im, log_dt, b_re, b_im, c_re, c_im, d_skip, w_glu):
    bsz, seq, _ = u.shape
    uf = u.astype(jnp.float32).reshape(bsz, seq, S5_GROUPS, S5_GROUP)
    dt = jnp.exp(log_dt.astype(jnp.float32))[:, None]
    lr = lam_re.astype(jnp.float32)
    li = lam_im.astype(jnp.float32)
    mag = jnp.exp(lr * dt)
    ang = li * dt
    ab_re = mag * jnp.cos(ang)
    ab_im = mag * jnp.sin(ang)
    den = lr * lr + li * li
    f_re = ((ab_re - 1.0) * lr + ab_im * li) / den
    f_im = (ab_im * lr - (ab_re - 1.0) * li) / den
    br = b_re.astype(jnp.float32)
    bi = b_im.astype(jnp.float32)
    bb_re = f_re[..., None] * br - f_im[..., None] * bi
    bb_im = f_re[..., None] * bi + f_im[..., None] * br
    bu_re = jnp.einsum('bsgc,gpc->bsgp', uf, bb_re)
    bu_im = jnp.einsum('bsgc,gpc->bsgp', uf, bb_im)
    a_re = jnp.broadcast_to(ab_re, bu_re.shape)
    a_im = jnp.broadcast_to(ab_im, bu_im.shape)

    def combine(e1, e2):
        a1r, a1i, b1r, b1i = e1
        a2r, a2i, b2r, b2i = e2
        return (a1r * a2r - a1i * a2i,
                a1r * a2i + a1i * a2r,
                a2r * b1r - a2i * b1i + b2r,
                a2r * b1i + a2i * b1r + b2i)

    _, _, h_re, h_im = lax.associative_scan(combine, (a_re, a_im, bu_re, bu_im), axis=1)
    y = (jnp.einsum('bsgp,gcp->bsgc', h_re, c_re.astype(jnp.float32))
         - jnp.einsum('bsgp,gcp->bsgc', h_im, c_im.astype(jnp.float32)))
    y = y.reshape(bsz, seq, SSM_WIDTH) + d_skip.astype(jnp.float32) * u.astype(jnp.float32)
    y = jax.nn.gelu(y)
    gl = y @ w_glu.astype(jnp.float32)
    out = gl[..., :SSM_WIDTH] * jax.nn.sigmoid(gl[..., SSM_WIDTH:])
    return out.astype(u.dtype)


def nsa_compress(kv, pe, w1, w2):
    bsz, seq = kv.shape[:2]
    n_cmp = (seq - CMP_LEN) // CMP_STRIDE + 1
    idx = np.arange(n_cmp)[:, None] * CMP_STRIDE + np.arange(CMP_LEN)[None, :]
    blk = kv[:, idx] + pe.astype(jnp.float32)[:, None, :]
    flat = blk.transpose(0, 1, 3, 2, 4).reshape(bsz, n_cmp, NSA_KV_HEADS, CMP_LEN * HEAD_DIM)
    return jax.nn.gelu(flat @ w1.astype(jnp.float32)) @ w2.astype(jnp.float32)


def nsa_mixer(q, kv, gate_logits, rel_table, pe_k, w1_k, w2_k, pe_v, w1_v, w2_v):
    bsz, seq = q.shape[:2]
    n_cmp = (seq - CMP_LEN) // CMP_STRIDE + 1
    n_sel = seq // SEL_LEN
    n_top = min(SEL_TOPK, n_sel)
    n_qb = seq // Q_BLOCK
    scale = HEAD_DIM ** -0.5
    kv = kv.astype(jnp.float32).reshape(bsz, seq, 6, NSA_KV_HEADS, HEAD_DIM)
    k_c, v_c, k_s, v_s, k_w, v_w = (kv[:, :, i] for i in range(6))

    kc = nsa_compress(k_c, pe_k, w1_k, w2_k)
    vc = nsa_compress(v_c, pe_v, w1_v, w2_v)
    cmp_end = jnp.arange(n_cmp) * CMP_STRIDE + CMP_LEN - 1
    cs = np.arange(n_cmp) * CMP_STRIDE
    ss = np.arange(n_sel) * SEL_LEN
    overlap = jnp.asarray(((cs[:, None] < ss[None, :] + SEL_LEN)
                           & (ss[None, :] < cs[:, None] + CMP_LEN)).astype(np.float32))

    ks_blk = k_s.reshape(bsz, n_sel, SEL_LEN, NSA_KV_HEADS, HEAD_DIM).transpose(0, 3, 1, 2, 4)
    vs_blk = v_s.reshape(bsz, n_sel, SEL_LEN, NSA_KV_HEADS, HEAD_DIM).transpose(0, 3, 1, 2, 4)

    pad = ((0, 0), (WINDOW, 0), (0, 0), (0, 0))
    kw_pad = jnp.pad(k_w, pad)
    vw_pad = jnp.pad(v_w, pad)

    q_blocks = (q.astype(jnp.float32) * scale).reshape(
        bsz, n_qb, Q_BLOCK, NSA_KV_HEADS, NSA_GQA, HEAD_DIM).transpose(1, 0, 2, 3, 4, 5)
    g_blocks = jax.nn.sigmoid(gate_logits.astype(jnp.float32)).reshape(
        bsz, n_qb, Q_BLOCK, NSA_KV_HEADS, NSA_GQA, 3).transpose(1, 0, 2, 3, 4, 5)
    sel_off = jnp.arange(SEL_LEN)
    sel_ids = jnp.arange(n_sel)
    win_off = jnp.arange(WINDOW + Q_BLOCK)
    gather = jax.vmap(jax.vmap(lambda tbl, ix: tbl[ix]))

    def block(args):
        qi, qb, gb = args
        t = qi * Q_BLOCK + jnp.arange(Q_BLOCK)
        dist_c = t[:, None] - cmp_end[None, :]
        s_c = jnp.einsum('bqhgd,bnhd->bhgqn', qb, kc) + rel_bias_dense(rel_table, dist_c)
        p_c = masked_softmax(s_c, dist_c >= 0)
        o_c = jnp.einsum('bhgqn,bnhd->bqhgd', p_c, vc)
        imp = jnp.einsum('bhgqn,nj->bhqj', p_c, overlap)
        cur = (t // SEL_LEN)[:, None]
        forced = (sel_ids == 0) | (sel_ids == cur) | (sel_ids == cur - 1)
        avail = sel_ids * SEL_LEN <= t[:, None]
        score = jnp.where(avail, imp + FORCE_BONUS * forced, NEG_INF)
        top_val, top_idx = lax.top_k(score, n_top)
        top_ok = top_val > 0.5 * NEG_INF
        ksel = gather(ks_blk, top_idx).reshape(bsz, NSA_KV_HEADS, Q_BLOCK, n_top * SEL_LEN, HEAD_DIM)
        vsel = gather(vs_blk, top_idx).reshape(bsz, NSA_KV_HEADS, Q_BLOCK, n_top * SEL_LEN, HEAD_DIM)
        pos = top_idx[..., None] * SEL_LEN + sel_off
        dist_s = t[:, None, None] - pos
        mask_s = ((dist_s >= 0) & top_ok[..., None]).reshape(bsz, NSA_KV_HEADS, 1, Q_BLOCK, -1)
        bias_s = rel_bias_selected(rel_table, dist_s).reshape(
            bsz, NSA_KV_HEADS, NSA_GQA, Q_BLOCK, n_top * SEL_LEN)
        s_s = jnp.einsum('bqhgd,bhqmd->bhgqm', qb, ksel) + bias_s
        p_s = masked_softmax(s_s, mask_s)
        o_s = jnp.einsum('bhgqm,bhqmd->bqhgd', p_s, vsel)
        start = qi * Q_BLOCK
        kwin = lax.dynamic_slice_in_dim(kw_pad, start, WINDOW + Q_BLOCK, axis=1)
        vwin = lax.dynamic_slice_in_dim(vw_pad, start, WINDOW + Q_BLOCK, axis=1)
        kpos = start - WINDOW + win_off
        dist_w = t[:, None] - kpos[None, :]
        mask_w = (dist_w >= 0) & (dist_w < WINDOW) & (kpos[None, :] >= 0)
        s_w = jnp.einsum('bqhgd,bkhd->bhgqk', qb, kwin) + rel_bias_dense(rel_table, dist_w)
        p_w = masked_softmax(s_w, mask_w)
        o_w = jnp.einsum('bhgqk,bkhd->bqhgd', p_w, vwin)
        o = gb[..., 0:1] * o_c + gb[..., 1:2] * o_s + gb[..., 2:3] * o_w
        return o.reshape(bsz, Q_BLOCK, NSA_WIDTH)

    out = lax.map(block, (jnp.arange(n_qb), q_blocks, g_blocks))
    return out.transpose(1, 0, 2, 3).reshape(bsz, seq, NSA_WIDTH).astype(q.dtype)


def rglru_mixer(xb, gb, conv_w, conv_b, w_a, b_a, w_x, b_x, lam):
    bsz, seq, width = xb.shape
    xc = lax.conv_general_dilated(xb, conv_w[:, None, :], window_strides=(1,),
                                  padding=((CONV_WIDTH - 1, 0),),
                                  dimension_numbers=('NWC', 'WIO', 'NWC'),
                                  feature_group_count=width) + conv_b
    xf = xc.astype(jnp.float32)
    xh = xf.reshape(bsz, seq, LRU_HEADS, LRU_HEAD_DIM)
    gate_r = jax.nn.sigmoid(jnp.einsum('bshi,hij->bshj', xh, w_a.astype(jnp.float32)).reshape(
        bsz, seq, width) + b_a.astype(jnp.float32))
    gate_i = jax.nn.sigmoid(jnp.einsum('bshi,hij->bshj', xh, w_x.astype(jnp.float32)).reshape(
        bsz, seq, width) + b_x.astype(jnp.float32))
    log_a = -LRU_C * gate_r * jax.nn.softplus(-lam.astype(jnp.float32))
    a = jnp.exp(log_a)
    mult = jnp.sqrt(-jnp.expm1(2.0 * log_a))
    bt = mult * gate_i * xf

    def combine(e1, e2):
        a1, b1 = e1
        a2, b2 = e2
        return (a1 * a2, a2 * b1 + b2)

    _, h = lax.associative_scan(combine, (a, bt), axis=1)
    y = h * jax.nn.gelu(gb.astype(jnp.float32))
    return y.astype(xb.dtype)


def swiglu(h, w_gate, w_up, w_down):
    return (jax.nn.silu(h @ w_gate) * (h @ w_up)) @ w_down


def setup_inputs(seed: int = 0) -> dict:
    key = jax.random.key(seed)
    ks = iter(jax.random.split(key, 48))
    f32 = jnp.float32
    L = DEPTH

    def nrm(shape, scale):
        return jax.random.normal(next(ks), shape, f32) * scale

    x = nrm((BATCH, SEQ, D_MODEL), 1.0)
    rel_bias_table = nrm((REL_BUCKETS, NSA_Q_HEADS), 0.1)
    norm_mix = 1.0 + nrm((L, D_MODEL), 0.02)
    w_in = nrm((L, D_MODEL, IN_WIDTH), D_MODEL ** -0.5)
    w_out = nrm((L, MIX_WIDTH, D_MODEL), MIX_WIDTH ** -0.5)
    s5_lam_re = -0.5 + nrm((L, S5_GROUPS, S5_STATE), 0.01)
    s5_lam_im = math.pi * jnp.arange(S5_STATE, dtype=f32) + nrm((L, S5_GROUPS, S5_STATE), 0.01)
    s5_log_dt = jax.random.uniform(next(ks), (L, S5_GROUPS), f32,
                                   minval=math.log(1e-3), maxval=math.log(1e-1))
    s5_b_re = nrm((L, S5_GROUPS, S5_STATE, S5_GROUP), (2 * S5_GROUP) ** -0.5)
    s5_b_im = nrm((L, S5_GROUPS, S5_STATE, S5_GROUP), (2 * S5_GROUP) ** -0.5)
    s5_c_re = nrm((L, S5_GROUPS, S5_GROUP, S5_STATE), S5_STATE ** -0.5)
    s5_c_im = nrm((L, S5_GROUPS, S5_GROUP, S5_STATE), S5_STATE ** -0.5)
    s5_d = nrm((L, SSM_WIDTH), 1.0)
    s5_w_glu = nrm((L, SSM_WIDTH, 2 * SSM_WIDTH), SSM_WIDTH ** -0.5)
    nsa_pe_k = nrm((L, CMP_LEN, HEAD_DIM), 0.02)
    nsa_w1_k = nrm((L, CMP_LEN * HEAD_DIM, HEAD_DIM), (CMP_LEN * HEAD_DIM) ** -0.5)
    nsa_w2_k = nrm((L, HEAD_DIM, HEAD_DIM), HEAD_DIM ** -0.5)
    nsa_pe_v = nrm((L, CMP_LEN, HEAD_DIM), 0.02)
    nsa_w1_v = nrm((L, CMP_LEN * HEAD_DIM, HEAD_DIM), (CMP_LEN * HEAD_DIM) ** -0.5)
    nsa_w2_v = nrm((L, HEAD_DIM, HEAD_DIM), HEAD_DIM ** -0.5)
    lru_conv_w = nrm((L, CONV_WIDTH, LRU_WIDTH), CONV_WIDTH ** -0.5)
    lru_conv_b = nrm((L, LRU_WIDTH), 0.01)
    lru_w_a = nrm((L, LRU_HEADS, LRU_HEAD_DIM, LRU_HEAD_DIM), LRU_HEAD_DIM ** -0.5)
    lru_b_a = nrm((L, LRU_WIDTH), 0.01)
    lru_w_x = nrm((L, LRU_HEADS, LRU_HEAD_DIM, LRU_HEAD_DIM), LRU_HEAD_DIM ** -0.5)
    lru_b_x = nrm((L, LRU_WIDTH), 0.01)
    a_pow = jax.random.uniform(next(ks), (L, LRU_WIDTH), f32, minval=0.9, maxval=0.999)
    a0 = a_pow ** (1.0 / LRU_C)
    lru_lam = jnp.log(a0) - jnp.log1p(-a0)
    norm_ffn = 1.0 + nrm((L, D_MODEL), 0.02)
    w_gate = nrm((L, D_MODEL, D_FF), D_MODEL ** -0.5)
    w_up = nrm((L, D_MODEL, D_FF), D_MODEL ** -0.5)
    w_down = nrm((L, D_FF, D_MODEL), D_FF ** -0.5)
    norm_final = 1.0 + nrm((D_MODEL,), 0.02)
    return {'x': x, 'rel_bias_table': rel_bias_table, 'norm_mix': norm_mix, 'w_in': w_in,
            'w_out': w_out, 's5_lam_re': s5_lam_re, 's5_lam_im': s5_lam_im,
            's5_log_dt': s5_log_dt, 's5_b_re': s5_b_re, 's5_b_im': s5_b_im,
            's5_c_re': s5_c_re, 's5_c_im': s5_c_im, 's5_d': s5_d, 's5_w_glu': s5_w_glu,
            'nsa_pe_k': nsa_pe_k, 'nsa_w1_k': nsa_w1_k, 'nsa_w2_k': nsa_w2_k,
            'nsa_pe_v': nsa_pe_v, 'nsa_w1_v': nsa_w1_v, 'nsa_w2_v': nsa_w2_v,
            'lru_conv_w': lru_conv_w, 'lru_conv_b': lru_conv_b, 'lru_w_a': lru_w_a,
            'lru_b_a': lru_b_a, 'lru_w_x': lru_w_x, 'lru_b_x': lru_b_x, 'lru_lam': lru_lam,
            'norm_ffn': norm_ffn, 'w_gate': w_gate, 'w_up': w_up, 'w_down': w_down,
            'norm_final': norm_final}


def reference(x, rel_bias_table, norm_mix, w_in, w_out, s5_lam_re, s5_lam_im, s5_log_dt,
              s5_b_re, s5_b_im, s5_c_re, s5_c_im, s5_d, s5_w_glu, nsa_pe_k, nsa_w1_k,
              nsa_w2_k, nsa_pe_v, nsa_w1_v, nsa_w2_v, lru_conv_w, lru_conv_b, lru_w_a,
              lru_b_a, lru_w_x, lru_b_x, lru_lam, norm_ffn, w_gate, w_up, w_down, norm_final):
    h = x
    splits = in_proj_splits()
    for l in range(DEPTH):
        hn = rmsnorm(h, norm_mix[l])
        z = hn @ w_in[l]
        u_ssm, q, kv, gate_logits, lru_x, lru_g = jnp.split(z, splits, axis=-1)
        y_ssm = s5_mixer(u_ssm, s5_lam_re[l], s5_lam_im[l], s5_log_dt[l], s5_b_re[l],
                         s5_b_im[l], s5_c_re[l], s5_c_im[l], s5_d[l], s5_w_glu[l])
        y_nsa = nsa_mixer(q, kv, gate_logits, rel_bias_table, nsa_pe_k[l], nsa_w1_k[l],
                          nsa_w2_k[l], nsa_pe_v[l], nsa_w1_v[l], nsa_w2_v[l])
        y_lru = rglru_mixer(lru_x, lru_g, lru_conv_w[l], lru_conv_b[l], lru_w_a[l], lru_b_a[l],
                            lru_w_x[l], lru_b_x[l], lru_lam[l])
        mix = jnp.concatenate([y_ssm, y_nsa, y_lru], axis=-1)
        h = h + mix @ w_out[l]
        h = h + swiglu(rmsnorm(h, norm_ffn[l]), w_gate[l], w_up[l], w_down[l])
    return rmsnorm(h, norm_final)
```

```python
import functools
import math

import numpy as np
import jax
import jax.numpy as jnp
from jax import lax
from jax.experimental import pallas as pl
from jax.experimental.pallas import tpu as pltpu

F32 = jnp.float32
BF16 = jnp.bfloat16

D_MODEL = 1024
SSM_WIDTH = 256
NSA_WIDTH = 512
LRU_WIDTH = 256
S5_GROUP = 16
S5_GROUPS = 16
S5_STATE = 64
HEAD_DIM = 64
NSA_Q_HEADS = 8
NSA_KV_HEADS = 2
NSA_GQA = 4
NSA_KV_WIDTH = 128
CMP_LEN = 32
CMP_STRIDE = 16
SEL_LEN = 64
SEL_TOPK = 8
WINDOW = 256
Q_BLOCK = 64
LRU_HEADS = 4
LRU_HEAD_DIM = 64
CONV_WIDTH = 4
LRU_C = 8.0
REL_BUCKETS = 32
REL_MAX_DIST = 128
D_FF = 2816
NEG_INF = -1e9
RMS_EPS = 1e-6

NO_SLOT = -3e38

S5_CHUNK = 16
Z_WIDTH = 2176
Z_Q, Z_U, Z_KV, Z_G, Z_LX, Z_LG = 0, 512, 768, 1536, 1664, 1920

VMEM_LIMIT = 56 * 1024 * 1024


def _cparams(*sem):
    return pltpu.CompilerParams(dimension_semantics=sem, vmem_limit_bytes=VMEM_LIMIT)


def _inproj_kernel(x_ref, g_ref, w_ref, o_ref):
    x = x_ref[...]
    y = x * lax.rsqrt(jnp.mean(x * x, axis=-1, keepdims=True) + RMS_EPS) * g_ref[...]
    o_ref[...] = jnp.dot(y.astype(BF16), w_ref[...], preferred_element_type=F32)


def _inproj(x2, g, w, tm=512):
    t, d = x2.shape
    n = w.shape[1]
    return pl.pallas_call(
        _inproj_kernel,
        grid=(t // tm,),
        in_specs=[pl.BlockSpec((tm, d), lambda i: (i, 0)),
                  pl.BlockSpec((1, d), lambda i: (0, 0)),
                  pl.BlockSpec((d, n), lambda i: (0, 0))],
        out_specs=pl.BlockSpec((tm, n), lambda i: (i, 0)),
        out_shape=jax.ShapeDtypeStruct((t, n), F32),
        compiler_params=_cparams("parallel"),
        name="inproj",
    )(x2, g, w)


def _lane_regroup(x, n_outer, n_inner):
    r = x.shape[0]
    nv = n_outer * n_inner * 16 // 128
    cols = [x[:, v * 128:(v + 1) * 128] for v in range(nv)]
    lane16 = lax.broadcasted_iota(jnp.int32, (r, 128), 1) // 16
    outs = []
    for w in range(nv):
        acc = None
        for k in range(8):
            dst = w * 8 + k
            i, o = dst // n_outer, dst % n_outer
            src = o * n_inner + i
            v, sk = src // 8, src % 8
            piece = cols[v]
            shift = ((k - sk) * 16) % 128
            if shift:
                piece = pltpu.roll(piece, shift, 1)
            acc = piece if acc is None else jnp.where(lane16 == k, piece, acc)
        outs.append(acc)
    return jnp.concatenate(outs, axis=1)


def _s5_kernel(u_ref, we_ref, tz_ref, cp_ref, a_ref, d_ref, o_ref, e_ref, carry_ref, *, nb, kc):
    @pl.when(pl.program_id(0) == 0)
    def _():
        carry_ref[...] = jnp.zeros_like(carry_ref)

    ug = _lane_regroup(u_ref[...], S5_CHUNK, S5_GROUPS)
    ugb = ug.astype(BF16)
    npair = S5_GROUPS // 2
    for p in range(npair):
        e_ref[:, p * 256:(p + 1) * 256] = jnp.dot(ugb[:, p * 512:(p + 1) * 512], we_ref[p],
                                                   preferred_element_type=F32)

    a = a_ref[...]

    def step(k, carry):
        r0 = pl.multiple_of(k * nb, nb)
        e = e_ref[pl.ds(r0, nb), :]
        e_ref[pl.ds(r0, nb), :] = carry
        new = []
        for p in range(npair):
            ar = a[:, p * 256:p * 256 + 128]
            ai = a[:, p * 256 + 128:(p + 1) * 256]
            cr = carry[:, p * 256:p * 256 + 128]
            ci = carry[:, p * 256 + 128:(p + 1) * 256]
            new.append(ar * cr - ai * ci + e[:, p * 256:p * 256 + 128])
            new.append(ar * ci + ai * cr + e[:, p * 256 + 128:(p + 1) * 256])
        return jnp.concatenate(new, axis=1)

    carry_ref[...] = lax.fori_loop(0, kc, step, carry_ref[...])

    eb = e_ref[...].astype(BF16)
    ys = []
    for p in range(npair):
        yc = jnp.dot(eb[:, p * 256:(p + 1) * 256], cp_ref[p], preferred_element_type=F32)
        y0 = jnp.dot(ugb[:, (2 * p) * 256:(2 * p + 1) * 256], tz_ref[2 * p], preferred_element_type=F32)
        y1 = jnp.dot(ugb[:, (2 * p + 1) * 256:(2 * p + 2) * 256], tz_ref[2 * p + 1],
                     preferred_element_type=F32)
        ys.append(yc + jnp.concatenate([y0, y1], axis=1))
    y = jnp.concatenate(ys, axis=1) + d_ref[...] * ug
    y = jax.nn.gelu(y)
    o_ref[...] = _lane_regroup(y, S5_GROUPS, S5_CHUNK)


def _s5(u_tm, we, tz, cp, a16, dg, nb, kc=16):
    rows, width = u_tm.shape
    r = kc * nb
    const3 = lambda i: (0, 0, 0)
    return pl.pallas_call(
        functools.partial(_s5_kernel, nb=nb, kc=kc),
        grid=(rows // r,),
        in_specs=[pl.BlockSpec((r, width), lambda i: (i, 0)),
                  pl.BlockSpec(we.shape, const3),
                  pl.BlockSpec(tz.shape, const3),
                  pl.BlockSpec(cp.shape, const3),
                  pl.BlockSpec(a16.shape, lambda i: (0, 0)),
                  pl.BlockSpec(dg.shape, lambda i: (0, 0))],
        out_specs=pl.BlockSpec((r, width), lambda i: (i, 0)),
        out_shape=jax.ShapeDtypeStruct((rows, width), F32),
        scratch_shapes=[pltpu.VMEM((r, 2048), F32), pltpu.VMEM((nb, 2048), F32)],
        compiler_params=_cparams("arbitrary"),
        name="s5",
    )(u_tm, we, tz, cp, a16, dg)


def _s5_tables(lam_re, lam_im, log_dt, b_re, b_im, c_re, c_im, d_skip):
    L = S5_CHUNK
    G, P, C = S5_GROUPS, S5_STATE, S5_GROUP
    dt = jnp.exp(log_dt.astype(F32))[:, None]
    lr = lam_re.astype(F32)
    li = lam_im.astype(F32)
    mag = jnp.exp(lr * dt)
    ang = li * dt
    ab_re = mag * jnp.cos(ang)
    ab_im = mag * jnp.sin(ang)
    den = lr * lr + li * li
    f_re = ((ab_re - 1.0) * lr + ab_im * li) / den
    f_im = (ab_im * lr - (ab_re - 1.0) * li) / den
    br = b_re.astype(F32)
    bi = b_im.astype(F32)
    bb_re = f_re[..., None] * br - f_im[..., None] * bi
    bb_im = f_re[..., None] * bi + f_im[..., None] * br
    cr = c_re.astype(F32)
    ci = c_im.astype(F32)
    tau = jnp.arange(L + 1, dtype=F32)[:, None, None]
    pr = jnp.exp(lr * dt * tau) * jnp.cos(li * dt * tau)
    pi = jnp.exp(lr * dt * tau) * jnp.sin(li * dt * tau)

    prs = pr[L - 1 - jnp.arange(L)]
    pis = pi[L - 1 - jnp.arange(L)]
    we_re = jnp.einsum('sgp,gpc->gscp', prs, bb_re) - jnp.einsum('sgp,gpc->gscp', pis, bb_im)
    we_im = jnp.einsum('sgp,gpc->gscp', prs, bb_im) + jnp.einsum('sgp,gpc->gscp', pis, bb_re)
    we_re = we_re.reshape(G, L * C, P)
    we_im = we_im.reshape(G, L * C, P)
    z = jnp.zeros_like(we_re[0::2])
    top = jnp.concatenate([we_re[0::2], z, we_im[0::2], z], axis=-1)
    bot = jnp.concatenate([z, we_re[1::2], z, we_im[1::2]], axis=-1)
    we = jnp.concatenate([top, bot], axis=1)

    m_re = pr[:L, :, :, None] * bb_re[None] - pi[:L, :, :, None] * bb_im[None]
    m_im = pr[:L, :, :, None] * bb_im[None] + pi[:L, :, :, None] * bb_re[None]
    kern = jnp.einsum('gop,tgpc->tgoc', cr, m_re) - jnp.einsum('gop,tgpc->tgoc', ci, m_im)
    s_idx = np.arange(L)[:, None]
    t_idx = np.arange(L)[None, :]
    lag = np.clip(t_idx - s_idx, 0, L - 1)
    causal = jnp.asarray((t_idx >= s_idx).astype(np.float32))
    tzf = kern[lag] * causal[:, :, None, None, None]
    tz = tzf.transpose(2, 0, 4, 1, 3).reshape(G, L * C, L * C)

    pr1 = pr[1:]
    pi1 = pi[1:]
    cp_re = jnp.einsum('gop,tgp->gpto', cr, pr1) - jnp.einsum('gop,tgp->gpto', ci, pi1)
    cp_im = -(jnp.einsum('gop,tgp->gpto', cr, pi1) + jnp.einsum('gop,tgp->gpto', ci, pr1))
    cp_re = cp_re.reshape(G, P, L * C)
    cp_im = cp_im.reshape(G, P, L * C)
    zc = jnp.zeros_like(cp_re[0::2])
    cp = jnp.concatenate([
        jnp.concatenate([cp_re[0::2], zc], axis=-1),
        jnp.concatenate([zc, cp_re[1::2]], axis=-1),
        jnp.concatenate([cp_im[0::2], zc], axis=-1),
        jnp.concatenate([zc, cp_im[1::2]], axis=-1)], axis=1)

    a_re = pr[L].reshape(G // 2, 2 * P)
    a_im = pi[L].reshape(G // 2, 2 * P)
    a16 = jnp.concatenate([a_re, a_im], axis=-1).reshape(1, G * 2 * P)
    dg = jnp.broadcast_to(d_skip.astype(F32).reshape(G, 1, C), (G, L, C)).reshape(1, G * L * C)
    return we.astype(BF16), tz.astype(BF16), cp.astype(BF16), a16, dg


def _compress_kernel(ak_ref, av_ref, w1k_ref, w1v_ref, w2k_ref, w2v_ref, c0k_ref, c0v_ref,
                     kc_ref, vct_ref):
    def run(a_ref, w1_ref, w2_ref, c0_ref):
        a = a_ref[0]
        p1 = jnp.dot(a, w1_ref[0], preferred_element_type=F32)
        p2 = jnp.dot(a, w1_ref[1], preferred_element_type=F32)
        n = p1.shape[0]
        pre = p1 + pltpu.roll(p2, n - 1, 0) + c0_ref[...]
        out = jnp.dot(jax.nn.gelu(pre).astype(BF16), w2_ref[...], preferred_element_type=F32)
        row = lax.broadcasted_iota(jnp.int32, out.shape, 0)
        return jnp.where(row < n - 1, out, 0.0)

    kc_ref[0] = run(ak_ref, w1k_ref, w2k_ref, c0k_ref).astype(BF16)
    vct_ref[0] = run(av_ref, w1v_ref, w2v_ref, c0v_ref).T.astype(BF16)


def _compress(ak, av, w1k, w1v, w2k, w2v, c0k, c0v):
    b, nc, kw = ak.shape
    c3 = lambda i: (0, 0, 0)
    c2 = lambda i: (0, 0)
    return pl.pallas_call(
        _compress_kernel,
        grid=(b,),
        in_specs=[pl.BlockSpec((1, nc, kw), lambda i: (i, 0, 0)),
                  pl.BlockSpec((1, nc, kw), lambda i: (i, 0, 0)),
                  pl.BlockSpec(w1k.shape, c3), pl.BlockSpec(w1v.shape, c3),
                  pl.BlockSpec(w2k.shape, c2), pl.BlockSpec(w2v.shape, c2),
                  pl.BlockSpec(c0k.shape, c2), pl.BlockSpec(c0v.shape, c2)],
        out_specs=[pl.BlockSpec((1, nc, 128), lambda i: (i, 0, 0)),
                   pl.BlockSpec((1, 128, nc), lambda i: (i, 0, 0))],
        out_shape=[jax.ShapeDtypeStruct((b, nc, 128), BF16),
                   jax.ShapeDtypeStruct((b, 128, nc), BF16)],
        compiler_params=_cparams("parallel"),
        name="compress",
    )(ak, av, w1k, w1v, w2k, w2v, c0k, c0v)


def _compress_weights(pe, w1, w2):
    w1 = w1.astype(F32)
    half = (CMP_LEN // 2) * HEAD_DIM
    z1 = jnp.zeros((half, HEAD_DIM), F32)

    def bd(w):
        return jnp.concatenate([jnp.concatenate([w, z1], axis=1),
                                jnp.concatenate([z1, w], axis=1)], axis=0)

    w1s = jnp.stack([bd(w1[:half]), bd(w1[half:])])
    z2 = jnp.zeros((HEAD_DIM, HEAD_DIM), F32)
    w2f = w2.astype(F32)
    w2s = jnp.concatenate([jnp.concatenate([w2f, z2], axis=1),
                           jnp.concatenate([z2, w2f], axis=1)], axis=0)
    c0 = pe.astype(F32).reshape(1, CMP_LEN * HEAD_DIM) @ w1
    c0 = jnp.concatenate([c0, c0], axis=1)
    return w1s.astype(BF16), w2s.astype(BF16), c0


def _t5_bucket_np(dist):
    n = np.maximum(dist, 0)
    max_exact = REL_BUCKETS // 2
    nf = np.maximum(n, 1).astype(np.float32)
    large = max_exact + (np.log(nf / max_exact) / math.log(REL_MAX_DIST / max_exact)
                         * (REL_BUCKETS - max_exact)).astype(np.int32)
    large = np.minimum(large, REL_BUCKETS - 1)
    return np.where(n < max_exact, n, large)


def _bias_tables(rel_table, seq):
    tbl = rel_table.astype(F32)
    r = np.arange(Q_BLOCK)
    c = np.arange(SEL_LEN)
    d = np.arange(4)
    dist = 64 * d[:, None, None] + r[None, None, :] - c[None, :, None]
    bkt = _t5_bucket_np(dist)
    bkt[3] = REL_BUCKETS - 1
    bb = tbl[bkt]
    bb = bb.transpose(0, 1, 3, 2).reshape(4, SEL_LEN, NSA_Q_HEADS * Q_BLOCK)
    nqb = seq // Q_BLOCK
    ncp = seq // CMP_STRIDE
    qi = np.arange(nqb)
    n = np.arange(ncp)
    dist_c = (64 * qi[:, None, None] + r[None, None, :]) - (CMP_STRIDE * n[None, :, None] + CMP_LEN - 1)
    bc = tbl[_t5_bucket_np(dist_c)]
    bc = bc.transpose(0, 1, 3, 2).reshape(nqb, ncp, NSA_Q_HEADS * Q_BLOCK)
    return bb, bc


def _nsa_kernel(q_ref, gl_ref, kc_ref, vct_ref, ks_ref, vst_ref, kw_ref, vwt_ref, bc_ref, bb_ref,
                ov_ref, o_ref, sel_ref, *, ncp):
    qi = pl.program_id(1)
    nrow = NSA_Q_HEADS * Q_BLOCK
    lane = lax.broadcasted_iota(jnp.int32, (Q_BLOCK, 128), 1)

    q = q_ref[...] * (HEAD_DIM ** -0.5)
    pieces = []
    for h in range(NSA_KV_HEADS):
        for g in range(NSA_GQA):
            tile = q[:, g * 128:(g + 1) * 128]
            keep = (lane >= 64) if h == 1 else (lane < 64)
            pieces.append(jnp.where(keep, tile, 0.0))
    qpad = jnp.concatenate(pieces, axis=0).astype(BF16)

    rcol = lax.broadcasted_iota(jnp.int32, (1, nrow), 1) % Q_BLOCK
    nt_dims = (((1,), (1,)), ((), ()))

    s = lax.dot_general(kc_ref[0], qpad, nt_dims, preferred_element_type=F32) + bc_ref[0]
    nidx = lax.broadcasted_iota(jnp.int32, (ncp, 1), 0)
    exists = nidx < ncp - 1
    valid = jnp.logical_and(CMP_STRIDE * nidx + (CMP_LEN - 1) <= Q_BLOCK * qi + rcol, exists)
    sm = jnp.where(valid, s, NEG_INF)
    sm = jnp.where(exists, sm, NO_SLOT)
    m = jnp.max(sm, axis=0, keepdims=True)
    e = jnp.exp(sm - m)
    p_c = jnp.where(valid, e / jnp.sum(e, axis=0, keepdims=True), 0.0)
    p_cb = p_c.astype(BF16)
    o_c = [jnp.dot(vct_ref[0, h * 64:(h + 1) * 64, :], p_cb[:, h * 256:(h + 1) * 256],
                   preferred_element_type=F32) for h in range(NSA_KV_HEADS)]

    lane_c = lax.broadcasted_iota(jnp.int32, (ncp, 128), 1)
    halves = []
    for h in range(NSA_KV_HEADS):
        ph = p_c[:, h * 256:h * 256 + 128] + p_c[:, h * 256 + 128:(h + 1) * 256]
        halves.append(ph + pltpu.roll(ph, 64, 1))
    impsum = jnp.where(lane_c < 64, halves[0], halves[1])
    imp = jnp.dot(ov_ref[...], impsum, preferred_element_type=F32,
                  precision=lax.Precision.HIGHEST)
    nsel = imp.shape[0]
    jidx = lax.broadcasted_iota(jnp.int32, (nsel, 1), 0)
    forced = jnp.logical_or(jidx == 0, jnp.logical_or(jidx == qi, jidx == qi - 1))
    avail = jidx <= qi
    cand = jnp.logical_and(avail, jnp.logical_not(forced))
    budget = SEL_TOPK - (1 + (qi >= 1).astype(jnp.int32) + (qi >= 2).astype(jnp.int32))
    rank = jnp.zeros((nsel, 128), jnp.int32)
    for jp in range(1, nsel - 2):
        row = imp[jp:jp + 1, :]
        ge = jnp.where(row >= imp, 1, 0)
        gt = jnp.where(row > imp, 1, 0)
        is_cand = (jp <= qi - 2).astype(jnp.int32)
        rank = rank + jnp.where(jidx > jp, ge, gt) * is_cand
    sel = jnp.logical_or(jnp.logical_and(forced, avail), jnp.logical_and(cand, rank < budget))
    self = jnp.where(sel, 1.0, 0.0)
    lane_s = lax.broadcasted_iota(jnp.int32, (nsel, 128), 1)
    rolled = pltpu.roll(self, 64, 1)
    sel_ref[0] = jnp.where(lane_s < 64, self, rolled)
    sel_ref[1] = jnp.where(lane_s < 64, rolled, self)

    cidx = lax.broadcasted_iota(jnp.int32, (SEL_LEN, 1), 0)
    rel = rcol - cidx

    def attend(carry, k_tile, vt_tile, jj, half_masks):
        m_i, l_i, acc = carry
        s = lax.dot_general(k_tile, qpad, nt_dims, preferred_element_type=F32)
        parts, valids = [], []
        for half in range(2):
            d = qi - 2 * jj - half
            bias = bb_ref[jnp.clip(d, 0, 3)]
            valid, exists = half_masks(d, half)
            sh = s[half * 64:(half + 1) * 64] + bias
            sh = jnp.where(valid, sh, NEG_INF)
            sh = jnp.where(exists, sh, NO_SLOT)
            parts.append(sh)
            valids.append(valid)
        sm = jnp.concatenate(parts, axis=0)
        vmask = jnp.concatenate(valids, axis=0)
        m_new = jnp.maximum(m_i, jnp.max(sm, axis=0, keepdims=True))
        alpha = jnp.exp(m_i - m_new)
        p = jnp.exp(sm - m_new)
        l_new = alpha * l_i + jnp.sum(p, axis=0, keepdims=True)
        pv = jnp.where(vmask, p, 0.0).astype(BF16)
        new_acc = []
        for h in range(NSA_KV_HEADS):
            upd = jnp.dot(vt_tile[h * 64:(h + 1) * 64, :], pv[:, h * 256:(h + 1) * 256],
                          preferred_element_type=F32)
            new_acc.append(alpha[:, h * 256:(h + 1) * 256] * acc[h] + upd)
        return m_new, l_new, tuple(new_acc)

    zero_acc = tuple(jnp.zeros((HEAD_DIM, 256), F32) for _ in range(NSA_KV_HEADS))

    def sel_masks(jj):
        def fn(d, half):
            ra = sel_ref[0, pl.ds(2 * jj + half, 1), :]
            rb = sel_ref[1, pl.ds(2 * jj + half, 1), :]
            exists = jnp.concatenate([ra, ra, rb, rb], axis=1) > 0.5
            causal = 64 * d + rel >= 0
            return jnp.logical_and(exists, causal), exists
        return fn

    def sel_body(jj, carry):
        r0 = pl.multiple_of(jj * 128, 128)
        return attend(carry, ks_ref[0, pl.ds(r0, 128), :], vst_ref[0, jj], jj, sel_masks(jj))

    n_extra = (SEL_LEN * jnp.maximum(SEL_TOPK - (qi + 1), 0)).astype(F32)
    m0 = jnp.where(n_extra > 0, NEG_INF, NO_SLOT) + jnp.zeros((1, nrow), F32)
    l0 = n_extra + jnp.zeros((1, nrow), F32)
    _, l_s, acc_s = lax.fori_loop(0, qi // 2 + 1, sel_body, (m0, l0, zero_acc))

    zrow = jnp.zeros((1, nrow), jnp.int32)

    def win_masks(jj):
        def fn(d, half):
            dist = 64 * d + rel
            dv = d + zrow
            exists = jnp.logical_and(dv >= 0, dv <= WINDOW // SEL_LEN)
            in_seq = (jj + zrow) >= 0
            valid = jnp.logical_and(jnp.logical_and(dist >= 0, dist < WINDOW),
                                    jnp.logical_and(exists, in_seq))
            return valid, exists
        return fn

    carry = (jnp.full((1, nrow), NO_SLOT, F32), jnp.zeros((1, nrow), F32), zero_acc)
    for i in range(3):
        jj = qi // 2 - 2 + i
        jc = jnp.maximum(jj, 0)
        r0 = pl.multiple_of(jc * 128, 128)
        carry = attend(carry, kw_ref[0, pl.ds(r0, 128), :], vwt_ref[0, jc], jj, win_masks(jj))
    _, l_w, acc_w = carry

    g = jax.nn.sigmoid(gl_ref[...])
    gt = jnp.concatenate([g, g], axis=0).T
    lane1 = lax.broadcasted_iota(jnp.int32, (1, 128), 1)

    def gate_vec(h, br):
        tiles = []
        for gg in range(2):
            c0 = (h * 4 + 2 * gg) * 3 + br
            c1 = (h * 4 + 2 * gg + 1) * 3 + br
            tiles.append(jnp.where(lane1 < 64, gt[c0:c0 + 1, :], gt[c1:c1 + 1, :]))
        return jnp.concatenate(tiles, axis=1)

    tot = []
    for h in range(NSA_KV_HEADS):
        sl = slice(h * 256, (h + 1) * 256)
        t = gate_vec(h, 0) * o_c[h]
        t = t + (gate_vec(h, 1) / l_s[:, sl]) * acc_s[h]
        t = t + (gate_vec(h, 2) / l_w[:, sl]) * acc_w[h]
        tot.append(t)
    ot = jnp.concatenate(tot, axis=0).T
    o_ref[...] = jnp.concatenate([ot[g * 64:(g + 1) * 64, :] for g in range(NSA_GQA)], axis=1)


def _nsa(z, kc, vct, ks, vst, kw, vwt, bc, bb, ov, b, seq):
    nqb = seq // Q_BLOCK
    ncp = seq // CMP_STRIDE
    nkt = seq // 128
    return pl.pallas_call(
        functools.partial(_nsa_kernel, ncp=ncp),
        grid=(b, nqb),
        in_specs=[pl.BlockSpec((Q_BLOCK, NSA_WIDTH), lambda i, j: (i * nqb + j, Z_Q // NSA_WIDTH)),
                  pl.BlockSpec((Q_BLOCK, 128), lambda i, j: (i * nqb + j, Z_G // 128)),
                  pl.BlockSpec((1, ncp, 128), lambda i, j: (i, 0, 0)),
                  pl.BlockSpec((1, 128, ncp), lambda i, j: (i, 0, 0)),
                  pl.BlockSpec((1, seq, 128), lambda i, j: (i, 0, 0)),
                  pl.BlockSpec((1, nkt, 128, 128), lambda i, j: (i, 0, 0, 0)),
                  pl.BlockSpec((1, seq, 128), lambda i, j: (i, 0, 0)),
                  pl.BlockSpec((1, nkt, 128, 128), lambda i, j: (i, 0, 0, 0)),
                  pl.BlockSpec((1, ncp, 512), lambda i, j: (j, 0, 0)),
                  pl.BlockSpec(bb.shape, lambda i, j: (0, 0, 0)),
                  pl.BlockSpec(ov.shape, lambda i, j: (0, 0))],
        out_specs=pl.BlockSpec((Q_BLOCK, NSA_WIDTH), lambda i, j: (i * nqb + j, 0)),
        out_shape=jax.ShapeDtypeStruct((b * seq, NSA_WIDTH), F32),
        scratch_shapes=[pltpu.VMEM((2, seq // SEL_LEN, 128), F32)],
        compiler_params=_cparams("parallel", "arbitrary"),
        name="nsa",
    )(z, z, kc, vct, ks, vst, kw, vwt, bc, bb, ov)


def _lru_kernel(x_ref, g_ref, cw_ref, cb_ref, wa_ref, ba_ref, wx_ref, bx_ref, sp_ref, o_ref,
                xprev_ref, h_ref, a_scr, b_scr, *, nb, tc):
    @pl.when(pl.program_id(0) == 0)
    def _():
        xprev_ref[...] = jnp.zeros_like(xprev_ref)
        h_ref[...] = jnp.zeros_like(h_ref)

    x = x_ref[...]
    rows = x.shape[0]
    xcat = jnp.concatenate([xprev_ref[...], x], axis=0)
    xprev_ref[...] = x[rows - (CONV_WIDTH - 1) * nb:, :]
    cw = cw_ref[...]
    xc = cb_ref[...] + cw[0:1, :] * xcat[0:rows]
    for i in range(1, CONV_WIDTH):
        xc = xc + cw[i:i + 1, :] * xcat[i * nb:i * nb + rows]
    xcb = xc.astype(BF16)
    gate_r = jax.nn.sigmoid(jnp.dot(xcb, wa_ref[...], preferred_element_type=F32) + ba_ref[...])
    gate_i = jax.nn.sigmoid(jnp.dot(xcb, wx_ref[...], preferred_element_type=F32) + bx_ref[...])
    log_a = -LRU_C * gate_r * sp_ref[...]
    a_scr[...] = jnp.exp(log_a)
    th = jnp.tanh(log_a)
    b_scr[...] = jnp.sqrt(-2.0 * th / (1.0 - th)) * gate_i * xc

    def step(t, h):
        r0 = pl.multiple_of(t * nb, nb)
        h = a_scr[pl.ds(r0, nb), :] * h + b_scr[pl.ds(r0, nb), :]
        b_scr[pl.ds(r0, nb), :] = h
        return h

    h_ref[...] = lax.fori_loop(0, tc, step, h_ref[...])
    o_ref[...] = b_scr[...] * jax.nn.gelu(g_ref[...])


def _lru(x_tm, g_tm, cw, cb, wa, ba, wx, bx, sp, nb, tc=64):
    rows, w = x_tm.shape
    r = tc * nb
    c2 = lambda i: (0, 0)
    return pl.pallas_call(
        functools.partial(_lru_kernel, nb=nb, tc=tc),
        grid=(rows // r,),
        in_specs=[pl.BlockSpec((r, w), lambda i: (i, 0)),
                  pl.BlockSpec((r, w), lambda i: (i, 0)),
                  pl.BlockSpec(cw.shape, c2), pl.BlockSpec(cb.shape, c2),
                  pl.BlockSpec(wa.shape, c2), pl.BlockSpec(ba.shape, c2),
                  pl.BlockSpec(wx.shape, c2), pl.BlockSpec(bx.shape, c2),
                  pl.BlockSpec(sp.shape, c2)],
        out_specs=pl.BlockSpec((r, w), lambda i: (i, 0)),
        out_shape=jax.ShapeDtypeStruct((rows, w), F32),
        scratch_shapes=[pltpu.VMEM(((CONV_WIDTH - 1) * nb, w), F32), pltpu.VMEM((nb, w), F32),
                        pltpu.VMEM((r, w), F32), pltpu.VMEM((r, w), F32)],
        compiler_params=_cparams("arbitrary"),
        name="lru",
    )(x_tm, g_tm, cw, cb, wa, ba, wx, bx, sp)


def _block_diag(w):
    h, i, j = w.shape
    eye = jnp.eye(h, dtype=w.dtype)
    return jnp.einsum('hij,hk->hikj', w, eye).reshape(h * i, h * j)


def _outproj_kernel(h_ref, ys_ref, yn_ref, yl_ref, wglu_ref, wos_ref, won_ref, wol_ref, o_ref):
    gl = jnp.dot(ys_ref[...].astype(BF16), wglu_ref[...], preferred_element_type=F32)
    s5 = gl[:, :SSM_WIDTH] * jax.nn.sigmoid(gl[:, SSM_WIDTH:])
    acc = jnp.dot(s5.astype(BF16), wos_ref[...], preferred_element_type=F32)
    acc = acc + jnp.dot(yn_ref[...].astype(BF16), won_ref[...], preferred_element_type=F32)
    acc = acc + jnp.dot(yl_ref[...].astype(BF16), wol_ref[...], preferred_element_type=F32)
    o_ref[...] = h_ref[...] + acc


def _outproj(h, ys, yn, yl, wglu, wos, won, wol, tm=512):
    t, d = h.shape
    c2 = lambda i: (0, 0)
    row = lambda w: pl.BlockSpec((tm, w), lambda i: (i, 0))
    return pl.pallas_call(
        _outproj_kernel,
        grid=(t // tm,),
        in_specs=[row(d), row(SSM_WIDTH), row(NSA_WIDTH), row(LRU_WIDTH),
                  pl.BlockSpec(wglu.shape, c2), pl.BlockSpec(wos.shape, c2),
                  pl.BlockSpec(won.shape, c2), pl.BlockSpec(wol.shape, c2)],
        out_specs=row(d),
        out_shape=jax.ShapeDtypeStruct((t, d), F32),
        compiler_params=_cparams("parallel"),
        name="outproj",
    )(h, ys, yn, yl, wglu, wos, won, wol)


def _ffn_kernel(h_ref, g_ref, wg_ref, wu_ref, wd_ref, gf_ref, o_ref, *, final):
    x = h_ref[...]
    y = (x * lax.rsqrt(jnp.mean(x * x, axis=-1, keepdims=True) + RMS_EPS) * g_ref[...]).astype(BF16)
    a = jnp.dot(y, wg_ref[...], preferred_element_type=F32)
    b = jnp.dot(y, wu_ref[...], preferred_element_type=F32)
    m = (jax.nn.silu(a) * b).astype(BF16)
    out = x + jnp.dot(m, wd_ref[...], preferred_element_type=F32)
    if final:
        out = out * lax.rsqrt(jnp.mean(out * out, axis=-1, keepdims=True) + RMS_EPS) * gf_ref[...]
    o_ref[...] = out


def _ffn(h, g, wg, wu, wd, gf, final, tm=256):
    t, d = h.shape
    c2 = lambda i: (0, 0)
    return pl.pallas_call(
        functools.partial(_ffn_kernel, final=final),
        grid=(t // tm,),
        in_specs=[pl.BlockSpec((tm, d), lambda i: (i, 0)),
                  pl.BlockSpec(g.shape, c2), pl.BlockSpec(wg.shape, c2), pl.BlockSpec(wu.shape, c2),
                  pl.BlockSpec(wd.shape, c2), pl.BlockSpec(gf.shape, c2)],
        out_specs=pl.BlockSpec((tm, d), lambda i: (i, 0)),
        out_shape=jax.ShapeDtypeStruct((t, d), F32),
        compiler_params=_cparams("parallel"),
        name="ffn",
    )(h, g, wg, wu, wd, gf)


def _q_perm():
    idx = np.zeros(NSA_WIDTH, np.int32)
    for g in range(NSA_GQA):
        for h in range(NSA_KV_HEADS):
            for d in range(HEAD_DIM):
                idx[g * 128 + h * 64 + d] = (h * NSA_GQA + g) * HEAD_DIM + d
    return idx


def _prep_w_in(w_in):
    o1 = SSM_WIDTH
    o2 = o1 + NSA_WIDTH
    o3 = o2 + 6 * NSA_KV_WIDTH
    o4 = o3 + 3 * NSA_Q_HEADS
    wq = w_in[:, o1:o2][:, _q_perm()]
    pad = jnp.zeros((w_in.shape[0], 128 - 3 * NSA_Q_HEADS), w_in.dtype)
    w = jnp.concatenate([wq, w_in[:, :o1], w_in[:, o2:o3], w_in[:, o3:o4], pad, w_in[:, o4:]], axis=1)
    return w.astype(BF16)


def _s5_mixer(z, b, seq, p):
    L = S5_CHUNK
    u_tm = z[:, Z_U:Z_U + SSM_WIDTH].reshape(b, seq // L, L * SSM_WIDTH).transpose(1, 0, 2)
    u_tm = u_tm.reshape((seq // L) * b, L * SSM_WIDTH)
    we, tz, cp, a16, dg = _s5_tables(p['s5_lam_re'], p['s5_lam_im'], p['s5_log_dt'], p['s5_b_re'],
                                     p['s5_b_im'], p['s5_c_re'], p['s5_c_im'], p['s5_d'])
    y_tm = _s5(u_tm, we, tz, cp, a16, dg, nb=b)
    return y_tm.reshape(seq // L, b, L * SSM_WIDTH).transpose(1, 0, 2).reshape(b * seq, SSM_WIDTH)


def _nsa_mixer(z, b, seq, bias_tabs, ov, p):
    kv = z[:, Z_KV:Z_KV + 6 * NSA_KV_WIDTH].reshape(b, seq, 6, NSA_KV_WIDTH)
    ncp = seq // CMP_STRIDE

    def cmp_layout(x):
        x = x.reshape(b, ncp, CMP_STRIDE, NSA_KV_HEADS, HEAD_DIM).transpose(0, 1, 3, 2, 4)
        return x.reshape(b, ncp, NSA_KV_HEADS * CMP_STRIDE * HEAD_DIM).astype(BF16)

    def key_t(x):
        return x.reshape(b, seq // 128, 128, 128).transpose(0, 1, 3, 2).astype(BF16)

    w1k, w2k, c0k = _compress_weights(p['nsa_pe_k'], p['nsa_w1_k'], p['nsa_w2_k'])
    w1v, w2v, c0v = _compress_weights(p['nsa_pe_v'], p['nsa_w1_v'], p['nsa_w2_v'])
    kc, vct = _compress(cmp_layout(kv[:, :, 0]), cmp_layout(kv[:, :, 1]), w1k, w1v, w2k, w2v, c0k, c0v)
    bb, bc = bias_tabs
    return _nsa(z, kc, vct, kv[:, :, 2].astype(BF16), key_t(kv[:, :, 3]),
                kv[:, :, 4].astype(BF16), key_t(kv[:, :, 5]), bc, bb, ov, b, seq)


def _lru_mixer(z, b, seq, p):
    t = b * seq

    def time_major(x):
        return x.reshape(b, seq, LRU_WIDTH).transpose(1, 0, 2).reshape(t, LRU_WIDTH)

    sp = jax.nn.softplus(-p['lru_lam'].astype(F32)).reshape(1, -1)
    yl_tm = _lru(time_major(z[:, Z_LX:Z_LX + LRU_WIDTH]), time_major(z[:, Z_LG:Z_LG + LRU_WIDTH]),
                 p['lru_conv_w'].astype(F32), p['lru_conv_b'].astype(F32).reshape(1, -1),
                 _block_diag(p['lru_w_a']).astype(BF16), p['lru_b_a'].astype(F32).reshape(1, -1),
                 _block_diag(p['lru_w_x']).astype(BF16), p['lru_b_x'].astype(F32).reshape(1, -1),
                 sp, nb=b)
    return yl_tm.reshape(seq, b, LRU_WIDTH).transpose(1, 0, 2).reshape(t, LRU_WIDTH)


def _layer(h2, b, seq, bias_tabs, ov, p, final, norm_final):
    z = _inproj(h2, p['norm_mix'].reshape(1, -1), _prep_w_in(p['w_in']))
    ys = _s5_mixer(z, b, seq, p)
    yn = _nsa_mixer(z, b, seq, bias_tabs, ov, p)
    yl = _lru_mixer(z, b, seq, p)
    w_out = p['w_out']
    wos = w_out[:SSM_WIDTH].astype(BF16)
    won = w_out[SSM_WIDTH:SSM_WIDTH + NSA_WIDTH][_q_perm()].astype(BF16)
    wol = w_out[SSM_WIDTH + NSA_WIDTH:].astype(BF16)
    h2 = _outproj(h2, ys, yn, yl, p['s5_w_glu'].astype(BF16), wos, won, wol)
    return _ffn(h2, p['norm_ffn'].reshape(1, -1), p['w_gate'].astype(BF16), p['w_up'].astype(BF16),
                p['w_down'].astype(BF16), norm_final.reshape(1, -1), final)


def _overlap_t(seq):
    ncp = seq // CMP_STRIDE
    nsel = seq // SEL_LEN
    cs = np.arange(ncp) * CMP_STRIDE
    ss = np.arange(nsel) * SEL_LEN
    ovl = (cs[None, :] < ss[:, None] + SEL_LEN) & (ss[:, None] < cs[None, :] + CMP_LEN)
    ovl[:, ncp - 1] = False
    return jnp.asarray(ovl.astype(np.float32))


def kernel(x, rel_bias_table, norm_mix, w_in, w_out, s5_lam_re, s5_lam_im, s5_log_dt, s5_b_re, s5_b_im, s5_c_re, s5_c_im, s5_d, s5_w_glu, nsa_pe_k, nsa_w1_k, nsa_w2_k, nsa_pe_v, nsa_w1_v, nsa_w2_v, lru_conv_w, lru_conv_b, lru_w_a, lru_b_a, lru_w_x, lru_b_x, lru_lam, norm_ffn, w_gate, w_up, w_down, norm_final):
    b, seq, d = x.shape
    depth = norm_mix.shape[0]
    per_layer = dict(norm_mix=norm_mix, w_in=w_in, w_out=w_out, s5_lam_re=s5_lam_re, s5_lam_im=s5_lam_im,
                     s5_log_dt=s5_log_dt, s5_b_re=s5_b_re, s5_b_im=s5_b_im, s5_c_re=s5_c_re,
                     s5_c_im=s5_c_im, s5_d=s5_d, s5_w_glu=s5_w_glu, nsa_pe_k=nsa_pe_k, nsa_w1_k=nsa_w1_k,
                     nsa_w2_k=nsa_w2_k, nsa_pe_v=nsa_pe_v, nsa_w1_v=nsa_w1_v, nsa_w2_v=nsa_w2_v,
                     lru_conv_w=lru_conv_w, lru_conv_b=lru_conv_b, lru_w_a=lru_w_a, lru_b_a=lru_b_a,
                     lru_w_x=lru_w_x, lru_b_x=lru_b_x, lru_lam=lru_lam, norm_ffn=norm_ffn, w_gate=w_gate,
                     w_up=w_up, w_down=w_down)
    bias_tabs = _bias_tables(rel_bias_table, seq)
    ov = _overlap_t(seq)
    h2 = x.reshape(b * seq, d)
    for l in range(depth):
        p = {k: v[l] for k, v in per_layer.items()}
        h2 = _layer(h2, b, seq, bias_tabs, ov, p, l == depth - 1, norm_final)
    return h2.reshape(b, seq, d)
```

```python
import functools
import math

import numpy as np
import jax
import jax.numpy as jnp
from jax import lax
from jax.experimental import pallas as pl
from jax.experimental.pallas import tpu as pltpu

F32 = jnp.float32
BF16 = jnp.bfloat16

D_MODEL = 1024
SSM_WIDTH = 256
NSA_WIDTH = 512
LRU_WIDTH = 256
S5_GROUP = 16
S5_GROUPS = 16
S5_STATE = 64
HEAD_DIM = 64
NSA_Q_HEADS = 8
NSA_KV_HEADS = 2
NSA_GQA = 4
NSA_KV_WIDTH = 128
CMP_LEN = 32
CMP_STRIDE = 16
SEL_LEN = 64
SEL_TOPK = 8
WINDOW = 256
Q_BLOCK = 64
LRU_HEADS = 4
LRU_HEAD_DIM = 64
CONV_WIDTH = 4
LRU_C = 8.0
REL_BUCKETS = 32
REL_MAX_DIST = 128
D_FF = 2816
NEG_INF = -1e9
RMS_EPS = 1e-6

NO_SLOT = -3e38

S5_CHUNK = 16
Z_WIDTH = 2176
Z_Q, Z_U, Z_KV, Z_G, Z_LX, Z_LG = 0, 512, 768, 1536, 1664, 1920

VMEM_LIMIT = 56 * 1024 * 1024


def _cparams(*sem):
    return pltpu.CompilerParams(dimension_semantics=sem, vmem_limit_bytes=VMEM_LIMIT)


def _inproj_kernel(x_ref, g_ref, w_ref, o_ref):
    x = x_ref[...]
    y = x * lax.rsqrt(jnp.mean(x * x, axis=-1, keepdims=True) + RMS_EPS) * g_ref[...]
    o_ref[...] = jnp.dot(y.astype(BF16), w_ref[...], preferred_element_type=F32)


def _inproj(x2, g, w, tm=512):
    t, d = x2.shape
    n = w.shape[1]
    return pl.pallas_call(
        _inproj_kernel,
        grid=(t // tm,),
        in_specs=[pl.BlockSpec((tm, d), lambda i: (i, 0)),
                  pl.BlockSpec((1, d), lambda i: (0, 0)),
                  pl.BlockSpec((d, n), lambda i: (0, 0))],
        out_specs=pl.BlockSpec((tm, n), lambda i: (i, 0)),
        out_shape=jax.ShapeDtypeStruct((t, n), F32),
        compiler_params=_cparams("parallel"),
        name="inproj",
    )(x2, g, w)


def _lane_regroup(x, n_outer, n_inner):
    r = x.shape[0]
    nv = n_outer * n_inner * 16 // 128
    cols = [x[:, v * 128:(v + 1) * 128] for v in range(nv)]
    lane16 = lax.broadcasted_iota(jnp.int32, (r, 128), 1) // 16
    outs = []
    for w in range(nv):
        acc = None
        for k in range(8):
            dst = w * 8 + k
            i, o = dst // n_outer, dst % n_outer
            src = o * n_inner + i
            v, sk = src // 8, src % 8
            piece = cols[v]
            shift = ((k - sk) * 16) % 128
            if shift:
                piece = pltpu.roll(piece, shift, 1)
            acc = piece if acc is None else jnp.where(lane16 == k, piece, acc)
        outs.append(acc)
    return jnp.concatenate(outs, axis=1)


def _s5_kernel(u_ref, we_ref, tz_ref, cp_ref, a_ref, d_ref, o_ref, e_ref, carry_ref, *, nb, kc):
    @pl.when(pl.program_id(0) == 0)
    def _():
        carry_ref[...] = jnp.zeros_like(carry_ref)

    ug = _lane_regroup(u_ref[...], S5_CHUNK, S5_GROUPS)
    ugb = ug.astype(BF16)
    npair = S5_GROUPS // 2
    for p in range(npair):
        e_ref[:, p * 256:(p + 1) * 256] = jnp.dot(ugb[:, p * 512:(p + 1) * 512], we_ref[p],
                                                   preferred_element_type=F32)

    a = a_ref[...]

    def step(k, carry):
        r0 = pl.multiple_of(k * nb, nb)
        e = e_ref[pl.ds(r0, nb), :]
        e_ref[pl.ds(r0, nb), :] = carry
        new = []
        for p in range(npair):
            ar = a[:, p * 256:p * 256 + 128]
            ai = a[:, p * 256 + 128:(p + 1) * 256]
            cr = carry[:, p * 256:p * 256 + 128]
            ci = carry[:, p * 256 + 128:(p + 1) * 256]
            new.append(ar * cr - ai * ci + e[:, p * 256:p * 256 + 128])
            new.append(ar * ci + ai * cr + e[:, p * 256 + 128:(p + 1) * 256])
        return jnp.concatenate(new, axis=1)

    carry_ref[...] = lax.fori_loop(0, kc, step, carry_ref[...])

    eb = e_ref[...].astype(BF16)
    ys = []
    for p in range(npair):
        yc = jnp.dot(eb[:, p * 256:(p + 1) * 256], cp_ref[p], preferred_element_type=F32)
        y0 = jnp.dot(ugb[:, (2 * p) * 256:(2 * p + 1) * 256], tz_ref[2 * p], preferred_element_type=F32)
        y1 = jnp.dot(ugb[:, (2 * p + 1) * 256:(2 * p + 2) * 256], tz_ref[2 * p + 1],
                     preferred_element_type=F32)
        ys.append(yc + jnp.concatenate([y0, y1], axis=1))
    y = jnp.concatenate(ys, axis=1) + d_ref[...] * ug
    y = jax.nn.gelu(y)
    o_ref[...] = _lane_regroup(y, S5_GROUPS, S5_CHUNK)


def _s5(u_tm, we, tz, cp, a16, dg, nb, kc=16):
    rows, width = u_tm.shape
    r = kc * nb
    const3 = lambda i: (0, 0, 0)
    return pl.pallas_call(
        functools.partial(_s5_kernel, nb=nb, kc=kc),
        grid=(rows // r,),
        in_specs=[pl.BlockSpec((r, width), lambda i: (i, 0)),
                  pl.BlockSpec(we.shape, const3),
                  pl.BlockSpec(tz.shape, const3),
                  pl.BlockSpec(cp.shape, const3),
                  pl.BlockSpec(a16.shape, lambda i: (0, 0)),
                  pl.BlockSpec(dg.shape, lambda i: (0, 0))],
        out_specs=pl.BlockSpec((r, width), lambda i: (i, 0)),
        out_shape=jax.ShapeDtypeStruct((rows, width), F32),
        scratch_shapes=[pltpu.VMEM((r, 2048), F32), pltpu.VMEM((nb, 2048), F32)],
        compiler_params=_cparams("arbitrary"),
        name="s5",
    )(u_tm, we, tz, cp, a16, dg)


def _s5_tables(lam_re, lam_im, log_dt, b_re, b_im, c_re, c_im, d_skip):
    L = S5_CHUNK
    G, P, C = S5_GROUPS, S5_STATE, S5_GROUP
    dt = jnp.exp(log_dt.astype(F32))[:, None]
    lr = lam_re.astype(F32)
    li = lam_im.astype(F32)
    mag = jnp.exp(lr * dt)
    ang = li * dt
    ab_re = mag * jnp.cos(ang)
    ab_im = mag * jnp.sin(ang)
    den = lr * lr + li * li
    f_re = ((ab_re - 1.0) * lr + ab_im * li) / den
    f_im = (ab_im * lr - (ab_re - 1.0) * li) / den
    br = b_re.astype(F32)
    bi = b_im.astype(F32)
    bb_re = f_re[..., None] * br - f_im[..., None] * bi
    bb_im = f_re[..., None] * bi + f_im[..., None] * br
    cr = c_re.astype(F32)
    ci = c_im.astype(F32)
    tau = jnp.arange(L + 1, dtype=F32)[:, None, None]
    pr = jnp.exp(lr * dt * tau) * jnp.cos(li * dt * tau)
    pi = jnp.exp(lr * dt * tau) * jnp.sin(li * dt * tau)

    prs = pr[L - 1 - jnp.arange(L)]
    pis = pi[L - 1 - jnp.arange(L)]
    we_re = jnp.einsum('sgp,gpc->gscp', prs, bb_re) - jnp.einsum('sgp,gpc->gscp', pis, bb_im)
    we_im = jnp.einsum('sgp,gpc->gscp', prs, bb_im) + jnp.einsum('sgp,gpc->gscp', pis, bb_re)
    we_re = we_re.reshape(G, L * C, P)
    we_im = we_im.reshape(G, L * C, P)
    z = jnp.zeros_like(we_re[0::2])
    top = jnp.concatenate([we_re[0::2], z, we_im[0::2], z], axis=-1)
    bot = jnp.concatenate([z, we_re[1::2], z, we_im[1::2]], axis=-1)
    we = jnp.concatenate([top, bot], axis=1)

    m_re = pr[:L, :, :, None] * bb_re[None] - pi[:L, :, :, None] * bb_im[None]
    m_im = pr[:L, :, :, None] * bb_im[None] + pi[:L, :, :, None] * bb_re[None]
    kern = jnp.einsum('gop,tgpc->tgoc', cr, m_re) - jnp.einsum('gop,tgpc->tgoc', ci, m_im)
    s_idx = np.arange(L)[:, None]
    t_idx = np.arange(L)[None, :]
    lag = np.clip(t_idx - s_idx, 0, L - 1)
    causal = jnp.asarray((t_idx >= s_idx).astype(np.float32))
    tzf = kern[lag] * causal[:, :, None, None, None]
    tz = tzf.transpose(2, 0, 4, 1, 3).reshape(G, L * C, L * C)

    pr1 = pr[1:]
    pi1 = pi[1:]
    cp_re = jnp.einsum('gop,tgp->gpto', cr, pr1) - jnp.einsum('gop,tgp->gpto', ci, pi1)
    cp_im = -(jnp.einsum('gop,tgp->gpto', cr, pi1) + jnp.einsum('gop,tgp->gpto', ci, pr1))
    cp_re = cp_re.reshape(G, P, L * C)
    cp_im = cp_im.reshape(G, P, L * C)
    zc = jnp.zeros_like(cp_re[0::2])
    cp = jnp.concatenate([
        jnp.concatenate([cp_re[0::2], zc], axis=-1),
        jnp.concatenate([zc, cp_re[1::2]], axis=-1),
        jnp.concatenate([cp_im[0::2], zc], axis=-1),
        jnp.concatenate([zc, cp_im[1::2]], axis=-1)], axis=1)

    a_re = pr[L].reshape(G // 2, 2 * P)
    a_im = pi[L].reshape(G // 2, 2 * P)
    a16 = jnp.concatenate([a_re, a_im], axis=-1).reshape(1, G * 2 * P)
    dg = jnp.broadcast_to(d_skip.astype(F32).reshape(G, 1, C), (G, L, C)).reshape(1, G * L * C)
    return we.astype(BF16), tz.astype(BF16), cp.astype(BF16), a16, dg


def _compress_kernel(ak_ref, av_ref, w1k_ref, w1v_ref, w2k_ref, w2v_ref, c0k_ref, c0v_ref,
                     kc_ref, vct_ref):
    def run(a_ref, w1_ref, w2_ref, c0_ref):
        a = a_ref[0]
        p1 = jnp.dot(a, w1_ref[0], preferred_element_type=F32)
        p2 = jnp.dot(a, w1_ref[1], preferred_element_type=F32)
        n = p1.shape[0]
        pre = p1 + pltpu.roll(p2, n - 1, 0) + c0_ref[...]
        out = jnp.dot(jax.nn.gelu(pre).astype(BF16), w2_ref[...], preferred_element_type=F32)
        row = lax.broadcasted_iota(jnp.int32, out.shape, 0)
        return jnp.where(row < n - 1, out, 0.0)

    kc_ref[0] = run(ak_ref, w1k_ref, w2k_ref, c0k_ref).astype(BF16)
    vct_ref[0] = run(av_ref, w1v_ref, w2v_ref, c0v_ref).T.astype(BF16)


def _compress(ak, av, w1k, w1v, w2k, w2v, c0k, c0v):
    b, nc, kw = ak.shape
    c3 = lambda i: (0, 0, 0)
    c2 = lambda i: (0, 0)
    return pl.pallas_call(
        _compress_kernel,
        grid=(b,),
        in_specs=[pl.BlockSpec((1, nc, kw), lambda i: (i, 0, 0)),
                  pl.BlockSpec((1, nc, kw), lambda i: (i, 0, 0)),
                  pl.BlockSpec(w1k.shape, c3), pl.BlockSpec(w1v.shape, c3),
                  pl.BlockSpec(w2k.shape, c2), pl.BlockSpec(w2v.shape, c2),
                  pl.BlockSpec(c0k.shape, c2), pl.BlockSpec(c0v.shape, c2)],
        out_specs=[pl.BlockSpec((1, nc, 128), lambda i: (i, 0, 0)),
                   pl.BlockSpec((1, 128, nc), lambda i: (i, 0, 0))],
        out_shape=[jax.ShapeDtypeStruct((b, nc, 128), BF16),
                   jax.ShapeDtypeStruct((b, 128, nc), BF16)],
        compiler_params=_cparams("parallel"),
        name="compress",
    )(ak, av, w1k, w1v, w2k, w2v, c0k, c0v)


def _compress_weights(pe, w1, w2):
    w1 = w1.astype(F32)
    half = (CMP_LEN // 2) * HEAD_DIM
    z1 = jnp.zeros((half, HEAD_DIM), F32)

    def bd(w):
        return jnp.concatenate([jnp.concatenate([w, z1], axis=1),
                                jnp.concatenate([z1, w], axis=1)], axis=0)

    w1s = jnp.stack([bd(w1[:half]), bd(w1[half:])])
    z2 = jnp.zeros((HEAD_DIM, HEAD_DIM), F32)
    w2f = w2.astype(F32)
    w2s = jnp.concatenate([jnp.concatenate([w2f, z2], axis=1),
                           jnp.concatenate([z2, w2f], axis=1)], axis=0)
    c0 = pe.astype(F32).reshape(1, CMP_LEN * HEAD_DIM) @ w1
    c0 = jnp.concatenate([c0, c0], axis=1)
    return w1s.astype(BF16), w2s.astype(BF16), c0


def _t5_bucket_np(dist):
    n = np.maximum(dist, 0)
    max_exact = REL_BUCKETS // 2
    nf = np.maximum(n, 1).astype(np.float32)
    large = max_exact + (np.log(nf / max_exact) / math.log(REL_MAX_DIST / max_exact)
                         * (REL_BUCKETS - max_exact)).astype(np.int32)
    large = np.minimum(large, REL_BUCKETS - 1)
    return np.where(n < max_exact, n, large)


TAB_FAR, TAB_WIN_EDGE, TAB_NONE = 3, 4, 5


def _bias_tables(rel_table, seq):
    tbl = rel_table.astype(F32)
    heads = NSA_Q_HEADS
    r = np.arange(Q_BLOCK)[None, :]
    c = np.arange(SEL_LEN)[:, None]
    near = tbl[_t5_bucket_np(64 * np.arange(3)[:, None, None] + (r - c)[None])]
    far = jnp.broadcast_to(tbl[REL_BUCKETS - 1], (SEL_LEN, Q_BLOCK, heads))
    none = jnp.full((SEL_LEN, Q_BLOCK, heads), NO_SLOT, F32)
    causal = jnp.asarray(r - c >= 0)[:, :, None]
    edge = jnp.asarray(c > r)[:, :, None]
    tab = jnp.stack([jnp.where(causal, near[0], none), near[1], near[2], far,
                     jnp.where(edge, far, none), none])
    tab = tab.transpose(0, 1, 3, 2).reshape(6, SEL_LEN, heads * Q_BLOCK)
    ncp = seq // CMP_STRIDE
    k = np.arange(2 * ncp + 4)[:, None]
    dist = r - CMP_STRIDE * (k - (ncp - 4)) - (CMP_LEN - 1)
    t2 = jnp.where(jnp.asarray(dist >= 0)[:, :, None], tbl[_t5_bucket_np(dist)], NO_SLOT)
    t2 = t2.transpose(0, 2, 1).reshape(2 * ncp + 4, heads * Q_BLOCK)
    t2 = jnp.stack([t2[:2 * ncp], t2[4:]])
    return tab, t2


def _nsa_kernel(q_ref, gl_ref, kc_ref, vct_ref, ks_ref, vst_ref, kw_ref, vwt_ref, t2_ref, tab_ref,
                ov_ref, o_ref, sel_ref, *, ncp):
    qi = pl.program_id(1)
    nrow = NSA_Q_HEADS * Q_BLOCK
    vrows = HEAD_DIM + 16
    lane = lax.broadcasted_iota(jnp.int32, (Q_BLOCK, 128), 1)

    q = q_ref[...] * (HEAD_DIM ** -0.5)
    pieces = []
    for h in range(NSA_KV_HEADS):
        for g in range(NSA_GQA):
            tile = q[:, g * 128:(g + 1) * 128]
            keep = (lane >= 64) if h == 1 else (lane < 64)
            pieces.append(jnp.where(keep, tile, 0.0))
    qpad = jnp.concatenate(pieces, axis=0).astype(BF16)

    rcol = lax.broadcasted_iota(jnp.int32, (1, nrow), 1) % Q_BLOCK
    tvec = Q_BLOCK * qi + rcol
    nt_dims = (((1,), (1,)), ((), ()))

    def slot_terms(m, n_masked):
        m_fin = jnp.where(n_masked > 0, jnp.maximum(m, NEG_INF), m)
        return jnp.exp(m - m_fin), n_masked * jnp.exp(NEG_INF - m_fin)

    start = (ncp - 4) - 4 * qi
    par = (start // 4) % 2
    a0 = pl.multiple_of(start - 4 * par, 8)
    s = lax.dot_general(kc_ref[0], qpad, nt_dims, preferred_element_type=F32) + t2_ref[par, pl.ds(a0, ncp), :]
    m = jnp.max(s, axis=0, keepdims=True)
    e = jnp.exp(s - m)
    n_valid = jnp.clip(jnp.right_shift(tvec - (CMP_STRIDE - 1), 4), 0, ncp - 1)
    scale, extra = slot_terms(m, (ncp - 1 - n_valid).astype(F32))
    p_c = e * (scale / (jnp.sum(e, axis=0, keepdims=True) * scale + extra))
    p_cb = p_c.astype(BF16)
    o_c = [jnp.dot(vct_ref[0, h * 64:(h + 1) * 64, :], p_cb[:, h * 256:(h + 1) * 256],
                   preferred_element_type=F32) for h in range(NSA_KV_HEADS)]

    lane_c = lax.broadcasted_iota(jnp.int32, (ncp, 128), 1)
    halves = []
    for h in range(NSA_KV_HEADS):
        ph = p_c[:, h * 256:h * 256 + 128] + p_c[:, h * 256 + 128:(h + 1) * 256]
        halves.append(ph + pltpu.roll(ph, 64, 1))
    impsum = jnp.where(lane_c < 64, halves[0], halves[1])
    imp = jnp.dot(ov_ref[...], impsum, preferred_element_type=F32,
                  precision=lax.Precision.HIGHEST)
    nsel = imp.shape[0]
    jidx = lax.broadcasted_iota(jnp.int32, (nsel, 1), 0)
    forced = jnp.logical_or(jidx == 0, jnp.logical_or(jidx == qi, jidx == qi - 1))
    avail = jidx <= qi
    cand = jnp.logical_and(avail, jnp.logical_not(forced))
    budget = SEL_TOPK - (1 + (qi >= 1).astype(jnp.int32) + (qi >= 2).astype(jnp.int32))
    rank = jnp.zeros((nsel, 128), jnp.int32)
    for jp in range(1, nsel - 2):
        row = imp[jp:jp + 1, :]
        ge = jnp.where(row >= imp, 1, 0)
        gt = jnp.where(row > imp, 1, 0)
        is_cand = (jp <= qi - 2).astype(jnp.int32)
        rank = rank + jnp.where(jidx > jp, ge, gt) * is_cand
    sel = jnp.logical_or(jnp.logical_and(forced, avail), jnp.logical_and(cand, rank < budget))
    seladd = jnp.where(sel, 0.0, NO_SLOT)
    lane_s = lax.broadcasted_iota(jnp.int32, (nsel, 128), 1)
    rolled = pltpu.roll(seladd, 64, 1)
    sel_ref[0] = jnp.where(lane_s < 64, seladd, rolled)
    sel_ref[1] = jnp.where(lane_s < 64, rolled, seladd)

    def softmax_tile(m_i, s, adds):
        sm = jnp.concatenate([s[k * 64:(k + 1) * 64] + adds[k] for k in range(len(adds))], axis=0)
        m_new = jnp.maximum(m_i, jnp.max(sm, axis=0, keepdims=True))
        return m_new, jnp.exp(m_i - m_new), jnp.exp(sm - m_new).astype(BF16)

    def accumulate(acc, alpha, vt_tile, p):
        return tuple(alpha[:, h * 256:(h + 1) * 256] * acc[h]
                     + jnp.dot(vt_tile[h * vrows:(h + 1) * vrows, :], p[:, h * 256:(h + 1) * 256],
                               preferred_element_type=F32) for h in range(NSA_KV_HEADS))

    m_init = jnp.full((1, nrow), NO_SLOT, F32)
    zero_acc = tuple(jnp.zeros((vrows, 256), F32) for _ in range(NSA_KV_HEADS))

    m_w, acc_w = m_init, zero_acc
    for i in range(3):
        jj = qi // 2 - 2 + i
        jc = jnp.maximum(jj, 0)
        r0 = pl.multiple_of(jc * 128, 128)
        s = lax.dot_general(kw_ref[0, pl.ds(r0, 128), :], qpad, nt_dims, preferred_element_type=F32)
        adds = []
        for half in range(2):
            d = qi - 2 * jj - half
            adds.append(tab_ref[jnp.where(jnp.logical_or(jj < 0, d < 0), TAB_NONE, d)])
        m_w, alpha, p = softmax_tile(m_w, s, adds)
        acc_w = accumulate(acc_w, alpha, vwt_ref[0, jc], p)

    def sel_body(u, carry):
        m_i, acc = carry
        r0 = pl.multiple_of(u * 256, 256)
        s = lax.dot_general(ks_ref[0, pl.ds(r0, 256), :], qpad, nt_dims, preferred_element_type=F32)
        adds = []
        for k in range(4):
            d = qi - 4 * u - k
            ra = sel_ref[0, pl.ds(4 * u + k, 1), :]
            rb = sel_ref[1, pl.ds(4 * u + k, 1), :]
            row = jnp.concatenate([ra, ra, rb, rb], axis=1)
            adds.append(tab_ref[jnp.where(d < 0, TAB_FAR, jnp.minimum(d, TAB_FAR))] + row)
        m_new, alpha, p = softmax_tile(m_i, s, adds)
        return m_new, accumulate(acc, alpha, vst_ref[0, u], p)

    m_s, acc_s = lax.fori_loop(0, qi // 4 + 1, sel_body, (m_init, zero_acc))

    n_extra = (SEL_LEN * jnp.maximum(SEL_TOPK - (qi + 1), 0)).astype(F32)
    nm_s = n_extra + (SEL_LEN - 1 - rcol).astype(F32)
    nm_w = (WINDOW + Q_BLOCK - jnp.minimum(tvec + 1, WINDOW)).astype(F32)

    def normalised(m, acc, n_masked):
        scale, extra = slot_terms(m, n_masked)
        out = []
        for h in range(NSA_KV_HEADS):
            sl = slice(h * 256, (h + 1) * 256)
            denom = acc[h][HEAD_DIM:HEAD_DIM + 1, :] * scale[:, sl] + extra[:, sl]
            out.append((scale[:, sl] / denom, acc[h][0:HEAD_DIM, :]))
        return out

    o_s = normalised(m_s, acc_s, nm_s)
    o_w = normalised(m_w, acc_w, nm_w)

    g = jax.nn.sigmoid(gl_ref[...])
    gt = jnp.concatenate([g, g], axis=0).T
    lane1 = lax.broadcasted_iota(jnp.int32, (1, 128), 1)

    def gate_vec(h, br):
        tiles = []
        for gg in range(2):
            c0 = (h * 4 + 2 * gg) * 3 + br
            c1 = (h * 4 + 2 * gg + 1) * 3 + br
            tiles.append(jnp.where(lane1 < 64, gt[c0:c0 + 1, :], gt[c1:c1 + 1, :]))
        return jnp.concatenate(tiles, axis=1)

    tot = []
    for h in range(NSA_KV_HEADS):
        t = gate_vec(h, 0) * o_c[h]
        t = t + (gate_vec(h, 1) * o_s[h][0]) * o_s[h][1]
        t = t + (gate_vec(h, 2) * o_w[h][0]) * o_w[h][1]
        tot.append(t)
    ot = jnp.concatenate(tot, axis=0).T
    o_ref[...] = jnp.concatenate([ot[g * 64:(g + 1) * 64, :] for g in range(NSA_GQA)], axis=1)


def _nsa(z, kc, vct, ks, vst, kw, vwt, t2, tab, ov, b, seq):
    nqb = seq // Q_BLOCK
    ncp = seq // CMP_STRIDE
    return pl.pallas_call(
        functools.partial(_nsa_kernel, ncp=ncp),
        grid=(b, nqb),
        in_specs=[pl.BlockSpec((Q_BLOCK, NSA_WIDTH), lambda i, j: (i * nqb + j, Z_Q // NSA_WIDTH)),
                  pl.BlockSpec((Q_BLOCK, 128), lambda i, j: (i * nqb + j, Z_G // 128)),
                  pl.BlockSpec((1, ncp, 128), lambda i, j: (i, 0, 0)),
                  pl.BlockSpec((1, 128, ncp), lambda i, j: (i, 0, 0)),
                  pl.BlockSpec((1, seq, 128), lambda i, j: (i, 0, 0)),
                  pl.BlockSpec((1,) + vst.shape[1:], lambda i, j: (i, 0, 0, 0)),
                  pl.BlockSpec((1, seq, 128), lambda i, j: (i, 0, 0)),
                  pl.BlockSpec((1,) + vwt.shape[1:], lambda i, j: (i, 0, 0, 0)),
                  pl.BlockSpec(t2.shape, lambda i, j: (0, 0, 0)),
                  pl.BlockSpec(tab.shape, lambda i, j: (0, 0, 0)),
                  pl.BlockSpec(ov.shape, lambda i, j: (0, 0))],
        out_specs=pl.BlockSpec((Q_BLOCK, NSA_WIDTH), lambda i, j: (i * nqb + j, 0)),
        out_shape=jax.ShapeDtypeStruct((b * seq, NSA_WIDTH), F32),
        scratch_shapes=[pltpu.VMEM((2, seq // SEL_LEN, 128), F32)],
        compiler_params=_cparams("parallel", "arbitrary"),
        name="nsa",
    )(z, z, kc, vct, ks, vst, kw, vwt, t2, tab, ov)


def _lru_kernel(x_ref, g_ref, cw_ref, cb_ref, wa_ref, ba_ref, wx_ref, bx_ref, sp_ref, o_ref,
                xprev_ref, h_ref, a_scr, b_scr, *, nb, tc):
    @pl.when(pl.program_id(0) == 0)
    def _():
        xprev_ref[...] = jnp.zeros_like(xprev_ref)
        h_ref[...] = jnp.zeros_like(h_ref)

    x = x_ref[...]
    rows = x.shape[0]
    xcat = jnp.concatenate([xprev_ref[...], x], axis=0)
    xprev_ref[...] = x[rows - (CONV_WIDTH - 1) * nb:, :]
    cw = cw_ref[...]
    xc = cb_ref[...] + cw[0:1, :] * xcat[0:rows]
    for i in range(1, CONV_WIDTH):
        xc = xc + cw[i:i + 1, :] * xcat[i * nb:i * nb + rows]
    xcb = xc.astype(BF16)
    gate_r = jax.nn.sigmoid(jnp.dot(xcb, wa_ref[...], preferred_element_type=F32) + ba_ref[...])
    gate_i = jax.nn.sigmoid(jnp.dot(xcb, wx_ref[...], preferred_element_type=F32) + bx_ref[...])
    log_a = -LRU_C * gate_r * sp_ref[...]
    a_scr[...] = jnp.exp(log_a)
    th = jnp.tanh(log_a)
    b_scr[...] = jnp.sqrt(-2.0 * th / (1.0 - th)) * gate_i * xc

    def step(t, h):
        r0 = pl.multiple_of(t * nb, nb)
        h = a_scr[pl.ds(r0, nb), :] * h + b_scr[pl.ds(r0, nb), :]
        b_scr[pl.ds(r0, nb), :] = h
        return h

    h_ref[...] = lax.fori_loop(0, tc, step, h_ref[...])
    o_ref[...] = b_scr[...] * jax.nn.gelu(g_ref[...])


def _lru(x_tm, g_tm, cw, cb, wa, ba, wx, bx, sp, nb, tc=64):
    rows, w = x_tm.shape
    r = tc * nb
    c2 = lambda i: (0, 0)
    return pl.pallas_call(
        functools.partial(_lru_kernel, nb=nb, tc=tc),
        grid=(rows // r,),
        in_specs=[pl.BlockSpec((r, w), lambda i: (i, 0)),
                  pl.BlockSpec((r, w), lambda i: (i, 0)),
                  pl.BlockSpec(cw.shape, c2), pl.BlockSpec(cb.shape, c2),
                  pl.BlockSpec(wa.shape, c2), pl.BlockSpec(ba.shape, c2),
                  pl.BlockSpec(wx.shape, c2), pl.BlockSpec(bx.shape, c2),
                  pl.BlockSpec(sp.shape, c2)],
        out_specs=pl.BlockSpec((r, w), lambda i: (i, 0)),
        out_shape=jax.ShapeDtypeStruct((rows, w), F32),
        scratch_shapes=[pltpu.VMEM(((CONV_WIDTH - 1) * nb, w), F32), pltpu.VMEM((nb, w), F32),
                        pltpu.VMEM((r, w), F32), pltpu.VMEM((r, w), F32)],
        compiler_params=_cparams("arbitrary"),
        name="lru",
    )(x_tm, g_tm, cw, cb, wa, ba, wx, bx, sp)


def _block_diag(w):
    h, i, j = w.shape
    eye = jnp.eye(h, dtype=w.dtype)
    return jnp.einsum('hij,hk->hikj', w, eye).reshape(h * i, h * j)


def _outproj_kernel(h_ref, ys_ref, yn_ref, yl_ref, wglu_ref, wos_ref, won_ref, wol_ref, o_ref):
    gl = jnp.dot(ys_ref[...].astype(BF16), wglu_ref[...], preferred_element_type=F32)
    s5 = gl[:, :SSM_WIDTH] * jax.nn.sigmoid(gl[:, SSM_WIDTH:])
    acc = jnp.dot(s5.astype(BF16), wos_ref[...], preferred_element_type=F32)
    acc = acc + jnp.dot(yn_ref[...].astype(BF16), won_ref[...], preferred_element_type=F32)
    acc = acc + jnp.dot(yl_ref[...].astype(BF16), wol_ref[...], preferred_element_type=F32)
    o_ref[...] = h_ref[...] + acc


def _outproj(h, ys, yn, yl, wglu, wos, won, wol, tm=512):
    t, d = h.shape
    c2 = lambda i: (0, 0)
    row = lambda w: pl.BlockSpec((tm, w), lambda i: (i, 0))
    return pl.pallas_call(
        _outproj_kernel,
        grid=(t // tm,),
        in_specs=[row(d), row(SSM_WIDTH), row(NSA_WIDTH), row(LRU_WIDTH),
                  pl.BlockSpec(wglu.shape, c2), pl.BlockSpec(wos.shape, c2),
                  pl.BlockSpec(won.shape, c2), pl.BlockSpec(wol.shape, c2)],
        out_specs=row(d),
        out_shape=jax.ShapeDtypeStruct((t, d), F32),
        compiler_params=_cparams("parallel"),
        name="outproj",
    )(h, ys, yn, yl, wglu, wos, won, wol)


def _ffn_kernel(h_ref, g_ref, wg_ref, wu_ref, wd_ref, gf_ref, o_ref, *, final):
    x = h_ref[...]
    y = (x * lax.rsqrt(jnp.mean(x * x, axis=-1, keepdims=True) + RMS_EPS) * g_ref[...]).astype(BF16)
    a = jnp.dot(y, wg_ref[...], preferred_element_type=F32)
    b = jnp.dot(y, wu_ref[...], preferred_element_type=F32)
    m = (jax.nn.silu(a) * b).astype(BF16)
    out = x + jnp.dot(m, wd_ref[...], preferred_element_type=F32)
    if final:
        out = out * lax.rsqrt(jnp.mean(out * out, axis=-1, keepdims=True) + RMS_EPS) * gf_ref[...]
    o_ref[...] = out


def _ffn(h, g, wg, wu, wd, gf, final, tm=256):
    t, d = h.shape
    c2 = lambda i: (0, 0)
    return pl.pallas_call(
        functools.partial(_ffn_kernel, final=final),
        grid=(t // tm,),
        in_specs=[pl.BlockSpec((tm, d), lambda i: (i, 0)),
                  pl.BlockSpec(g.shape, c2), pl.BlockSpec(wg.shape, c2), pl.BlockSpec(wu.shape, c2),
                  pl.BlockSpec(wd.shape, c2), pl.BlockSpec(gf.shape, c2)],
        out_specs=pl.BlockSpec((tm, d), lambda i: (i, 0)),
        out_shape=jax.ShapeDtypeStruct((t, d), F32),
        compiler_params=_cparams("parallel"),
        name="ffn",
    )(h, g, wg, wu, wd, gf)


def _q_perm():
    idx = np.zeros(NSA_WIDTH, np.int32)
    for g in range(NSA_GQA):
        for h in range(NSA_KV_HEADS):
            for d in range(HEAD_DIM):
                idx[g * 128 + h * 64 + d] = (h * NSA_GQA + g) * HEAD_DIM + d
    return idx


def _prep_w_in(w_in):
    o1 = SSM_WIDTH
    o2 = o1 + NSA_WIDTH
    o3 = o2 + 6 * NSA_KV_WIDTH
    o4 = o3 + 3 * NSA_Q_HEADS
    wq = w_in[:, o1:o2][:, _q_perm()]
    pad = jnp.zeros((w_in.shape[0], 128 - 3 * NSA_Q_HEADS), w_in.dtype)
    w = jnp.concatenate([wq, w_in[:, :o1], w_in[:, o2:o3], w_in[:, o3:o4], pad, w_in[:, o4:]], axis=1)
    return w.astype(BF16)


def _s5_mixer(z, b, seq, p):
    L = S5_CHUNK
    u_tm = z[:, Z_U:Z_U + SSM_WIDTH].reshape(b, seq // L, L * SSM_WIDTH).transpose(1, 0, 2)
    u_tm = u_tm.reshape((seq // L) * b, L * SSM_WIDTH)
    we, tz, cp, a16, dg = _s5_tables(p['s5_lam_re'], p['s5_lam_im'], p['s5_log_dt'], p['s5_b_re'],
                                     p['s5_b_im'], p['s5_c_re'], p['s5_c_im'], p['s5_d'])
    y_tm = _s5(u_tm, we, tz, cp, a16, dg, nb=b)
    return y_tm.reshape(seq // L, b, L * SSM_WIDTH).transpose(1, 0, 2).reshape(b * seq, SSM_WIDTH)


def _nsa_mixer(z, b, seq, bias_tabs, ov, p):
    kv = z[:, Z_KV:Z_KV + 6 * NSA_KV_WIDTH].reshape(b, seq, 6, NSA_KV_WIDTH)
    ncp = seq // CMP_STRIDE

    def cmp_layout(x):
        x = x.reshape(b, ncp, CMP_STRIDE, NSA_KV_HEADS, HEAD_DIM).transpose(0, 1, 3, 2, 4)
        return x.reshape(b, ncp, NSA_KV_HEADS * CMP_STRIDE * HEAD_DIM).astype(BF16)

    def val_t(x, kt):
        xt = x.reshape(b, seq // kt, kt, NSA_KV_HEADS, HEAD_DIM).transpose(0, 1, 3, 4, 2)
        ones = jnp.ones((b, seq // kt, NSA_KV_HEADS, 16, kt), x.dtype)
        return jnp.concatenate([xt, ones], axis=3).reshape(b, seq // kt, -1, kt).astype(BF16)

    w1k, w2k, c0k = _compress_weights(p['nsa_pe_k'], p['nsa_w1_k'], p['nsa_w2_k'])
    w1v, w2v, c0v = _compress_weights(p['nsa_pe_v'], p['nsa_w1_v'], p['nsa_w2_v'])
    kc, vct = _compress(cmp_layout(kv[:, :, 0]), cmp_layout(kv[:, :, 1]), w1k, w1v, w2k, w2v, c0k, c0v)
    tab, t2 = bias_tabs
    return _nsa(z, kc, vct, kv[:, :, 2].astype(BF16), val_t(kv[:, :, 3], 256),
                kv[:, :, 4].astype(BF16), val_t(kv[:, :, 5], 128), t2, tab, ov, b, seq)


def _lru_mixer(z, b, seq, p):
    t = b * seq

    def time_major(x):
        return x.reshape(b, seq, LRU_WIDTH).transpose(1, 0, 2).reshape(t, LRU_WIDTH)

    sp = jax.nn.softplus(-p['lru_lam'].astype(F32)).reshape(1, -1)
    yl_tm = _lru(time_major(z[:, Z_LX:Z_LX + LRU_WIDTH]), time_major(z[:, Z_LG:Z_LG + LRU_WIDTH]),
                 p['lru_conv_w'].astype(F32), p['lru_conv_b'].astype(F32).reshape(1, -1),
                 _block_diag(p['lru_w_a']).astype(BF16), p['lru_b_a'].astype(F32).reshape(1, -1),
                 _block_diag(p['lru_w_x']).astype(BF16), p['lru_b_x'].astype(F32).reshape(1, -1),
                 sp, nb=b)
    return yl_tm.reshape(seq, b, LRU_WIDTH).transpose(1, 0, 2).reshape(t, LRU_WIDTH)


def _layer(h2, b, seq, bias_tabs, ov, p, final, norm_final):
    z = _inproj(h2, p['norm_mix'].reshape(1, -1), _prep_w_in(p['w_in']))
    ys = _s5_mixer(z, b, seq, p)
    yn = _nsa_mixer(z, b, seq, bias_tabs, ov, p)
    yl = _lru_mixer(z, b, seq, p)
    w_out = p['w_out']
    wos = w_out[:SSM_WIDTH].astype(BF16)
    won = w_out[SSM_WIDTH:SSM_WIDTH + NSA_WIDTH][_q_perm()].astype(BF16)
    wol = w_out[SSM_WIDTH + NSA_WIDTH:].astype(BF16)
    h2 = _outproj(h2, ys, yn, yl, p['s5_w_glu'].astype(BF16), wos, won, wol)
    return _ffn(h2, p['norm_ffn'].reshape(1, -1), p['w_gate'].astype(BF16), p['w_up'].astype(BF16),
                p['w_down'].astype(BF16), norm_final.reshape(1, -1), final)


def _overlap_t(seq):
    ncp = seq // CMP_STRIDE
    nsel = seq // SEL_LEN
    cs = np.arange(ncp) * CMP_STRIDE
    ss = np.arange(nsel) * SEL_LEN
    ovl = (cs[None, :] < ss[:, None] + SEL_LEN) & (ss[:, None] < cs[None, :] + CMP_LEN)
    ovl[:, ncp - 1] = False
    return jnp.asarray(ovl.astype(np.float32))


def kernel(x, rel_bias_table, norm_mix, w_in, w_out, s5_lam_re, s5_lam_im, s5_log_dt, s5_b_re, s5_b_im, s5_c_re, s5_c_im, s5_d, s5_w_glu, nsa_pe_k, nsa_w1_k, nsa_w2_k, nsa_pe_v, nsa_w1_v, nsa_w2_v, lru_conv_w, lru_conv_b, lru_w_a, lru_b_a, lru_w_x, lru_b_x, lru_lam, norm_ffn, w_gate, w_up, w_down, norm_final):
    b, seq, d = x.shape
    depth = norm_mix.shape[0]
    per_layer = dict(norm_mix=norm_mix, w_in=w_in, w_out=w_out, s5_lam_re=s5_lam_re, s5_lam_im=s5_lam_im,
                     s5_log_dt=s5_log_dt, s5_b_re=s5_b_re, s5_b_im=s5_b_im, s5_c_re=s5_c_re,
                     s5_c_im=s5_c_im, s5_d=s5_d, s5_w_glu=s5_w_glu, nsa_pe_k=nsa_pe_k, nsa_w1_k=nsa_w1_k,
                     nsa_w2_k=nsa_w2_k, nsa_pe_v=nsa_pe_v, nsa_w1_v=nsa_w1_v, nsa_w2_v=nsa_w2_v,
                     lru_conv_w=lru_conv_w, lru_conv_b=lru_conv_b, lru_w_a=lru_w_a, lru_b_a=lru_b_a,
                     lru_w_x=lru_w_x, lru_b_x=lru_b_x, lru_lam=lru_lam, norm_ffn=norm_ffn, w_gate=w_gate,
                     w_up=w_up, w_down=w_down)
    bias_tabs = _bias_tables(rel_bias_table, seq)
    ov = _overlap_t(seq)
    h2 = x.reshape(b * seq, d)
    for l in range(depth):
        p = {k: v[l] for k, v in per_layer.items()}
        h2 = _layer(h2, b, seq, bias_tabs, ov, p, l == depth - 1, norm_final)
    return h2.reshape(b, seq, d)
```

```python
import functools
import math

import numpy as np
import jax
import jax.numpy as jnp
from jax import lax
from jax.experimental import pallas as pl
from jax.experimental.pallas import tpu as pltpu

F32 = jnp.float32
BF16 = jnp.bfloat16

D_MODEL = 1024
SSM_WIDTH = 256
NSA_WIDTH = 512
LRU_WIDTH = 256
S5_GROUP = 16
S5_GROUPS = 16
S5_STATE = 64
HEAD_DIM = 64
NSA_Q_HEADS = 8
NSA_KV_HEADS = 2
NSA_GQA = 4
NSA_KV_WIDTH = 128
CMP_LEN = 32
CMP_STRIDE = 16
SEL_LEN = 64
SEL_TOPK = 8
WINDOW = 256
Q_BLOCK = 64
LRU_HEADS = 4
LRU_HEAD_DIM = 64
CONV_WIDTH = 4
LRU_C = 8.0
REL_BUCKETS = 32
REL_MAX_DIST = 128
D_FF = 2816
NEG_INF = -1e9
RMS_EPS = 1e-6

NO_SLOT = -3e38

S5_CHUNK = 16
Z_WIDTH = 2176
Z_Q, Z_U, Z_KV, Z_G, Z_LX, Z_LG = 0, 512, 768, 1536, 1664, 1920

VMEM_LIMIT = 56 * 1024 * 1024


def _cparams(*sem):
    return pltpu.CompilerParams(dimension_semantics=sem, vmem_limit_bytes=VMEM_LIMIT)


def _inproj_kernel(x_ref, g_ref, w_ref, o_ref):
    x = x_ref[...]
    y = x * lax.rsqrt(jnp.mean(x * x, axis=-1, keepdims=True) + RMS_EPS) * g_ref[...]
    o_ref[...] = jnp.dot(y.astype(BF16), w_ref[...], preferred_element_type=F32)


def _inproj(x2, g, w, tm=512):
    t, d = x2.shape
    n = w.shape[1]
    return pl.pallas_call(
        _inproj_kernel,
        grid=(t // tm,),
        in_specs=[pl.BlockSpec((tm, d), lambda i: (i, 0)),
                  pl.BlockSpec((1, d), lambda i: (0, 0)),
                  pl.BlockSpec((d, n), lambda i: (0, 0))],
        out_specs=pl.BlockSpec((tm, n), lambda i: (i, 0)),
        out_shape=jax.ShapeDtypeStruct((t, n), F32),
        compiler_params=_cparams("parallel"),
        name="inproj",
    )(x2, g, w)


def _lane_regroup(x, n_outer, n_inner):
    r = x.shape[0]
    nv = n_outer * n_inner * 16 // 128
    cols = [x[:, v * 128:(v + 1) * 128] for v in range(nv)]
    lane16 = lax.broadcasted_iota(jnp.int32, (r, 128), 1) // 16
    outs = []
    for w in range(nv):
        acc = None
        for k in range(8):
            dst = w * 8 + k
            i, o = dst // n_outer, dst % n_outer
            src = o * n_inner + i
            v, sk = src // 8, src % 8
            piece = cols[v]
            shift = ((k - sk) * 16) % 128
            if shift:
                piece = pltpu.roll(piece, shift, 1)
            acc = piece if acc is None else jnp.where(lane16 == k, piece, acc)
        outs.append(acc)
    return jnp.concatenate(outs, axis=1)


def _s5_kernel(u_ref, we_ref, tz_ref, cp_ref, a_ref, d_ref, o_ref, e_ref, carry_ref, *, nb, kc):
    @pl.when(pl.program_id(0) == 0)
    def _():
        carry_ref[...] = jnp.zeros_like(carry_ref)

    ug = _lane_regroup(u_ref[...], S5_CHUNK, S5_GROUPS)
    ugb = ug.astype(BF16)
    npair = S5_GROUPS // 2
    for p in range(npair):
        e_ref[:, p * 256:(p + 1) * 256] = jnp.dot(ugb[:, p * 512:(p + 1) * 512], we_ref[p],
                                                   preferred_element_type=F32)

    a = a_ref[...]

    def step(k, carry):
        r0 = pl.multiple_of(k * nb, nb)
        e = e_ref[pl.ds(r0, nb), :]
        e_ref[pl.ds(r0, nb), :] = carry
        new = []
        for p in range(npair):
            ar = a[:, p * 256:p * 256 + 128]
            ai = a[:, p * 256 + 128:(p + 1) * 256]
            cr = carry[:, p * 256:p * 256 + 128]
            ci = carry[:, p * 256 + 128:(p + 1) * 256]
            new.append(ar * cr - ai * ci + e[:, p * 256:p * 256 + 128])
            new.append(ar * ci + ai * cr + e[:, p * 256 + 128:(p + 1) * 256])
        return jnp.concatenate(new, axis=1)

    carry_ref[...] = lax.fori_loop(0, kc, step, carry_ref[...])

    eb = e_ref[...].astype(BF16)
    ys = []
    for p in range(npair):
        yc = jnp.dot(eb[:, p * 256:(p + 1) * 256], cp_ref[p], preferred_element_type=F32)
        y0 = jnp.dot(ugb[:, (2 * p) * 256:(2 * p + 1) * 256], tz_ref[2 * p], preferred_element_type=F32)
        y1 = jnp.dot(ugb[:, (2 * p + 1) * 256:(2 * p + 2) * 256], tz_ref[2 * p + 1],
                     preferred_element_type=F32)
        ys.append(yc + jnp.concatenate([y0, y1], axis=1))
    y = jnp.concatenate(ys, axis=1) + d_ref[...] * ug
    y = jax.nn.gelu(y)
    o_ref[...] = _lane_regroup(y, S5_GROUPS, S5_CHUNK)


def _s5(u_tm, we, tz, cp, a16, dg, nb, kc=16):
    rows, width = u_tm.shape
    r = kc * nb
    const3 = lambda i: (0, 0, 0)
    return pl.pallas_call(
        functools.partial(_s5_kernel, nb=nb, kc=kc),
        grid=(rows // r,),
        in_specs=[pl.BlockSpec((r, width), lambda i: (i, 0)),
                  pl.BlockSpec(we.shape, const3),
                  pl.BlockSpec(tz.shape, const3),
                  pl.BlockSpec(cp.shape, const3),
                  pl.BlockSpec(a16.shape, lambda i: (0, 0)),
                  pl.BlockSpec(dg.shape, lambda i: (0, 0))],
        out_specs=pl.BlockSpec((r, width), lambda i: (i, 0)),
        out_shape=jax.ShapeDtypeStruct((rows, width), F32),
        scratch_shapes=[pltpu.VMEM((r, 2048), F32), pltpu.VMEM((nb, 2048), F32)],
        compiler_params=_cparams("arbitrary"),
        name="s5",
    )(u_tm, we, tz, cp, a16, dg)


def _s5_tables(lam_re, lam_im, log_dt, b_re, b_im, c_re, c_im, d_skip):
    L = S5_CHUNK
    G, P, C = S5_GROUPS, S5_STATE, S5_GROUP
    dt = jnp.exp(log_dt.astype(F32))[:, None]
    lr = lam_re.astype(F32)
    li = lam_im.astype(F32)
    mag = jnp.exp(lr * dt)
    ang = li * dt
    ab_re = mag * jnp.cos(ang)
    ab_im = mag * jnp.sin(ang)
    den = lr * lr + li * li
    f_re = ((ab_re - 1.0) * lr + ab_im * li) / den
    f_im = (ab_im * lr - (ab_re - 1.0) * li) / den
    br = b_re.astype(F32)
    bi = b_im.astype(F32)
    bb_re = f_re[..., None] * br - f_im[..., None] * bi
    bb_im = f_re[..., None] * bi + f_im[..., None] * br
    cr = c_re.astype(F32)
    ci = c_im.astype(F32)
    tau = jnp.arange(L + 1, dtype=F32)[:, None, None]
    pr = jnp.exp(lr * dt * tau) * jnp.cos(li * dt * tau)
    pi = jnp.exp(lr * dt * tau) * jnp.sin(li * dt * tau)

    prs = pr[L - 1 - jnp.arange(L)]
    pis = pi[L - 1 - jnp.arange(L)]
    we_re = jnp.einsum('sgp,gpc->gscp', prs, bb_re) - jnp.einsum('sgp,gpc->gscp', pis, bb_im)
    we_im = jnp.einsum('sgp,gpc->gscp', prs, bb_im) + jnp.einsum('sgp,gpc->gscp', pis, bb_re)
    we_re = we_re.reshape(G, L * C, P)
    we_im = we_im.reshape(G, L * C, P)
    z = jnp.zeros_like(we_re[0::2])
    top = jnp.concatenate([we_re[0::2], z, we_im[0::2], z], axis=-1)
    bot = jnp.concatenate([z, we_re[1::2], z, we_im[1::2]], axis=-1)
    we = jnp.concatenate([top, bot], axis=1)

    m_re = pr[:L, :, :, None] * bb_re[None] - pi[:L, :, :, None] * bb_im[None]
    m_im = pr[:L, :, :, None] * bb_im[None] + pi[:L, :, :, None] * bb_re[None]
    kern = jnp.einsum('gop,tgpc->tgoc', cr, m_re) - jnp.einsum('gop,tgpc->tgoc', ci, m_im)
    s_idx = np.arange(L)[:, None]
    t_idx = np.arange(L)[None, :]
    lag = np.clip(t_idx - s_idx, 0, L - 1)
    causal = jnp.asarray((t_idx >= s_idx).astype(np.float32))
    tzf = kern[lag] * causal[:, :, None, None, None]
    tz = tzf.transpose(2, 0, 4, 1, 3).reshape(G, L * C, L * C)

    pr1 = pr[1:]
    pi1 = pi[1:]
    cp_re = jnp.einsum('gop,tgp->gpto', cr, pr1) - jnp.einsum('gop,tgp->gpto', ci, pi1)
    cp_im = -(jnp.einsum('gop,tgp->gpto', cr, pi1) + jnp.einsum('gop,tgp->gpto', ci, pr1))
    cp_re = cp_re.reshape(G, P, L * C)
    cp_im = cp_im.reshape(G, P, L * C)
    zc = jnp.zeros_like(cp_re[0::2])
    cp = jnp.concatenate([
        jnp.concatenate([cp_re[0::2], zc], axis=-1),
        jnp.concatenate([zc, cp_re[1::2]], axis=-1),
        jnp.concatenate([cp_im[0::2], zc], axis=-1),
        jnp.concatenate([zc, cp_im[1::2]], axis=-1)], axis=1)

    a_re = pr[L].reshape(G // 2, 2 * P)
    a_im = pi[L].reshape(G // 2, 2 * P)
    a16 = jnp.concatenate([a_re, a_im], axis=-1).reshape(1, G * 2 * P)
    dg = jnp.broadcast_to(d_skip.astype(F32).reshape(G, 1, C), (G, L, C)).reshape(1, G * L * C)
    return we.astype(BF16), tz.astype(BF16), cp.astype(BF16), a16, dg


def _compress_kernel(ak_ref, av_ref, w1k_ref, w1v_ref, w2k_ref, w2v_ref, c0k_ref, c0v_ref,
                     kc_ref, vct_ref):
    def run(a_ref, w1_ref, w2_ref, c0_ref):
        a = a_ref[0]
        p1 = jnp.dot(a, w1_ref[0], preferred_element_type=F32)
        p2 = jnp.dot(a, w1_ref[1], preferred_element_type=F32)
        n = p1.shape[0]
        pre = p1 + pltpu.roll(p2, n - 1, 0) + c0_ref[...]
        out = jnp.dot(jax.nn.gelu(pre).astype(BF16), w2_ref[...], preferred_element_type=F32)
        row = lax.broadcasted_iota(jnp.int32, out.shape, 0)
        return jnp.where(row < n - 1, out, 0.0)

    kc_ref[0] = run(ak_ref, w1k_ref, w2k_ref, c0k_ref).astype(BF16)
    vct_ref[0] = run(av_ref, w1v_ref, w2v_ref, c0v_ref).T.astype(BF16)


def _compress(ak, av, w1k, w1v, w2k, w2v, c0k, c0v):
    b, nc, kw = ak.shape
    c3 = lambda i: (0, 0, 0)
    c2 = lambda i: (0, 0)
    return pl.pallas_call(
        _compress_kernel,
        grid=(b,),
        in_specs=[pl.BlockSpec((1, nc, kw), lambda i: (i, 0, 0)),
                  pl.BlockSpec((1, nc, kw), lambda i: (i, 0, 0)),
                  pl.BlockSpec(w1k.shape, c3), pl.BlockSpec(w1v.shape, c3),
                  pl.BlockSpec(w2k.shape, c2), pl.BlockSpec(w2v.shape, c2),
                  pl.BlockSpec(c0k.shape, c2), pl.BlockSpec(c0v.shape, c2)],
        out_specs=[pl.BlockSpec((1, nc, 128), lambda i: (i, 0, 0)),
                   pl.BlockSpec((1, 128, nc), lambda i: (i, 0, 0))],
        out_shape=[jax.ShapeDtypeStruct((b, nc, 128), BF16),
                   jax.ShapeDtypeStruct((b, 128, nc), BF16)],
        compiler_params=_cparams("parallel"),
        name="compress",
    )(ak, av, w1k, w1v, w2k, w2v, c0k, c0v)


def _compress_weights(pe, w1, w2):
    w1 = w1.astype(F32)
    half = (CMP_LEN // 2) * HEAD_DIM
    z1 = jnp.zeros((half, HEAD_DIM), F32)

    def bd(w):
        return jnp.concatenate([jnp.concatenate([w, z1], axis=1),
                                jnp.concatenate([z1, w], axis=1)], axis=0)

    w1s = jnp.stack([bd(w1[:half]), bd(w1[half:])])
    z2 = jnp.zeros((HEAD_DIM, HEAD_DIM), F32)
    w2f = w2.astype(F32)
    w2s = jnp.concatenate([jnp.concatenate([w2f, z2], axis=1),
                           jnp.concatenate([z2, w2f], axis=1)], axis=0)
    c0 = pe.astype(F32).reshape(1, CMP_LEN * HEAD_DIM) @ w1
    c0 = jnp.concatenate([c0, c0], axis=1)
    return w1s.astype(BF16), w2s.astype(BF16), c0


def _t5_bucket_np(dist):
    n = np.maximum(dist, 0)
    max_exact = REL_BUCKETS // 2
    nf = np.maximum(n, 1).astype(np.float32)
    large = max_exact + (np.log(nf / max_exact) / math.log(REL_MAX_DIST / max_exact)
                         * (REL_BUCKETS - max_exact)).astype(np.int32)
    large = np.minimum(large, REL_BUCKETS - 1)
    return np.where(n < max_exact, n, large)


TAB_FAR, TAB_WIN_EDGE, TAB_NONE = 3, 4, 5


def _bias_tables(rel_table, seq):
    tbl = rel_table.astype(F32)
    heads = NSA_Q_HEADS
    r = np.arange(Q_BLOCK)[None, :]
    c = np.arange(SEL_LEN)[:, None]
    near = tbl[_t5_bucket_np(64 * np.arange(3)[:, None, None] + (r - c)[None])]
    far = jnp.broadcast_to(tbl[REL_BUCKETS - 1], (SEL_LEN, Q_BLOCK, heads))
    none = jnp.full((SEL_LEN, Q_BLOCK, heads), NO_SLOT, F32)
    causal = jnp.asarray(r - c >= 0)[:, :, None]
    edge = jnp.asarray(c > r)[:, :, None]
    tab = jnp.stack([jnp.where(causal, near[0], none), near[1], near[2], far,
                     jnp.where(edge, far, none), none])
    tab = tab.transpose(0, 1, 3, 2).reshape(6, SEL_LEN, heads * Q_BLOCK)
    ncp = seq // CMP_STRIDE
    k = np.arange(2 * ncp + 4)[:, None]
    dist = r - CMP_STRIDE * (k - (ncp - 4)) - (CMP_LEN - 1)
    t2 = jnp.where(jnp.asarray(dist >= 0)[:, :, None], tbl[_t5_bucket_np(dist)], NO_SLOT)
    t2 = t2.transpose(0, 2, 1).reshape(2 * ncp + 4, heads * Q_BLOCK)
    t2 = jnp.stack([t2[:2 * ncp], t2[4:]])
    return tab, t2


def _nsa_kernel(q_ref, gl_ref, kc_ref, vct_ref, ks_ref, vst_ref, kw_ref, vwt_ref, t2_ref, tab_ref,
                ov_ref, o_ref, sel_ref, s_ref, p_ref, acc_ref, accw_ref, *, ncp, nw):
    qi0 = pl.program_id(1) * nw
    cw = nw * NSA_GQA * Q_BLOCK
    ncol = NSA_KV_HEADS * cw
    vrows = HEAD_DIM + 16
    lane = lax.broadcasted_iota(jnp.int32, (Q_BLOCK, 128), 1)

    q = q_ref[...] * (HEAD_DIM ** -0.5)
    pieces = []
    for h in range(NSA_KV_HEADS):
        keep = (lane >= 64) if h == 1 else (lane < 64)
        for w in range(nw):
            for g in range(NSA_GQA):
                pieces.append(jnp.where(keep, q[w * 64:(w + 1) * 64, g * 128:(g + 1) * 128], 0.0))
    qpad = jnp.concatenate(pieces, axis=0).astype(BF16)

    col = lax.broadcasted_iota(jnp.int32, (1, ncol), 1)
    rcol = col % Q_BLOCK
    qcol = qi0 + (col // (NSA_GQA * Q_BLOCK)) % nw
    tvec = Q_BLOCK * qcol + rcol
    nt_dims = (((1,), (1,)), ((), ()))

    def per_head_cols(tiles):
        return jnp.concatenate([tiles[w][:, h * 256:(h + 1) * 256]
                                for h in range(NSA_KV_HEADS) for w in range(nw)], axis=1)

    def slot_terms(m, n_masked):
        m_fin = jnp.where(n_masked > 0, jnp.maximum(m, NEG_INF), m)
        return jnp.exp(m - m_fin), n_masked * jnp.exp(NEG_INF - m_fin)

    t2_tiles = []
    for w in range(nw):
        start = (ncp - 4) - 4 * (qi0 + w)
        par = (start // 4) % 2
        a0 = pl.multiple_of(start - 4 * par, 8)
        t2_tiles.append(t2_ref[par, pl.ds(a0, ncp), :])
    s = lax.dot_general(kc_ref[0], qpad, nt_dims, preferred_element_type=F32) + per_head_cols(t2_tiles)
    m = jnp.max(s, axis=0, keepdims=True)
    e = jnp.exp(s - m)
    n_valid = jnp.clip(jnp.right_shift(tvec - (CMP_STRIDE - 1), 4), 0, ncp - 1)
    scale, extra = slot_terms(m, (ncp - 1 - n_valid).astype(F32))
    p_c = e * (scale / (jnp.sum(e, axis=0, keepdims=True) * scale + extra))
    p_cb = p_c.astype(BF16)
    o_c = [jnp.dot(vct_ref[0, h * 64:(h + 1) * 64, :], p_cb[:, h * cw:(h + 1) * cw],
                   preferred_element_type=F32) for h in range(NSA_KV_HEADS)]

    lane_c = lax.broadcasted_iota(jnp.int32, (ncp, 128), 1)
    parts = []
    for w in range(nw):
        halves = []
        for h in range(NSA_KV_HEADS):
            base = h * cw + w * 256
            ph = p_c[:, base:base + 128] + p_c[:, base + 128:base + 256]
            halves.append(ph + pltpu.roll(ph, 64, 1))
        parts.append(jnp.where(lane_c < 64, halves[0], halves[1]))
    imp = jnp.dot(ov_ref[...], jnp.concatenate(parts, axis=1), preferred_element_type=F32,
                  precision=lax.Precision.HIGHEST)
    nsel = imp.shape[0]
    jidx = lax.broadcasted_iota(jnp.int32, (nsel, 1), 0)
    qv = qi0 + lax.broadcasted_iota(jnp.int32, (1, nw * 128), 1) // 128
    forced = jnp.logical_or(jidx == 0, jnp.logical_or(jidx == qv, jidx == qv - 1))
    avail = jidx <= qv
    cand = jnp.logical_and(avail, jnp.logical_not(forced))
    budget = SEL_TOPK - (1 + jnp.where(qv >= 1, 1, 0) + jnp.where(qv >= 2, 1, 0))
    rank = jnp.zeros((nsel, nw * 128), jnp.int32)
    for jp in range(1, nsel - 2):
        row = imp[jp:jp + 1, :]
        ge = jnp.where(row >= imp, 1, 0)
        gt = jnp.where(row > imp, 1, 0)
        is_cand = jnp.where(qv - 2 >= jp, 1, 0)
        rank = rank + jnp.where(jidx > jp, ge, gt) * is_cand
    sel = jnp.logical_or(jnp.logical_and(forced, avail), jnp.logical_and(cand, rank < budget))
    seladd = jnp.where(sel, 0.0, NO_SLOT)
    lane_s = lax.broadcasted_iota(jnp.int32, (nsel, 128), 1)
    for w in range(nw):
        tile = seladd[:, w * 128:(w + 1) * 128]
        rolled = pltpu.roll(tile, 64, 1)
        sel_ref[w] = jnp.where(lane_s < 64, tile, rolled)
        sel_ref[nw + w] = jnp.where(lane_s < 64, rolled, tile)

    def block_adds(blk, selected):
        tiles = []
        for w in range(nw):
            d = qi0 + w - blk
            if selected:
                e = jnp.where(d < 0, TAB_FAR, jnp.minimum(d, TAB_FAR))
            else:
                outside = jnp.logical_or(blk < 0, jnp.logical_or(d < 0, d > TAB_NONE))
                e = jnp.where(outside, TAB_NONE, d)
            tiles.append(tab_ref[e])
        add = per_head_cols(tiles)
        if selected:
            rows = [sel_ref[h * nw + w, pl.ds(blk, 1), :] for h in range(NSA_KV_HEADS) for w in range(nw)]
            add = add + jnp.concatenate([x for r in rows for x in (r, r)], axis=1)
        return add

    def softmax_tile(m_i, s, adds):
        sm = jnp.concatenate([s[k * 64:(k + 1) * 64] + adds[k] for k in range(len(adds))], axis=0)
        m_new = jnp.maximum(m_i, jnp.max(sm, axis=0, keepdims=True))
        return m_new, jnp.exp(m_i - m_new), jnp.exp(sm - m_new).astype(BF16)

    def add_values(ref, alpha, vt_tile, p):
        for h in range(NSA_KV_HEADS):
            ref[h] = alpha[:, h * cw:(h + 1) * cw] * ref[h] + jnp.dot(
                vt_tile[h * vrows:(h + 1) * vrows, :], p[:, h * cw:(h + 1) * cw], preferred_element_type=F32)

    m_init = jnp.full((1, ncol), NO_SLOT, F32)

    accw_ref[...] = jnp.zeros_like(accw_ref)
    m_w = m_init
    for i in range(nw // 2 + 2):
        jj = qi0 // 2 - 2 + i
        jc = jnp.maximum(jj, 0)
        r0 = pl.multiple_of(jc * 128, 128)
        s = lax.dot_general(kw_ref[0, pl.ds(r0, 128), :], qpad, nt_dims, preferred_element_type=F32)
        m_w, alpha, p = softmax_tile(m_w, s, [block_adds(2 * jj + half, False) for half in range(2)])
        add_values(accw_ref, alpha, vwt_ref[0, jc], p)

    n_tiles = (qi0 + nw - 1) // 4 + 1

    def scores(u):
        r0 = pl.multiple_of(u * 256, 256)
        return lax.dot_general(ks_ref[0, pl.ds(r0, 256), :], qpad, nt_dims, preferred_element_type=F32)

    s_ref[...] = scores(0)
    p_ref[...] = jnp.zeros_like(p_ref)
    acc_ref[...] = jnp.zeros_like(acc_ref)

    def sel_body(u, carry):
        m_i, alpha_prev = carry
        add_values(acc_ref, alpha_prev, vst_ref[0, jnp.maximum(u - 1, 0)], p_ref[...])
        m_new, alpha, p = softmax_tile(m_i, s_ref[...], [block_adds(4 * u + k, True) for k in range(4)])
        p_ref[...] = p
        s_ref[...] = scores(jnp.minimum(u + 1, n_tiles - 1))
        return m_new, alpha

    m_s, alpha_last = lax.fori_loop(0, n_tiles, sel_body, (m_init, jnp.ones((1, ncol), F32)))
    add_values(acc_ref, alpha_last, vst_ref[0, n_tiles - 1], p_ref[...])

    nm_s = (SEL_LEN * jnp.maximum(SEL_TOPK - (qcol + 1), 0) + (SEL_LEN - 1 - rcol)).astype(F32)
    nm_w = (WINDOW + Q_BLOCK - jnp.minimum(tvec + 1, WINDOW)).astype(F32)

    def normalised(m, ref, n_masked):
        scale, extra = slot_terms(m, n_masked)
        out = []
        for h in range(NSA_KV_HEADS):
            sl = slice(h * cw, (h + 1) * cw)
            acc = ref[h]
            denom = acc[HEAD_DIM:HEAD_DIM + 1, :] * scale[:, sl] + extra[:, sl]
            out.append((scale[:, sl] / denom, acc[0:HEAD_DIM, :]))
        return out

    o_s = normalised(m_s, acc_ref, nm_s)
    o_w = normalised(m_w, accw_ref, nm_w)

    lane1 = lax.broadcasted_iota(jnp.int32, (1, 128), 1)
    gts = []
    for w in range(nw):
        g = jax.nn.sigmoid(gl_ref[w * 64:(w + 1) * 64, :])
        gts.append(jnp.concatenate([g, g], axis=0).T)

    def gate_vec(h, br):
        tiles = []
        for w in range(nw):
            for gg in range(2):
                c0 = (h * 4 + 2 * gg) * 3 + br
                c1 = (h * 4 + 2 * gg + 1) * 3 + br
                tiles.append(jnp.where(lane1 < 64, gts[w][c0:c0 + 1, :], gts[w][c1:c1 + 1, :]))
        return jnp.concatenate(tiles, axis=1)

    tot = []
    for h in range(NSA_KV_HEADS):
        t = gate_vec(h, 0) * o_c[h]
        t = t + (gate_vec(h, 1) * o_s[h][0]) * o_s[h][1]
        t = t + (gate_vec(h, 2) * o_w[h][0]) * o_w[h][1]
        tot.append(t)
    for w in range(nw):
        ot = jnp.concatenate([t[:, w * 256:(w + 1) * 256] for t in tot], axis=0).T
        o_ref[w * 64:(w + 1) * 64, :] = jnp.concatenate([ot[g * 64:(g + 1) * 64, :]
                                                         for g in range(NSA_GQA)], axis=1)


def _nsa(z, kc, vct, ks, vst, kw, vwt, t2, tab, ov, b, seq, nw=4):
    nst = seq // (Q_BLOCK * nw)
    ncp = seq // CMP_STRIDE
    rows = Q_BLOCK * nw
    cw = nw * NSA_GQA * Q_BLOCK
    return pl.pallas_call(
        functools.partial(_nsa_kernel, ncp=ncp, nw=nw),
        grid=(b, nst),
        in_specs=[pl.BlockSpec((rows, NSA_WIDTH), lambda i, j: (i * nst + j, Z_Q // NSA_WIDTH)),
                  pl.BlockSpec((rows, 128), lambda i, j: (i * nst + j, Z_G // 128)),
                  pl.BlockSpec((1, ncp, 128), lambda i, j: (i, 0, 0)),
                  pl.BlockSpec((1, 128, ncp), lambda i, j: (i, 0, 0)),
                  pl.BlockSpec((1, seq, 128), lambda i, j: (i, 0, 0)),
                  pl.BlockSpec((1,) + vst.shape[1:], lambda i, j: (i, 0, 0, 0)),
                  pl.BlockSpec((1, seq, 128), lambda i, j: (i, 0, 0)),
                  pl.BlockSpec((1,) + vwt.shape[1:], lambda i, j: (i, 0, 0, 0)),
                  pl.BlockSpec(t2.shape, lambda i, j: (0, 0, 0)),
                  pl.BlockSpec(tab.shape, lambda i, j: (0, 0, 0)),
                  pl.BlockSpec(ov.shape, lambda i, j: (0, 0))],
        out_specs=pl.BlockSpec((rows, NSA_WIDTH), lambda i, j: (i * nst + j, 0)),
        out_shape=jax.ShapeDtypeStruct((b * seq, NSA_WIDTH), F32),
        scratch_shapes=[pltpu.VMEM((NSA_KV_HEADS * nw, seq // SEL_LEN, 128), F32),
                        pltpu.VMEM((256, NSA_KV_HEADS * cw), F32),
                        pltpu.VMEM((256, NSA_KV_HEADS * cw), BF16),
                        pltpu.VMEM((NSA_KV_HEADS, HEAD_DIM + 16, cw), F32),
                        pltpu.VMEM((NSA_KV_HEADS, HEAD_DIM + 16, cw), F32)],
        compiler_params=_cparams("parallel", "arbitrary"),
        name="nsa",
    )(z, z, kc, vct, ks, vst, kw, vwt, t2, tab, ov)


def _lru_kernel(x_ref, g_ref, cw_ref, cb_ref, wa_ref, ba_ref, wx_ref, bx_ref, sp_ref, o_ref,
                xprev_ref, h_ref, a_scr, b_scr, *, nb, tc):
    @pl.when(pl.program_id(0) == 0)
    def _():
        xprev_ref[...] = jnp.zeros_like(xprev_ref)
        h_ref[...] = jnp.zeros_like(h_ref)

    x = x_ref[...]
    rows = x.shape[0]
    xcat = jnp.concatenate([xprev_ref[...], x], axis=0)
    xprev_ref[...] = x[rows - (CONV_WIDTH - 1) * nb:, :]
    cw = cw_ref[...]
    xc = cb_ref[...] + cw[0:1, :] * xcat[0:rows]
    for i in range(1, CONV_WIDTH):
        xc = xc + cw[i:i + 1, :] * xcat[i * nb:i * nb + rows]
    xcb = xc.astype(BF16)
    gate_r = jax.nn.sigmoid(jnp.dot(xcb, wa_ref[...], preferred_element_type=F32) + ba_ref[...])
    gate_i = jax.nn.sigmoid(jnp.dot(xcb, wx_ref[...], preferred_element_type=F32) + bx_ref[...])
    log_a = -LRU_C * gate_r * sp_ref[...]
    a_scr[...] = jnp.exp(log_a)
    th = jnp.tanh(log_a)
    b_scr[...] = jnp.sqrt(-2.0 * th / (1.0 - th)) * gate_i * xc

    def step(t, h):
        r0 = pl.multiple_of(t * nb, nb)
        h = a_scr[pl.ds(r0, nb), :] * h + b_scr[pl.ds(r0, nb), :]
        b_scr[pl.ds(r0, nb), :] = h
        return h

    h_ref[...] = lax.fori_loop(0, tc, step, h_ref[...])
    o_ref[...] = b_scr[...] * jax.nn.gelu(g_ref[...])


def _lru(x_tm, g_tm, cw, cb, wa, ba, wx, bx, sp, nb, tc=64):
    rows, w = x_tm.shape
    r = tc * nb
    c2 = lambda i: (0, 0)
    return pl.pallas_call(
        functools.partial(_lru_kernel, nb=nb, tc=tc),
        grid=(rows // r,),
        in_specs=[pl.BlockSpec((r, w), lambda i: (i, 0)),
                  pl.BlockSpec((r, w), lambda i: (i, 0)),
                  pl.BlockSpec(cw.shape, c2), pl.BlockSpec(cb.shape, c2),
                  pl.BlockSpec(wa.shape, c2), pl.BlockSpec(ba.shape, c2),
                  pl.BlockSpec(wx.shape, c2), pl.BlockSpec(bx.shape, c2),
                  pl.BlockSpec(sp.shape, c2)],
        out_specs=pl.BlockSpec((r, w), lambda i: (i, 0)),
        out_shape=jax.ShapeDtypeStruct((rows, w), F32),
        scratch_shapes=[pltpu.VMEM(((CONV_WIDTH - 1) * nb, w), F32), pltpu.VMEM((nb, w), F32),
                        pltpu.VMEM((r, w), F32), pltpu.VMEM((r, w), F32)],
        compiler_params=_cparams("arbitrary"),
        name="lru",
    )(x_tm, g_tm, cw, cb, wa, ba, wx, bx, sp)


def _block_diag(w):
    h, i, j = w.shape
    eye = jnp.eye(h, dtype=w.dtype)
    return jnp.einsum('hij,hk->hikj', w, eye).reshape(h * i, h * j)


def _outproj_kernel(h_ref, ys_ref, yn_ref, yl_ref, wglu_ref, wos_ref, won_ref, wol_ref, o_ref):
    gl = jnp.dot(ys_ref[...].astype(BF16), wglu_ref[...], preferred_element_type=F32)
    s5 = gl[:, :SSM_WIDTH] * jax.nn.sigmoid(gl[:, SSM_WIDTH:])
    acc = jnp.dot(s5.astype(BF16), wos_ref[...], preferred_element_type=F32)
    acc = acc + jnp.dot(yn_ref[...].astype(BF16), won_ref[...], preferred_element_type=F32)
    acc = acc + jnp.dot(yl_ref[...].astype(BF16), wol_ref[...], preferred_element_type=F32)
    o_ref[...] = h_ref[...] + acc


def _outproj(h, ys, yn, yl, wglu, wos, won, wol, tm=512):
    t, d = h.shape
    c2 = lambda i: (0, 0)
    row = lambda w: pl.BlockSpec((tm, w), lambda i: (i, 0))
    return pl.pallas_call(
        _outproj_kernel,
        grid=(t // tm,),
        in_specs=[row(d), row(SSM_WIDTH), row(NSA_WIDTH), row(LRU_WIDTH),
                  pl.BlockSpec(wglu.shape, c2), pl.BlockSpec(wos.shape, c2),
                  pl.BlockSpec(won.shape, c2), pl.BlockSpec(wol.shape, c2)],
        out_specs=row(d),
        out_shape=jax.ShapeDtypeStruct((t, d), F32),
        compiler_params=_cparams("parallel"),
        name="outproj",
    )(h, ys, yn, yl, wglu, wos, won, wol)


def _ffn_kernel(h_ref, g_ref, wg_ref, wu_ref, wd_ref, gf_ref, o_ref, *, final):
    x = h_ref[...]
    y = (x * lax.rsqrt(jnp.mean(x * x, axis=-1, keepdims=True) + RMS_EPS) * g_ref[...]).astype(BF16)
    a = jnp.dot(y, wg_ref[...], preferred_element_type=F32)
    b = jnp.dot(y, wu_ref[...], preferred_element_type=F32)
    m = (jax.nn.silu(a) * b).astype(BF16)
    out = x + jnp.dot(m, wd_ref[...], preferred_element_type=F32)
    if final:
        out = out * lax.rsqrt(jnp.mean(out * out, axis=-1, keepdims=True) + RMS_EPS) * gf_ref[...]
    o_ref[...] = out


def _ffn(h, g, wg, wu, wd, gf, final, tm=256):
    t, d = h.shape
    c2 = lambda i: (0, 0)
    return pl.pallas_call(
        functools.partial(_ffn_kernel, final=final),
        grid=(t // tm,),
        in_specs=[pl.BlockSpec((tm, d), lambda i: (i, 0)),
                  pl.BlockSpec(g.shape, c2), pl.BlockSpec(wg.shape, c2), pl.BlockSpec(wu.shape, c2),
                  pl.BlockSpec(wd.shape, c2), pl.BlockSpec(gf.shape, c2)],
        out_specs=pl.BlockSpec((tm, d), lambda i: (i, 0)),
        out_shape=jax.ShapeDtypeStruct((t, d), F32),
        compiler_params=_cparams("parallel"),
        name="ffn",
    )(h, g, wg, wu, wd, gf)


def _q_perm():
    idx = np.zeros(NSA_WIDTH, np.int32)
    for g in range(NSA_GQA):
        for h in range(NSA_KV_HEADS):
            for d in range(HEAD_DIM):
                idx[g * 128 + h * 64 + d] = (h * NSA_GQA + g) * HEAD_DIM + d
    return idx


def _prep_w_in(w_in):
    o1 = SSM_WIDTH
    o2 = o1 + NSA_WIDTH
    o3 = o2 + 6 * NSA_KV_WIDTH
    o4 = o3 + 3 * NSA_Q_HEADS
    wq = w_in[:, o1:o2][:, _q_perm()]
    pad = jnp.zeros((w_in.shape[0], 128 - 3 * NSA_Q_HEADS), w_in.dtype)
    w = jnp.concatenate([wq, w_in[:, :o1], w_in[:, o2:o3], w_in[:, o3:o4], pad, w_in[:, o4:]], axis=1)
    return w.astype(BF16)


def _s5_mixer(z, b, seq, p):
    L = S5_CHUNK
    u_tm = z[:, Z_U:Z_U + SSM_WIDTH].reshape(b, seq // L, L * SSM_WIDTH).transpose(1, 0, 2)
    u_tm = u_tm.reshape((seq // L) * b, L * SSM_WIDTH)
    we, tz, cp, a16, dg = _s5_tables(p['s5_lam_re'], p['s5_lam_im'], p['s5_log_dt'], p['s5_b_re'],
                                     p['s5_b_im'], p['s5_c_re'], p['s5_c_im'], p['s5_d'])
    y_tm = _s5(u_tm, we, tz, cp, a16, dg, nb=b)
    return y_tm.reshape(seq // L, b, L * SSM_WIDTH).transpose(1, 0, 2).reshape(b * seq, SSM_WIDTH)


def _nsa_mixer(z, b, seq, bias_tabs, ov, p):
    kv = z[:, Z_KV:Z_KV + 6 * NSA_KV_WIDTH].reshape(b, seq, 6, NSA_KV_WIDTH)
    ncp = seq // CMP_STRIDE

    def cmp_layout(x):
        x = x.reshape(b, ncp, CMP_STRIDE, NSA_KV_HEADS, HEAD_DIM).transpose(0, 1, 3, 2, 4)
        return x.reshape(b, ncp, NSA_KV_HEADS * CMP_STRIDE * HEAD_DIM).astype(BF16)

    def val_t(x, kt):
        xt = x.reshape(b, seq // kt, kt, NSA_KV_HEADS, HEAD_DIM).transpose(0, 1, 3, 4, 2)
        ones = jnp.ones((b, seq // kt, NSA_KV_HEADS, 16, kt), x.dtype)
        return jnp.concatenate([xt, ones], axis=3).reshape(b, seq // kt, -1, kt).astype(BF16)

    w1k, w2k, c0k = _compress_weights(p['nsa_pe_k'], p['nsa_w1_k'], p['nsa_w2_k'])
    w1v, w2v, c0v = _compress_weights(p['nsa_pe_v'], p['nsa_w1_v'], p['nsa_w2_v'])
    kc, vct = _compress(cmp_layout(kv[:, :, 0]), cmp_layout(kv[:, :, 1]), w1k, w1v, w2k, w2v, c0k, c0v)
    tab, t2 = bias_tabs
    return _nsa(z, kc, vct, kv[:, :, 2].astype(BF16), val_t(kv[:, :, 3], 256),
                kv[:, :, 4].astype(BF16), val_t(kv[:, :, 5], 128), t2, tab, ov, b, seq)


def _lru_mixer(z, b, seq, p):
    t = b * seq

    def time_major(x):
        return x.reshape(b, seq, LRU_WIDTH).transpose(1, 0, 2).reshape(t, LRU_WIDTH)

    sp = jax.nn.softplus(-p['lru_lam'].astype(F32)).reshape(1, -1)
    yl_tm = _lru(time_major(z[:, Z_LX:Z_LX + LRU_WIDTH]), time_major(z[:, Z_LG:Z_LG + LRU_WIDTH]),
                 p['lru_conv_w'].astype(F32), p['lru_conv_b'].astype(F32).reshape(1, -1),
                 _block_diag(p['lru_w_a']).astype(BF16), p['lru_b_a'].astype(F32).reshape(1, -1),
                 _block_diag(p['lru_w_x']).astype(BF16), p['lru_b_x'].astype(F32).reshape(1, -1),
                 sp, nb=b)
    return yl_tm.reshape(seq, b, LRU_WIDTH).transpose(1, 0, 2).reshape(t, LRU_WIDTH)


def _layer(h2, b, seq, bias_tabs, ov, p, final, norm_final):
    z = _inproj(h2, p['norm_mix'].reshape(1, -1), _prep_w_in(p['w_in']))
    ys = _s5_mixer(z, b, seq, p)
    yn = _nsa_mixer(z, b, seq, bias_tabs, ov, p)
    yl = _lru_mixer(z, b, seq, p)
    w_out = p['w_out']
    wos = w_out[:SSM_WIDTH].astype(BF16)
    won = w_out[SSM_WIDTH:SSM_WIDTH + NSA_WIDTH][_q_perm()].astype(BF16)
    wol = w_out[SSM_WIDTH + NSA_WIDTH:].astype(BF16)
    h2 = _outproj(h2, ys, yn, yl, p['s5_w_glu'].astype(BF16), wos, won, wol)
    return _ffn(h2, p['norm_ffn'].reshape(1, -1), p['w_gate'].astype(BF16), p['w_up'].astype(BF16),
                p['w_down'].astype(BF16), norm_final.reshape(1, -1), final)


def _overlap_t(seq):
    ncp = seq // CMP_STRIDE
    nsel = seq // SEL_LEN
    cs = np.arange(ncp) * CMP_STRIDE
    ss = np.arange(nsel) * SEL_LEN
    ovl = (cs[None, :] < ss[:, None] + SEL_LEN) & (ss[:, None] < cs[None, :] + CMP_LEN)
    ovl[:, ncp - 1] = False
    return jnp.asarray(ovl.astype(np.float32))


def kernel(x, rel_bias_table, norm_mix, w_in, w_out, s5_lam_re, s5_lam_im, s5_log_dt, s5_b_re, s5_b_im, s5_c_re, s5_c_im, s5_d, s5_w_glu, nsa_pe_k, nsa_w1_k, nsa_w2_k, nsa_pe_v, nsa_w1_v, nsa_w2_v, lru_conv_w, lru_conv_b, lru_w_a, lru_b_a, lru_w_x, lru_b_x, lru_lam, norm_ffn, w_gate, w_up, w_down, norm_final):
    b, seq, d = x.shape
    depth = norm_mix.shape[0]
    per_layer = dict(norm_mix=norm_mix, w_in=w_in, w_out=w_out, s5_lam_re=s5_lam_re, s5_lam_im=s5_lam_im,
                     s5_log_dt=s5_log_dt, s5_b_re=s5_b_re, s5_b_im=s5_b_im, s5_c_re=s5_c_re,
                     s5_c_im=s5_c_im, s5_d=s5_d, s5_w_glu=s5_w_glu, nsa_pe_k=nsa_pe_k, nsa_w1_k=nsa_w1_k,
                     nsa_w2_k=nsa_w2_k, nsa_pe_v=nsa_pe_v, nsa_w1_v=nsa_w1_v, nsa_w2_v=nsa_w2_v,
                     lru_conv_w=lru_conv_w, lru_conv_b=lru_conv_b, lru_w_a=lru_w_a, lru_b_a=lru_b_a,
                     lru_w_x=lru_w_x, lru_b_x=lru_b_x, lru_lam=lru_lam, norm_ffn=norm_ffn, w_gate=w_gate,
                     w_up=w_up, w_down=w_down)
    bias_tabs = _bias_tables(rel_bias_table, seq)
    ov = _overlap_t(seq)
    h2 = x.reshape(b * seq, d)
    for l in range(depth):
        p = {k: v[l] for k, v in per_layer.items()}
        h2 = _layer(h2, b, seq, bias_tabs, ov, p, l == depth - 1, norm_final)
    return h2.reshape(b, seq, d)
```

```python
import functools
import math

import numpy as np
import jax
import jax.numpy as jnp
from jax import lax
from jax.experimental import pallas as pl
from jax.experimental.pallas import tpu as pltpu

F32 = jnp.float32
BF16 = jnp.bfloat16

D_MODEL = 1024
SSM_WIDTH = 256
NSA_WIDTH = 512
LRU_WIDTH = 256
S5_GROUP = 16
S5_GROUPS = 16
S5_STATE = 64
HEAD_DIM = 64
NSA_Q_HEADS = 8
NSA_KV_HEADS = 2
NSA_GQA = 4
NSA_KV_WIDTH = 128
CMP_LEN = 32
CMP_STRIDE = 16
SEL_LEN = 64
SEL_TOPK = 8
WINDOW = 256
Q_BLOCK = 64
LRU_HEADS = 4
LRU_HEAD_DIM = 64
CONV_WIDTH = 4
LRU_C = 8.0
REL_BUCKETS = 32
REL_MAX_DIST = 128
D_FF = 2816
NEG_INF = -1e9
RMS_EPS = 1e-6

NO_SLOT = -3e38

S5_CHUNK = 16
Z_Q, Z_KV, Z_G, Z_NSA = 0, 512, 1280, 1408
N_SLAB = 6
OCT = 8

VMEM_LIMIT = 56 * 1024 * 1024


def _cparams(*sem):
    return pltpu.CompilerParams(dimension_semantics=sem, vmem_limit_bytes=VMEM_LIMIT)


def _inproj_kernel(x_ref, g_ref, w_ref, o_ref, *slab_refs):
    x = x_ref[...]
    y = x * lax.rsqrt(jnp.mean(x * x, axis=-1, keepdims=True) + RMS_EPS) * g_ref[...]
    res = jnp.dot(y.astype(BF16), w_ref[...], preferred_element_type=F32)
    o_ref[...] = res[:, :Z_NSA]
    for m, s_ref in enumerate(slab_refs):
        s_ref[...] = res[:, Z_NSA + m * 128:Z_NSA + (m + 1) * 128].reshape(s_ref.shape)


def _slab_spec(tm, nt):
    return pl.BlockSpec((tm // OCT, None, OCT, 128), lambda b, i: (i, b, 0, 0))


def _inproj(x2, g, w, b, seq, tm=512):
    d = x2.shape[1]
    n = w.shape[1]
    nt = seq // tm
    slab = jax.ShapeDtypeStruct((seq // OCT, b, OCT, 128), F32)
    outs = pl.pallas_call(
        _inproj_kernel,
        grid=(b, nt),
        in_specs=[pl.BlockSpec((tm, d), lambda b_, i: (b_ * nt + i, 0)),
                  pl.BlockSpec((1, d), lambda b_, i: (0, 0)),
                  pl.BlockSpec((d, n), lambda b_, i: (0, 0))],
        out_specs=[pl.BlockSpec((tm, Z_NSA), lambda b_, i: (b_ * nt + i, 0))] + [_slab_spec(tm, nt)] * N_SLAB,
        out_shape=[jax.ShapeDtypeStruct((b * seq, Z_NSA), F32)] + [slab] * N_SLAB,
        compiler_params=_cparams("parallel", "parallel"),
        name="inproj",
    )(x2, g, w)
    return outs[0], [s.reshape(b * seq, 128) for s in outs[1:]]


def _lane_regroup(x):
    r = x.shape[0]
    cols = [x[:, v * 128:(v + 1) * 128] for v in range(32)]
    piece = lax.broadcasted_iota(jnp.int32, (r, 128), 1) // 16
    out = [None] * 32
    for ah in range(2):
        for bh in range(2):
            vs = [cols[2 * (8 * ah + i) + bh] for i in range(8)]
            for stage in (4, 2, 1):
                upper = (piece & stage) != 0
                for i in range(8):
                    if i & stage:
                        continue
                    lo, hi = vs[i], vs[i + stage]
                    vs[i] = jnp.where(upper, pltpu.roll(hi, stage * 16, 1), lo)
                    vs[i + stage] = jnp.where(upper, hi, pltpu.roll(lo, 128 - stage * 16, 1))
            for j in range(8):
                out[2 * (8 * bh + j) + ah] = vs[j]
    return jnp.concatenate(out, axis=1)


def _octet_row(t, nb):
    return (t // OCT) * (nb * OCT) + t % OCT


def _s5_kernel(u0_ref, u1_ref, we_ref, tz_ref, cp_ref, a_ref, d_ref, o0_ref, o1_ref, e_ref, carry_ref,
               *, nb, kc):
    @pl.when(pl.program_id(0) == 0)
    def _():
        carry_ref[...] = jnp.zeros_like(carry_ref)

    u_refs = (u0_ref, u1_ref)
    u = jnp.concatenate(
        [jnp.concatenate([u_refs[j][pl.ds(_octet_row(k * S5_CHUNK + s, nb), nb, stride=OCT), :]
                          for s in range(S5_CHUNK) for j in range(2)], axis=1)
         for k in range(kc)], axis=0)
    ug = _lane_regroup(u)
    ugb = ug.astype(BF16)
    npair = S5_GROUPS // 2
    for p in range(npair):
        e_ref[:, p * 256:(p + 1) * 256] = jnp.dot(ugb[:, p * 512:(p + 1) * 512], we_ref[p],
                                                   preferred_element_type=F32)

    a = a_ref[...]

    def step(k, carry):
        r0 = pl.multiple_of(k * nb, nb)
        e = e_ref[pl.ds(r0, nb), :]
        e_ref[pl.ds(r0, nb), :] = carry
        new = []
        for p in range(npair):
            ar = a[:, p * 256:p * 256 + 128]
            ai = a[:, p * 256 + 128:(p + 1) * 256]
            cr = carry[:, p * 256:p * 256 + 128]
            ci = carry[:, p * 256 + 128:(p + 1) * 256]
            new.append(ar * cr - ai * ci + e[:, p * 256:p * 256 + 128])
            new.append(ar * ci + ai * cr + e[:, p * 256 + 128:(p + 1) * 256])
        return jnp.concatenate(new, axis=1)

    carry_ref[...] = lax.fori_loop(0, kc, step, carry_ref[...])

    eb = e_ref[...].astype(BF16)
    ys = []
    for p in range(npair):
        yc = jnp.dot(eb[:, p * 256:(p + 1) * 256], cp_ref[p], preferred_element_type=F32)
        y0 = jnp.dot(ugb[:, (2 * p) * 256:(2 * p + 1) * 256], tz_ref[2 * p], preferred_element_type=F32)
        y1 = jnp.dot(ugb[:, (2 * p + 1) * 256:(2 * p + 2) * 256], tz_ref[2 * p + 1],
                     preferred_element_type=F32)
        ys.append(yc + jnp.concatenate([y0, y1], axis=1))
    y = jnp.concatenate(ys, axis=1) + d_ref[...] * ug
    y = jax.nn.gelu(y)
    y = _lane_regroup(y)
    o_refs = (o0_ref, o1_ref)
    for k in range(kc):
        for s in range(S5_CHUNK):
            for j in range(2):
                lo = s * SSM_WIDTH + j * 128
                o_refs[j][pl.ds(_octet_row(k * S5_CHUNK + s, nb), nb, stride=OCT), :] = \
                    y[k * nb:(k + 1) * nb, lo:lo + 128]


def _s5(u0, u1, we, tz, cp, a16, dg, nb, kc=16):
    rows = u0.shape[0]
    r = kc * S5_CHUNK * nb
    const3 = lambda i: (0, 0, 0)
    slab = pl.BlockSpec((r, 128), lambda i: (i, 0))
    return pl.pallas_call(
        functools.partial(_s5_kernel, nb=nb, kc=kc),
        grid=(rows // r,),
        in_specs=[slab, slab,
                  pl.BlockSpec(we.shape, const3),
                  pl.BlockSpec(tz.shape, const3),
                  pl.BlockSpec(cp.shape, const3),
                  pl.BlockSpec(a16.shape, lambda i: (0, 0)),
                  pl.BlockSpec(dg.shape, lambda i: (0, 0))],
        out_specs=[slab, slab],
        out_shape=[jax.ShapeDtypeStruct((rows, 128), F32)] * 2,
        scratch_shapes=[pltpu.VMEM((kc * nb, 2048), F32), pltpu.VMEM((nb, 2048), F32)],
        compiler_params=_cparams("arbitrary"),
        name="s5",
    )(u0, u1, we, tz, cp, a16, dg)


def _s5_tables(lam_re, lam_im, log_dt, b_re, b_im, c_re, c_im, d_skip):
    L = S5_CHUNK
    G, P, C = S5_GROUPS, S5_STATE, S5_GROUP
    dt = jnp.exp(log_dt.astype(F32))[:, None]
    lr = lam_re.astype(F32)
    li = lam_im.astype(F32)
    mag = jnp.exp(lr * dt)
    ang = li * dt
    ab_re = mag * jnp.cos(ang)
    ab_im = mag * jnp.sin(ang)
    den = lr * lr + li * li
    f_re = ((ab_re - 1.0) * lr + ab_im * li) / den
    f_im = (ab_im * lr - (ab_re - 1.0) * li) / den
    br = b_re.astype(F32)
    bi = b_im.astype(F32)
    bb_re = f_re[..., None] * br - f_im[..., None] * bi
    bb_im = f_re[..., None] * bi + f_im[..., None] * br
    cr = c_re.astype(F32)
    ci = c_im.astype(F32)
    tau = jnp.arange(L + 1, dtype=F32)[:, None, None]
    pr = jnp.exp(lr * dt * tau) * jnp.cos(li * dt * tau)
    pi = jnp.exp(lr * dt * tau) * jnp.sin(li * dt * tau)

    prs = pr[L - 1 - jnp.arange(L)]
    pis = pi[L - 1 - jnp.arange(L)]
    we_re = jnp.einsum('sgp,gpc->gscp', prs, bb_re) - jnp.einsum('sgp,gpc->gscp', pis, bb_im)
    we_im = jnp.einsum('sgp,gpc->gscp', prs, bb_im) + jnp.einsum('sgp,gpc->gscp', pis, bb_re)
    we_re = we_re.reshape(G, L * C, P)
    we_im = we_im.reshape(G, L * C, P)
    z = jnp.zeros_like(we_re[0::2])
    top = jnp.concatenate([we_re[0::2], z, we_im[0::2], z], axis=-1)
    bot = jnp.concatenate([z, we_re[1::2], z, we_im[1::2]], axis=-1)
    we = jnp.concatenate([top, bot], axis=1)

    m_re = pr[:L, :, :, None] * bb_re[None] - pi[:L, :, :, None] * bb_im[None]
    m_im = pr[:L, :, :, None] * bb_im[None] + pi[:L, :, :, None] * bb_re[None]
    kern = jnp.einsum('gop,tgpc->tgoc', cr, m_re) - jnp.einsum('gop,tgpc->tgoc', ci, m_im)
    s_idx = np.arange(L)[:, None]
    t_idx = np.arange(L)[None, :]
    lag = np.clip(t_idx - s_idx, 0, L - 1)
    causal = jnp.asarray((t_idx >= s_idx).astype(np.float32))
    tzf = kern[lag] * causal[:, :, None, None, None]
    tz = tzf.transpose(2, 0, 4, 1, 3).reshape(G, L * C, L * C)

    pr1 = pr[1:]
    pi1 = pi[1:]
    cp_re = jnp.einsum('gop,tgp->gpto', cr, pr1) - jnp.einsum('gop,tgp->gpto', ci, pi1)
    cp_im = -(jnp.einsum('gop,tgp->gpto', cr, pi1) + jnp.einsum('gop,tgp->gpto', ci, pr1))
    cp_re = cp_re.reshape(G, P, L * C)
    cp_im = cp_im.reshape(G, P, L * C)
    zc = jnp.zeros_like(cp_re[0::2])
    cp = jnp.concatenate([
        jnp.concatenate([cp_re[0::2], zc], axis=-1),
        jnp.concatenate([zc, cp_re[1::2]], axis=-1),
        jnp.concatenate([cp_im[0::2], zc], axis=-1),
        jnp.concatenate([zc, cp_im[1::2]], axis=-1)], axis=1)

    a_re = pr[L].reshape(G // 2, 2 * P)
    a_im = pi[L].reshape(G // 2, 2 * P)
    a16 = jnp.concatenate([a_re, a_im], axis=-1).reshape(1, G * 2 * P)
    dg = jnp.broadcast_to(d_skip.astype(F32).reshape(G, 1, C), (G, L, C)).reshape(1, G * L * C)
    return we.astype(BF16), tz.astype(BF16), cp.astype(BF16), a16, dg


def _compress_kernel(ak_ref, av_ref, w1k_ref, w1v_ref, w2k_ref, w2v_ref, c0k_ref, c0v_ref,
                     kc_ref, vct_ref):
    def run(a_ref, w1_ref, w2_ref, c0_ref):
        a = a_ref[0]
        p1 = jnp.dot(a, w1_ref[0], preferred_element_type=F32)
        p2 = jnp.dot(a, w1_ref[1], preferred_element_type=F32)
        n = p1.shape[0]
        pre = p1 + pltpu.roll(p2, n - 1, 0) + c0_ref[...]
        out = jnp.dot(jax.nn.gelu(pre).astype(BF16), w2_ref[...], preferred_element_type=F32)
        row = lax.broadcasted_iota(jnp.int32, out.shape, 0)
        return jnp.where(row < n - 1, out, 0.0)

    kc_ref[0] = run(ak_ref, w1k_ref, w2k_ref, c0k_ref).astype(BF16)
    vct_ref[0] = run(av_ref, w1v_ref, w2v_ref, c0v_ref).T.astype(BF16)


def _compress(ak, av, w1k, w1v, w2k, w2v, c0k, c0v):
    b, nc, kw = ak.shape
    c3 = lambda i: (0, 0, 0)
    c2 = lambda i: (0, 0)
    return pl.pallas_call(
        _compress_kernel,
        grid=(b,),
        in_specs=[pl.BlockSpec((1, nc, kw), lambda i: (i, 0, 0)),
                  pl.BlockSpec((1, nc, kw), lambda i: (i, 0, 0)),
                  pl.BlockSpec(w1k.shape, c3), pl.BlockSpec(w1v.shape, c3),
                  pl.BlockSpec(w2k.shape, c2), pl.BlockSpec(w2v.shape, c2),
                  pl.BlockSpec(c0k.shape, c2), pl.BlockSpec(c0v.shape, c2)],
        out_specs=[pl.BlockSpec((1, nc, 128), lambda i: (i, 0, 0)),
                   pl.BlockSpec((1, 128, nc), lambda i: (i, 0, 0))],
        out_shape=[jax.ShapeDtypeStruct((b, nc, 128), BF16),
                   jax.ShapeDtypeStruct((b, 128, nc), BF16)],
        compiler_params=_cparams("parallel"),
        name="compress",
    )(ak, av, w1k, w1v, w2k, w2v, c0k, c0v)


def _compress_weights(pe, w1, w2):
    w1 = w1.astype(F32)
    half = (CMP_LEN // 2) * HEAD_DIM
    z1 = jnp.zeros((half, HEAD_DIM), F32)

    def bd(w):
        return jnp.concatenate([jnp.concatenate([w, z1], axis=1),
                                jnp.concatenate([z1, w], axis=1)], axis=0)

    w1s = jnp.stack([bd(w1[:half]), bd(w1[half:])])
    z2 = jnp.zeros((HEAD_DIM, HEAD_DIM), F32)
    w2f = w2.astype(F32)
    w2s = jnp.concatenate([jnp.concatenate([w2f, z2], axis=1),
                           jnp.concatenate([z2, w2f], axis=1)], axis=0)
    c0 = pe.astype(F32).reshape(1, CMP_LEN * HEAD_DIM) @ w1
    c0 = jnp.concatenate([c0, c0], axis=1)
    return w1s.astype(BF16), w2s.astype(BF16), c0


def _t5_bucket_np(dist):
    n = np.maximum(dist, 0)
    max_exact = REL_BUCKETS // 2
    nf = np.maximum(n, 1).astype(np.float32)
    large = max_exact + (np.log(nf / max_exact) / math.log(REL_MAX_DIST / max_exact)
                         * (REL_BUCKETS - max_exact)).astype(np.int32)
    large = np.minimum(large, REL_BUCKETS - 1)
    return np.where(n < max_exact, n, large)


TAB_FAR, TAB_WIN_EDGE, TAB_NONE = 3, 4, 5


def _bias_tables(rel_table, seq):
    tbl = rel_table.astype(F32)
    heads = NSA_Q_HEADS
    r = np.arange(Q_BLOCK)[None, :]
    c = np.arange(SEL_LEN)[:, None]
    near = tbl[_t5_bucket_np(64 * np.arange(3)[:, None, None] + (r - c)[None])]
    far = jnp.broadcast_to(tbl[REL_BUCKETS - 1], (SEL_LEN, Q_BLOCK, heads))
    none = jnp.full((SEL_LEN, Q_BLOCK, heads), NO_SLOT, F32)
    causal = jnp.asarray(r - c >= 0)[:, :, None]
    edge = jnp.asarray(c > r)[:, :, None]
    tab = jnp.stack([jnp.where(causal, near[0], none), near[1], near[2], far,
                     jnp.where(edge, far, none), none])
    tab = tab.transpose(0, 1, 3, 2).reshape(6, SEL_LEN, heads * Q_BLOCK)
    ncp = seq // CMP_STRIDE
    k = np.arange(2 * ncp + 4)[:, None]
    dist = r - CMP_STRIDE * (k - (ncp - 4)) - (CMP_LEN - 1)
    t2 = jnp.where(jnp.asarray(dist >= 0)[:, :, None], tbl[_t5_bucket_np(dist)], NO_SLOT)
    t2 = t2.transpose(0, 2, 1).reshape(2 * ncp + 4, heads * Q_BLOCK)
    t2 = jnp.stack([t2[:2 * ncp], t2[4:]])
    return tab, t2


def _nsa_kernel(q_ref, gl_ref, kc_ref, vct_ref, ks_ref, vst_ref, kw_ref, vwt_ref, t2_ref, tab_ref,
                ov_ref, o_ref, sel_ref, s_ref, p_ref, acc_ref, accw_ref, *, ncp, nw):
    qi0 = pl.program_id(1) * nw
    cw = nw * NSA_GQA * Q_BLOCK
    ncol = NSA_KV_HEADS * cw
    vrows = HEAD_DIM + 16
    lane = lax.broadcasted_iota(jnp.int32, (Q_BLOCK, 128), 1)

    q = q_ref[...] * (HEAD_DIM ** -0.5)
    pieces = []
    for h in range(NSA_KV_HEADS):
        keep = (lane >= 64) if h == 1 else (lane < 64)
        for w in range(nw):
            for g in range(NSA_GQA):
                pieces.append(jnp.where(keep, q[w * 64:(w + 1) * 64, g * 128:(g + 1) * 128], 0.0))
    qpad = jnp.concatenate(pieces, axis=0).astype(BF16)

    col = lax.broadcasted_iota(jnp.int32, (1, ncol), 1)
    rcol = col % Q_BLOCK
    qcol = qi0 + (col // (NSA_GQA * Q_BLOCK)) % nw
    tvec = Q_BLOCK * qcol + rcol
    nt_dims = (((1,), (1,)), ((), ()))

    def per_head_cols(tiles):
        return jnp.concatenate([tiles[w][:, h * 256:(h + 1) * 256]
                                for h in range(NSA_KV_HEADS) for w in range(nw)], axis=1)

    def slot_terms(m, n_masked):
        m_fin = jnp.where(n_masked > 0, jnp.maximum(m, NEG_INF), m)
        return jnp.exp(m - m_fin), n_masked * jnp.exp(NEG_INF - m_fin)

    t2_tiles = []
    for w in range(nw):
        start = (ncp - 4) - 4 * (qi0 + w)
        par = (start // 4) % 2
        a0 = pl.multiple_of(start - 4 * par, 8)
        t2_tiles.append(t2_ref[par, pl.ds(a0, ncp), :])
    s = lax.dot_general(kc_ref[0], qpad, nt_dims, preferred_element_type=F32) + per_head_cols(t2_tiles)
    m = jnp.max(s, axis=0, keepdims=True)
    e = jnp.exp(s - m)
    n_valid = jnp.clip(jnp.right_shift(tvec - (CMP_STRIDE - 1), 4), 0, ncp - 1)
    scale, extra = slot_terms(m, (ncp - 1 - n_valid).astype(F32))
    p_c = e * (scale / (jnp.sum(e, axis=0, keepdims=True) * scale + extra))
    p_cb = p_c.astype(BF16)
    o_c = [jnp.dot(vct_ref[0, h * 64:(h + 1) * 64, :], p_cb[:, h * cw:(h + 1) * cw],
                   preferred_element_type=F32) for h in range(NSA_KV_HEADS)]

    lane_c = lax.broadcasted_iota(jnp.int32, (ncp, 128), 1)
    parts = []
    for w in range(nw):
        halves = []
        for h in range(NSA_KV_HEADS):
            base = h * cw + w * 256
            ph = p_c[:, base:base + 128] + p_c[:, base + 128:base + 256]
            halves.append(ph + pltpu.roll(ph, 64, 1))
        parts.append(jnp.where(lane_c < 64, halves[0], halves[1]))
    imp = jnp.dot(ov_ref[...], jnp.concatenate(parts, axis=1), preferred_element_type=F32,
                  precision=lax.Precision.HIGHEST)
    nsel = imp.shape[0]
    jidx = lax.broadcasted_iota(jnp.int32, (nsel, 1), 0)
    qv = qi0 + lax.broadcasted_iota(jnp.int32, (1, nw * 128), 1) // 128
    forced = jnp.logical_or(jidx == 0, jnp.logical_or(jidx == qv, jidx == qv - 1))
    avail = jidx <= qv
    cand = jnp.logical_and(avail, jnp.logical_not(forced))
    budget = SEL_TOPK - (1 + jnp.where(qv >= 1, 1, 0) + jnp.where(qv >= 2, 1, 0))
    rank = jnp.zeros((nsel, nw * 128), jnp.int32)
    for jp in range(1, nsel - 2):
        row = imp[jp:jp + 1, :]
        ge = jnp.where(row >= imp, 1, 0)
        gt = jnp.where(row > imp, 1, 0)
        is_cand = jnp.where(qv - 2 >= jp, 1, 0)
        rank = rank + jnp.where(jidx > jp, ge, gt) * is_cand
    sel = jnp.logical_or(jnp.logical_and(forced, avail), jnp.logical_and(cand, rank < budget))
    seladd = jnp.where(sel, 0.0, NO_SLOT)
    lane_s = lax.broadcasted_iota(jnp.int32, (nsel, 128), 1)
    for w in range(nw):
        tile = seladd[:, w * 128:(w + 1) * 128]
        rolled = pltpu.roll(tile, 64, 1)
        sel_ref[w] = jnp.where(lane_s < 64, tile, rolled)
        sel_ref[nw + w] = jnp.where(lane_s < 64, rolled, tile)

    def block_adds(blk, selected):
        tiles = []
        for w in range(nw):
            d = qi0 + w - blk
            if selected:
                e = jnp.where(d < 0, TAB_FAR, jnp.minimum(d, TAB_FAR))
            else:
                outside = jnp.logical_or(blk < 0, jnp.logical_or(d < 0, d > TAB_NONE))
                e = jnp.where(outside, TAB_NONE, d)
            tiles.append(tab_ref[e])
        add = per_head_cols(tiles)
        if selected:
            rows = [sel_ref[h * nw + w, pl.ds(blk, 1), :] for h in range(NSA_KV_HEADS) for w in range(nw)]
            add = add + jnp.concatenate([x for r in rows for x in (r, r)], axis=1)
        return add

    def softmax_tile(m_i, s, adds):
        sm = jnp.concatenate([s[k * 64:(k + 1) * 64] + adds[k] for k in range(len(adds))], axis=0)
        m_new = jnp.maximum(m_i, jnp.max(sm, axis=0, keepdims=True))
        return m_new, jnp.exp(m_i - m_new), jnp.exp(sm - m_new).astype(BF16)

    def add_values(ref, alpha, vt_tile, p):
        for h in range(NSA_KV_HEADS):
            ref[h] = alpha[:, h * cw:(h + 1) * cw] * ref[h] + jnp.dot(
                vt_tile[h * vrows:(h + 1) * vrows, :], p[:, h * cw:(h + 1) * cw], preferred_element_type=F32)

    m_init = jnp.full((1, ncol), NO_SLOT, F32)

    accw_ref[...] = jnp.zeros_like(accw_ref)
    m_w = m_init
    for i in range(nw // 2 + 2):
        jj = qi0 // 2 - 2 + i
        jc = jnp.maximum(jj, 0)
        r0 = pl.multiple_of(jc * 128, 128)
        s = lax.dot_general(kw_ref[0, pl.ds(r0, 128), :], qpad, nt_dims, preferred_element_type=F32)
        m_w, alpha, p = softmax_tile(m_w, s, [block_adds(2 * jj + half, False) for half in range(2)])
        add_values(accw_ref, alpha, vwt_ref[0, jc], p)

    n_tiles = (qi0 + nw - 1) // 4 + 1

    def scores(u):
        r0 = pl.multiple_of(u * 256, 256)
        return lax.dot_general(ks_ref[0, pl.ds(r0, 256), :], qpad, nt_dims, preferred_element_type=F32)

    s_ref[...] = scores(0)
    p_ref[...] = jnp.zeros_like(p_ref)
    acc_ref[...] = jnp.zeros_like(acc_ref)

    def sel_body(u, carry):
        m_i, alpha_prev = carry
        add_values(acc_ref, alpha_prev, vst_ref[0, jnp.maximum(u - 1, 0)], p_ref[...])
        m_new, alpha, p = softmax_tile(m_i, s_ref[...], [block_adds(4 * u + k, True) for k in range(4)])
        p_ref[...] = p
        s_ref[...] = scores(jnp.minimum(u + 1, n_tiles - 1))
        return m_new, alpha

    m_s, alpha_last = lax.fori_loop(0, n_tiles, sel_body, (m_init, jnp.ones((1, ncol), F32)))
    add_values(acc_ref, alpha_last, vst_ref[0, n_tiles - 1], p_ref[...])

    nm_s = (SEL_LEN * jnp.maximum(SEL_TOPK - (qcol + 1), 0) + (SEL_LEN - 1 - rcol)).astype(F32)
    nm_w = (WINDOW + Q_BLOCK - jnp.minimum(tvec + 1, WINDOW)).astype(F32)

    def normalised(m, ref, n_masked):
        scale, extra = slot_terms(m, n_masked)
        out = []
        for h in range(NSA_KV_HEADS):
            sl = slice(h * cw, (h + 1) * cw)
            acc = ref[h]
            denom = acc[HEAD_DIM:HEAD_DIM + 1, :] * scale[:, sl] + extra[:, sl]
            out.append((scale[:, sl] / denom, acc[0:HEAD_DIM, :]))
        return out

    o_s = normalised(m_s, acc_ref, nm_s)
    o_w = normalised(m_w, accw_ref, nm_w)

    lane1 = lax.broadcasted_iota(jnp.int32, (1, 128), 1)
    gts = []
    for w in range(nw):
        g = jax.nn.sigmoid(gl_ref[w * 64:(w + 1) * 64, :])
        gts.append(jnp.concatenate([g, g], axis=0).T)

    def gate_vec(h, br):
        tiles = []
        for w in range(nw):
            for gg in range(2):
                c0 = (h * 4 + 2 * gg) * 3 + br
                c1 = (h * 4 + 2 * gg + 1) * 3 + br
                tiles.append(jnp.where(lane1 < 64, gts[w][c0:c0 + 1, :], gts[w][c1:c1 + 1, :]))
        return jnp.concatenate(tiles, axis=1)

    tot = []
    for h in range(NSA_KV_HEADS):
        t = gate_vec(h, 0) * o_c[h]
        t = t + (gate_vec(h, 1) * o_s[h][0]) * o_s[h][1]
        t = t + (gate_vec(h, 2) * o_w[h][0]) * o_w[h][1]
        tot.append(t)
    for w in range(nw):
        ot = jnp.concatenate([t[:, w * 256:(w + 1) * 256] for t in tot], axis=0).T
        o_ref[w * 64:(w + 1) * 64, :] = jnp.concatenate([ot[g * 64:(g + 1) * 64, :]
                                                         for g in range(NSA_GQA)], axis=1)


def _nsa(z, kc, vct, ks, vst, kw, vwt, t2, tab, ov, b, seq, nw=4):
    nst = seq // (Q_BLOCK * nw)
    ncp = seq // CMP_STRIDE
    rows = Q_BLOCK * nw
    cw = nw * NSA_GQA * Q_BLOCK
    return pl.pallas_call(
        functools.partial(_nsa_kernel, ncp=ncp, nw=nw),
        grid=(b, nst),
        in_specs=[pl.BlockSpec((rows, NSA_WIDTH), lambda i, j: (i * nst + j, Z_Q // NSA_WIDTH)),
                  pl.BlockSpec((rows, 128), lambda i, j: (i * nst + j, Z_G // 128)),
                  pl.BlockSpec((1, ncp, 128), lambda i, j: (i, 0, 0)),
                  pl.BlockSpec((1, 128, ncp), lambda i, j: (i, 0, 0)),
                  pl.BlockSpec((1, seq, 128), lambda i, j: (i, 0, 0)),
                  pl.BlockSpec((1,) + vst.shape[1:], lambda i, j: (i, 0, 0, 0)),
                  pl.BlockSpec((1, seq, 128), lambda i, j: (i, 0, 0)),
                  pl.BlockSpec((1,) + vwt.shape[1:], lambda i, j: (i, 0, 0, 0)),
                  pl.BlockSpec(t2.shape, lambda i, j: (0, 0, 0)),
                  pl.BlockSpec(tab.shape, lambda i, j: (0, 0, 0)),
                  pl.BlockSpec(ov.shape, lambda i, j: (0, 0))],
        out_specs=pl.BlockSpec((rows, NSA_WIDTH), lambda i, j: (i * nst + j, 0)),
        out_shape=jax.ShapeDtypeStruct((b * seq, NSA_WIDTH), F32),
        scratch_shapes=[pltpu.VMEM((NSA_KV_HEADS * nw, seq // SEL_LEN, 128), F32),
                        pltpu.VMEM((256, NSA_KV_HEADS * cw), F32),
                        pltpu.VMEM((256, NSA_KV_HEADS * cw), BF16),
                        pltpu.VMEM((NSA_KV_HEADS, HEAD_DIM + 16, cw), F32),
                        pltpu.VMEM((NSA_KV_HEADS, HEAD_DIM + 16, cw), F32)],
        compiler_params=_cparams("parallel", "arbitrary"),
        name="nsa",
    )(z, z, kc, vct, ks, vst, kw, vwt, t2, tab, ov)


def _lru_kernel(x0_ref, x1_ref, g0_ref, g1_ref, cw_ref, cb_ref, wa_ref, ba_ref, wx_ref, bx_ref, sp_ref,
                o0_ref, o1_ref, xprev_ref, h_ref, a_scr, b_scr, *, nb, tc):
    @pl.when(pl.program_id(0) == 0)
    def _():
        xprev_ref[...] = jnp.zeros_like(xprev_ref)
        h_ref[...] = jnp.zeros_like(h_ref)

    x = jnp.concatenate([x0_ref[...], x1_ref[...]], axis=1)
    rows, width = x.shape
    grp = nb * OCT
    xm = jnp.concatenate([xprev_ref[...], x[:rows - grp]], axis=0)
    xprev_ref[...] = x[rows - grp:]
    x3 = x.reshape(rows // OCT, OCT, width)
    xm3 = xm.reshape(rows // OCT, OCT, width)
    tlo = lax.broadcasted_iota(jnp.int32, (1, OCT, 1), 1)
    cw = cw_ref[...]
    xc = cb_ref[...].reshape(1, 1, width) + cw[CONV_WIDTH - 1:CONV_WIDTH, :].reshape(1, 1, width) * x3
    for k in range(1, CONV_WIDTH):
        delayed = jnp.where(tlo >= k, pltpu.roll(x3, k, 1), pltpu.roll(xm3, k, 1))
        xc = xc + cw[CONV_WIDTH - 1 - k:CONV_WIDTH - k, :].reshape(1, 1, width) * delayed
    xc = xc.reshape(rows, width)
    xcb = xc.astype(BF16)
    gate_r = jax.nn.sigmoid(jnp.dot(xcb, wa_ref[...], preferred_element_type=F32) + ba_ref[...])
    gate_i = jax.nn.sigmoid(jnp.dot(xcb, wx_ref[...], preferred_element_type=F32) + bx_ref[...])
    log_a = -LRU_C * gate_r * sp_ref[...]
    a = jnp.exp(log_a)
    th = jnp.tanh(log_a)
    bt = jnp.sqrt(-2.0 * th / (1.0 - th)) * gate_i * xc
    for j in range(2):
        a_scr[j] = a[:, j * 128:(j + 1) * 128]
        b_scr[j] = bt[:, j * 128:(j + 1) * 128]

    def step(t, h):
        r0 = _octet_row(t, nb)
        new = []
        for j in range(2):
            hj = a_scr[j, pl.ds(r0, nb, stride=OCT), :] * h[j] + b_scr[j, pl.ds(r0, nb, stride=OCT), :]
            b_scr[j, pl.ds(r0, nb, stride=OCT), :] = hj
            new.append(hj)
        return tuple(new)

    h = lax.fori_loop(0, tc, step, (h_ref[0], h_ref[1]))
    for j, (g_ref, o_ref) in enumerate(((g0_ref, o0_ref), (g1_ref, o1_ref))):
        h_ref[j] = h[j]
        o_ref[...] = b_scr[j] * jax.nn.gelu(g_ref[...])


def _lru(x0, x1, g0, g1, cw, cb, wa, ba, wx, bx, sp, nb, tc=64):
    rows = x0.shape[0]
    r = tc * nb
    c2 = lambda i: (0, 0)
    slab = pl.BlockSpec((r, 128), lambda i: (i, 0))
    return pl.pallas_call(
        functools.partial(_lru_kernel, nb=nb, tc=tc),
        grid=(rows // r,),
        in_specs=[slab, slab, slab, slab,
                  pl.BlockSpec(cw.shape, c2), pl.BlockSpec(cb.shape, c2),
                  pl.BlockSpec(wa.shape, c2), pl.BlockSpec(ba.shape, c2),
                  pl.BlockSpec(wx.shape, c2), pl.BlockSpec(bx.shape, c2),
                  pl.BlockSpec(sp.shape, c2)],
        out_specs=[slab, slab],
        out_shape=[jax.ShapeDtypeStruct((rows, 128), F32)] * 2,
        scratch_shapes=[pltpu.VMEM((nb * OCT, LRU_WIDTH), F32), pltpu.VMEM((2, nb, 128), F32),
                        pltpu.VMEM((2, r, 128), F32), pltpu.VMEM((2, r, 128), F32)],
        compiler_params=_cparams("arbitrary"),
        name="lru",
    )(x0, x1, g0, g1, cw, cb, wa, ba, wx, bx, sp)


def _block_diag(w):
    h, i, j = w.shape
    eye = jnp.eye(h, dtype=w.dtype)
    return jnp.einsum('hij,hk->hikj', w, eye).reshape(h * i, h * j)


def _outproj_kernel(h_ref, ys0_ref, ys1_ref, yn_ref, yl0_ref, yl1_ref, wglu_ref, wos_ref, won_ref, wol_ref,
                    o_ref):
    tm = h_ref.shape[0]

    def rows(a_ref, b_ref):
        return jnp.concatenate([a_ref[...].reshape(tm, 128), b_ref[...].reshape(tm, 128)], axis=1)

    gl = jnp.dot(rows(ys0_ref, ys1_ref).astype(BF16), wglu_ref[...], preferred_element_type=F32)
    s5 = gl[:, :SSM_WIDTH] * jax.nn.sigmoid(gl[:, SSM_WIDTH:])
    acc = jnp.dot(s5.astype(BF16), wos_ref[...], preferred_element_type=F32)
    acc = acc + jnp.dot(yn_ref[...].astype(BF16), won_ref[...], preferred_element_type=F32)
    acc = acc + jnp.dot(rows(yl0_ref, yl1_ref).astype(BF16), wol_ref[...], preferred_element_type=F32)
    o_ref[...] = h_ref[...] + acc


def _outproj(h, ys, yn, yl, wglu, wos, won, wol, b, seq, tm=512):
    d = h.shape[1]
    nt = seq // tm
    c2 = lambda b_, i: (0, 0)
    row = lambda w: pl.BlockSpec((tm, w), lambda b_, i: (b_ * nt + i, 0))
    slab4 = lambda s: s.reshape(seq // OCT, b, OCT, 128)
    return pl.pallas_call(
        _outproj_kernel,
        grid=(b, nt),
        in_specs=[row(d), _slab_spec(tm, nt), _slab_spec(tm, nt), row(NSA_WIDTH),
                  _slab_spec(tm, nt), _slab_spec(tm, nt),
                  pl.BlockSpec(wglu.shape, c2), pl.BlockSpec(wos.shape, c2),
                  pl.BlockSpec(won.shape, c2), pl.BlockSpec(wol.shape, c2)],
        out_specs=row(d),
        out_shape=jax.ShapeDtypeStruct((b * seq, d), F32),
        compiler_params=_cparams("parallel", "parallel"),
        name="outproj",
    )(h, slab4(ys[0]), slab4(ys[1]), yn, slab4(yl[0]), slab4(yl[1]), wglu, wos, won, wol)


def _ffn_kernel(h_ref, g_ref, wg_ref, wu_ref, wd_ref, gf_ref, o_ref, *, final):
    x = h_ref[...]
    y = (x * lax.rsqrt(jnp.mean(x * x, axis=-1, keepdims=True) + RMS_EPS) * g_ref[...]).astype(BF16)
    a = jnp.dot(y, wg_ref[...], preferred_element_type=F32)
    b = jnp.dot(y, wu_ref[...], preferred_element_type=F32)
    m = (jax.nn.silu(a) * b).astype(BF16)
    out = x + jnp.dot(m, wd_ref[...], preferred_element_type=F32)
    if final:
        out = out * lax.rsqrt(jnp.mean(out * out, axis=-1, keepdims=True) + RMS_EPS) * gf_ref[...]
    o_ref[...] = out


def _ffn(h, g, wg, wu, wd, gf, final, tm=256):
    t, d = h.shape
    c2 = lambda i: (0, 0)
    return pl.pallas_call(
        functools.partial(_ffn_kernel, final=final),
        grid=(t // tm,),
        in_specs=[pl.BlockSpec((tm, d), lambda i: (i, 0)),
                  pl.BlockSpec(g.shape, c2), pl.BlockSpec(wg.shape, c2), pl.BlockSpec(wu.shape, c2),
                  pl.BlockSpec(wd.shape, c2), pl.BlockSpec(gf.shape, c2)],
        out_specs=pl.BlockSpec((tm, d), lambda i: (i, 0)),
        out_shape=jax.ShapeDtypeStruct((t, d), F32),
        compiler_params=_cparams("parallel"),
        name="ffn",
    )(h, g, wg, wu, wd, gf)


def _q_perm():
    idx = np.zeros(NSA_WIDTH, np.int32)
    for g in range(NSA_GQA):
        for h in range(NSA_KV_HEADS):
            for d in range(HEAD_DIM):
                idx[g * 128 + h * 64 + d] = (h * NSA_GQA + g) * HEAD_DIM + d
    return idx


def _prep_w_in(w_in):
    o1 = SSM_WIDTH
    o2 = o1 + NSA_WIDTH
    o3 = o2 + 6 * NSA_KV_WIDTH
    o4 = o3 + 3 * NSA_Q_HEADS
    wq = w_in[:, o1:o2][:, _q_perm()]
    pad = jnp.zeros((w_in.shape[0], 128 - 3 * NSA_Q_HEADS), w_in.dtype)
    w = jnp.concatenate([wq, w_in[:, o2:o3], w_in[:, o3:o4], pad, w_in[:, :o1], w_in[:, o4:]], axis=1)
    return w.astype(BF16)


def _s5_mixer(slabs, b, p):
    we, tz, cp, a16, dg = _s5_tables(p['s5_lam_re'], p['s5_lam_im'], p['s5_log_dt'], p['s5_b_re'],
                                     p['s5_b_im'], p['s5_c_re'], p['s5_c_im'], p['s5_d'])
    return _s5(slabs[0], slabs[1], we, tz, cp, a16, dg, nb=b)


def _nsa_mixer(z, b, seq, bias_tabs, ov, p):
    kv = z[:, Z_KV:Z_KV + 6 * NSA_KV_WIDTH].reshape(b, seq, 6, NSA_KV_WIDTH)
    ncp = seq // CMP_STRIDE

    def cmp_layout(x):
        x = x.reshape(b, ncp, CMP_STRIDE, NSA_KV_HEADS, HEAD_DIM).transpose(0, 1, 3, 2, 4)
        return x.reshape(b, ncp, NSA_KV_HEADS * CMP_STRIDE * HEAD_DIM).astype(BF16)

    def val_t(x, kt):
        xt = x.reshape(b, seq // kt, kt, NSA_KV_HEADS, HEAD_DIM).transpose(0, 1, 3, 4, 2)
        ones = jnp.ones((b, seq // kt, NSA_KV_HEADS, 16, kt), x.dtype)
        return jnp.concatenate([xt, ones], axis=3).reshape(b, seq // kt, -1, kt).astype(BF16)

    w1k, w2k, c0k = _compress_weights(p['nsa_pe_k'], p['nsa_w1_k'], p['nsa_w2_k'])
    w1v, w2v, c0v = _compress_weights(p['nsa_pe_v'], p['nsa_w1_v'], p['nsa_w2_v'])
    kc, vct = _compress(cmp_layout(kv[:, :, 0]), cmp_layout(kv[:, :, 1]), w1k, w1v, w2k, w2v, c0k, c0v)
    tab, t2 = bias_tabs
    return _nsa(z, kc, vct, kv[:, :, 2].astype(BF16), val_t(kv[:, :, 3], 256),
                kv[:, :, 4].astype(BF16), val_t(kv[:, :, 5], 128), t2, tab, ov, b, seq)


def _lru_mixer(slabs, b, p):
    sp = jax.nn.softplus(-p['lru_lam'].astype(F32)).reshape(1, -1)
    return _lru(slabs[0], slabs[1], slabs[2], slabs[3],
                p['lru_conv_w'].astype(F32), p['lru_conv_b'].astype(F32).reshape(1, -1),
                _block_diag(p['lru_w_a']).astype(BF16), p['lru_b_a'].astype(F32).reshape(1, -1),
                _block_diag(p['lru_w_x']).astype(BF16), p['lru_b_x'].astype(F32).reshape(1, -1),
                sp, nb=b)


def _layer(h2, b, seq, bias_tabs, ov, p, final, norm_final):
    z, slabs = _inproj(h2, p['norm_mix'].reshape(1, -1), _prep_w_in(p['w_in']), b, seq)
    ys = _s5_mixer(slabs[0:2], b, p)
    yn = _nsa_mixer(z, b, seq, bias_tabs, ov, p)
    yl = _lru_mixer(slabs[2:6], b, p)
    w_out = p['w_out']
    wos = w_out[:SSM_WIDTH].astype(BF16)
    won = w_out[SSM_WIDTH:SSM_WIDTH + NSA_WIDTH][_q_perm()].astype(BF16)
    wol = w_out[SSM_WIDTH + NSA_WIDTH:].astype(BF16)
    h2 = _outproj(h2, ys, yn, yl, p['s5_w_glu'].astype(BF16), wos, won, wol, b, seq)
    return _ffn(h2, p['norm_ffn'].reshape(1, -1), p['w_gate'].astype(BF16), p['w_up'].astype(BF16),
                p['w_down'].astype(BF16), norm_final.reshape(1, -1), final)


def _overlap_t(seq):
    ncp = seq // CMP_STRIDE
    nsel = seq // SEL_LEN
    cs = np.arange(ncp) * CMP_STRIDE
    ss = np.arange(nsel) * SEL_LEN
    ovl = (cs[None, :] < ss[:, None] + SEL_LEN) & (ss[:, None] < cs[None, :] + CMP_LEN)
    ovl[:, ncp - 1] = False
    return jnp.asarray(ovl.astype(np.float32))


def kernel(x, rel_bias_table, norm_mix, w_in, w_out, s5_lam_re, s5_lam_im, s5_log_dt, s5_b_re, s5_b_im, s5_c_re, s5_c_im, s5_d, s5_w_glu, nsa_pe_k, nsa_w1_k, nsa_w2_k, nsa_pe_v, nsa_w1_v, nsa_w2_v, lru_conv_w, lru_conv_b, lru_w_a, lru_b_a, lru_w_x, lru_b_x, lru_lam, norm_ffn, w_gate, w_up, w_down, norm_final):
    b, seq, d = x.shape
    depth = norm_mix.shape[0]
    per_layer = dict(norm_mix=norm_mix, w_in=w_in, w_out=w_out, s5_lam_re=s5_lam_re, s5_lam_im=s5_lam_im,
                     s5_log_dt=s5_log_dt, s5_b_re=s5_b_re, s5_b_im=s5_b_im, s5_c_re=s5_c_re,
                     s5_c_im=s5_c_im, s5_d=s5_d, s5_w_glu=s5_w_glu, nsa_pe_k=nsa_pe_k, nsa_w1_k=nsa_w1_k,
                     nsa_w2_k=nsa_w2_k, nsa_pe_v=nsa_pe_v, nsa_w1_v=nsa_w1_v, nsa_w2_v=nsa_w2_v,
                     lru_conv_w=lru_conv_w, lru_conv_b=lru_conv_b, lru_w_a=lru_w_a, lru_b_a=lru_b_a,
                     lru_w_x=lru_w_x, lru_b_x=lru_b_x, lru_lam=lru_lam, norm_ffn=norm_ffn, w_gate=w_gate,
                     w_up=w_up, w_down=w_down)
    bias_tabs = _bias_tables(rel_bias_table, seq)
    ov = _overlap_t(seq)
    h2 = x.reshape(b * seq, d)
    for l in range(depth):
        p = {k: v[l] for k, v in per_layer.items()}
        h2 = _layer(h2, b, seq, bias_tabs, ov, p, l == depth - 1, norm_final)
    return h2.reshape(b, seq, d)
```

```python
import functools
import math

import numpy as np
import jax
import jax.numpy as jnp
from jax import lax
from jax.experimental import pallas as pl
from jax.experimental.pallas import tpu as pltpu

F32 = jnp.float32
BF16 = jnp.bfloat16

D_MODEL = 1024
SSM_WIDTH = 256
NSA_WIDTH = 512
LRU_WIDTH = 256
S5_GROUP = 16
S5_GROUPS = 16
S5_STATE = 64
HEAD_DIM = 64
NSA_Q_HEADS = 8
NSA_KV_HEADS = 2
NSA_GQA = 4
NSA_KV_WIDTH = 128
CMP_LEN = 32
CMP_STRIDE = 16
SEL_LEN = 64
SEL_TOPK = 8
WINDOW = 256
Q_BLOCK = 64
LRU_HEADS = 4
LRU_HEAD_DIM = 64
CONV_WIDTH = 4
LRU_C = 8.0
REL_BUCKETS = 32
REL_MAX_DIST = 128
D_FF = 2816
NEG_INF = -1e9
RMS_EPS = 1e-6

NO_SLOT = -3e38

S5_CHUNK = 16
Z_Q, Z_KV, Z_G, Z_NSA = 0, 512, 1280, 1408
N_SLAB = 6
OCT = 8

VMEM_LIMIT = 56 * 1024 * 1024


def _cparams(*sem):
    return pltpu.CompilerParams(dimension_semantics=sem, vmem_limit_bytes=VMEM_LIMIT)


def _inproj_kernel(x_ref, g_ref, w_ref, o_ref, *slab_refs):
    x = x_ref[...]
    y = x * lax.rsqrt(jnp.mean(x * x, axis=-1, keepdims=True) + RMS_EPS) * g_ref[...]
    res = jnp.dot(y.astype(BF16), w_ref[...], preferred_element_type=F32)
    o_ref[...] = res[:, :Z_NSA]
    for m, s_ref in enumerate(slab_refs):
        s_ref[...] = res[:, Z_NSA + m * 128:Z_NSA + (m + 1) * 128].reshape(s_ref.shape)


def _slab_spec(tm, nt):
    return pl.BlockSpec((tm // OCT, None, OCT, 128), lambda b, i: (i, b, 0, 0))


def _inproj(x2, g, w, b, seq, tm=512):
    d = x2.shape[1]
    n = w.shape[1]
    nt = seq // tm
    slab = jax.ShapeDtypeStruct((seq // OCT, b, OCT, 128), F32)
    outs = pl.pallas_call(
        _inproj_kernel,
        grid=(b, nt),
        in_specs=[pl.BlockSpec((tm, d), lambda b_, i: (b_ * nt + i, 0)),
                  pl.BlockSpec((1, d), lambda b_, i: (0, 0)),
                  pl.BlockSpec((d, n), lambda b_, i: (0, 0))],
        out_specs=[pl.BlockSpec((tm, Z_NSA), lambda b_, i: (b_ * nt + i, 0))] + [_slab_spec(tm, nt)] * N_SLAB,
        out_shape=[jax.ShapeDtypeStruct((b * seq, Z_NSA), F32)] + [slab] * N_SLAB,
        compiler_params=_cparams("parallel", "parallel"),
        name="inproj",
    )(x2, g, w)
    return outs[0], [s.reshape(b * seq, 128) for s in outs[1:]]


def _lane_regroup(x):
    r = x.shape[0]
    cols = [x[:, v * 128:(v + 1) * 128] for v in range(32)]
    piece = lax.broadcasted_iota(jnp.int32, (r, 128), 1) // 16
    out = [None] * 32
    for ah in range(2):
        for bh in range(2):
            vs = [cols[2 * (8 * ah + i) + bh] for i in range(8)]
            for stage in (4, 2, 1):
                upper = (piece & stage) != 0
                for i in range(8):
                    if i & stage:
                        continue
                    lo, hi = vs[i], vs[i + stage]
                    vs[i] = jnp.where(upper, pltpu.roll(hi, stage * 16, 1), lo)
                    vs[i + stage] = jnp.where(upper, hi, pltpu.roll(lo, 128 - stage * 16, 1))
            for j in range(8):
                out[2 * (8 * bh + j) + ah] = vs[j]
    return jnp.concatenate(out, axis=1)


def _octet_row(t, nb):
    return (t // OCT) * (nb * OCT) + t % OCT


def _s5_kernel(u0_ref, u1_ref, we_ref, tz_ref, cp_ref, a_ref, d_ref, o0_ref, o1_ref, e_ref, carry_ref,
               *, nb, kc):
    @pl.when(pl.program_id(0) == 0)
    def _():
        carry_ref[...] = jnp.zeros_like(carry_ref)

    u_refs = (u0_ref, u1_ref)
    u = jnp.concatenate(
        [jnp.concatenate([u_refs[j][pl.ds(_octet_row(k * S5_CHUNK + s, nb), nb, stride=OCT), :]
                          for s in range(S5_CHUNK) for j in range(2)], axis=1)
         for k in range(kc)], axis=0)
    ug = _lane_regroup(u)
    ugb = ug.astype(BF16)
    npair = S5_GROUPS // 2
    for p in range(npair):
        e_ref[:, p * 256:(p + 1) * 256] = jnp.dot(ugb[:, p * 512:(p + 1) * 512], we_ref[p],
                                                   preferred_element_type=F32)

    a = a_ref[...]

    def step(k, carry):
        r0 = pl.multiple_of(k * nb, nb)
        e = e_ref[pl.ds(r0, nb), :]
        e_ref[pl.ds(r0, nb), :] = carry
        new = []
        for p in range(npair):
            ar = a[:, p * 256:p * 256 + 128]
            ai = a[:, p * 256 + 128:(p + 1) * 256]
            cr = carry[:, p * 256:p * 256 + 128]
            ci = carry[:, p * 256 + 128:(p + 1) * 256]
            new.append(ar * cr - ai * ci + e[:, p * 256:p * 256 + 128])
            new.append(ar * ci + ai * cr + e[:, p * 256 + 128:(p + 1) * 256])
        return jnp.concatenate(new, axis=1)

    carry_ref[...] = lax.fori_loop(0, kc, step, carry_ref[...])

    eb = e_ref[...].astype(BF16)
    ys = []
    for p in range(npair):
        yc = jnp.dot(eb[:, p * 256:(p + 1) * 256], cp_ref[p], preferred_element_type=F32)
        y0 = jnp.dot(ugb[:, (2 * p) * 256:(2 * p + 1) * 256], tz_ref[2 * p], preferred_element_type=F32)
        y1 = jnp.dot(ugb[:, (2 * p + 1) * 256:(2 * p + 2) * 256], tz_ref[2 * p + 1],
                     preferred_element_type=F32)
        ys.append(yc + jnp.concatenate([y0, y1], axis=1))
    y = jnp.concatenate(ys, axis=1) + d_ref[...] * ug
    y = jax.nn.gelu(y)
    y = _lane_regroup(y)
    o_refs = (o0_ref, o1_ref)
    for k in range(kc):
        for s in range(S5_CHUNK):
            for j in range(2):
                lo = s * SSM_WIDTH + j * 128
                o_refs[j][pl.ds(_octet_row(k * S5_CHUNK + s, nb), nb, stride=OCT), :] = \
                    y[k * nb:(k + 1) * nb, lo:lo + 128]


def _s5(u0, u1, we, tz, cp, a16, dg, nb, kc=16):
    rows = u0.shape[0]
    r = kc * S5_CHUNK * nb
    const3 = lambda i: (0, 0, 0)
    slab = pl.BlockSpec((r, 128), lambda i: (i, 0))
    return pl.pallas_call(
        functools.partial(_s5_kernel, nb=nb, kc=kc),
        grid=(rows // r,),
        in_specs=[slab, slab,
                  pl.BlockSpec(we.shape, const3),
                  pl.BlockSpec(tz.shape, const3),
                  pl.BlockSpec(cp.shape, const3),
                  pl.BlockSpec(a16.shape, lambda i: (0, 0)),
                  pl.BlockSpec(dg.shape, lambda i: (0, 0))],
        out_specs=[slab, slab],
        out_shape=[jax.ShapeDtypeStruct((rows, 128), F32)] * 2,
        scratch_shapes=[pltpu.VMEM((kc * nb, 2048), F32), pltpu.VMEM((nb, 2048), F32)],
        compiler_params=_cparams("arbitrary"),
        name="s5",
    )(u0, u1, we, tz, cp, a16, dg)


def _s5_tables(lam_re, lam_im, log_dt, b_re, b_im, c_re, c_im, d_skip):
    L = S5_CHUNK
    G, P, C = S5_GROUPS, S5_STATE, S5_GROUP
    dt = jnp.exp(log_dt.astype(F32))[:, None]
    lr = lam_re.astype(F32)
    li = lam_im.astype(F32)
    mag = jnp.exp(lr * dt)
    ang = li * dt
    ab_re = mag * jnp.cos(ang)
    ab_im = mag * jnp.sin(ang)
    den = lr * lr + li * li
    f_re = ((ab_re - 1.0) * lr + ab_im * li) / den
    f_im = (ab_im * lr - (ab_re - 1.0) * li) / den
    br = b_re.astype(F32)
    bi = b_im.astype(F32)
    bb_re = f_re[..., None] * br - f_im[..., None] * bi
    bb_im = f_re[..., None] * bi + f_im[..., None] * br
    cr = c_re.astype(F32)
    ci = c_im.astype(F32)
    tau = jnp.arange(L + 1, dtype=F32)[:, None, None]
    pr = jnp.exp(lr * dt * tau) * jnp.cos(li * dt * tau)
    pi = jnp.exp(lr * dt * tau) * jnp.sin(li * dt * tau)

    prs = pr[L - 1 - jnp.arange(L)]
    pis = pi[L - 1 - jnp.arange(L)]
    we_re = jnp.einsum('sgp,gpc->gscp', prs, bb_re) - jnp.einsum('sgp,gpc->gscp', pis, bb_im)
    we_im = jnp.einsum('sgp,gpc->gscp', prs, bb_im) + jnp.einsum('sgp,gpc->gscp', pis, bb_re)
    we_re = we_re.reshape(G, L * C, P)
    we_im = we_im.reshape(G, L * C, P)
    z = jnp.zeros_like(we_re[0::2])
    top = jnp.concatenate([we_re[0::2], z, we_im[0::2], z], axis=-1)
    bot = jnp.concatenate([z, we_re[1::2], z, we_im[1::2]], axis=-1)
    we = jnp.concatenate([top, bot], axis=1)

    m_re = pr[:L, :, :, None] * bb_re[None] - pi[:L, :, :, None] * bb_im[None]
    m_im = pr[:L, :, :, None] * bb_im[None] + pi[:L, :, :, None] * bb_re[None]
    kern = jnp.einsum('gop,tgpc->tgoc', cr, m_re) - jnp.einsum('gop,tgpc->tgoc', ci, m_im)
    s_idx = np.arange(L)[:, None]
    t_idx = np.arange(L)[None, :]
    lag = np.clip(t_idx - s_idx, 0, L - 1)
    causal = jnp.asarray((t_idx >= s_idx).astype(np.float32))
    tzf = kern[lag] * causal[:, :, None, None, None]
    tz = tzf.transpose(2, 0, 4, 1, 3).reshape(G, L * C, L * C)

    pr1 = pr[1:]
    pi1 = pi[1:]
    cp_re = jnp.einsum('gop,tgp->gpto', cr, pr1) - jnp.einsum('gop,tgp->gpto', ci, pi1)
    cp_im = -(jnp.einsum('gop,tgp->gpto', cr, pi1) + jnp.einsum('gop,tgp->gpto', ci, pr1))
    cp_re = cp_re.reshape(G, P, L * C)
    cp_im = cp_im.reshape(G, P, L * C)
    zc = jnp.zeros_like(cp_re[0::2])
    cp = jnp.concatenate([
        jnp.concatenate([cp_re[0::2], zc], axis=-1),
        jnp.concatenate([zc, cp_re[1::2]], axis=-1),
        jnp.concatenate([cp_im[0::2], zc], axis=-1),
        jnp.concatenate([zc, cp_im[1::2]], axis=-1)], axis=1)

    a_re = pr[L].reshape(G // 2, 2 * P)
    a_im = pi[L].reshape(G // 2, 2 * P)
    a16 = jnp.concatenate([a_re, a_im], axis=-1).reshape(1, G * 2 * P)
    dg = jnp.broadcast_to(d_skip.astype(F32).reshape(G, 1, C), (G, L, C)).reshape(1, G * L * C)
    return we.astype(BF16), tz.astype(BF16), cp.astype(BF16), a16, dg


def _compress_kernel(ak_ref, av_ref, w1k_ref, w1v_ref, w2k_ref, w2v_ref, c0k_ref, c0v_ref,
                     kc_ref, vct_ref):
    def run(a_ref, w1_ref, w2_ref, c0_ref):
        a = a_ref[0]
        p1 = jnp.dot(a, w1_ref[0], preferred_element_type=F32)
        p2 = jnp.dot(a, w1_ref[1], preferred_element_type=F32)
        n = p1.shape[0]
        pre = p1 + pltpu.roll(p2, n - 1, 0) + c0_ref[...]
        out = jnp.dot(jax.nn.gelu(pre).astype(BF16), w2_ref[...], preferred_element_type=F32)
        row = lax.broadcasted_iota(jnp.int32, out.shape, 0)
        return jnp.where(row < n - 1, out, 0.0)

    kc_ref[0] = run(ak_ref, w1k_ref, w2k_ref, c0k_ref).astype(BF16)
    vct_ref[0] = run(av_ref, w1v_ref, w2v_ref, c0v_ref).T.astype(BF16)


def _compress(ak, av, w1k, w1v, w2k, w2v, c0k, c0v):
    b, nc, kw = ak.shape
    c3 = lambda i: (0, 0, 0)
    c2 = lambda i: (0, 0)
    return pl.pallas_call(
        _compress_kernel,
        grid=(b,),
        in_specs=[pl.BlockSpec((1, nc, kw), lambda i: (i, 0, 0)),
                  pl.BlockSpec((1, nc, kw), lambda i: (i, 0, 0)),
                  pl.BlockSpec(w1k.shape, c3), pl.BlockSpec(w1v.shape, c3),
                  pl.BlockSpec(w2k.shape, c2), pl.BlockSpec(w2v.shape, c2),
                  pl.BlockSpec(c0k.shape, c2), pl.BlockSpec(c0v.shape, c2)],
        out_specs=[pl.BlockSpec((1, nc, 128), lambda i: (i, 0, 0)),
                   pl.BlockSpec((1, 128, nc), lambda i: (i, 0, 0))],
        out_shape=[jax.ShapeDtypeStruct((b, nc, 128), BF16),
                   jax.ShapeDtypeStruct((b, 128, nc), BF16)],
        compiler_params=_cparams("parallel"),
        name="compress",
    )(ak, av, w1k, w1v, w2k, w2v, c0k, c0v)


def _compress_weights(pe, w1, w2):
    w1 = w1.astype(F32)
    half = (CMP_LEN // 2) * HEAD_DIM
    z1 = jnp.zeros((half, HEAD_DIM), F32)

    def bd(w):
        return jnp.concatenate([jnp.concatenate([w, z1], axis=1),
                                jnp.concatenate([z1, w], axis=1)], axis=0)

    w1s = jnp.stack([bd(w1[:half]), bd(w1[half:])])
    z2 = jnp.zeros((HEAD_DIM, HEAD_DIM), F32)
    w2f = w2.astype(F32)
    w2s = jnp.concatenate([jnp.concatenate([w2f, z2], axis=1),
                           jnp.concatenate([z2, w2f], axis=1)], axis=0)
    c0 = pe.astype(F32).reshape(1, CMP_LEN * HEAD_DIM) @ w1
    c0 = jnp.concatenate([c0, c0], axis=1)
    return w1s.astype(BF16), w2s.astype(BF16), c0


def _t5_bucket_np(dist):
    n = np.maximum(dist, 0)
    max_exact = REL_BUCKETS // 2
    nf = np.maximum(n, 1).astype(np.float32)
    large = max_exact + (np.log(nf / max_exact) / math.log(REL_MAX_DIST / max_exact)
                         * (REL_BUCKETS - max_exact)).astype(np.int32)
    large = np.minimum(large, REL_BUCKETS - 1)
    return np.where(n < max_exact, n, large)


TAB_FAR, TAB_WIN_EDGE, TAB_NONE = 3, 4, 5


def _bias_tables(rel_table, seq):
    tbl = rel_table.astype(F32)
    heads = NSA_Q_HEADS

    def lookup(bucket):
        onehot = jnp.asarray(bucket[..., None] == np.arange(REL_BUCKETS)).astype(F32)
        return jnp.einsum('...k,kh->...h', onehot, tbl, precision=lax.Precision.HIGHEST)

    r = np.arange(Q_BLOCK)[None, :]
    c = np.arange(SEL_LEN)[:, None]
    near = lookup(_t5_bucket_np(64 * np.arange(3)[:, None, None] + (r - c)[None]))
    far = jnp.broadcast_to(tbl[REL_BUCKETS - 1], (SEL_LEN, Q_BLOCK, heads))
    none = jnp.full((SEL_LEN, Q_BLOCK, heads), NO_SLOT, F32)
    causal = jnp.asarray(r - c >= 0)[:, :, None]
    edge = jnp.asarray(c > r)[:, :, None]
    tab = jnp.stack([jnp.where(causal, near[0], none), near[1], near[2], far,
                     jnp.where(edge, far, none), none])
    tab = tab.transpose(0, 1, 3, 2).reshape(6, SEL_LEN, heads * Q_BLOCK)
    ncp = seq // CMP_STRIDE
    k = np.arange(2 * ncp + 4)[:, None]
    dist = r - CMP_STRIDE * (k - (ncp - 4)) - (CMP_LEN - 1)
    t2 = jnp.where(jnp.asarray(dist >= 0)[:, :, None], lookup(_t5_bucket_np(dist)), NO_SLOT)
    t2 = t2.transpose(0, 2, 1).reshape(2 * ncp + 4, heads * Q_BLOCK)
    t2 = jnp.stack([t2[:2 * ncp], t2[4:]])
    return tab, t2


def _nsa_kernel(q_ref, gl_ref, kc_ref, vct_ref, ks_ref, vs_ref, kw_ref, vw_ref, t2_ref, tab_ref,
                ov_ref, o_ref, sel_ref, s_ref, p_ref, acc_ref, accw_ref, vst_ref, vwt_ref, *, ncp, nw):
    @pl.when(pl.program_id(1) == 0)
    def _():
        for v_ref, vt_ref in ((vs_ref, vst_ref), (vw_ref, vwt_ref)):
            kt = vt_ref.shape[2]
            ones = jnp.ones((16, kt), F32)
            for u in range(vt_ref.shape[0]):
                vt = v_ref[0, u * kt:(u + 1) * kt, :].T
                vt_ref[u] = jnp.concatenate([vt[0:HEAD_DIM], ones, vt[HEAD_DIM:], ones], axis=0).astype(BF16)

    qi0 = pl.program_id(1) * nw
    cw = nw * NSA_GQA * Q_BLOCK
    ncol = NSA_KV_HEADS * cw
    vrows = HEAD_DIM + 16
    lane = lax.broadcasted_iota(jnp.int32, (Q_BLOCK, 128), 1)

    q = q_ref[...] * (HEAD_DIM ** -0.5)
    pieces = []
    for h in range(NSA_KV_HEADS):
        keep = (lane >= 64) if h == 1 else (lane < 64)
        for w in range(nw):
            for g in range(NSA_GQA):
                pieces.append(jnp.where(keep, q[w * 64:(w + 1) * 64, g * 128:(g + 1) * 128], 0.0))
    qpad = jnp.concatenate(pieces, axis=0).astype(BF16)

    col = lax.broadcasted_iota(jnp.int32, (1, ncol), 1)
    rcol = col % Q_BLOCK
    qcol = qi0 + (col // (NSA_GQA * Q_BLOCK)) % nw
    tvec = Q_BLOCK * qcol + rcol
    nt_dims = (((1,), (1,)), ((), ()))

    def per_head_cols(tiles):
        return jnp.concatenate([tiles[w][:, h * 256:(h + 1) * 256]
                                for h in range(NSA_KV_HEADS) for w in range(nw)], axis=1)

    def slot_terms(m, n_masked):
        m_fin = jnp.where(n_masked > 0, jnp.maximum(m, NEG_INF), m)
        return jnp.exp(m - m_fin), n_masked * jnp.exp(NEG_INF - m_fin)

    t2_tiles = []
    for w in range(nw):
        start = (ncp - 4) - 4 * (qi0 + w)
        par = (start // 4) % 2
        a0 = pl.multiple_of(start - 4 * par, 8)
        t2_tiles.append(t2_ref[par, pl.ds(a0, ncp), :])
    s = lax.dot_general(kc_ref[0], qpad, nt_dims, preferred_element_type=F32) + per_head_cols(t2_tiles)
    m = jnp.max(s, axis=0, keepdims=True)
    e = jnp.exp(s - m)
    n_valid = jnp.clip(jnp.right_shift(tvec - (CMP_STRIDE - 1), 4), 0, ncp - 1)
    scale, extra = slot_terms(m, (ncp - 1 - n_valid).astype(F32))
    p_c = e * (scale / (jnp.sum(e, axis=0, keepdims=True) * scale + extra))
    p_cb = p_c.astype(BF16)
    o_c = [jnp.dot(vct_ref[0, h * 64:(h + 1) * 64, :], p_cb[:, h * cw:(h + 1) * cw],
                   preferred_element_type=F32) for h in range(NSA_KV_HEADS)]

    lane_c = lax.broadcasted_iota(jnp.int32, (ncp, 128), 1)
    parts = []
    for w in range(nw):
        halves = []
        for h in range(NSA_KV_HEADS):
            base = h * cw + w * 256
            ph = p_c[:, base:base + 128] + p_c[:, base + 128:base + 256]
            halves.append(ph + pltpu.roll(ph, 64, 1))
        parts.append(jnp.where(lane_c < 64, halves[0], halves[1]))
    imp = jnp.dot(ov_ref[...], jnp.concatenate(parts, axis=1), preferred_element_type=F32,
                  precision=lax.Precision.HIGHEST)
    nsel = imp.shape[0]
    jidx = lax.broadcasted_iota(jnp.int32, (nsel, 1), 0)
    qv = qi0 + lax.broadcasted_iota(jnp.int32, (1, nw * 128), 1) // 128
    forced = jnp.logical_or(jidx == 0, jnp.logical_or(jidx == qv, jidx == qv - 1))
    avail = jidx <= qv
    cand = jnp.logical_and(avail, jnp.logical_not(forced))
    budget = SEL_TOPK - (1 + jnp.where(qv >= 1, 1, 0) + jnp.where(qv >= 2, 1, 0))
    rank = jnp.zeros((nsel, nw * 128), jnp.int32)
    for jp in range(1, nsel - 2):
        row = imp[jp:jp + 1, :]
        ge = jnp.where(row >= imp, 1, 0)
        gt = jnp.where(row > imp, 1, 0)
        is_cand = jnp.where(qv - 2 >= jp, 1, 0)
        rank = rank + jnp.where(jidx > jp, ge, gt) * is_cand
    sel = jnp.logical_or(jnp.logical_and(forced, avail), jnp.logical_and(cand, rank < budget))
    seladd = jnp.where(sel, 0.0, NO_SLOT)
    lane_s = lax.broadcasted_iota(jnp.int32, (nsel, 128), 1)
    for w in range(nw):
        tile = seladd[:, w * 128:(w + 1) * 128]
        rolled = pltpu.roll(tile, 64, 1)
        sel_ref[w] = jnp.where(lane_s < 64, tile, rolled)
        sel_ref[nw + w] = jnp.where(lane_s < 64, rolled, tile)

    def block_adds(blk, selected):
        tiles = []
        for w in range(nw):
            d = qi0 + w - blk
            if selected:
                e = jnp.where(d < 0, TAB_FAR, jnp.minimum(d, TAB_FAR))
            else:
                outside = jnp.logical_or(blk < 0, jnp.logical_or(d < 0, d > TAB_NONE))
                e = jnp.where(outside, TAB_NONE, d)
            tiles.append(tab_ref[e])
        add = per_head_cols(tiles)
        if selected:
            rows = [sel_ref[h * nw + w, pl.ds(blk, 1), :] for h in range(NSA_KV_HEADS) for w in range(nw)]
            add = add + jnp.concatenate([x for r in rows for x in (r, r)], axis=1)
        return add

    def softmax_tile(m_i, s, adds):
        sm = jnp.concatenate([s[k * 64:(k + 1) * 64] + adds[k] for k in range(len(adds))], axis=0)
        m_new = jnp.maximum(m_i, jnp.max(sm, axis=0, keepdims=True))
        return m_new, jnp.exp(m_i - m_new), jnp.exp(sm - m_new).astype(BF16)

    def add_values(ref, alpha, vt_tile, p):
        for h in range(NSA_KV_HEADS):
            ref[h] = alpha[:, h * cw:(h + 1) * cw] * ref[h] + jnp.dot(
                vt_tile[h * vrows:(h + 1) * vrows, :], p[:, h * cw:(h + 1) * cw], preferred_element_type=F32)

    m_init = jnp.full((1, ncol), NO_SLOT, F32)

    accw_ref[...] = jnp.zeros_like(accw_ref)
    m_w = m_init
    for i in range(nw // 2 + 2):
        jj = qi0 // 2 - 2 + i
        jc = jnp.maximum(jj, 0)
        r0 = pl.multiple_of(jc * 128, 128)
        s = lax.dot_general(kw_ref[0, pl.ds(r0, 128), :].astype(BF16), qpad, nt_dims,
                            preferred_element_type=F32)
        m_w, alpha, p = softmax_tile(m_w, s, [block_adds(2 * jj + half, False) for half in range(2)])
        add_values(accw_ref, alpha, vwt_ref[jc], p)

    n_tiles = (qi0 + nw - 1) // 4 + 1

    def scores(u):
        r0 = pl.multiple_of(u * 256, 256)
        return lax.dot_general(ks_ref[0, pl.ds(r0, 256), :].astype(BF16), qpad, nt_dims,
                               preferred_element_type=F32)

    s_ref[...] = scores(0)
    p_ref[...] = jnp.zeros_like(p_ref)
    acc_ref[...] = jnp.zeros_like(acc_ref)

    def sel_body(u, carry):
        m_i, alpha_prev = carry
        add_values(acc_ref, alpha_prev, vst_ref[jnp.maximum(u - 1, 0)], p_ref[...])
        m_new, alpha, p = softmax_tile(m_i, s_ref[...], [block_adds(4 * u + k, True) for k in range(4)])
        p_ref[...] = p
        s_ref[...] = scores(jnp.minimum(u + 1, n_tiles - 1))
        return m_new, alpha

    m_s, alpha_last = lax.fori_loop(0, n_tiles, sel_body, (m_init, jnp.ones((1, ncol), F32)))
    add_values(acc_ref, alpha_last, vst_ref[n_tiles - 1], p_ref[...])

    nm_s = (SEL_LEN * jnp.maximum(SEL_TOPK - (qcol + 1), 0) + (SEL_LEN - 1 - rcol)).astype(F32)
    nm_w = (WINDOW + Q_BLOCK - jnp.minimum(tvec + 1, WINDOW)).astype(F32)

    def normalised(m, ref, n_masked):
        scale, extra = slot_terms(m, n_masked)
        out = []
        for h in range(NSA_KV_HEADS):
            sl = slice(h * cw, (h + 1) * cw)
            acc = ref[h]
            denom = acc[HEAD_DIM:HEAD_DIM + 1, :] * scale[:, sl] + extra[:, sl]
            out.append((scale[:, sl] / denom, acc[0:HEAD_DIM, :]))
        return out

    o_s = normalised(m_s, acc_ref, nm_s)
    o_w = normalised(m_w, accw_ref, nm_w)

    lane1 = lax.broadcasted_iota(jnp.int32, (1, 128), 1)
    gts = []
    for w in range(nw):
        g = jax.nn.sigmoid(gl_ref[w * 64:(w + 1) * 64, :])
        gts.append(jnp.concatenate([g, g], axis=0).T)

    def gate_vec(h, br):
        tiles = []
        for w in range(nw):
            for gg in range(2):
                c0 = (h * 4 + 2 * gg) * 3 + br
                c1 = (h * 4 + 2 * gg + 1) * 3 + br
                tiles.append(jnp.where(lane1 < 64, gts[w][c0:c0 + 1, :], gts[w][c1:c1 + 1, :]))
        return jnp.concatenate(tiles, axis=1)

    tot = []
    for h in range(NSA_KV_HEADS):
        t = gate_vec(h, 0) * o_c[h]
        t = t + (gate_vec(h, 1) * o_s[h][0]) * o_s[h][1]
        t = t + (gate_vec(h, 2) * o_w[h][0]) * o_w[h][1]
        tot.append(t)
    for w in range(nw):
        ot = jnp.concatenate([t[:, w * 256:(w + 1) * 256] for t in tot], axis=0).T
        o_ref[w * 64:(w + 1) * 64, :] = jnp.concatenate([ot[g * 64:(g + 1) * 64, :]
                                                         for g in range(NSA_GQA)], axis=1)


def _nsa(z, kc, vct, t2, tab, ov, b, seq, nw=4):
    nst = seq // (Q_BLOCK * nw)
    ncp = seq // CMP_STRIDE
    rows = Q_BLOCK * nw
    cw = nw * NSA_GQA * Q_BLOCK
    z3 = z.reshape(b, seq, Z_NSA)
    kv_col = lambda n: pl.BlockSpec((1, seq, 128), lambda i, j: (i, 0, Z_KV // 128 + n))
    return pl.pallas_call(
        functools.partial(_nsa_kernel, ncp=ncp, nw=nw),
        grid=(b, nst),
        in_specs=[pl.BlockSpec((rows, NSA_WIDTH), lambda i, j: (i * nst + j, Z_Q // NSA_WIDTH)),
                  pl.BlockSpec((rows, 128), lambda i, j: (i * nst + j, Z_G // 128)),
                  pl.BlockSpec((1, ncp, 128), lambda i, j: (i, 0, 0)),
                  pl.BlockSpec((1, 128, ncp), lambda i, j: (i, 0, 0)),
                  kv_col(2), kv_col(3), kv_col(4), kv_col(5),
                  pl.BlockSpec(t2.shape, lambda i, j: (0, 0, 0)),
                  pl.BlockSpec(tab.shape, lambda i, j: (0, 0, 0)),
                  pl.BlockSpec(ov.shape, lambda i, j: (0, 0))],
        out_specs=pl.BlockSpec((rows, NSA_WIDTH), lambda i, j: (i * nst + j, 0)),
        out_shape=jax.ShapeDtypeStruct((b * seq, NSA_WIDTH), F32),
        scratch_shapes=[pltpu.VMEM((NSA_KV_HEADS * nw, seq // SEL_LEN, 128), F32),
                        pltpu.VMEM((256, NSA_KV_HEADS * cw), F32),
                        pltpu.VMEM((256, NSA_KV_HEADS * cw), BF16),
                        pltpu.VMEM((NSA_KV_HEADS, HEAD_DIM + 16, cw), F32),
                        pltpu.VMEM((NSA_KV_HEADS, HEAD_DIM + 16, cw), F32),
                        pltpu.VMEM((seq // 256, NSA_KV_HEADS * (HEAD_DIM + 16), 256), BF16),
                        pltpu.VMEM((seq // 128, NSA_KV_HEADS * (HEAD_DIM + 16), 128), BF16)],
        compiler_params=_cparams("parallel", "arbitrary"),
        name="nsa",
    )(z, z, kc, vct, z3, z3, z3, z3, t2, tab, ov)


def _lru_kernel(x0_ref, x1_ref, g0_ref, g1_ref, cw_ref, cb_ref, wa_ref, ba_ref, wx_ref, bx_ref, sp_ref,
                o0_ref, o1_ref, xprev_ref, h_ref, a_scr, b_scr, *, nb, tc):
    @pl.when(pl.program_id(0) == 0)
    def _():
        xprev_ref[...] = jnp.zeros_like(xprev_ref)
        h_ref[...] = jnp.zeros_like(h_ref)

    x = jnp.concatenate([x0_ref[...], x1_ref[...]], axis=1)
    rows, width = x.shape
    grp = nb * OCT
    xm = jnp.concatenate([xprev_ref[...], x[:rows - grp]], axis=0)
    xprev_ref[...] = x[rows - grp:]
    x3 = x.reshape(rows // OCT, OCT, width)
    xm3 = xm.reshape(rows // OCT, OCT, width)
    tlo = lax.broadcasted_iota(jnp.int32, (1, OCT, 1), 1)
    cw = cw_ref[...]
    xc = cb_ref[...].reshape(1, 1, width) + cw[CONV_WIDTH - 1:CONV_WIDTH, :].reshape(1, 1, width) * x3
    for k in range(1, CONV_WIDTH):
        delayed = jnp.where(tlo >= k, pltpu.roll(x3, k, 1), pltpu.roll(xm3, k, 1))
        xc = xc + cw[CONV_WIDTH - 1 - k:CONV_WIDTH - k, :].reshape(1, 1, width) * delayed
    xc = xc.reshape(rows, width)
    xcb = xc.astype(BF16)
    gate_r = jax.nn.sigmoid(jnp.dot(xcb, wa_ref[...], preferred_element_type=F32) + ba_ref[...])
    gate_i = jax.nn.sigmoid(jnp.dot(xcb, wx_ref[...], preferred_element_type=F32) + bx_ref[...])
    log_a = -LRU_C * gate_r * sp_ref[...]
    a = jnp.exp(log_a)
    th = jnp.tanh(log_a)
    bt = jnp.sqrt(-2.0 * th / (1.0 - th)) * gate_i * xc
    for j in range(2):
        a_scr[j] = a[:, j * 128:(j + 1) * 128]
        b_scr[j] = bt[:, j * 128:(j + 1) * 128]

    def step(t, h):
        r0 = _octet_row(t, nb)
        new = []
        for j in range(2):
            hj = a_scr[j, pl.ds(r0, nb, stride=OCT), :] * h[j] + b_scr[j, pl.ds(r0, nb, stride=OCT), :]
            b_scr[j, pl.ds(r0, nb, stride=OCT), :] = hj
            new.append(hj)
        return tuple(new)

    h = lax.fori_loop(0, tc, step, (h_ref[0], h_ref[1]))
    for j, (g_ref, o_ref) in enumerate(((g0_ref, o0_ref), (g1_ref, o1_ref))):
        h_ref[j] = h[j]
        o_ref[...] = b_scr[j] * jax.nn.gelu(g_ref[...])


def _lru(x0, x1, g0, g1, cw, cb, wa, ba, wx, bx, sp, nb, tc=64):
    rows = x0.shape[0]
    r = tc * nb
    c2 = lambda i: (0, 0)
    slab = pl.BlockSpec((r, 128), lambda i: (i, 0))
    return pl.pallas_call(
        functools.partial(_lru_kernel, nb=nb, tc=tc),
        grid=(rows // r,),
        in_specs=[slab, slab, slab, slab,
                  pl.BlockSpec(cw.shape, c2), pl.BlockSpec(cb.shape, c2),
                  pl.BlockSpec(wa.shape, c2), pl.BlockSpec(ba.shape, c2),
                  pl.BlockSpec(wx.shape, c2), pl.BlockSpec(bx.shape, c2),
                  pl.BlockSpec(sp.shape, c2)],
        out_specs=[slab, slab],
        out_shape=[jax.ShapeDtypeStruct((rows, 128), F32)] * 2,
        scratch_shapes=[pltpu.VMEM((nb * OCT, LRU_WIDTH), F32), pltpu.VMEM((2, nb, 128), F32),
                        pltpu.VMEM((2, r, 128), F32), pltpu.VMEM((2, r, 128), F32)],
        compiler_params=_cparams("arbitrary"),
        name="lru",
    )(x0, x1, g0, g1, cw, cb, wa, ba, wx, bx, sp)


def _block_diag(w):
    h, i, j = w.shape
    eye = jnp.eye(h, dtype=w.dtype)
    return jnp.einsum('hij,hk->hikj', w, eye).reshape(h * i, h * j)


def _mix_ffn_kernel(h_ref, ys0_ref, ys1_ref, yn_ref, yl0_ref, yl1_ref, wglu_ref, wos_ref, won_ref, wol_ref,
                    g_ref, wg_ref, wu_ref, wd_ref, gf_ref, o_ref, *, final):
    tm = h_ref.shape[0]

    def rows(a_ref, b_ref):
        return jnp.concatenate([a_ref[...].reshape(tm, 128), b_ref[...].reshape(tm, 128)], axis=1)

    gl = jnp.dot(rows(ys0_ref, ys1_ref).astype(BF16), wglu_ref[...], preferred_element_type=F32)
    s5 = gl[:, :SSM_WIDTH] * jax.nn.sigmoid(gl[:, SSM_WIDTH:])
    acc = jnp.dot(s5.astype(BF16), wos_ref[...], preferred_element_type=F32)
    acc = acc + jnp.dot(yn_ref[...].astype(BF16), won_ref[...], preferred_element_type=F32)
    acc = acc + jnp.dot(rows(yl0_ref, yl1_ref).astype(BF16), wol_ref[...], preferred_element_type=F32)
    x = h_ref[...] + acc

    y = (x * lax.rsqrt(jnp.mean(x * x, axis=-1, keepdims=True) + RMS_EPS) * g_ref[...]).astype(BF16)
    a = jnp.dot(y, wg_ref[...], preferred_element_type=F32)
    b = jnp.dot(y, wu_ref[...], preferred_element_type=F32)
    m = (jax.nn.silu(a) * b).astype(BF16)
    out = x + jnp.dot(m, wd_ref[...], preferred_element_type=F32)
    if final:
        out = out * lax.rsqrt(jnp.mean(out * out, axis=-1, keepdims=True) + RMS_EPS) * gf_ref[...]
    o_ref[...] = out


def _mix_ffn(h, ys, yn, yl, wglu, wos, won, wol, g, wg, wu, wd, gf, final, b, seq, tm=256):
    d = h.shape[1]
    nt = seq // tm
    row = lambda w: pl.BlockSpec((tm, w), lambda b_, i: (b_ * nt + i, 0))
    const = lambda a: pl.BlockSpec(a.shape, lambda b_, i: (0, 0), pipeline_mode=pl.Buffered(1))
    slab4 = lambda s: s.reshape(seq // OCT, b, OCT, 128)
    return pl.pallas_call(
        functools.partial(_mix_ffn_kernel, final=final),
        grid=(b, nt),
        in_specs=[row(d), _slab_spec(tm, nt), _slab_spec(tm, nt), row(NSA_WIDTH),
                  _slab_spec(tm, nt), _slab_spec(tm, nt),
                  const(wglu), const(wos), const(won), const(wol),
                  const(g), const(wg), const(wu), const(wd), const(gf)],
        out_specs=row(d),
        out_shape=jax.ShapeDtypeStruct((b * seq, d), F32),
        compiler_params=_cparams("parallel", "parallel"),
        name="mix_ffn",
    )(h, slab4(ys[0]), slab4(ys[1]), yn, slab4(yl[0]), slab4(yl[1]), wglu, wos, won, wol, g, wg, wu, wd, gf)


def _q_perm():
    idx = np.zeros(NSA_WIDTH, np.int32)
    for g in range(NSA_GQA):
        for h in range(NSA_KV_HEADS):
            for d in range(HEAD_DIM):
                idx[g * 128 + h * 64 + d] = (h * NSA_GQA + g) * HEAD_DIM + d
    return idx


def _prep_w_in(w_in):
    o1 = SSM_WIDTH
    o2 = o1 + NSA_WIDTH
    o3 = o2 + 6 * NSA_KV_WIDTH
    o4 = o3 + 3 * NSA_Q_HEADS
    wq = w_in[:, o1:o2][:, _q_perm()]
    pad = jnp.zeros((w_in.shape[0], 128 - 3 * NSA_Q_HEADS), w_in.dtype)
    w = jnp.concatenate([wq, w_in[:, o2:o3], w_in[:, o3:o4], pad, w_in[:, :o1], w_in[:, o4:]], axis=1)
    return w.astype(BF16)


def _s5_mixer(slabs, b, p):
    we, tz, cp, a16, dg = _s5_tables(p['s5_lam_re'], p['s5_lam_im'], p['s5_log_dt'], p['s5_b_re'],
                                     p['s5_b_im'], p['s5_c_re'], p['s5_c_im'], p['s5_d'])
    return _s5(slabs[0], slabs[1], we, tz, cp, a16, dg, nb=b)


def _nsa_mixer(z, b, seq, bias_tabs, ov, p):
    kv = z[:, Z_KV:Z_KV + 6 * NSA_KV_WIDTH].reshape(b, seq, 6, NSA_KV_WIDTH)
    ncp = seq // CMP_STRIDE

    def cmp_layout(x):
        x = x.reshape(b, ncp, CMP_STRIDE, NSA_KV_HEADS, HEAD_DIM).transpose(0, 1, 3, 2, 4)
        return x.reshape(b, ncp, NSA_KV_HEADS * CMP_STRIDE * HEAD_DIM).astype(BF16)

    w1k, w2k, c0k = _compress_weights(p['nsa_pe_k'], p['nsa_w1_k'], p['nsa_w2_k'])
    w1v, w2v, c0v = _compress_weights(p['nsa_pe_v'], p['nsa_w1_v'], p['nsa_w2_v'])
    kc, vct = _compress(cmp_layout(kv[:, :, 0]), cmp_layout(kv[:, :, 1]), w1k, w1v, w2k, w2v, c0k, c0v)
    tab, t2 = bias_tabs
    return _nsa(z, kc, vct, t2, tab, ov, b, seq)


def _lru_mixer(slabs, b, p):
    sp = jax.nn.softplus(-p['lru_lam'].astype(F32)).reshape(1, -1)
    return _lru(slabs[0], slabs[1], slabs[2], slabs[3],
                p['lru_conv_w'].astype(F32), p['lru_conv_b'].astype(F32).reshape(1, -1),
                _block_diag(p['lru_w_a']).astype(BF16), p['lru_b_a'].astype(F32).reshape(1, -1),
                _block_diag(p['lru_w_x']).astype(BF16), p['lru_b_x'].astype(F32).reshape(1, -1),
                sp, nb=b)


def _layer(h2, b, seq, bias_tabs, ov, p, final, norm_final):
    z, slabs = _inproj(h2, p['norm_mix'].reshape(1, -1), _prep_w_in(p['w_in']), b, seq)
    ys = _s5_mixer(slabs[0:2], b, p)
    yn = _nsa_mixer(z, b, seq, bias_tabs, ov, p)
    yl = _lru_mixer(slabs[2:6], b, p)
    w_out = p['w_out']
    wos = w_out[:SSM_WIDTH].astype(BF16)
    won = w_out[SSM_WIDTH:SSM_WIDTH + NSA_WIDTH][_q_perm()].astype(BF16)
    wol = w_out[SSM_WIDTH + NSA_WIDTH:].astype(BF16)
    return _mix_ffn(h2, ys, yn, yl, p['s5_w_glu'].astype(BF16), wos, won, wol,
                    p['norm_ffn'].reshape(1, -1), p['w_gate'].astype(BF16), p['w_up'].astype(BF16),
                    p['w_down'].astype(BF16), norm_final.reshape(1, -1), final, b, seq)


def _overlap_t(seq):
    ncp = seq // CMP_STRIDE
    nsel = seq // SEL_LEN
    cs = np.arange(ncp) * CMP_STRIDE
    ss = np.arange(nsel) * SEL_LEN
    ovl = (cs[None, :] < ss[:, None] + SEL_LEN) & (ss[:, None] < cs[None, :] + CMP_LEN)
    ovl[:, ncp - 1] = False
    return jnp.asarray(ovl.astype(np.float32))


def kernel(x, rel_bias_table, norm_mix, w_in, w_out, s5_lam_re, s5_lam_im, s5_log_dt, s5_b_re, s5_b_im, s5_c_re, s5_c_im, s5_d, s5_w_glu, nsa_pe_k, nsa_w1_k, nsa_w2_k, nsa_pe_v, nsa_w1_v, nsa_w2_v, lru_conv_w, lru_conv_b, lru_w_a, lru_b_a, lru_w_x, lru_b_x, lru_lam, norm_ffn, w_gate, w_up, w_down, norm_final):
    b, seq, d = x.shape
    depth = norm_mix.shape[0]
    per_layer = dict(norm_mix=norm_mix, w_in=w_in, w_out=w_out, s5_lam_re=s5_lam_re, s5_lam_im=s5_lam_im,
                     s5_log_dt=s5_log_dt, s5_b_re=s5_b_re, s5_b_im=s5_b_im, s5_c_re=s5_c_re,
                     s5_c_im=s5_c_im, s5_d=s5_d, s5_w_glu=s5_w_glu, nsa_pe_k=nsa_pe_k, nsa_w1_k=nsa_w1_k,
                     nsa_w2_k=nsa_w2_k, nsa_pe_v=nsa_pe_v, nsa_w1_v=nsa_w1_v, nsa_w2_v=nsa_w2_v,
                     lru_conv_w=lru_conv_w, lru_conv_b=lru_conv_b, lru_w_a=lru_w_a, lru_b_a=lru_b_a,
                     lru_w_x=lru_w_x, lru_b_x=lru_b_x, lru_lam=lru_lam, norm_ffn=norm_ffn, w_gate=w_gate,
                     w_up=w_up, w_down=w_down)
    bias_tabs = _bias_tables(rel_bias_table, seq)
    ov = _overlap_t(seq)
    h2 = x.reshape(b * seq, d)
    for l in range(depth):
        p = {k: v[l] for k, v in per_layer.items()}
        h2 = _layer(h2, b, seq, bias_tabs, ov, p, l == depth - 1, norm_final)
    return h2.reshape(b, seq, d)
```

```python
import functools
import math

import numpy as np
import jax
import jax.numpy as jnp
from jax import lax
from jax.experimental import pallas as pl
from jax.experimental.pallas import tpu as pltpu

F32 = jnp.float32
BF16 = jnp.bfloat16

D_MODEL = 1024
SSM_WIDTH = 256
NSA_WIDTH = 512
LRU_WIDTH = 256
S5_GROUP = 16
S5_GROUPS = 16
S5_STATE = 64
HEAD_DIM = 64
NSA_Q_HEADS = 8
NSA_KV_HEADS = 2
NSA_GQA = 4
NSA_KV_WIDTH = 128
CMP_LEN = 32
CMP_STRIDE = 16
SEL_LEN = 64
SEL_TOPK = 8
WINDOW = 256
Q_BLOCK = 64
LRU_HEADS = 4
LRU_HEAD_DIM = 64
CONV_WIDTH = 4
LRU_C = 8.0
REL_BUCKETS = 32
REL_MAX_DIST = 128
D_FF = 2816
NEG_INF = -1e9
RMS_EPS = 1e-6

NO_SLOT = -3e38

S5_CHUNK = 16
Z_Q, Z_KV, Z_G, Z_NSA = 0, 512, 1280, 1408
N_SLAB = 6
OCT = 8

VMEM_LIMIT = 56 * 1024 * 1024


def _cparams(*sem):
    return pltpu.CompilerParams(dimension_semantics=sem, vmem_limit_bytes=VMEM_LIMIT)


def _inproj_kernel(x_ref, g_ref, w_ref, o_ref, *slab_refs):
    x = x_ref[...]
    y = x * lax.rsqrt(jnp.mean(x * x, axis=-1, keepdims=True) + RMS_EPS) * g_ref[...]
    res = jnp.dot(y.astype(BF16), w_ref[...], preferred_element_type=F32)
    o_ref[...] = res[:, :Z_NSA]
    for m, s_ref in enumerate(slab_refs):
        s_ref[...] = res[:, Z_NSA + m * 128:Z_NSA + (m + 1) * 128].reshape(s_ref.shape)


def _slab_spec(tm, nt):
    return pl.BlockSpec((tm // OCT, None, OCT, 128), lambda b, i: (i, b, 0, 0))


def _inproj(x2, g, w, b, seq, tm=512):
    d = x2.shape[1]
    n = w.shape[1]
    nt = seq // tm
    slab = jax.ShapeDtypeStruct((seq // OCT, b, OCT, 128), F32)
    outs = pl.pallas_call(
        _inproj_kernel,
        grid=(b, nt),
        in_specs=[pl.BlockSpec((tm, d), lambda b_, i: (b_ * nt + i, 0)),
                  pl.BlockSpec((1, d), lambda b_, i: (0, 0)),
                  pl.BlockSpec((d, n), lambda b_, i: (0, 0))],
        out_specs=[pl.BlockSpec((tm, Z_NSA), lambda b_, i: (b_ * nt + i, 0))] + [_slab_spec(tm, nt)] * N_SLAB,
        out_shape=[jax.ShapeDtypeStruct((b * seq, Z_NSA), F32)] + [slab] * N_SLAB,
        compiler_params=_cparams("parallel", "parallel"),
        name="inproj",
    )(x2, g, w)
    return outs[0], [s.reshape(b * seq, 128) for s in outs[1:]]


def _lane_regroup(x):
    r = x.shape[0]
    cols = [x[:, v * 128:(v + 1) * 128] for v in range(32)]
    piece = lax.broadcasted_iota(jnp.int32, (r, 128), 1) // 16
    out = [None] * 32
    for ah in range(2):
        for bh in range(2):
            vs = [cols[2 * (8 * ah + i) + bh] for i in range(8)]
            for stage in (4, 2, 1):
                upper = (piece & stage) != 0
                for i in range(8):
                    if i & stage:
                        continue
                    lo, hi = vs[i], vs[i + stage]
                    vs[i] = jnp.where(upper, pltpu.roll(hi, stage * 16, 1), lo)
                    vs[i + stage] = jnp.where(upper, hi, pltpu.roll(lo, 128 - stage * 16, 1))
            for j in range(8):
                out[2 * (8 * bh + j) + ah] = vs[j]
    return jnp.concatenate(out, axis=1)


def _octet_row(t, nb):
    return (t // OCT) * (nb * OCT) + t % OCT


def _s5_kernel(u0_ref, u1_ref, we_ref, tz_ref, cp_ref, a_ref, d_ref, o0_ref, o1_ref, e_ref, carry_ref,
               *, nb, kc):
    @pl.when(pl.program_id(0) == 0)
    def _():
        carry_ref[...] = jnp.zeros_like(carry_ref)

    u_refs = (u0_ref, u1_ref)
    u = jnp.concatenate(
        [jnp.concatenate([u_refs[j][pl.ds(_octet_row(k * S5_CHUNK + s, nb), nb, stride=OCT), :]
                          for s in range(S5_CHUNK) for j in range(2)], axis=1)
         for k in range(kc)], axis=0)
    ug = _lane_regroup(u)
    ugb = ug.astype(BF16)
    npair = S5_GROUPS // 2
    for p in range(npair):
        e_ref[:, p * 256:(p + 1) * 256] = jnp.dot(ugb[:, p * 512:(p + 1) * 512], we_ref[p],
                                                   preferred_element_type=F32)

    a = a_ref[...]

    def step(k, carry):
        r0 = pl.multiple_of(k * nb, nb)
        e = e_ref[pl.ds(r0, nb), :]
        e_ref[pl.ds(r0, nb), :] = carry
        new = []
        for p in range(npair):
            ar = a[:, p * 256:p * 256 + 128]
            ai = a[:, p * 256 + 128:(p + 1) * 256]
            cr = carry[:, p * 256:p * 256 + 128]
            ci = carry[:, p * 256 + 128:(p + 1) * 256]
            new.append(ar * cr - ai * ci + e[:, p * 256:p * 256 + 128])
            new.append(ar * ci + ai * cr + e[:, p * 256 + 128:(p + 1) * 256])
        return jnp.concatenate(new, axis=1)

    carry_ref[...] = lax.fori_loop(0, kc, step, carry_ref[...])

    eb = e_ref[...].astype(BF16)
    ys = []
    for p in range(npair):
        yc = jnp.dot(eb[:, p * 256:(p + 1) * 256], cp_ref[p], preferred_element_type=F32)
        y0 = jnp.dot(ugb[:, (2 * p) * 256:(2 * p + 1) * 256], tz_ref[2 * p], preferred_element_type=F32)
        y1 = jnp.dot(ugb[:, (2 * p + 1) * 256:(2 * p + 2) * 256], tz_ref[2 * p + 1],
                     preferred_element_type=F32)
        ys.append(yc + jnp.concatenate([y0, y1], axis=1))
    y = jnp.concatenate(ys, axis=1) + d_ref[...] * ug
    y = jax.nn.gelu(y)
    y = _lane_regroup(y)
    o_refs = (o0_ref, o1_ref)
    for k in range(kc):
        for s in range(S5_CHUNK):
            for j in range(2):
                lo = s * SSM_WIDTH + j * 128
                o_refs[j][pl.ds(_octet_row(k * S5_CHUNK + s, nb), nb, stride=OCT), :] = \
                    y[k * nb:(k + 1) * nb, lo:lo + 128]


def _s5(u0, u1, we, tz, cp, a16, dg, nb, kc=16):
    rows = u0.shape[0]
    r = kc * S5_CHUNK * nb
    const3 = lambda i: (0, 0, 0)
    slab = pl.BlockSpec((r, 128), lambda i: (i, 0))
    return pl.pallas_call(
        functools.partial(_s5_kernel, nb=nb, kc=kc),
        grid=(rows // r,),
        in_specs=[slab, slab,
                  pl.BlockSpec(we.shape, const3),
                  pl.BlockSpec(tz.shape, const3),
                  pl.BlockSpec(cp.shape, const3),
                  pl.BlockSpec(a16.shape, lambda i: (0, 0)),
                  pl.BlockSpec(dg.shape, lambda i: (0, 0))],
        out_specs=[slab, slab],
        out_shape=[jax.ShapeDtypeStruct((rows, 128), F32)] * 2,
        scratch_shapes=[pltpu.VMEM((kc * nb, 2048), F32), pltpu.VMEM((nb, 2048), F32)],
        compiler_params=_cparams("arbitrary"),
        name="s5",
    )(u0, u1, we, tz, cp, a16, dg)


def _s5_tables(lam_re, lam_im, log_dt, b_re, b_im, c_re, c_im, d_skip):
    L = S5_CHUNK
    G, P, C = S5_GROUPS, S5_STATE, S5_GROUP
    dt = jnp.exp(log_dt.astype(F32))[:, None]
    lr = lam_re.astype(F32)
    li = lam_im.astype(F32)
    mag = jnp.exp(lr * dt)
    ang = li * dt
    ab_re = mag * jnp.cos(ang)
    ab_im = mag * jnp.sin(ang)
    den = lr * lr + li * li
    f_re = ((ab_re - 1.0) * lr + ab_im * li) / den
    f_im = (ab_im * lr - (ab_re - 1.0) * li) / den
    br = b_re.astype(F32)
    bi = b_im.astype(F32)
    bb_re = f_re[..., None] * br - f_im[..., None] * bi
    bb_im = f_re[..., None] * bi + f_im[..., None] * br
    cr = c_re.astype(F32)
    ci = c_im.astype(F32)
    tau = jnp.arange(L + 1, dtype=F32)[:, None, None]
    pr = jnp.exp(lr * dt * tau) * jnp.cos(li * dt * tau)
    pi = jnp.exp(lr * dt * tau) * jnp.sin(li * dt * tau)

    prs = pr[L - 1 - jnp.arange(L)]
    pis = pi[L - 1 - jnp.arange(L)]
    we_re = jnp.einsum('sgp,gpc->gscp', prs, bb_re) - jnp.einsum('sgp,gpc->gscp', pis, bb_im)
    we_im = jnp.einsum('sgp,gpc->gscp', prs, bb_im) + jnp.einsum('sgp,gpc->gscp', pis, bb_re)
    we_re = we_re.reshape(G, L * C, P)
    we_im = we_im.reshape(G, L * C, P)
    z = jnp.zeros_like(we_re[0::2])
    top = jnp.concatenate([we_re[0::2], z, we_im[0::2], z], axis=-1)
    bot = jnp.concatenate([z, we_re[1::2], z, we_im[1::2]], axis=-1)
    we = jnp.concatenate([top, bot], axis=1)

    m_re = pr[:L, :, :, None] * bb_re[None] - pi[:L, :, :, None] * bb_im[None]
    m_im = pr[:L, :, :, None] * bb_im[None] + pi[:L, :, :, None] * bb_re[None]
    kern = jnp.einsum('gop,tgpc->tgoc', cr, m_re) - jnp.einsum('gop,tgpc->tgoc', ci, m_im)
    s_idx = np.arange(L)[:, None]
    t_idx = np.arange(L)[None, :]
    lag = np.clip(t_idx - s_idx, 0, L - 1)
    causal = jnp.asarray((t_idx >= s_idx).astype(np.float32))
    tzf = kern[lag] * causal[:, :, None, None, None]
    tz = tzf.transpose(2, 0, 4, 1, 3).reshape(G, L * C, L * C)

    pr1 = pr[1:]
    pi1 = pi[1:]
    cp_re = jnp.einsum('gop,tgp->gpto', cr, pr1) - jnp.einsum('gop,tgp->gpto', ci, pi1)
    cp_im = -(jnp.einsum('gop,tgp->gpto', cr, pi1) + jnp.einsum('gop,tgp->gpto', ci, pr1))
    cp_re = cp_re.reshape(G, P, L * C)
    cp_im = cp_im.reshape(G, P, L * C)
    zc = jnp.zeros_like(cp_re[0::2])
    cp = jnp.concatenate([
        jnp.concatenate([cp_re[0::2], zc], axis=-1),
        jnp.concatenate([zc, cp_re[1::2]], axis=-1),
        jnp.concatenate([cp_im[0::2], zc], axis=-1),
        jnp.concatenate([zc, cp_im[1::2]], axis=-1)], axis=1)

    a_re = pr[L].reshape(G // 2, 2 * P)
    a_im = pi[L].reshape(G // 2, 2 * P)
    a16 = jnp.concatenate([a_re, a_im], axis=-1).reshape(1, G * 2 * P)
    dg = jnp.broadcast_to(d_skip.astype(F32).reshape(G, 1, C), (G, L, C)).reshape(1, G * L * C)
    return we.astype(BF16), tz.astype(BF16), cp.astype(BF16), a16, dg


def _compress_kernel(ak_ref, av_ref, w1k_ref, w1v_ref, w2k_ref, w2v_ref, c0k_ref, c0v_ref,
                     kc_ref, vct_ref):
    def run(a_ref, w1_ref, w2_ref, c0_ref):
        nc = a_ref.shape[1] // CMP_STRIDE
        a = jnp.concatenate([a_ref[0, pl.ds(l, nc, stride=CMP_STRIDE), :] for l in range(CMP_STRIDE)],
                            axis=1).astype(BF16)
        p1 = jnp.dot(a, w1_ref[0], preferred_element_type=F32)
        p2 = jnp.dot(a, w1_ref[1], preferred_element_type=F32)
        n = p1.shape[0]
        pre = p1 + pltpu.roll(p2, n - 1, 0) + c0_ref[...]
        out = jnp.dot(jax.nn.gelu(pre).astype(BF16), w2_ref[...], preferred_element_type=F32)
        row = lax.broadcasted_iota(jnp.int32, out.shape, 0)
        return jnp.where(row < n - 1, out, 0.0)

    kc_ref[0] = run(ak_ref, w1k_ref, w2k_ref, c0k_ref).astype(BF16)
    vct_ref[0] = run(av_ref, w1v_ref, w2v_ref, c0v_ref).T.astype(BF16)


def _compress(z3, w1k, w1v, w2k, w2v, c0k, c0v):
    b, seq, _ = z3.shape
    nc = seq // CMP_STRIDE
    c3 = lambda i: (0, 0, 0)
    c2 = lambda i: (0, 0)
    return pl.pallas_call(
        _compress_kernel,
        grid=(b,),
        in_specs=[pl.BlockSpec((1, seq, 128), lambda i: (i, 0, Z_KV // 128)),
                  pl.BlockSpec((1, seq, 128), lambda i: (i, 0, Z_KV // 128 + 1)),
                  pl.BlockSpec(w1k.shape, c3), pl.BlockSpec(w1v.shape, c3),
                  pl.BlockSpec(w2k.shape, c2), pl.BlockSpec(w2v.shape, c2),
                  pl.BlockSpec(c0k.shape, c2), pl.BlockSpec(c0v.shape, c2)],
        out_specs=[pl.BlockSpec((1, nc, 128), lambda i: (i, 0, 0)),
                   pl.BlockSpec((1, 128, nc), lambda i: (i, 0, 0))],
        out_shape=[jax.ShapeDtypeStruct((b, nc, 128), BF16),
                   jax.ShapeDtypeStruct((b, 128, nc), BF16)],
        compiler_params=_cparams("parallel"),
        name="compress",
    )(z3, z3, w1k, w1v, w2k, w2v, c0k, c0v)


def _compress_weights(pe, w1, w2):
    w1 = w1.astype(F32)
    half = (CMP_LEN // 2) * HEAD_DIM
    eye = jnp.eye(NSA_KV_HEADS, dtype=F32)

    def bd(w):
        w = w.reshape(CMP_STRIDE, HEAD_DIM, HEAD_DIM)
        return jnp.einsum('ldo,hk->lhdko', w, eye).reshape(NSA_KV_HEADS * half, NSA_KV_HEADS * HEAD_DIM)

    w1s = jnp.stack([bd(w1[:half]), bd(w1[half:])])
    z2 = jnp.zeros((HEAD_DIM, HEAD_DIM), F32)
    w2f = w2.astype(F32)
    w2s = jnp.concatenate([jnp.concatenate([w2f, z2], axis=1),
                           jnp.concatenate([z2, w2f], axis=1)], axis=0)
    c0 = pe.astype(F32).reshape(1, CMP_LEN * HEAD_DIM) @ w1
    c0 = jnp.concatenate([c0, c0], axis=1)
    return w1s.astype(BF16), w2s.astype(BF16), c0


def _t5_bucket_np(dist):
    n = np.maximum(dist, 0)
    max_exact = REL_BUCKETS // 2
    nf = np.maximum(n, 1).astype(np.float32)
    large = max_exact + (np.log(nf / max_exact) / math.log(REL_MAX_DIST / max_exact)
                         * (REL_BUCKETS - max_exact)).astype(np.int32)
    large = np.minimum(large, REL_BUCKETS - 1)
    return np.where(n < max_exact, n, large)


TAB_FAR, TAB_WIN_EDGE, TAB_NONE = 3, 4, 5


def _bias_tables(rel_table, seq):
    tbl = rel_table.astype(F32)
    heads = NSA_Q_HEADS

    def lookup(bucket):
        onehot = jnp.asarray(bucket[..., None] == np.arange(REL_BUCKETS)).astype(F32)
        return jnp.einsum('...k,kh->...h', onehot, tbl, precision=lax.Precision.HIGHEST)

    r = np.arange(Q_BLOCK)[None, :]
    c = np.arange(SEL_LEN)[:, None]
    near = lookup(_t5_bucket_np(64 * np.arange(3)[:, None, None] + (r - c)[None]))
    far = jnp.broadcast_to(tbl[REL_BUCKETS - 1], (SEL_LEN, Q_BLOCK, heads))
    none = jnp.full((SEL_LEN, Q_BLOCK, heads), NO_SLOT, F32)
    causal = jnp.asarray(r - c >= 0)[:, :, None]
    edge = jnp.asarray(c > r)[:, :, None]
    tab = jnp.stack([jnp.where(causal, near[0], none), near[1], near[2], far,
                     jnp.where(edge, far, none), none])
    tab = tab.transpose(0, 1, 3, 2).reshape(6, SEL_LEN, heads * Q_BLOCK)
    ncp = seq // CMP_STRIDE
    k = np.arange(2 * ncp + 4)[:, None]
    dist = r - CMP_STRIDE * (k - (ncp - 4)) - (CMP_LEN - 1)
    t2 = jnp.where(jnp.asarray(dist >= 0)[:, :, None], lookup(_t5_bucket_np(dist)), NO_SLOT)
    t2 = t2.transpose(0, 2, 1).reshape(2 * ncp + 4, heads * Q_BLOCK)
    t2 = jnp.stack([t2[:2 * ncp], t2[4:]])
    return tab, t2


def _nsa_kernel(q_ref, gl_ref, kc_ref, vct_ref, ks_ref, vs_ref, kw_ref, vw_ref, t2_ref, tab_ref,
                ov_ref, o_ref, sel_ref, s_ref, p_ref, acc_ref, accw_ref, vst_ref, vwt_ref, *, ncp, nw):
    @pl.when(pl.program_id(1) == 0)
    def _():
        for v_ref, vt_ref in ((vs_ref, vst_ref), (vw_ref, vwt_ref)):
            kt = vt_ref.shape[2]
            ones = jnp.ones((16, kt), F32)
            for u in range(vt_ref.shape[0]):
                vt = v_ref[0, u * kt:(u + 1) * kt, :].T
                vt_ref[u] = jnp.concatenate([vt[0:HEAD_DIM], ones, vt[HEAD_DIM:], ones], axis=0).astype(BF16)

    qi0 = pl.program_id(1) * nw
    cw = nw * NSA_GQA * Q_BLOCK
    ncol = NSA_KV_HEADS * cw
    vrows = HEAD_DIM + 16
    lane = lax.broadcasted_iota(jnp.int32, (Q_BLOCK, 128), 1)

    q = q_ref[...] * (HEAD_DIM ** -0.5)
    pieces = []
    for h in range(NSA_KV_HEADS):
        keep = (lane >= 64) if h == 1 else (lane < 64)
        for w in range(nw):
            for g in range(NSA_GQA):
                pieces.append(jnp.where(keep, q[w * 64:(w + 1) * 64, g * 128:(g + 1) * 128], 0.0))
    qpad = jnp.concatenate(pieces, axis=0).astype(BF16)

    col = lax.broadcasted_iota(jnp.int32, (1, ncol), 1)
    rcol = col % Q_BLOCK
    qcol = qi0 + (col // (NSA_GQA * Q_BLOCK)) % nw
    tvec = Q_BLOCK * qcol + rcol
    nt_dims = (((1,), (1,)), ((), ()))

    def per_head_cols(tiles):
        return jnp.concatenate([tiles[w][:, h * 256:(h + 1) * 256]
                                for h in range(NSA_KV_HEADS) for w in range(nw)], axis=1)

    def slot_terms(m, n_masked):
        m_fin = jnp.where(n_masked > 0, jnp.maximum(m, NEG_INF), m)
        return jnp.exp(m - m_fin), n_masked * jnp.exp(NEG_INF - m_fin)

    t2_tiles = []
    for w in range(nw):
        start = (ncp - 4) - 4 * (qi0 + w)
        par = (start // 4) % 2
        a0 = pl.multiple_of(start - 4 * par, 8)
        t2_tiles.append(t2_ref[par, pl.ds(a0, ncp), :])
    s = lax.dot_general(kc_ref[0], qpad, nt_dims, preferred_element_type=F32) + per_head_cols(t2_tiles)
    m = jnp.max(s, axis=0, keepdims=True)
    e = jnp.exp(s - m)
    n_valid = jnp.clip(jnp.right_shift(tvec - (CMP_STRIDE - 1), 4), 0, ncp - 1)
    scale, extra = slot_terms(m, (ncp - 1 - n_valid).astype(F32))
    p_c = e * (scale / (jnp.sum(e, axis=0, keepdims=True) * scale + extra))
    p_cb = p_c.astype(BF16)
    o_c = [jnp.dot(vct_ref[0, h * 64:(h + 1) * 64, :], p_cb[:, h * cw:(h + 1) * cw],
                   preferred_element_type=F32) for h in range(NSA_KV_HEADS)]

    lane_c = lax.broadcasted_iota(jnp.int32, (ncp, 128), 1)
    parts = []
    for w in range(nw):
        halves = []
        for h in range(NSA_KV_HEADS):
            base = h * cw + w * 256
            ph = p_c[:, base:base + 128] + p_c[:, base + 128:base + 256]
            halves.append(ph + pltpu.roll(ph, 64, 1))
        parts.append(jnp.where(lane_c < 64, halves[0], halves[1]))
    imp = jnp.dot(ov_ref[...], jnp.concatenate(parts, axis=1), preferred_element_type=F32,
                  precision=lax.Precision.HIGHEST)
    nsel = imp.shape[0]
    jidx = lax.broadcasted_iota(jnp.int32, (nsel, 1), 0)
    qv = qi0 + lax.broadcasted_iota(jnp.int32, (1, nw * 128), 1) // 128
    forced = jnp.logical_or(jidx == 0, jnp.logical_or(jidx == qv, jidx == qv - 1))
    avail = jidx <= qv
    cand = jnp.logical_and(avail, jnp.logical_not(forced))
    budget = SEL_TOPK - (1 + jnp.where(qv >= 1, 1, 0) + jnp.where(qv >= 2, 1, 0))
    rank = jnp.zeros((nsel, nw * 128), jnp.int32)
    for jp in range(1, nsel - 2):
        row = imp[jp:jp + 1, :]
        ge = jnp.where(row >= imp, 1, 0)
        gt = jnp.where(row > imp, 1, 0)
        is_cand = jnp.where(qv - 2 >= jp, 1, 0)
        rank = rank + jnp.where(jidx > jp, ge, gt) * is_cand
    sel = jnp.logical_or(jnp.logical_and(forced, avail), jnp.logical_and(cand, rank < budget))
    seladd = jnp.where(sel, 0.0, NO_SLOT)
    lane_s = lax.broadcasted_iota(jnp.int32, (nsel, 128), 1)
    for w in range(nw):
        tile = seladd[:, w * 128:(w + 1) * 128]
        rolled = pltpu.roll(tile, 64, 1)
        sel_ref[w] = jnp.where(lane_s < 64, tile, rolled)
        sel_ref[nw + w] = jnp.where(lane_s < 64, rolled, tile)

    def block_adds(blk, selected):
        tiles = []
        for w in range(nw):
            d = qi0 + w - blk
            if selected:
                e = jnp.where(d < 0, TAB_FAR, jnp.minimum(d, TAB_FAR))
            else:
                outside = jnp.logical_or(blk < 0, jnp.logical_or(d < 0, d > TAB_NONE))
                e = jnp.where(outside, TAB_NONE, d)
            tiles.append(tab_ref[e])
        add = per_head_cols(tiles)
        if selected:
            rows = [sel_ref[h * nw + w, pl.ds(blk, 1), :] for h in range(NSA_KV_HEADS) for w in range(nw)]
            add = add + jnp.concatenate([x for r in rows for x in (r, r)], axis=1)
        return add

    def softmax_tile(m_i, s, adds):
        sm = jnp.concatenate([s[k * 64:(k + 1) * 64] + adds[k] for k in range(len(adds))], axis=0)
        m_new = jnp.maximum(m_i, jnp.max(sm, axis=0, keepdims=True))
        return m_new, jnp.exp(m_i - m_new), jnp.exp(sm - m_new).astype(BF16)

    def add_values(ref, alpha, vt_tile, p):
        for h in range(NSA_KV_HEADS):
            ref[h] = alpha[:, h * cw:(h + 1) * cw] * ref[h] + jnp.dot(
                vt_tile[h * vrows:(h + 1) * vrows, :], p[:, h * cw:(h + 1) * cw], preferred_element_type=F32)

    m_init = jnp.full((1, ncol), NO_SLOT, F32)

    accw_ref[...] = jnp.zeros_like(accw_ref)
    m_w = m_init
    for i in range(nw // 2 + 2):
        jj = qi0 // 2 - 2 + i
        jc = jnp.maximum(jj, 0)
        r0 = pl.multiple_of(jc * 128, 128)
        s = lax.dot_general(kw_ref[0, pl.ds(r0, 128), :].astype(BF16), qpad, nt_dims,
                            preferred_element_type=F32)
        m_w, alpha, p = softmax_tile(m_w, s, [block_adds(2 * jj + half, False) for half in range(2)])
        add_values(accw_ref, alpha, vwt_ref[jc], p)

    n_tiles = (qi0 + nw - 1) // 4 + 1

    def masked_scores(u):
        r0 = pl.multiple_of(u * 256, 256)
        s = lax.dot_general(ks_ref[0, pl.ds(r0, 256), :].astype(BF16), qpad, nt_dims,
                            preferred_element_type=F32)
        tmax = None
        for k in range(4):
            sk = s[k * 64:(k + 1) * 64] + block_adds(4 * u + k, True)
            s_ref[k * 64:(k + 1) * 64, :] = sk
            kmax = jnp.max(sk, axis=0, keepdims=True)
            tmax = kmax if tmax is None else jnp.maximum(tmax, kmax)
        return tmax

    tmax0 = masked_scores(0)
    p_ref[...] = jnp.zeros_like(p_ref)
    acc_ref[...] = jnp.zeros_like(acc_ref)

    def sel_body(u, carry):
        m_i, alpha_prev, tmax = carry
        add_values(acc_ref, alpha_prev, vst_ref[jnp.maximum(u - 1, 0)], p_ref[...])
        m_new = jnp.maximum(m_i, tmax)
        p_ref[...] = jnp.exp(s_ref[...] - m_new).astype(BF16)
        tmax_next = masked_scores(jnp.minimum(u + 1, n_tiles - 1))
        return m_new, jnp.exp(m_i - m_new), tmax_next

    m_s, alpha_last, _ = lax.fori_loop(0, n_tiles, sel_body, (m_init, jnp.ones((1, ncol), F32), tmax0))
    add_values(acc_ref, alpha_last, vst_ref[n_tiles - 1], p_ref[...])

    nm_s = (SEL_LEN * jnp.maximum(SEL_TOPK - (qcol + 1), 0) + (SEL_LEN - 1 - rcol)).astype(F32)
    nm_w = (WINDOW + Q_BLOCK - jnp.minimum(tvec + 1, WINDOW)).astype(F32)

    def normalised(m, ref, n_masked):
        scale, extra = slot_terms(m, n_masked)
        out = []
        for h in range(NSA_KV_HEADS):
            sl = slice(h * cw, (h + 1) * cw)
            acc = ref[h]
            denom = acc[HEAD_DIM:HEAD_DIM + 1, :] * scale[:, sl] + extra[:, sl]
            out.append((scale[:, sl] / denom, acc[0:HEAD_DIM, :]))
        return out

    o_s = normalised(m_s, acc_ref, nm_s)
    o_w = normalised(m_w, accw_ref, nm_w)

    lane1 = lax.broadcasted_iota(jnp.int32, (1, 128), 1)
    gts = []
    for w in range(nw):
        g = jax.nn.sigmoid(gl_ref[w * 64:(w + 1) * 64, :])
        gts.append(jnp.concatenate([g, g], axis=0).T)

    def gate_vec(h, br):
        tiles = []
        for w in range(nw):
            for gg in range(2):
                c0 = (h * 4 + 2 * gg) * 3 + br
                c1 = (h * 4 + 2 * gg + 1) * 3 + br
                tiles.append(jnp.where(lane1 < 64, gts[w][c0:c0 + 1, :], gts[w][c1:c1 + 1, :]))
        return jnp.concatenate(tiles, axis=1)

    tot = []
    for h in range(NSA_KV_HEADS):
        t = gate_vec(h, 0) * o_c[h]
        t = t + (gate_vec(h, 1) * o_s[h][0]) * o_s[h][1]
        t = t + (gate_vec(h, 2) * o_w[h][0]) * o_w[h][1]
        tot.append(t)
    for w in range(nw):
        ot = jnp.concatenate([t[:, w * 256:(w + 1) * 256] for t in tot], axis=0).T
        o_ref[w * 64:(w + 1) * 64, :] = jnp.concatenate([ot[g * 64:(g + 1) * 64, :]
                                                         for g in range(NSA_GQA)], axis=1)


def _nsa(z, kc, vct, t2, tab, ov, b, seq, nw=4):
    nst = seq // (Q_BLOCK * nw)
    ncp = seq // CMP_STRIDE
    rows = Q_BLOCK * nw
    cw = nw * NSA_GQA * Q_BLOCK
    z3 = z.reshape(b, seq, Z_NSA)
    kv_col = lambda n: pl.BlockSpec((1, seq, 128), lambda i, j: (i, 0, Z_KV // 128 + n))
    return pl.pallas_call(
        functools.partial(_nsa_kernel, ncp=ncp, nw=nw),
        grid=(b, nst),
        in_specs=[pl.BlockSpec((rows, NSA_WIDTH), lambda i, j: (i * nst + j, Z_Q // NSA_WIDTH)),
                  pl.BlockSpec((rows, 128), lambda i, j: (i * nst + j, Z_G // 128)),
                  pl.BlockSpec((1, ncp, 128), lambda i, j: (i, 0, 0)),
                  pl.BlockSpec((1, 128, ncp), lambda i, j: (i, 0, 0)),
                  kv_col(2), kv_col(3), kv_col(4), kv_col(5),
                  pl.BlockSpec(t2.shape, lambda i, j: (0, 0, 0)),
                  pl.BlockSpec(tab.shape, lambda i, j: (0, 0, 0)),
                  pl.BlockSpec(ov.shape, lambda i, j: (0, 0))],
        out_specs=pl.BlockSpec((rows, NSA_WIDTH), lambda i, j: (i * nst + j, 0)),
        out_shape=jax.ShapeDtypeStruct((b * seq, NSA_WIDTH), F32),
        scratch_shapes=[pltpu.VMEM((NSA_KV_HEADS * nw, seq // SEL_LEN, 128), F32),
                        pltpu.VMEM((256, NSA_KV_HEADS * cw), F32),
                        pltpu.VMEM((256, NSA_KV_HEADS * cw), BF16),
                        pltpu.VMEM((NSA_KV_HEADS, HEAD_DIM + 16, cw), F32),
                        pltpu.VMEM((NSA_KV_HEADS, HEAD_DIM + 16, cw), F32),
                        pltpu.VMEM((seq // 256, NSA_KV_HEADS * (HEAD_DIM + 16), 256), BF16),
                        pltpu.VMEM((seq // 128, NSA_KV_HEADS * (HEAD_DIM + 16), 128), BF16)],
        compiler_params=_cparams("parallel", "arbitrary"),
        name="nsa",
    )(z, z, kc, vct, z3, z3, z3, z3, t2, tab, ov)


def _lru_kernel(x0_ref, x1_ref, g0_ref, g1_ref, cw_ref, cb_ref, wa_ref, ba_ref, wx_ref, bx_ref, sp_ref,
                o0_ref, o1_ref, xprev_ref, h_ref, a_scr, b_scr, *, nb, tc):
    @pl.when(pl.program_id(0) == 0)
    def _():
        xprev_ref[...] = jnp.zeros_like(xprev_ref)
        h_ref[...] = jnp.zeros_like(h_ref)

    x = jnp.concatenate([x0_ref[...], x1_ref[...]], axis=1)
    rows, width = x.shape
    grp = nb * OCT
    xm = jnp.concatenate([xprev_ref[...], x[:rows - grp]], axis=0)
    xprev_ref[...] = x[rows - grp:]
    x3 = x.reshape(rows // OCT, OCT, width)
    xm3 = xm.reshape(rows // OCT, OCT, width)
    tlo = lax.broadcasted_iota(jnp.int32, (1, OCT, 1), 1)
    cw = cw_ref[...]
    xc = cb_ref[...].reshape(1, 1, width) + cw[CONV_WIDTH - 1:CONV_WIDTH, :].reshape(1, 1, width) * x3
    for k in range(1, CONV_WIDTH):
        delayed = jnp.where(tlo >= k, pltpu.roll(x3, k, 1), pltpu.roll(xm3, k, 1))
        xc = xc + cw[CONV_WIDTH - 1 - k:CONV_WIDTH - k, :].reshape(1, 1, width) * delayed
    xc = xc.reshape(rows, width)
    xcb = xc.astype(BF16)
    gate_r = jax.nn.sigmoid(jnp.dot(xcb, wa_ref[...], preferred_element_type=F32) + ba_ref[...])
    gate_i = jax.nn.sigmoid(jnp.dot(xcb, wx_ref[...], preferred_element_type=F32) + bx_ref[...])
    log_a = -LRU_C * gate_r * sp_ref[...]
    a = jnp.exp(log_a)
    th = jnp.tanh(log_a)
    bt = jnp.sqrt(-2.0 * th / (1.0 - th)) * gate_i * xc
    for j in range(2):
        a_scr[j] = a[:, j * 128:(j + 1) * 128]
        b_scr[j] = bt[:, j * 128:(j + 1) * 128]

    def step(t, h):
        r0 = _octet_row(t, nb)
        new = []
        for j in range(2):
            hj = a_scr[j, pl.ds(r0, nb, stride=OCT), :] * h[j] + b_scr[j, pl.ds(r0, nb, stride=OCT), :]
            b_scr[j, pl.ds(r0, nb, stride=OCT), :] = hj
            new.append(hj)
        return tuple(new)

    h = lax.fori_loop(0, tc, step, (h_ref[0], h_ref[1]))
    for j, (g_ref, o_ref) in enumerate(((g0_ref, o0_ref), (g1_ref, o1_ref))):
        h_ref[j] = h[j]
        o_ref[...] = b_scr[j] * jax.nn.gelu(g_ref[...])


def _lru(x0, x1, g0, g1, cw, cb, wa, ba, wx, bx, sp, nb, tc=64):
    rows = x0.shape[0]
    r = tc * nb
    c2 = lambda i: (0, 0)
    slab = pl.BlockSpec((r, 128), lambda i: (i, 0))
    return pl.pallas_call(
        functools.partial(_lru_kernel, nb=nb, tc=tc),
        grid=(rows // r,),
        in_specs=[slab, slab, slab, slab,
                  pl.BlockSpec(cw.shape, c2), pl.BlockSpec(cb.shape, c2),
                  pl.BlockSpec(wa.shape, c2), pl.BlockSpec(ba.shape, c2),
                  pl.BlockSpec(wx.shape, c2), pl.BlockSpec(bx.shape, c2),
                  pl.BlockSpec(sp.shape, c2)],
        out_specs=[slab, slab],
        out_shape=[jax.ShapeDtypeStruct((rows, 128), F32)] * 2,
        scratch_shapes=[pltpu.VMEM((nb * OCT, LRU_WIDTH), F32), pltpu.VMEM((2, nb, 128), F32),
                        pltpu.VMEM((2, r, 128), F32), pltpu.VMEM((2, r, 128), F32)],
        compiler_params=_cparams("arbitrary"),
        name="lru",
    )(x0, x1, g0, g1, cw, cb, wa, ba, wx, bx, sp)


def _block_diag(w):
    h, i, j = w.shape
    eye = jnp.eye(h, dtype=w.dtype)
    return jnp.einsum('hij,hk->hikj', w, eye).reshape(h * i, h * j)


def _mix_ffn_kernel(h_ref, ys0_ref, ys1_ref, yn_ref, yl0_ref, yl1_ref, wglu_ref, wos_ref, won_ref, wol_ref,
                    g_ref, wg_ref, wu_ref, wd_ref, gf_ref, o_ref, *, final):
    tm = h_ref.shape[0]

    def rows(a_ref, b_ref):
        return jnp.concatenate([a_ref[...].reshape(tm, 128), b_ref[...].reshape(tm, 128)], axis=1)

    gl = jnp.dot(rows(ys0_ref, ys1_ref).astype(BF16), wglu_ref[...], preferred_element_type=F32)
    s5 = gl[:, :SSM_WIDTH] * jax.nn.sigmoid(gl[:, SSM_WIDTH:])
    acc = jnp.dot(s5.astype(BF16), wos_ref[...], preferred_element_type=F32)
    acc = acc + jnp.dot(yn_ref[...].astype(BF16), won_ref[...], preferred_element_type=F32)
    acc = acc + jnp.dot(rows(yl0_ref, yl1_ref).astype(BF16), wol_ref[...], preferred_element_type=F32)
    x = h_ref[...] + acc

    y = (x * lax.rsqrt(jnp.mean(x * x, axis=-1, keepdims=True) + RMS_EPS) * g_ref[...]).astype(BF16)
    a = jnp.dot(y, wg_ref[...], preferred_element_type=F32)
    b = jnp.dot(y, wu_ref[...], preferred_element_type=F32)
    m = (jax.nn.silu(a) * b).astype(BF16)
    out = x + jnp.dot(m, wd_ref[...], preferred_element_type=F32)
    if final:
        out = out * lax.rsqrt(jnp.mean(out * out, axis=-1, keepdims=True) + RMS_EPS) * gf_ref[...]
    o_ref[...] = out


def _mix_ffn(h, ys, yn, yl, wglu, wos, won, wol, g, wg, wu, wd, gf, final, b, seq, tm=256):
    d = h.shape[1]
    nt = seq // tm
    row = lambda w: pl.BlockSpec((tm, w), lambda b_, i: (b_ * nt + i, 0))
    const = lambda a: pl.BlockSpec(a.shape, lambda b_, i: (0, 0), pipeline_mode=pl.Buffered(1))
    slab4 = lambda s: s.reshape(seq // OCT, b, OCT, 128)
    return pl.pallas_call(
        functools.partial(_mix_ffn_kernel, final=final),
        grid=(b, nt),
        in_specs=[row(d), _slab_spec(tm, nt), _slab_spec(tm, nt), row(NSA_WIDTH),
                  _slab_spec(tm, nt), _slab_spec(tm, nt),
                  const(wglu), const(wos), const(won), const(wol),
                  const(g), const(wg), const(wu), const(wd), const(gf)],
        out_specs=row(d),
        out_shape=jax.ShapeDtypeStruct((b * seq, d), F32),
        compiler_params=_cparams("parallel", "parallel"),
        name="mix_ffn",
    )(h, slab4(ys[0]), slab4(ys[1]), yn, slab4(yl[0]), slab4(yl[1]), wglu, wos, won, wol, g, wg, wu, wd, gf)


def _head_interleave(w, axis):
    shape = w.shape
    split = shape[:axis] + (NSA_KV_HEADS, NSA_GQA, HEAD_DIM) + shape[axis + 1:]
    return jnp.swapaxes(w.reshape(split), axis, axis + 1).reshape(shape)


def _prep_w_in(w_in):
    o1 = SSM_WIDTH
    o2 = o1 + NSA_WIDTH
    o3 = o2 + 6 * NSA_KV_WIDTH
    o4 = o3 + 3 * NSA_Q_HEADS
    wq = _head_interleave(w_in[:, o1:o2], 1)
    pad = jnp.zeros((w_in.shape[0], 128 - 3 * NSA_Q_HEADS), w_in.dtype)
    w = jnp.concatenate([wq, w_in[:, o2:o3], w_in[:, o3:o4], pad, w_in[:, :o1], w_in[:, o4:]], axis=1)
    return w.astype(BF16)


def _s5_mixer(slabs, b, p):
    we, tz, cp, a16, dg = _s5_tables(p['s5_lam_re'], p['s5_lam_im'], p['s5_log_dt'], p['s5_b_re'],
                                     p['s5_b_im'], p['s5_c_re'], p['s5_c_im'], p['s5_d'])
    return _s5(slabs[0], slabs[1], we, tz, cp, a16, dg, nb=b)


def _nsa_mixer(z, b, seq, bias_tabs, ov, p):
    w1k, w2k, c0k = _compress_weights(p['nsa_pe_k'], p['nsa_w1_k'], p['nsa_w2_k'])
    w1v, w2v, c0v = _compress_weights(p['nsa_pe_v'], p['nsa_w1_v'], p['nsa_w2_v'])
    kc, vct = _compress(z.reshape(b, seq, Z_NSA), w1k, w1v, w2k, w2v, c0k, c0v)
    tab, t2 = bias_tabs
    return _nsa(z, kc, vct, t2, tab, ov, b, seq)


def _lru_mixer(slabs, b, p):
    sp = jax.nn.softplus(-p['lru_lam'].astype(F32)).reshape(1, -1)
    return _lru(slabs[0], slabs[1], slabs[2], slabs[3],
                p['lru_conv_w'].astype(F32), p['lru_conv_b'].astype(F32).reshape(1, -1),
                _block_diag(p['lru_w_a']).astype(BF16), p['lru_b_a'].astype(F32).reshape(1, -1),
                _block_diag(p['lru_w_x']).astype(BF16), p['lru_b_x'].astype(F32).reshape(1, -1),
                sp, nb=b)


def _layer(h2, b, seq, bias_tabs, ov, p, final, norm_final):
    z, slabs = _inproj(h2, p['norm_mix'].reshape(1, -1), _prep_w_in(p['w_in']), b, seq)
    ys = _s5_mixer(slabs[0:2], b, p)
    yn = _nsa_mixer(z, b, seq, bias_tabs, ov, p)
    yl = _lru_mixer(slabs[2:6], b, p)
    w_out = p['w_out']
    wos = w_out[:SSM_WIDTH].astype(BF16)
    won = _head_interleave(w_out[SSM_WIDTH:SSM_WIDTH + NSA_WIDTH], 0).astype(BF16)
    wol = w_out[SSM_WIDTH + NSA_WIDTH:].astype(BF16)
    return _mix_ffn(h2, ys, yn, yl, p['s5_w_glu'].astype(BF16), wos, won, wol,
                    p['norm_ffn'].reshape(1, -1), p['w_gate'].astype(BF16), p['w_up'].astype(BF16),
                    p['w_down'].astype(BF16), norm_final.reshape(1, -1), final, b, seq)


def _overlap_t(seq):
    ncp = seq // CMP_STRIDE
    nsel = seq // SEL_LEN
    cs = np.arange(ncp) * CMP_STRIDE
    ss = np.arange(nsel) * SEL_LEN
    ovl = (cs[None, :] < ss[:, None] + SEL_LEN) & (ss[:, None] < cs[None, :] + CMP_LEN)
    ovl[:, ncp - 1] = False
    return jnp.asarray(ovl.astype(np.float32))


def kernel(x, rel_bias_table, norm_mix, w_in, w_out, s5_lam_re, s5_lam_im, s5_log_dt, s5_b_re, s5_b_im, s5_c_re, s5_c_im, s5_d, s5_w_glu, nsa_pe_k, nsa_w1_k, nsa_w2_k, nsa_pe_v, nsa_w1_v, nsa_w2_v, lru_conv_w, lru_conv_b, lru_w_a, lru_b_a, lru_w_x, lru_b_x, lru_lam, norm_ffn, w_gate, w_up, w_down, norm_final):
    b, seq, d = x.shape
    depth = norm_mix.shape[0]
    per_layer = dict(norm_mix=norm_mix, w_in=w_in, w_out=w_out, s5_lam_re=s5_lam_re, s5_lam_im=s5_lam_im,
                     s5_log_dt=s5_log_dt, s5_b_re=s5_b_re, s5_b_im=s5_b_im, s5_c_re=s5_c_re,
                     s5_c_im=s5_c_im, s5_d=s5_d, s5_w_glu=s5_w_glu, nsa_pe_k=nsa_pe_k, nsa_w1_k=nsa_w1_k,
                     nsa_w2_k=nsa_w2_k, nsa_pe_v=nsa_pe_v, nsa_w1_v=nsa_w1_v, nsa_w2_v=nsa_w2_v,
                     lru_conv_w=lru_conv_w, lru_conv_b=lru_conv_b, lru_w_a=lru_w_a, lru_b_a=lru_b_a,
                     lru_w_x=lru_w_x, lru_b_x=lru_b_x, lru_lam=lru_lam, norm_ffn=norm_ffn, w_gate=w_gate,
                     w_up=w_up, w_down=w_down)
    bias_tabs = _bias_tables(rel_bias_table, seq)
    ov = _overlap_t(seq)
    h2 = x.reshape(b * seq, d)
    for l in range(depth):
        p = {k: v[l] for k, v in per_layer.items()}
        h2 = _layer(h2, b, seq, bias_tabs, ov, p, l == depth - 1, norm_final)
    return h2.reshape(b, seq, d)
```

```python
import functools
import math

import numpy as np
import jax
import jax.numpy as jnp
from jax import lax
from jax.experimental import pallas as pl
from jax.experimental.pallas import tpu as pltpu

F32 = jnp.float32
BF16 = jnp.bfloat16

D_MODEL = 1024
SSM_WIDTH = 256
NSA_WIDTH = 512
LRU_WIDTH = 256
S5_GROUP = 16
S5_GROUPS = 16
S5_STATE = 64
HEAD_DIM = 64
NSA_Q_HEADS = 8
NSA_KV_HEADS = 2
NSA_GQA = 4
NSA_KV_WIDTH = 128
CMP_LEN = 32
CMP_STRIDE = 16
SEL_LEN = 64
SEL_TOPK = 8
WINDOW = 256
Q_BLOCK = 64
LRU_HEADS = 4
LRU_HEAD_DIM = 64
CONV_WIDTH = 4
LRU_C = 8.0
REL_BUCKETS = 32
REL_MAX_DIST = 128
D_FF = 2816
NEG_INF = -1e9
LOG2E = 1.4426950408889634
RMS_EPS = 1e-6

NO_SLOT = -3e38

S5_CHUNK = 16
Z_Q, Z_KV, Z_G, Z_NSA = 0, 512, 1280, 1408
N_SLAB = 6
OCT = 8

VMEM_LIMIT = 56 * 1024 * 1024


def _cparams(*sem):
    return pltpu.CompilerParams(dimension_semantics=sem, vmem_limit_bytes=VMEM_LIMIT)


def _inproj_kernel(x_ref, g_ref, w_ref, o_ref, *slab_refs):
    x = x_ref[...]
    y = x * lax.rsqrt(jnp.mean(x * x, axis=-1, keepdims=True) + RMS_EPS) * g_ref[...]
    res = jnp.dot(y.astype(BF16), w_ref[...], preferred_element_type=F32)
    o_ref[...] = res[:, :Z_NSA]
    for m, s_ref in enumerate(slab_refs):
        s_ref[...] = res[:, Z_NSA + m * 128:Z_NSA + (m + 1) * 128].reshape(s_ref.shape)


def _slab_spec(tm, nt):
    return pl.BlockSpec((tm // OCT, None, OCT, 128), lambda b, i: (i, b, 0, 0))


def _inproj(x2, g, w, b, seq, tm=512):
    d = x2.shape[1]
    n = w.shape[1]
    nt = seq // tm
    slab = jax.ShapeDtypeStruct((seq // OCT, b, OCT, 128), F32)
    outs = pl.pallas_call(
        _inproj_kernel,
        grid=(b, nt),
        in_specs=[pl.BlockSpec((tm, d), lambda b_, i: (b_ * nt + i, 0)),
                  pl.BlockSpec((1, d), lambda b_, i: (0, 0)),
                  pl.BlockSpec((d, n), lambda b_, i: (0, 0))],
        out_specs=[pl.BlockSpec((tm, Z_NSA), lambda b_, i: (b_ * nt + i, 0))] + [_slab_spec(tm, nt)] * N_SLAB,
        out_shape=[jax.ShapeDtypeStruct((b * seq, Z_NSA), F32)] + [slab] * N_SLAB,
        compiler_params=_cparams("parallel", "parallel"),
        name="inproj",
    )(x2, g, w)
    return outs[0], [s.reshape(b * seq, 128) for s in outs[1:]]


def _lane_regroup(x):
    r = x.shape[0]
    cols = [x[:, v * 128:(v + 1) * 128] for v in range(32)]
    piece = lax.broadcasted_iota(jnp.int32, (r, 128), 1) // 16
    out = [None] * 32
    for ah in range(2):
        for bh in range(2):
            vs = [cols[2 * (8 * ah + i) + bh] for i in range(8)]
            for stage in (4, 2, 1):
                upper = (piece & stage) != 0
                for i in range(8):
                    if i & stage:
                        continue
                    lo, hi = vs[i], vs[i + stage]
                    vs[i] = jnp.where(upper, pltpu.roll(hi, stage * 16, 1), lo)
                    vs[i + stage] = jnp.where(upper, hi, pltpu.roll(lo, 128 - stage * 16, 1))
            for j in range(8):
                out[2 * (8 * bh + j) + ah] = vs[j]
    return jnp.concatenate(out, axis=1)


def _octet_row(t, nb):
    return (t // OCT) * (nb * OCT) + t % OCT


def _s5_kernel(u0_ref, u1_ref, we_ref, tz_ref, cp_ref, a_ref, d_ref, o0_ref, o1_ref, e_ref, carry_ref,
               *, nb, kc):
    @pl.when(pl.program_id(0) == 0)
    def _():
        carry_ref[...] = jnp.zeros_like(carry_ref)

    u_refs = (u0_ref, u1_ref)
    u = jnp.concatenate(
        [jnp.concatenate([u_refs[j][pl.ds(_octet_row(k * S5_CHUNK + s, nb), nb, stride=OCT), :]
                          for s in range(S5_CHUNK) for j in range(2)], axis=1)
         for k in range(kc)], axis=0)
    ug = _lane_regroup(u)
    ugb = ug.astype(BF16)
    npair = S5_GROUPS // 2
    for p in range(npair):
        e_ref[:, p * 256:(p + 1) * 256] = jnp.dot(ugb[:, p * 512:(p + 1) * 512], we_ref[p],
                                                   preferred_element_type=F32)

    a = a_ref[...]

    def step(k, carry):
        r0 = pl.multiple_of(k * nb, nb)
        e = e_ref[pl.ds(r0, nb), :]
        e_ref[pl.ds(r0, nb), :] = carry
        new = []
        for p in range(npair):
            ar = a[:, p * 256:p * 256 + 128]
            ai = a[:, p * 256 + 128:(p + 1) * 256]
            cr = carry[:, p * 256:p * 256 + 128]
            ci = carry[:, p * 256 + 128:(p + 1) * 256]
            new.append(ar * cr - ai * ci + e[:, p * 256:p * 256 + 128])
            new.append(ar * ci + ai * cr + e[:, p * 256 + 128:(p + 1) * 256])
        return jnp.concatenate(new, axis=1)

    carry_ref[...] = lax.fori_loop(0, kc, step, carry_ref[...])

    eb = e_ref[...].astype(BF16)
    ys = []
    for p in range(npair):
        yc = jnp.dot(eb[:, p * 256:(p + 1) * 256], cp_ref[p], preferred_element_type=F32)
        y0 = jnp.dot(ugb[:, (2 * p) * 256:(2 * p + 1) * 256], tz_ref[2 * p], preferred_element_type=F32)
        y1 = jnp.dot(ugb[:, (2 * p + 1) * 256:(2 * p + 2) * 256], tz_ref[2 * p + 1],
                     preferred_element_type=F32)
        ys.append(yc + jnp.concatenate([y0, y1], axis=1))
    y = jnp.concatenate(ys, axis=1) + d_ref[...] * ug
    y = jax.nn.gelu(y)
    y = _lane_regroup(y)
    o_refs = (o0_ref, o1_ref)
    for k in range(kc):
        for s in range(S5_CHUNK):
            for j in range(2):
                lo = s * SSM_WIDTH + j * 128
                o_refs[j][pl.ds(_octet_row(k * S5_CHUNK + s, nb), nb, stride=OCT), :] = \
                    y[k * nb:(k + 1) * nb, lo:lo + 128]


def _s5(u0, u1, we, tz, cp, a16, dg, nb, kc=16):
    rows = u0.shape[0]
    r = kc * S5_CHUNK * nb
    const3 = lambda i: (0, 0, 0)
    slab = pl.BlockSpec((r, 128), lambda i: (i, 0))
    return pl.pallas_call(
        functools.partial(_s5_kernel, nb=nb, kc=kc),
        grid=(rows // r,),
        in_specs=[slab, slab,
                  pl.BlockSpec(we.shape, const3),
                  pl.BlockSpec(tz.shape, const3),
                  pl.BlockSpec(cp.shape, const3),
                  pl.BlockSpec(a16.shape, lambda i: (0, 0)),
                  pl.BlockSpec(dg.shape, lambda i: (0, 0))],
        out_specs=[slab, slab],
        out_shape=[jax.ShapeDtypeStruct((rows, 128), F32)] * 2,
        scratch_shapes=[pltpu.VMEM((kc * nb, 2048), F32), pltpu.VMEM((nb, 2048), F32)],
        compiler_params=_cparams("arbitrary"),
        name="s5",
    )(u0, u1, we, tz, cp, a16, dg)


def _s5_tables(lam_re, lam_im, log_dt, b_re, b_im, c_re, c_im, d_skip):
    L = S5_CHUNK
    G, P, C = S5_GROUPS, S5_STATE, S5_GROUP
    dt = jnp.exp(log_dt.astype(F32))[:, None]
    lr = lam_re.astype(F32)
    li = lam_im.astype(F32)
    mag = jnp.exp(lr * dt)
    ang = li * dt
    ab_re = mag * jnp.cos(ang)
    ab_im = mag * jnp.sin(ang)
    den = lr * lr + li * li
    f_re = ((ab_re - 1.0) * lr + ab_im * li) / den
    f_im = (ab_im * lr - (ab_re - 1.0) * li) / den
    br = b_re.astype(F32)
    bi = b_im.astype(F32)
    bb_re = f_re[..., None] * br - f_im[..., None] * bi
    bb_im = f_re[..., None] * bi + f_im[..., None] * br
    cr = c_re.astype(F32)
    ci = c_im.astype(F32)
    tau = jnp.arange(L + 1, dtype=F32)[:, None, None]
    pr = jnp.exp(lr * dt * tau) * jnp.cos(li * dt * tau)
    pi = jnp.exp(lr * dt * tau) * jnp.sin(li * dt * tau)

    prs = pr[L - 1 - jnp.arange(L)]
    pis = pi[L - 1 - jnp.arange(L)]
    we_re = jnp.einsum('sgp,gpc->gscp', prs, bb_re) - jnp.einsum('sgp,gpc->gscp', pis, bb_im)
    we_im = jnp.einsum('sgp,gpc->gscp', prs, bb_im) + jnp.einsum('sgp,gpc->gscp', pis, bb_re)
    we_re = we_re.reshape(G, L * C, P)
    we_im = we_im.reshape(G, L * C, P)
    z = jnp.zeros_like(we_re[0::2])
    top = jnp.concatenate([we_re[0::2], z, we_im[0::2], z], axis=-1)
    bot = jnp.concatenate([z, we_re[1::2], z, we_im[1::2]], axis=-1)
    we = jnp.concatenate([top, bot], axis=1)

    m_re = pr[:L, :, :, None] * bb_re[None] - pi[:L, :, :, None] * bb_im[None]
    m_im = pr[:L, :, :, None] * bb_im[None] + pi[:L, :, :, None] * bb_re[None]
    kern = jnp.einsum('gop,tgpc->tgoc', cr, m_re) - jnp.einsum('gop,tgpc->tgoc', ci, m_im)
    s_idx = np.arange(L)[:, None]
    t_idx = np.arange(L)[None, :]
    lag = np.clip(t_idx - s_idx, 0, L - 1)
    causal = jnp.asarray((t_idx >= s_idx).astype(np.float32))
    tzf = kern[lag] * causal[:, :, None, None, None]
    tz = tzf.transpose(2, 0, 4, 1, 3).reshape(G, L * C, L * C)

    pr1 = pr[1:]
    pi1 = pi[1:]
    cp_re = jnp.einsum('gop,tgp->gpto', cr, pr1) - jnp.einsum('gop,tgp->gpto', ci, pi1)
    cp_im = -(jnp.einsum('gop,tgp->gpto', cr, pi1) + jnp.einsum('gop,tgp->gpto', ci, pr1))
    cp_re = cp_re.reshape(G, P, L * C)
    cp_im = cp_im.reshape(G, P, L * C)
    zc = jnp.zeros_like(cp_re[0::2])
    cp = jnp.concatenate([
        jnp.concatenate([cp_re[0::2], zc], axis=-1),
        jnp.concatenate([zc, cp_re[1::2]], axis=-1),
        jnp.concatenate([cp_im[0::2], zc], axis=-1),
        jnp.concatenate([zc, cp_im[1::2]], axis=-1)], axis=1)

    a_re = pr[L].reshape(G // 2, 2 * P)
    a_im = pi[L].reshape(G // 2, 2 * P)
    a16 = jnp.concatenate([a_re, a_im], axis=-1).reshape(1, G * 2 * P)
    dg = jnp.broadcast_to(d_skip.astype(F32).reshape(G, 1, C), (G, L, C)).reshape(1, G * L * C)
    return we.astype(BF16), tz.astype(BF16), cp.astype(BF16), a16, dg


def _compress_kernel(ak_ref, av_ref, w1k_ref, w1v_ref, w2k_ref, w2v_ref, c0k_ref, c0v_ref,
                     kc_ref, vct_ref):
    def run(a_ref, w1_ref, w2_ref, c0_ref):
        nc = a_ref.shape[1] // CMP_STRIDE
        a = jnp.concatenate([a_ref[0, pl.ds(l, nc, stride=CMP_STRIDE), :] for l in range(CMP_STRIDE)],
                            axis=1).astype(BF16)
        p1 = jnp.dot(a, w1_ref[0], preferred_element_type=F32)
        p2 = jnp.dot(a, w1_ref[1], preferred_element_type=F32)
        n = p1.shape[0]
        pre = p1 + pltpu.roll(p2, n - 1, 0) + c0_ref[...]
        out = jnp.dot(jax.nn.gelu(pre).astype(BF16), w2_ref[...], preferred_element_type=F32)
        row = lax.broadcasted_iota(jnp.int32, out.shape, 0)
        return jnp.where(row < n - 1, out, 0.0)

    kc_ref[0] = run(ak_ref, w1k_ref, w2k_ref, c0k_ref).astype(BF16)
    vct_ref[0] = run(av_ref, w1v_ref, w2v_ref, c0v_ref).T.astype(BF16)


def _compress(z3, w1k, w1v, w2k, w2v, c0k, c0v):
    b, seq, _ = z3.shape
    nc = seq // CMP_STRIDE
    c3 = lambda i: (0, 0, 0)
    c2 = lambda i: (0, 0)
    return pl.pallas_call(
        _compress_kernel,
        grid=(b,),
        in_specs=[pl.BlockSpec((1, seq, 128), lambda i: (i, 0, Z_KV // 128)),
                  pl.BlockSpec((1, seq, 128), lambda i: (i, 0, Z_KV // 128 + 1)),
                  pl.BlockSpec(w1k.shape, c3), pl.BlockSpec(w1v.shape, c3),
                  pl.BlockSpec(w2k.shape, c2), pl.BlockSpec(w2v.shape, c2),
                  pl.BlockSpec(c0k.shape, c2), pl.BlockSpec(c0v.shape, c2)],
        out_specs=[pl.BlockSpec((1, nc, 128), lambda i: (i, 0, 0)),
                   pl.BlockSpec((1, 128, nc), lambda i: (i, 0, 0))],
        out_shape=[jax.ShapeDtypeStruct((b, nc, 128), BF16),
                   jax.ShapeDtypeStruct((b, 128, nc), BF16)],
        compiler_params=_cparams("parallel"),
        name="compress",
    )(z3, z3, w1k, w1v, w2k, w2v, c0k, c0v)


def _compress_weights(pe, w1, w2):
    w1 = w1.astype(F32)
    half = (CMP_LEN // 2) * HEAD_DIM
    eye = jnp.eye(NSA_KV_HEADS, dtype=F32)

    def bd(w):
        w = w.reshape(CMP_STRIDE, HEAD_DIM, HEAD_DIM)
        return jnp.einsum('ldo,hk->lhdko', w, eye).reshape(NSA_KV_HEADS * half, NSA_KV_HEADS * HEAD_DIM)

    w1s = jnp.stack([bd(w1[:half]), bd(w1[half:])])
    z2 = jnp.zeros((HEAD_DIM, HEAD_DIM), F32)
    w2f = w2.astype(F32)
    w2s = jnp.concatenate([jnp.concatenate([w2f, z2], axis=1),
                           jnp.concatenate([z2, w2f], axis=1)], axis=0)
    c0 = pe.astype(F32).reshape(1, CMP_LEN * HEAD_DIM) @ w1
    c0 = jnp.concatenate([c0, c0], axis=1)
    return w1s.astype(BF16), w2s.astype(BF16), c0


def _t5_bucket_np(dist):
    n = np.maximum(dist, 0)
    max_exact = REL_BUCKETS // 2
    nf = np.maximum(n, 1).astype(np.float32)
    large = max_exact + (np.log(nf / max_exact) / math.log(REL_MAX_DIST / max_exact)
                         * (REL_BUCKETS - max_exact)).astype(np.int32)
    large = np.minimum(large, REL_BUCKETS - 1)
    return np.where(n < max_exact, n, large)


TAB_FAR, TAB_WIN_EDGE, TAB_NONE = 3, 4, 5


def _bias_tables(rel_table, seq):
    tbl = rel_table.astype(F32)
    heads = NSA_Q_HEADS

    def lookup(bucket):
        onehot = jnp.asarray(bucket[..., None] == np.arange(REL_BUCKETS)).astype(F32)
        return jnp.einsum('...k,kh->...h', onehot, tbl, precision=lax.Precision.HIGHEST) * LOG2E

    r = np.arange(Q_BLOCK)[None, :]
    c = np.arange(SEL_LEN)[:, None]
    near = lookup(_t5_bucket_np(64 * np.arange(3)[:, None, None] + (r - c)[None]))
    far = jnp.broadcast_to(tbl[REL_BUCKETS - 1] * LOG2E, (SEL_LEN, Q_BLOCK, heads))
    none = jnp.full((SEL_LEN, Q_BLOCK, heads), NO_SLOT, F32)
    causal = jnp.asarray(r - c >= 0)[:, :, None]
    edge = jnp.asarray(c > r)[:, :, None]
    tab = jnp.stack([jnp.where(causal, near[0], none), near[1], near[2], far,
                     jnp.where(edge, far, none), none])
    tab = tab.transpose(0, 1, 3, 2).reshape(6, SEL_LEN, heads * Q_BLOCK)
    ncp = seq // CMP_STRIDE
    k = np.arange(2 * ncp + 4)[:, None]
    dist = r - CMP_STRIDE * (k - (ncp - 4)) - (CMP_LEN - 1)
    t2 = jnp.where(jnp.asarray(dist >= 0)[:, :, None], lookup(_t5_bucket_np(dist)), NO_SLOT)
    t2 = t2.transpose(0, 2, 1).reshape(2 * ncp + 4, heads * Q_BLOCK)
    t2 = jnp.stack([t2[:2 * ncp], t2[4:]])
    return tab, t2


def _nsa_kernel(q_ref, gl_ref, kc_ref, vct_ref, ks_ref, vs_ref, kw_ref, vw_ref, t2_ref, tab_ref,
                ov_ref, o_ref, s_ref, p_ref, acc_ref, accw_ref, vst_ref, vwt_ref, *, ncp, nw):
    @pl.when(pl.program_id(1) == 0)
    def _():
        for v_ref, vt_ref in ((vs_ref, vst_ref), (vw_ref, vwt_ref)):
            kt = vt_ref.shape[2]
            ones = jnp.ones((16, kt), F32)
            for u in range(vt_ref.shape[0]):
                vt = v_ref[0, u * kt:(u + 1) * kt, :].T
                vt_ref[u] = jnp.concatenate([vt[0:HEAD_DIM], ones, vt[HEAD_DIM:], ones], axis=0).astype(BF16)

    qi0 = pl.program_id(1) * nw
    cw = nw * NSA_GQA * Q_BLOCK
    ncol = NSA_KV_HEADS * cw
    vrows = HEAD_DIM + 16
    lane = lax.broadcasted_iota(jnp.int32, (Q_BLOCK, 128), 1)

    q = q_ref[...] * (HEAD_DIM ** -0.5 * LOG2E)
    pieces = []
    for h in range(NSA_KV_HEADS):
        keep = (lane >= 64) if h == 1 else (lane < 64)
        for w in range(nw):
            for g in range(NSA_GQA):
                pieces.append(jnp.where(keep, q[w * 64:(w + 1) * 64, g * 128:(g + 1) * 128], 0.0))
    qpad = jnp.concatenate(pieces, axis=0).astype(BF16)

    col = lax.broadcasted_iota(jnp.int32, (1, ncol), 1)
    rcol = col % Q_BLOCK
    qcol = qi0 + (col // (NSA_GQA * Q_BLOCK)) % nw
    tvec = Q_BLOCK * qcol + rcol
    nt_dims = (((1,), (1,)), ((), ()))

    def per_head_cols(tiles):
        return jnp.concatenate([tiles[w][:, h * 256:(h + 1) * 256]
                                for h in range(NSA_KV_HEADS) for w in range(nw)], axis=1)

    def slot_terms(m, n_masked):
        m_fin = jnp.where(n_masked > 0, jnp.maximum(m, NEG_INF * LOG2E), m)
        return jnp.exp2(m - m_fin), n_masked * jnp.exp2(NEG_INF * LOG2E - m_fin)

    t2_tiles = []
    for w in range(nw):
        start = (ncp - 4) - 4 * (qi0 + w)
        par = (start // 4) % 2
        a0 = pl.multiple_of(start - 4 * par, 8)
        t2_tiles.append(t2_ref[par, pl.ds(a0, ncp), :])
    s = lax.dot_general(kc_ref[0], qpad, nt_dims, preferred_element_type=F32) + per_head_cols(t2_tiles)
    m = jnp.max(s, axis=0, keepdims=True)
    e = jnp.exp2(s - m)
    n_valid = jnp.clip(jnp.right_shift(tvec - (CMP_STRIDE - 1), 4), 0, ncp - 1)
    scale, extra = slot_terms(m, (ncp - 1 - n_valid).astype(F32))
    p_c = e * (scale / (jnp.sum(e, axis=0, keepdims=True) * scale + extra))
    p_cb = p_c.astype(BF16)
    o_c = [jnp.dot(vct_ref[0, h * 64:(h + 1) * 64, :], p_cb[:, h * cw:(h + 1) * cw],
                   preferred_element_type=F32) for h in range(NSA_KV_HEADS)]

    lane_c = lax.broadcasted_iota(jnp.int32, (ncp, 128), 1)
    parts = []
    for w in range(nw):
        halves = []
        for h in range(NSA_KV_HEADS):
            base = h * cw + w * 256
            ph = p_c[:, base:base + 128] + p_c[:, base + 128:base + 256]
            halves.append(ph + pltpu.roll(ph, 64, 1))
        parts.append(jnp.where(lane_c < 64, halves[0], halves[1]))
    imp = jnp.dot(ov_ref[...], jnp.concatenate(parts, axis=1), preferred_element_type=F32,
                  precision=lax.Precision.HIGHEST)
    nsel = imp.shape[0]
    jidx = lax.broadcasted_iota(jnp.int32, (nsel, 1), 0)
    qv = qi0 + lax.broadcasted_iota(jnp.int32, (1, nw * 128), 1) // 128
    forced = jnp.logical_or(jidx == 0, jnp.logical_or(jidx == qv, jidx == qv - 1))
    avail = jidx <= qv
    cand = jnp.logical_and(avail, jnp.logical_not(forced))
    budget = SEL_TOPK - (1 + jnp.where(qv >= 1, 1, 0) + jnp.where(qv >= 2, 1, 0))
    rank = jnp.zeros((nsel, nw * 128), jnp.int32)
    for jp in range(1, nsel - 2):
        row = imp[jp:jp + 1, :]
        ge = jnp.where(row >= imp, 1, 0)
        gt = jnp.where(row > imp, 1, 0)
        is_cand = jnp.where(qv - 2 >= jp, 1, 0)
        rank = rank + jnp.where(jidx > jp, ge, gt) * is_cand
    sel = jnp.logical_or(jnp.logical_and(forced, avail), jnp.logical_and(cand, rank < budget))
    seladd = jnp.where(sel, 0.0, NO_SLOT)
    pad = jnp.zeros((128 - nsel, 128), F32)
    mask_rows = []
    for w in range(nw):
        mask_rows.append(jnp.concatenate([seladd[:, w * 128:(w + 1) * 128], pad], axis=0).T)
    sel_cols = jnp.concatenate([mask_rows[w][h * 64:(h + 1) * 64, :] for h in range(NSA_KV_HEADS)
                                for w in range(nw) for _ in range(NSA_GQA)], axis=0)
    qsel = jnp.concatenate([qpad, sel_cols.astype(BF16)], axis=1)

    def block_adds(blk, selected):
        tiles = []
        for w in range(nw):
            d = qi0 + w - blk
            if selected:
                e = jnp.where(d < 0, TAB_FAR, jnp.minimum(d, TAB_FAR))
            else:
                outside = jnp.logical_or(blk < 0, jnp.logical_or(d < 0, d > TAB_NONE))
                e = jnp.where(outside, TAB_NONE, d)
            tiles.append(tab_ref[e])
        return per_head_cols(tiles)

    def softmax_tile(m_i, s, adds):
        sm = jnp.concatenate([s[k * 64:(k + 1) * 64] + adds[k] for k in range(len(adds))], axis=0)
        m_new = jnp.maximum(m_i, jnp.max(sm, axis=0, keepdims=True))
        return m_new, jnp.exp2(m_i - m_new), jnp.exp2(sm - m_new).astype(BF16)

    def add_values(ref, alpha, vt_tile, p):
        for h in range(NSA_KV_HEADS):
            ref[h] = alpha[:, h * cw:(h + 1) * cw] * ref[h] + jnp.dot(
                vt_tile[h * vrows:(h + 1) * vrows, :], p[:, h * cw:(h + 1) * cw], preferred_element_type=F32)

    m_init = jnp.full((1, ncol), NO_SLOT, F32)

    accw_ref[...] = jnp.zeros_like(accw_ref)
    m_w = m_init
    for i in range(nw // 2 + 2):
        jj = qi0 // 2 - 2 + i
        jc = jnp.maximum(jj, 0)
        r0 = pl.multiple_of(jc * 128, 128)
        s = lax.dot_general(kw_ref[0, pl.ds(r0, 128), :].astype(BF16), qpad, nt_dims,
                            preferred_element_type=F32)
        m_w, alpha, p = softmax_tile(m_w, s, [block_adds(2 * jj + half, False) for half in range(2)])
        add_values(accw_ref, alpha, vwt_ref[jc], p)

    n_tiles = (qi0 + nw - 1) // 4 + 1

    def masked_scores(u):
        r0 = pl.multiple_of(u * 256, 256)
        block_of_row = 4 * u + lax.broadcasted_iota(jnp.int32, (256, 128), 0) // SEL_LEN
        onehot = jnp.where(lax.broadcasted_iota(jnp.int32, (256, 128), 1) == block_of_row, 1.0, 0.0)
        keys = jnp.concatenate([ks_ref[0, pl.ds(r0, 256), :].astype(BF16), onehot.astype(BF16)], axis=1)
        s = lax.dot_general(keys, qsel, nt_dims, preferred_element_type=F32)
        tmax = None
        for k in range(4):
            sk = s[k * 64:(k + 1) * 64] + block_adds(4 * u + k, True)
            s_ref[k * 64:(k + 1) * 64, :] = sk
            kmax = jnp.max(sk, axis=0, keepdims=True)
            tmax = kmax if tmax is None else jnp.maximum(tmax, kmax)
        return tmax

    tmax0 = masked_scores(0)
    p_ref[...] = jnp.zeros_like(p_ref)
    acc_ref[...] = jnp.zeros_like(acc_ref)

    def sel_body(u, carry):
        m_i, alpha_prev, tmax = carry
        add_values(acc_ref, alpha_prev, vst_ref[jnp.maximum(u - 1, 0)], p_ref[...])
        m_new = jnp.maximum(m_i, tmax)
        p_ref[...] = jnp.exp2(s_ref[...] - m_new).astype(BF16)
        tmax_next = masked_scores(jnp.minimum(u + 1, n_tiles - 1))
        return m_new, jnp.exp2(m_i - m_new), tmax_next

    m_s, alpha_last, _ = lax.fori_loop(0, n_tiles, sel_body, (m_init, jnp.ones((1, ncol), F32), tmax0))
    add_values(acc_ref, alpha_last, vst_ref[n_tiles - 1], p_ref[...])

    nm_s = (SEL_LEN * jnp.maximum(SEL_TOPK - (qcol + 1), 0) + (SEL_LEN - 1 - rcol)).astype(F32)
    nm_w = (WINDOW + Q_BLOCK - jnp.minimum(tvec + 1, WINDOW)).astype(F32)

    def normalised(m, ref, n_masked):
        scale, extra = slot_terms(m, n_masked)
        out = []
        for h in range(NSA_KV_HEADS):
            sl = slice(h * cw, (h + 1) * cw)
            acc = ref[h]
            denom = acc[HEAD_DIM:HEAD_DIM + 1, :] * scale[:, sl] + extra[:, sl]
            out.append((scale[:, sl] / denom, acc[0:HEAD_DIM, :]))
        return out

    o_s = normalised(m_s, acc_ref, nm_s)
    o_w = normalised(m_w, accw_ref, nm_w)

    lane1 = lax.broadcasted_iota(jnp.int32, (1, 128), 1)
    gts = []
    for w in range(nw):
        g = jax.nn.sigmoid(gl_ref[w * 64:(w + 1) * 64, :])
        gts.append(jnp.concatenate([g, g], axis=0).T)

    def gate_vec(h, br):
        tiles = []
        for w in range(nw):
            for gg in range(2):
                c0 = (h * 4 + 2 * gg) * 3 + br
                c1 = (h * 4 + 2 * gg + 1) * 3 + br
                tiles.append(jnp.where(lane1 < 64, gts[w][c0:c0 + 1, :], gts[w][c1:c1 + 1, :]))
        return jnp.concatenate(tiles, axis=1)

    tot = []
    for h in range(NSA_KV_HEADS):
        t = gate_vec(h, 0) * o_c[h]
        t = t + (gate_vec(h, 1) * o_s[h][0]) * o_s[h][1]
        t = t + (gate_vec(h, 2) * o_w[h][0]) * o_w[h][1]
        tot.append(t)
    for w in range(nw):
        ot = jnp.concatenate([t[:, w * 256:(w + 1) * 256] for t in tot], axis=0).T
        o_ref[w * 64:(w + 1) * 64, :] = jnp.concatenate([ot[g * 64:(g + 1) * 64, :]
                                                         for g in range(NSA_GQA)], axis=1)


def _nsa(z, kc, vct, t2, tab, ov, b, seq, nw=4):
    nst = seq // (Q_BLOCK * nw)
    ncp = seq // CMP_STRIDE
    rows = Q_BLOCK * nw
    cw = nw * NSA_GQA * Q_BLOCK
    z3 = z.reshape(b, seq, Z_NSA)
    kv_col = lambda n: pl.BlockSpec((1, seq, 128), lambda i, j: (i, 0, Z_KV // 128 + n))
    return pl.pallas_call(
        functools.partial(_nsa_kernel, ncp=ncp, nw=nw),
        grid=(b, nst),
        in_specs=[pl.BlockSpec((rows, NSA_WIDTH), lambda i, j: (i * nst + j, Z_Q // NSA_WIDTH)),
                  pl.BlockSpec((rows, 128), lambda i, j: (i * nst + j, Z_G // 128)),
                  pl.BlockSpec((1, ncp, 128), lambda i, j: (i, 0, 0)),
                  pl.BlockSpec((1, 128, ncp), lambda i, j: (i, 0, 0)),
                  kv_col(2), kv_col(3), kv_col(4), kv_col(5),
                  pl.BlockSpec(t2.shape, lambda i, j: (0, 0, 0)),
                  pl.BlockSpec(tab.shape, lambda i, j: (0, 0, 0)),
                  pl.BlockSpec(ov.shape, lambda i, j: (0, 0))],
        out_specs=pl.BlockSpec((rows, NSA_WIDTH), lambda i, j: (i * nst + j, 0)),
        out_shape=jax.ShapeDtypeStruct((b * seq, NSA_WIDTH), F32),
        scratch_shapes=[pltpu.VMEM((256, NSA_KV_HEADS * cw), F32),
                        pltpu.VMEM((256, NSA_KV_HEADS * cw), BF16),
                        pltpu.VMEM((NSA_KV_HEADS, HEAD_DIM + 16, cw), F32),
                        pltpu.VMEM((NSA_KV_HEADS, HEAD_DIM + 16, cw), F32),
                        pltpu.VMEM((seq // 256, NSA_KV_HEADS * (HEAD_DIM + 16), 256), BF16),
                        pltpu.VMEM((seq // 128, NSA_KV_HEADS * (HEAD_DIM + 16), 128), BF16)],
        compiler_params=_cparams("parallel", "arbitrary"),
        name="nsa",
    )(z, z, kc, vct, z3, z3, z3, z3, t2, tab, ov)


def _lru_kernel(x0_ref, x1_ref, g0_ref, g1_ref, cw_ref, cb_ref, wa_ref, ba_ref, wx_ref, bx_ref, sp_ref,
                o0_ref, o1_ref, xprev_ref, h_ref, a_scr, b_scr, *, nb, tc):
    @pl.when(pl.program_id(0) == 0)
    def _():
        xprev_ref[...] = jnp.zeros_like(xprev_ref)
        h_ref[...] = jnp.zeros_like(h_ref)

    x = jnp.concatenate([x0_ref[...], x1_ref[...]], axis=1)
    rows, width = x.shape
    grp = nb * OCT
    xm = jnp.concatenate([xprev_ref[...], x[:rows - grp]], axis=0)
    xprev_ref[...] = x[rows - grp:]
    x3 = x.reshape(rows // OCT, OCT, width)
    xm3 = xm.reshape(rows // OCT, OCT, width)
    tlo = lax.broadcasted_iota(jnp.int32, (1, OCT, 1), 1)
    cw = cw_ref[...]
    xc = cb_ref[...].reshape(1, 1, width) + cw[CONV_WIDTH - 1:CONV_WIDTH, :].reshape(1, 1, width) * x3
    for k in range(1, CONV_WIDTH):
        delayed = jnp.where(tlo >= k, pltpu.roll(x3, k, 1), pltpu.roll(xm3, k, 1))
        xc = xc + cw[CONV_WIDTH - 1 - k:CONV_WIDTH - k, :].reshape(1, 1, width) * delayed
    xc = xc.reshape(rows, width)
    xcb = xc.astype(BF16)
    gate_r = jax.nn.sigmoid(jnp.dot(xcb, wa_ref[...], preferred_element_type=F32) + ba_ref[...])
    gate_i = jax.nn.sigmoid(jnp.dot(xcb, wx_ref[...], preferred_element_type=F32) + bx_ref[...])
    log_a = -LRU_C * gate_r * sp_ref[...]
    a = jnp.exp(log_a)
    th = jnp.tanh(log_a)
    bt = jnp.sqrt(-2.0 * th / (1.0 - th)) * gate_i * xc
    for j in range(2):
        a_scr[j] = a[:, j * 128:(j + 1) * 128]
        b_scr[j] = bt[:, j * 128:(j + 1) * 128]

    def step(t, h):
        r0 = _octet_row(t, nb)
        new = []
        for j in range(2):
            hj = a_scr[j, pl.ds(r0, nb, stride=OCT), :] * h[j] + b_scr[j, pl.ds(r0, nb, stride=OCT), :]
            b_scr[j, pl.ds(r0, nb, stride=OCT), :] = hj
            new.append(hj)
        return tuple(new)

    h = lax.fori_loop(0, tc, step, (h_ref[0], h_ref[1]))
    for j, (g_ref, o_ref) in enumerate(((g0_ref, o0_ref), (g1_ref, o1_ref))):
        h_ref[j] = h[j]
        o_ref[...] = b_scr[j] * jax.nn.gelu(g_ref[...])


def _lru(x0, x1, g0, g1, cw, cb, wa, ba, wx, bx, sp, nb, tc=64):
    rows = x0.shape[0]
    r = tc * nb
    c2 = lambda i: (0, 0)
    slab = pl.BlockSpec((r, 128), lambda i: (i, 0))
    return pl.pallas_call(
        functools.partial(_lru_kernel, nb=nb, tc=tc),
        grid=(rows // r,),
        in_specs=[slab, slab, slab, slab,
                  pl.BlockSpec(cw.shape, c2), pl.BlockSpec(cb.shape, c2),
                  pl.BlockSpec(wa.shape, c2), pl.BlockSpec(ba.shape, c2),
                  pl.BlockSpec(wx.shape, c2), pl.BlockSpec(bx.shape, c2),
                  pl.BlockSpec(sp.shape, c2)],
        out_specs=[slab, slab],
        out_shape=[jax.ShapeDtypeStruct((rows, 128), F32)] * 2,
        scratch_shapes=[pltpu.VMEM((nb * OCT, LRU_WIDTH), F32), pltpu.VMEM((2, nb, 128), F32),
                        pltpu.VMEM((2, r, 128), F32), pltpu.VMEM((2, r, 128), F32)],
        compiler_params=_cparams("arbitrary"),
        name="lru",
    )(x0, x1, g0, g1, cw, cb, wa, ba, wx, bx, sp)


def _block_diag(w):
    h, i, j = w.shape
    eye = jnp.eye(h, dtype=w.dtype)
    return jnp.einsum('hij,hk->hikj', w, eye).reshape(h * i, h * j)


def _mix_ffn_kernel(h_ref, ys0_ref, ys1_ref, yn_ref, yl0_ref, yl1_ref, wglu_ref, wos_ref, won_ref, wol_ref,
                    g_ref, wg_ref, wu_ref, wd_ref, gf_ref, o_ref, *, final):
    tm = h_ref.shape[0]

    def rows(a_ref, b_ref):
        return jnp.concatenate([a_ref[...].reshape(tm, 128), b_ref[...].reshape(tm, 128)], axis=1)

    gl = jnp.dot(rows(ys0_ref, ys1_ref).astype(BF16), wglu_ref[...], preferred_element_type=F32)
    s5 = gl[:, :SSM_WIDTH] * jax.nn.sigmoid(gl[:, SSM_WIDTH:])
    acc = jnp.dot(s5.astype(BF16), wos_ref[...], preferred_element_type=F32)
    acc = acc + jnp.dot(yn_ref[...].astype(BF16), won_ref[...], preferred_element_type=F32)
    acc = acc + jnp.dot(rows(yl0_ref, yl1_ref).astype(BF16), wol_ref[...], preferred_element_type=F32)
    x = h_ref[...] + acc

    y = (x * lax.rsqrt(jnp.mean(x * x, axis=-1, keepdims=True) + RMS_EPS) * g_ref[...]).astype(BF16)
    a = jnp.dot(y, wg_ref[...], preferred_element_type=F32)
    b = jnp.dot(y, wu_ref[...], preferred_element_type=F32)
    m = (jax.nn.silu(a) * b).astype(BF16)
    out = x + jnp.dot(m, wd_ref[...], preferred_element_type=F32)
    if final:
        out = out * lax.rsqrt(jnp.mean(out * out, axis=-1, keepdims=True) + RMS_EPS) * gf_ref[...]
    o_ref[...] = out


def _mix_ffn(h, ys, yn, yl, wglu, wos, won, wol, g, wg, wu, wd, gf, final, b, seq, tm=256):
    d = h.shape[1]
    nt = seq // tm
    row = lambda w: pl.BlockSpec((tm, w), lambda b_, i: (b_ * nt + i, 0))
    const = lambda a: pl.BlockSpec(a.shape, lambda b_, i: (0, 0), pipeline_mode=pl.Buffered(1))
    slab4 = lambda s: s.reshape(seq // OCT, b, OCT, 128)
    return pl.pallas_call(
        functools.partial(_mix_ffn_kernel, final=final),
        grid=(b, nt),
        in_specs=[row(d), _slab_spec(tm, nt), _slab_spec(tm, nt), row(NSA_WIDTH),
                  _slab_spec(tm, nt), _slab_spec(tm, nt),
                  const(wglu), const(wos), const(won), const(wol),
                  const(g), const(wg), const(wu), const(wd), const(gf)],
        out_specs=row(d),
        out_shape=jax.ShapeDtypeStruct((b * seq, d), F32),
        compiler_params=_cparams("parallel", "parallel"),
        name="mix_ffn",
    )(h, slab4(ys[0]), slab4(ys[1]), yn, slab4(yl[0]), slab4(yl[1]), wglu, wos, won, wol, g, wg, wu, wd, gf)


def _head_interleave(w, axis):
    shape = w.shape
    split = shape[:axis] + (NSA_KV_HEADS, NSA_GQA, HEAD_DIM) + shape[axis + 1:]
    return jnp.swapaxes(w.reshape(split), axis, axis + 1).reshape(shape)


def _prep_w_in(w_in):
    o1 = SSM_WIDTH
    o2 = o1 + NSA_WIDTH
    o3 = o2 + 6 * NSA_KV_WIDTH
    o4 = o3 + 3 * NSA_Q_HEADS
    wq = _head_interleave(w_in[:, o1:o2], 1)
    pad = jnp.zeros((w_in.shape[0], 128 - 3 * NSA_Q_HEADS), w_in.dtype)
    w = jnp.concatenate([wq, w_in[:, o2:o3], w_in[:, o3:o4], pad, w_in[:, :o1], w_in[:, o4:]], axis=1)
    return w.astype(BF16)


def _s5_mixer(slabs, b, p):
    we, tz, cp, a16, dg = _s5_tables(p['s5_lam_re'], p['s5_lam_im'], p['s5_log_dt'], p['s5_b_re'],
                                     p['s5_b_im'], p['s5_c_re'], p['s5_c_im'], p['s5_d'])
    return _s5(slabs[0], slabs[1], we, tz, cp, a16, dg, nb=b)


def _nsa_mixer(z, b, seq, bias_tabs, ov, p):
    w1k, w2k, c0k = _compress_weights(p['nsa_pe_k'], p['nsa_w1_k'], p['nsa_w2_k'])
    w1v, w2v, c0v = _compress_weights(p['nsa_pe_v'], p['nsa_w1_v'], p['nsa_w2_v'])
    kc, vct = _compress(z.reshape(b, seq, Z_NSA), w1k, w1v, w2k, w2v, c0k, c0v)
    tab, t2 = bias_tabs
    return _nsa(z, kc, vct, t2, tab, ov, b, seq)


def _lru_mixer(slabs, b, p):
    sp = jax.nn.softplus(-p['lru_lam'].astype(F32)).reshape(1, -1)
    return _lru(slabs[0], slabs[1], slabs[2], slabs[3],
                p['lru_conv_w'].astype(F32), p['lru_conv_b'].astype(F32).reshape(1, -1),
                _block_diag(p['lru_w_a']).astype(BF16), p['lru_b_a'].astype(F32).reshape(1, -1),
                _block_diag(p['lru_w_x']).astype(BF16), p['lru_b_x'].astype(F32).reshape(1, -1),
                sp, nb=b)


def _layer(h2, b, seq, bias_tabs, ov, p, final, norm_final):
    z, slabs = _inproj(h2, p['norm_mix'].reshape(1, -1), _prep_w_in(p['w_in']), b, seq)
    ys = _s5_mixer(slabs[0:2], b, p)
    yn = _nsa_mixer(z, b, seq, bias_tabs, ov, p)
    yl = _lru_mixer(slabs[2:6], b, p)
    w_out = p['w_out']
    wos = w_out[:SSM_WIDTH].astype(BF16)
    won = _head_interleave(w_out[SSM_WIDTH:SSM_WIDTH + NSA_WIDTH], 0).astype(BF16)
    wol = w_out[SSM_WIDTH + NSA_WIDTH:].astype(BF16)
    return _mix_ffn(h2, ys, yn, yl, p['s5_w_glu'].astype(BF16), wos, won, wol,
                    p['norm_ffn'].reshape(1, -1), p['w_gate'].astype(BF16), p['w_up'].astype(BF16),
                    p['w_down'].astype(BF16), norm_final.reshape(1, -1), final, b, seq)


def _overlap_t(seq):
    ncp = seq // CMP_STRIDE
    nsel = seq // SEL_LEN
    cs = np.arange(ncp) * CMP_STRIDE
    ss = np.arange(nsel) * SEL_LEN
    ovl = (cs[None, :] < ss[:, None] + SEL_LEN) & (ss[:, None] < cs[None, :] + CMP_LEN)
    ovl[:, ncp - 1] = False
    return jnp.asarray(ovl.astype(np.float32))


def kernel(x, rel_bias_table, norm_mix, w_in, w_out, s5_lam_re, s5_lam_im, s5_log_dt, s5_b_re, s5_b_im, s5_c_re, s5_c_im, s5_d, s5_w_glu, nsa_pe_k, nsa_w1_k, nsa_w2_k, nsa_pe_v, nsa_w1_v, nsa_w2_v, lru_conv_w, lru_conv_b, lru_w_a, lru_b_a, lru_w_x, lru_b_x, lru_lam, norm_ffn, w_gate, w_up, w_down, norm_final):
    b, seq, d = x.shape
    depth = norm_mix.shape[0]
    per_layer = dict(norm_mix=norm_mix, w_in=w_in, w_out=w_out, s5_lam_re=s5_lam_re, s5_lam_im=s5_lam_im,
                     s5_log_dt=s5_log_dt, s5_b_re=s5_b_re, s5_b_im=s5_b_im, s5_c_re=s5_c_re,
                     s5_c_im=s5_c_im, s5_d=s5_d, s5_w_glu=s5_w_glu, nsa_pe_k=nsa_pe_k, nsa_w1_k=nsa_w1_k,
                     nsa_w2_k=nsa_w2_k, nsa_pe_v=nsa_pe_v, nsa_w1_v=nsa_w1_v, nsa_w2_v=nsa_w2_v,
                     lru_conv_w=lru_conv_w, lru_conv_b=lru_conv_b, lru_w_a=lru_w_a, lru_b_a=lru_b_a,
                     lru_w_x=lru_w_x, lru_b_x=lru_b_x, lru_lam=lru_lam, norm_ffn=norm_ffn, w_gate=w_gate,
                     w_up=w_up, w_down=w_down)
    bias_tabs = _bias_tables(rel_bias_table, seq)
    ov = _overlap_t(seq)
    h2 = x.reshape(b * seq, d)
    for l in range(depth):
        p = {k: v[l] for k, v in per_layer.items()}
        h2 = _layer(h2, b, seq, bias_tabs, ov, p, l == depth - 1, norm_final)
    return h2.reshape(b, seq, d)
```

```python
import functools
import math

import numpy as np
import jax
import jax.numpy as jnp
from jax import lax
from jax.experimental import pallas as pl
from jax.experimental.pallas import tpu as pltpu

F32 = jnp.float32
BF16 = jnp.bfloat16

D_MODEL = 1024
SSM_WIDTH = 256
NSA_WIDTH = 512
LRU_WIDTH = 256
S5_GROUP = 16
S5_GROUPS = 16
S5_STATE = 64
HEAD_DIM = 64
NSA_Q_HEADS = 8
NSA_KV_HEADS = 2
NSA_GQA = 4
NSA_KV_WIDTH = 128
CMP_LEN = 32
CMP_STRIDE = 16
SEL_LEN = 64
SEL_TOPK = 8
WINDOW = 256
Q_BLOCK = 64
LRU_HEADS = 4
LRU_HEAD_DIM = 64
CONV_WIDTH = 4
LRU_C = 8.0
REL_BUCKETS = 32
REL_MAX_DIST = 128
D_FF = 2816
NEG_INF = -1e9
LOG2E = 1.4426950408889634
RMS_EPS = 1e-6

NO_SLOT = -3e38

S5_CHUNK = 16
Z_Q, Z_KV, Z_G, Z_NSA = 0, 512, 1280, 1408
N_SLAB = 6
OCT = 8

VMEM_LIMIT = 56 * 1024 * 1024


def _cparams(*sem):
    return pltpu.CompilerParams(dimension_semantics=sem, vmem_limit_bytes=VMEM_LIMIT)


def _inproj_kernel(x_ref, g_ref, w_ref, o_ref, *slab_refs):
    x = x_ref[...]
    y = x * lax.rsqrt(jnp.mean(x * x, axis=-1, keepdims=True) + RMS_EPS) * g_ref[...]
    res = jnp.dot(y.astype(BF16), w_ref[...], preferred_element_type=F32)
    o_ref[...] = res[:, :Z_NSA]
    for m, s_ref in enumerate(slab_refs):
        s_ref[...] = res[:, Z_NSA + m * 128:Z_NSA + (m + 1) * 128].reshape(s_ref.shape)


def _slab_spec(tm, nt):
    return pl.BlockSpec((tm // OCT, None, OCT, 128), lambda b, i: (i, b, 0, 0))


def _inproj(x2, g, w, b, seq, tm=512):
    d = x2.shape[1]
    n = w.shape[1]
    nt = seq // tm
    slab = jax.ShapeDtypeStruct((seq // OCT, b, OCT, 128), F32)
    outs = pl.pallas_call(
        _inproj_kernel,
        grid=(b, nt),
        in_specs=[pl.BlockSpec((tm, d), lambda b_, i: (b_ * nt + i, 0)),
                  pl.BlockSpec((1, d), lambda b_, i: (0, 0)),
                  pl.BlockSpec((d, n), lambda b_, i: (0, 0))],
        out_specs=[pl.BlockSpec((tm, Z_NSA), lambda b_, i: (b_ * nt + i, 0))] + [_slab_spec(tm, nt)] * N_SLAB,
        out_shape=[jax.ShapeDtypeStruct((b * seq, Z_NSA), F32)] + [slab] * N_SLAB,
        compiler_params=_cparams("parallel", "parallel"),
        name="inproj",
    )(x2, g, w)
    return outs[0], [s.reshape(b * seq, 128) for s in outs[1:]]


def _lane_regroup(x):
    r = x.shape[0]
    cols = [x[:, v * 128:(v + 1) * 128] for v in range(32)]
    piece = lax.broadcasted_iota(jnp.int32, (r, 128), 1) // 16
    out = [None] * 32
    for ah in range(2):
        for bh in range(2):
            vs = [cols[2 * (8 * ah + i) + bh] for i in range(8)]
            for stage in (4, 2, 1):
                upper = (piece & stage) != 0
                for i in range(8):
                    if i & stage:
                        continue
                    lo, hi = vs[i], vs[i + stage]
                    vs[i] = jnp.where(upper, pltpu.roll(hi, stage * 16, 1), lo)
                    vs[i + stage] = jnp.where(upper, hi, pltpu.roll(lo, 128 - stage * 16, 1))
            for j in range(8):
                out[2 * (8 * bh + j) + ah] = vs[j]
    return jnp.concatenate(out, axis=1)


def _octet_row(t, nb):
    return (t // OCT) * (nb * OCT) + t % OCT


def _s5_kernel(u0_ref, u1_ref, we_ref, tz_ref, cp_ref, a_ref, d_ref, o0_ref, o1_ref, e_ref, carry_ref,
               *, nb, kc):
    @pl.when(pl.program_id(0) == 0)
    def _():
        carry_ref[...] = jnp.zeros_like(carry_ref)

    u_refs = (u0_ref, u1_ref)
    u = jnp.concatenate(
        [jnp.concatenate([u_refs[j][pl.ds(_octet_row(k * S5_CHUNK + s, nb), nb, stride=OCT), :]
                          for s in range(S5_CHUNK) for j in range(2)], axis=1)
         for k in range(kc)], axis=0)
    ug = _lane_regroup(u)
    ugb = ug.astype(BF16)
    npair = S5_GROUPS // 2
    for p in range(npair):
        e_ref[:, p * 256:(p + 1) * 256] = jnp.dot(ugb[:, p * 512:(p + 1) * 512], we_ref[p],
                                                   preferred_element_type=F32)

    a = a_ref[...]

    def step(k, carry):
        r0 = pl.multiple_of(k * nb, nb)
        e = e_ref[pl.ds(r0, nb), :]
        e_ref[pl.ds(r0, nb), :] = carry
        new = []
        for p in range(npair):
            ar = a[:, p * 256:p * 256 + 128]
            ai = a[:, p * 256 + 128:(p + 1) * 256]
            cr = carry[:, p * 256:p * 256 + 128]
            ci = carry[:, p * 256 + 128:(p + 1) * 256]
            new.append(ar * cr - ai * ci + e[:, p * 256:p * 256 + 128])
            new.append(ar * ci + ai * cr + e[:, p * 256 + 128:(p + 1) * 256])
        return jnp.concatenate(new, axis=1)

    carry_ref[...] = lax.fori_loop(0, kc, step, carry_ref[...])

    eb = e_ref[...].astype(BF16)
    ys = []
    for p in range(npair):
        yc = jnp.dot(eb[:, p * 256:(p + 1) * 256], cp_ref[p], preferred_element_type=F32)
        y0 = jnp.dot(ugb[:, (2 * p) * 256:(2 * p + 1) * 256], tz_ref[2 * p], preferred_element_type=F32)
        y1 = jnp.dot(ugb[:, (2 * p + 1) * 256:(2 * p + 2) * 256], tz_ref[2 * p + 1],
                     preferred_element_type=F32)
        ys.append(yc + jnp.concatenate([y0, y1], axis=1))
    y = jnp.concatenate(ys, axis=1) + d_ref[...] * ug
    y = jax.nn.gelu(y)
    y = _lane_regroup(y)
    o_refs = (o0_ref, o1_ref)
    for k in range(kc):
        for s in range(S5_CHUNK):
            for j in range(2):
                lo = s * SSM_WIDTH + j * 128
                o_refs[j][pl.ds(_octet_row(k * S5_CHUNK + s, nb), nb, stride=OCT), :] = \
                    y[k * nb:(k + 1) * nb, lo:lo + 128]


def _s5(u0, u1, we, tz, cp, a16, dg, nb, kc=16):
    rows = u0.shape[0]
    r = kc * S5_CHUNK * nb
    const3 = lambda i: (0, 0, 0)
    slab = pl.BlockSpec((r, 128), lambda i: (i, 0))
    return pl.pallas_call(
        functools.partial(_s5_kernel, nb=nb, kc=kc),
        grid=(rows // r,),
        in_specs=[slab, slab,
                  pl.BlockSpec(we.shape, const3),
                  pl.BlockSpec(tz.shape, const3),
                  pl.BlockSpec(cp.shape, const3),
                  pl.BlockSpec(a16.shape, lambda i: (0, 0)),
                  pl.BlockSpec(dg.shape, lambda i: (0, 0))],
        out_specs=[slab, slab],
        out_shape=[jax.ShapeDtypeStruct((rows, 128), F32)] * 2,
        scratch_shapes=[pltpu.VMEM((kc * nb, 2048), F32), pltpu.VMEM((nb, 2048), F32)],
        compiler_params=_cparams("arbitrary"),
        name="s5",
    )(u0, u1, we, tz, cp, a16, dg)


def _s5_tables(lam_re, lam_im, log_dt, b_re, b_im, c_re, c_im, d_skip):
    L = S5_CHUNK
    G, P, C = S5_GROUPS, S5_STATE, S5_GROUP
    dt = jnp.exp(log_dt.astype(F32))[:, None]
    lr = lam_re.astype(F32)
    li = lam_im.astype(F32)
    mag = jnp.exp(lr * dt)
    ang = li * dt
    ab_re = mag * jnp.cos(ang)
    ab_im = mag * jnp.sin(ang)
    den = lr * lr + li * li
    f_re = ((ab_re - 1.0) * lr + ab_im * li) / den
    f_im = (ab_im * lr - (ab_re - 1.0) * li) / den
    br = b_re.astype(F32)
    bi = b_im.astype(F32)
    bb_re = f_re[..., None] * br - f_im[..., None] * bi
    bb_im = f_re[..., None] * bi + f_im[..., None] * br
    cr = c_re.astype(F32)
    ci = c_im.astype(F32)
    tau = jnp.arange(L + 1, dtype=F32)[:, None, None]
    pr = jnp.exp(lr * dt * tau) * jnp.cos(li * dt * tau)
    pi = jnp.exp(lr * dt * tau) * jnp.sin(li * dt * tau)

    prs = pr[:L][::-1]
    pis = pi[:L][::-1]
    we_re = jnp.einsum('sgp,gpc->gscp', prs, bb_re) - jnp.einsum('sgp,gpc->gscp', pis, bb_im)
    we_im = jnp.einsum('sgp,gpc->gscp', prs, bb_im) + jnp.einsum('sgp,gpc->gscp', pis, bb_re)
    we_re = we_re.reshape(G, L * C, P)
    we_im = we_im.reshape(G, L * C, P)
    z = jnp.zeros_like(we_re[0::2])
    top = jnp.concatenate([we_re[0::2], z, we_im[0::2], z], axis=-1)
    bot = jnp.concatenate([z, we_re[1::2], z, we_im[1::2]], axis=-1)
    we = jnp.concatenate([top, bot], axis=1)

    m_re = pr[:L, :, :, None] * bb_re[None] - pi[:L, :, :, None] * bb_im[None]
    m_im = pr[:L, :, :, None] * bb_im[None] + pi[:L, :, :, None] * bb_re[None]
    kern = jnp.einsum('gop,tgpc->gcto', cr, m_re) - jnp.einsum('gop,tgpc->gcto', ci, m_im)
    padded = jnp.concatenate([jnp.zeros((G, C, L * C), F32), kern.reshape(G, C, L * C)], axis=-1)
    tz = jnp.stack([padded[:, :, (L - s) * C:(2 * L - s) * C] for s in range(L)], axis=1)
    tz = tz.reshape(G, L * C, L * C)

    pr1 = pr[1:]
    pi1 = pi[1:]
    cp_re = jnp.einsum('gop,tgp->gpto', cr, pr1) - jnp.einsum('gop,tgp->gpto', ci, pi1)
    cp_im = -(jnp.einsum('gop,tgp->gpto', cr, pi1) + jnp.einsum('gop,tgp->gpto', ci, pr1))
    cp_re = cp_re.reshape(G, P, L * C)
    cp_im = cp_im.reshape(G, P, L * C)
    zc = jnp.zeros_like(cp_re[0::2])
    cp = jnp.concatenate([
        jnp.concatenate([cp_re[0::2], zc], axis=-1),
        jnp.concatenate([zc, cp_re[1::2]], axis=-1),
        jnp.concatenate([cp_im[0::2], zc], axis=-1),
        jnp.concatenate([zc, cp_im[1::2]], axis=-1)], axis=1)

    a_re = pr[L].reshape(G // 2, 2 * P)
    a_im = pi[L].reshape(G // 2, 2 * P)
    a16 = jnp.concatenate([a_re, a_im], axis=-1).reshape(1, G * 2 * P)
    dg = jnp.broadcast_to(d_skip.astype(F32).reshape(G, 1, C), (G, L, C)).reshape(1, G * L * C)
    return we.astype(BF16), tz.astype(BF16), cp.astype(BF16), a16, dg


def _compress_kernel(ak_ref, av_ref, w1k_ref, w1v_ref, w2k_ref, w2v_ref, c0k_ref, c0v_ref,
                     kc_ref, vct_ref):
    def run(a_ref, w1_ref, w2_ref, c0_ref):
        nc = a_ref.shape[1] // CMP_STRIDE
        a = jnp.concatenate([a_ref[0, pl.ds(l, nc, stride=CMP_STRIDE), :] for l in range(CMP_STRIDE)],
                            axis=1).astype(BF16)
        p1 = jnp.dot(a, w1_ref[0], preferred_element_type=F32)
        p2 = jnp.dot(a, w1_ref[1], preferred_element_type=F32)
        n = p1.shape[0]
        pre = p1 + pltpu.roll(p2, n - 1, 0) + c0_ref[...]
        out = jnp.dot(jax.nn.gelu(pre).astype(BF16), w2_ref[...], preferred_element_type=F32)
        row = lax.broadcasted_iota(jnp.int32, out.shape, 0)
        return jnp.where(row < n - 1, out, 0.0)

    kc_ref[0] = run(ak_ref, w1k_ref, w2k_ref, c0k_ref).astype(BF16)
    vct_ref[0] = run(av_ref, w1v_ref, w2v_ref, c0v_ref).T.astype(BF16)


def _compress(z3, w1k, w1v, w2k, w2v, c0k, c0v):
    b, seq, _ = z3.shape
    nc = seq // CMP_STRIDE
    c3 = lambda i: (0, 0, 0)
    c2 = lambda i: (0, 0)
    return pl.pallas_call(
        _compress_kernel,
        grid=(b,),
        in_specs=[pl.BlockSpec((1, seq, 128), lambda i: (i, 0, Z_KV // 128)),
                  pl.BlockSpec((1, seq, 128), lambda i: (i, 0, Z_KV // 128 + 1)),
                  pl.BlockSpec(w1k.shape, c3), pl.BlockSpec(w1v.shape, c3),
                  pl.BlockSpec(w2k.shape, c2), pl.BlockSpec(w2v.shape, c2),
                  pl.BlockSpec(c0k.shape, c2), pl.BlockSpec(c0v.shape, c2)],
        out_specs=[pl.BlockSpec((1, nc, 128), lambda i: (i, 0, 0)),
                   pl.BlockSpec((1, 128, nc), lambda i: (i, 0, 0))],
        out_shape=[jax.ShapeDtypeStruct((b, nc, 128), BF16),
                   jax.ShapeDtypeStruct((b, 128, nc), BF16)],
        compiler_params=_cparams("parallel"),
        name="compress",
    )(z3, z3, w1k, w1v, w2k, w2v, c0k, c0v)


def _compress_weights(pe, w1, w2):
    w1 = w1.astype(F32)
    half = (CMP_LEN // 2) * HEAD_DIM
    eye = jnp.eye(NSA_KV_HEADS, dtype=F32)

    def bd(w):
        w = w.reshape(CMP_STRIDE, HEAD_DIM, HEAD_DIM)
        return jnp.einsum('ldo,hk->lhdko', w, eye).reshape(NSA_KV_HEADS * half, NSA_KV_HEADS * HEAD_DIM)

    w1s = jnp.stack([bd(w1[:half]), bd(w1[half:])])
    z2 = jnp.zeros((HEAD_DIM, HEAD_DIM), F32)
    w2f = w2.astype(F32)
    w2s = jnp.concatenate([jnp.concatenate([w2f, z2], axis=1),
                           jnp.concatenate([z2, w2f], axis=1)], axis=0)
    c0 = pe.astype(F32).reshape(1, CMP_LEN * HEAD_DIM) @ w1
    c0 = jnp.concatenate([c0, c0], axis=1)
    return w1s.astype(BF16), w2s.astype(BF16), c0


def _t5_bucket_np(dist):
    n = np.maximum(dist, 0)
    max_exact = REL_BUCKETS // 2
    nf = np.maximum(n, 1).astype(np.float32)
    large = max_exact + (np.log(nf / max_exact) / math.log(REL_MAX_DIST / max_exact)
                         * (REL_BUCKETS - max_exact)).astype(np.int32)
    large = np.minimum(large, REL_BUCKETS - 1)
    return np.where(n < max_exact, n, large)


TAB_FAR, TAB_WIN_EDGE, TAB_NONE = 3, 4, 5


def _bias_tables(rel_table, seq):
    tbl = rel_table.astype(F32)
    heads = NSA_Q_HEADS

    def lookup(bucket):
        onehot = jnp.asarray(bucket[..., None] == np.arange(REL_BUCKETS)).astype(F32)
        return jnp.einsum('...k,kh->...h', onehot, tbl, precision=lax.Precision.HIGHEST) * LOG2E

    r = np.arange(Q_BLOCK)[None, :]
    c = np.arange(SEL_LEN)[:, None]
    near = lookup(_t5_bucket_np(64 * np.arange(3)[:, None, None] + (r - c)[None]))
    far = jnp.broadcast_to(tbl[REL_BUCKETS - 1] * LOG2E, (SEL_LEN, Q_BLOCK, heads))
    none = jnp.full((SEL_LEN, Q_BLOCK, heads), NO_SLOT, F32)
    causal = jnp.asarray(r - c >= 0)[:, :, None]
    edge = jnp.asarray(c > r)[:, :, None]
    tab = jnp.stack([jnp.where(causal, near[0], none), near[1], near[2], far,
                     jnp.where(edge, far, none), none])
    tab = tab.transpose(0, 1, 3, 2).reshape(6, SEL_LEN, heads * Q_BLOCK)
    ncp = seq // CMP_STRIDE
    k = np.arange(2 * ncp + 4)[:, None]
    dist = r - CMP_STRIDE * (k - (ncp - 4)) - (CMP_LEN - 1)
    t2 = jnp.where(jnp.asarray(dist >= 0)[:, :, None], lookup(_t5_bucket_np(dist)), NO_SLOT)
    t2 = t2.transpose(0, 2, 1).reshape(2 * ncp + 4, heads * Q_BLOCK)
    t2 = jnp.stack([t2[:2 * ncp], t2[4:]])
    return tab, t2


def _nsa_kernel(q_ref, gl_ref, kc_ref, vct_ref, ks_ref, vs_ref, kw_ref, vw_ref, t2_ref, tab_ref,
                ov_ref, o_ref, s_ref, p_ref, acc_ref, accw_ref, vst_ref, vwt_ref, *, ncp, nw):
    @pl.when(pl.program_id(1) == 0)
    def _():
        for v_ref, vt_ref in ((vs_ref, vst_ref), (vw_ref, vwt_ref)):
            kt = vt_ref.shape[2]
            ones = jnp.ones((16, kt), F32)
            for u in range(vt_ref.shape[0]):
                vt = v_ref[0, u * kt:(u + 1) * kt, :].T
                vt_ref[u] = jnp.concatenate([vt[0:HEAD_DIM], ones, vt[HEAD_DIM:], ones], axis=0).astype(BF16)

    qi0 = pl.program_id(1) * nw
    cw = nw * NSA_GQA * Q_BLOCK
    ncol = NSA_KV_HEADS * cw
    vrows = HEAD_DIM + 16
    lane = lax.broadcasted_iota(jnp.int32, (Q_BLOCK, 128), 1)

    q = q_ref[...] * (HEAD_DIM ** -0.5 * LOG2E)
    pieces = []
    for h in range(NSA_KV_HEADS):
        keep = (lane >= 64) if h == 1 else (lane < 64)
        for w in range(nw):
            for g in range(NSA_GQA):
                pieces.append(jnp.where(keep, q[w * 64:(w + 1) * 64, g * 128:(g + 1) * 128], 0.0))
    qpad = jnp.concatenate(pieces, axis=0).astype(BF16)

    col = lax.broadcasted_iota(jnp.int32, (1, ncol), 1)
    rcol = col % Q_BLOCK
    qcol = qi0 + (col // (NSA_GQA * Q_BLOCK)) % nw
    tvec = Q_BLOCK * qcol + rcol
    nt_dims = (((1,), (1,)), ((), ()))

    def per_head_cols(tiles):
        return jnp.concatenate([tiles[w][:, h * 256:(h + 1) * 256]
                                for h in range(NSA_KV_HEADS) for w in range(nw)], axis=1)

    def slot_terms(m, n_masked):
        m_fin = jnp.where(n_masked > 0, jnp.maximum(m, NEG_INF * LOG2E), m)
        return jnp.exp2(m - m_fin), n_masked * jnp.exp2(NEG_INF * LOG2E - m_fin)

    t2_tiles = []
    for w in range(nw):
        start = (ncp - 4) - 4 * (qi0 + w)
        par = (start // 4) % 2
        a0 = pl.multiple_of(start - 4 * par, 8)
        t2_tiles.append(t2_ref[par, pl.ds(a0, ncp), :])
    s = lax.dot_general(kc_ref[0], qpad, nt_dims, preferred_element_type=F32) + per_head_cols(t2_tiles)
    m = jnp.max(s, axis=0, keepdims=True)
    e = jnp.exp2(s - m)
    n_valid = jnp.clip(jnp.right_shift(tvec - (CMP_STRIDE - 1), 4), 0, ncp - 1)
    scale, extra = slot_terms(m, (ncp - 1 - n_valid).astype(F32))
    p_c = e * (scale / (jnp.sum(e, axis=0, keepdims=True) * scale + extra))
    p_cb = p_c.astype(BF16)
    o_c = [jnp.dot(vct_ref[0, h * 64:(h + 1) * 64, :], p_cb[:, h * cw:(h + 1) * cw],
                   preferred_element_type=F32) for h in range(NSA_KV_HEADS)]

    lane_c = lax.broadcasted_iota(jnp.int32, (ncp, 128), 1)
    parts = []
    for w in range(nw):
        halves = []
        for h in range(NSA_KV_HEADS):
            base = h * cw + w * 256
            ph = p_c[:, base:base + 128] + p_c[:, base + 128:base + 256]
            halves.append(ph + pltpu.roll(ph, 64, 1))
        parts.append(jnp.where(lane_c < 64, halves[0], halves[1]))
    imp = jnp.dot(ov_ref[...], jnp.concatenate(parts, axis=1), preferred_element_type=F32,
                  precision=lax.Precision.HIGHEST)
    nsel = imp.shape[0]
    jidx = lax.broadcasted_iota(jnp.int32, (nsel, 1), 0)
    qv = qi0 + lax.broadcasted_iota(jnp.int32, (1, nw * 128), 1) // 128
    forced = jnp.logical_or(jidx == 0, jnp.logical_or(jidx == qv, jidx == qv - 1))
    avail = jidx <= qv
    cand = jnp.logical_and(avail, jnp.logical_not(forced))
    budget = SEL_TOPK - (1 + jnp.where(qv >= 1, 1, 0) + jnp.where(qv >= 2, 1, 0))
    rank = jnp.zeros((nsel, nw * 128), jnp.int32)
    for jp in range(1, nsel - 2):
        row = imp[jp:jp + 1, :]
        ge = jnp.where(row >= imp, 1, 0)
        gt = jnp.where(row > imp, 1, 0)
        is_cand = jnp.where(qv - 2 >= jp, 1, 0)
        rank = rank + jnp.where(jidx > jp, ge, gt) * is_cand
    sel = jnp.logical_or(jnp.logical_and(forced, avail), jnp.logical_and(cand, rank < budget))
    seladd = jnp.where(sel, 0.0, NO_SLOT)
    pad = jnp.zeros((128 - nsel, 128), F32)
    mask_rows = []
    for w in range(nw):
        mask_rows.append(jnp.concatenate([seladd[:, w * 128:(w + 1) * 128], pad], axis=0).T)
    sel_cols = jnp.concatenate([mask_rows[w][h * 64:(h + 1) * 64, :] for h in range(NSA_KV_HEADS)
                                for w in range(nw) for _ in range(NSA_GQA)], axis=0)
    qsel = jnp.concatenate([qpad, sel_cols.astype(BF16)], axis=1)

    def block_adds(blk, selected):
        tiles = []
        for w in range(nw):
            d = qi0 + w - blk
            if selected:
                e = jnp.where(d < 0, TAB_FAR, jnp.minimum(d, TAB_FAR))
            else:
                outside = jnp.logical_or(blk < 0, jnp.logical_or(d < 0, d > TAB_NONE))
                e = jnp.where(outside, TAB_NONE, d)
            tiles.append(tab_ref[e])
        return per_head_cols(tiles)

    def softmax_tile(m_i, s, adds):
        sm = jnp.concatenate([s[k * 64:(k + 1) * 64] + adds[k] for k in range(len(adds))], axis=0)
        m_new = jnp.maximum(m_i, jnp.max(sm, axis=0, keepdims=True))
        return m_new, jnp.exp2(m_i - m_new), jnp.exp2(sm - m_new).astype(BF16)

    def add_values(ref, alpha, vt_tile, p):
        for h in range(NSA_KV_HEADS):
            ref[h] = alpha[:, h * cw:(h + 1) * cw] * ref[h] + jnp.dot(
                vt_tile[h * vrows:(h + 1) * vrows, :], p[:, h * cw:(h + 1) * cw], preferred_element_type=F32)

    m_init = jnp.full((1, ncol), NO_SLOT, F32)

    accw_ref[...] = jnp.zeros_like(accw_ref)
    m_w = m_init
    for i in range(nw // 2 + 2):
        jj = qi0 // 2 - 2 + i
        jc = jnp.maximum(jj, 0)
        r0 = pl.multiple_of(jc * 128, 128)
        s = lax.dot_general(kw_ref[0, pl.ds(r0, 128), :].astype(BF16), qpad, nt_dims,
                            preferred_element_type=F32)
        m_w, alpha, p = softmax_tile(m_w, s, [block_adds(2 * jj + half, False) for half in range(2)])
        add_values(accw_ref, alpha, vwt_ref[jc], p)

    n_tiles = (qi0 + nw - 1) // 4 + 1

    def masked_scores(u):
        r0 = pl.multiple_of(u * 256, 256)
        block_of_row = 4 * u + lax.broadcasted_iota(jnp.int32, (256, 128), 0) // SEL_LEN
        onehot = jnp.where(lax.broadcasted_iota(jnp.int32, (256, 128), 1) == block_of_row, 1.0, 0.0)
        keys = jnp.concatenate([ks_ref[0, pl.ds(r0, 256), :].astype(BF16), onehot.astype(BF16)], axis=1)
        s = lax.dot_general(keys, qsel, nt_dims, preferred_element_type=F32)
        tmax = None
        for k in range(4):
            sk = s[k * 64:(k + 1) * 64] + block_adds(4 * u + k, True)
            s_ref[k * 64:(k + 1) * 64, :] = sk
            kmax = jnp.max(sk, axis=0, keepdims=True)
            tmax = kmax if tmax is None else jnp.maximum(tmax, kmax)
        return tmax

    tmax0 = masked_scores(0)
    p_ref[...] = jnp.zeros_like(p_ref)
    acc_ref[...] = jnp.zeros_like(acc_ref)

    def sel_body(u, carry):
        m_i, alpha_prev, tmax = carry
        add_values(acc_ref, alpha_prev, vst_ref[jnp.maximum(u - 1, 0)], p_ref[...])
        m_new = jnp.maximum(m_i, tmax)
        p_ref[...] = jnp.exp2(s_ref[...] - m_new).astype(BF16)
        tmax_next = masked_scores(jnp.minimum(u + 1, n_tiles - 1))
        return m_new, jnp.exp2(m_i - m_new), tmax_next

    m_s, alpha_last, _ = lax.fori_loop(0, n_tiles, sel_body, (m_init, jnp.ones((1, ncol), F32), tmax0))
    add_values(acc_ref, alpha_last, vst_ref[n_tiles - 1], p_ref[...])

    nm_s = (SEL_LEN * jnp.maximum(SEL_TOPK - (qcol + 1), 0) + (SEL_LEN - 1 - rcol)).astype(F32)
    nm_w = (WINDOW + Q_BLOCK - jnp.minimum(tvec + 1, WINDOW)).astype(F32)

    def normalised(m, ref, n_masked):
        scale, extra = slot_terms(m, n_masked)
        out = []
        for h in range(NSA_KV_HEADS):
            sl = slice(h * cw, (h + 1) * cw)
            acc = ref[h]
            denom = acc[HEAD_DIM:HEAD_DIM + 1, :] * scale[:, sl] + extra[:, sl]
            out.append((scale[:, sl] / denom, acc[0:HEAD_DIM, :]))
        return out

    o_s = normalised(m_s, acc_ref, nm_s)
    o_w = normalised(m_w, accw_ref, nm_w)

    lane1 = lax.broadcasted_iota(jnp.int32, (1, 128), 1)
    gts = []
    for w in range(nw):
        g = jax.nn.sigmoid(gl_ref[w * 64:(w + 1) * 64, :])
        gts.append(jnp.concatenate([g, g], axis=0).T)

    def gate_vec(h, br):
        tiles = []
        for w in range(nw):
            for gg in range(2):
                c0 = (h * 4 + 2 * gg) * 3 + br
                c1 = (h * 4 + 2 * gg + 1) * 3 + br
                tiles.append(jnp.where(lane1 < 64, gts[w][c0:c0 + 1, :], gts[w][c1:c1 + 1, :]))
        return jnp.concatenate(tiles, axis=1)

    tot = []
    for h in range(NSA_KV_HEADS):
        t = gate_vec(h, 0) * o_c[h]
        t = t + (gate_vec(h, 1) * o_s[h][0]) * o_s[h][1]
        t = t + (gate_vec(h, 2) * o_w[h][0]) * o_w[h][1]
        tot.append(t)
    for w in range(nw):
        ot = jnp.concatenate([t[:, w * 256:(w + 1) * 256] for t in tot], axis=0).T
        o_ref[w * 64:(w + 1) * 64, :] = jnp.concatenate([ot[g * 64:(g + 1) * 64, :]
                                                         for g in range(NSA_GQA)], axis=1)


def _nsa(z, kc, vct, t2, tab, ov, b, seq, nw=4):
    nst = seq // (Q_BLOCK * nw)
    ncp = seq // CMP_STRIDE
    rows = Q_BLOCK * nw
    cw = nw * NSA_GQA * Q_BLOCK
    z3 = z.reshape(b, seq, Z_NSA)
    kv_col = lambda n: pl.BlockSpec((1, seq, 128), lambda i, j: (i, 0, Z_KV // 128 + n))
    return pl.pallas_call(
        functools.partial(_nsa_kernel, ncp=ncp, nw=nw),
        grid=(b, nst),
        in_specs=[pl.BlockSpec((rows, NSA_WIDTH), lambda i, j: (i * nst + j, Z_Q // NSA_WIDTH)),
                  pl.BlockSpec((rows, 128), lambda i, j: (i * nst + j, Z_G // 128)),
                  pl.BlockSpec((1, ncp, 128), lambda i, j: (i, 0, 0)),
                  pl.BlockSpec((1, 128, ncp), lambda i, j: (i, 0, 0)),
                  kv_col(2), kv_col(3), kv_col(4), kv_col(5),
                  pl.BlockSpec(t2.shape, lambda i, j: (0, 0, 0)),
                  pl.BlockSpec(tab.shape, lambda i, j: (0, 0, 0)),
                  pl.BlockSpec(ov.shape, lambda i, j: (0, 0))],
        out_specs=pl.BlockSpec((rows, NSA_WIDTH), lambda i, j: (i * nst + j, 0)),
        out_shape=jax.ShapeDtypeStruct((b * seq, NSA_WIDTH), F32),
        scratch_shapes=[pltpu.VMEM((256, NSA_KV_HEADS * cw), F32),
                        pltpu.VMEM((256, NSA_KV_HEADS * cw), BF16),
                        pltpu.VMEM((NSA_KV_HEADS, HEAD_DIM + 16, cw), F32),
                        pltpu.VMEM((NSA_KV_HEADS, HEAD_DIM + 16, cw), F32),
                        pltpu.VMEM((seq // 256, NSA_KV_HEADS * (HEAD_DIM + 16), 256), BF16),
                        pltpu.VMEM((seq // 128, NSA_KV_HEADS * (HEAD_DIM + 16), 128), BF16)],
        compiler_params=_cparams("parallel", "arbitrary"),
        name="nsa",
    )(z, z, kc, vct, z3, z3, z3, z3, t2, tab, ov)


def _lru_kernel(x0_ref, x1_ref, g0_ref, g1_ref, cw_ref, cb_ref, wa_ref, ba_ref, wx_ref, bx_ref, sp_ref,
                o0_ref, o1_ref, xprev_ref, h_ref, a_scr, b_scr, *, nb, tc):
    @pl.when(pl.program_id(0) == 0)
    def _():
        xprev_ref[...] = jnp.zeros_like(xprev_ref)
        h_ref[...] = jnp.zeros_like(h_ref)

    x = jnp.concatenate([x0_ref[...], x1_ref[...]], axis=1)
    rows, width = x.shape
    grp = nb * OCT
    xm = jnp.concatenate([xprev_ref[...], x[:rows - grp]], axis=0)
    xprev_ref[...] = x[rows - grp:]
    x3 = x.reshape(rows // OCT, OCT, width)
    xm3 = xm.reshape(rows // OCT, OCT, width)
    tlo = lax.broadcasted_iota(jnp.int32, (1, OCT, 1), 1)
    cw = cw_ref[...]
    xc = cb_ref[...].reshape(1, 1, width) + cw[CONV_WIDTH - 1:CONV_WIDTH, :].reshape(1, 1, width) * x3
    for k in range(1, CONV_WIDTH):
        delayed = jnp.where(tlo >= k, pltpu.roll(x3, k, 1), pltpu.roll(xm3, k, 1))
        xc = xc + cw[CONV_WIDTH - 1 - k:CONV_WIDTH - k, :].reshape(1, 1, width) * delayed
    xc = xc.reshape(rows, width)
    xcb = xc.astype(BF16)
    gate_r = jax.nn.sigmoid(jnp.dot(xcb, wa_ref[...], preferred_element_type=F32) + ba_ref[...])
    gate_i = jax.nn.sigmoid(jnp.dot(xcb, wx_ref[...], preferred_element_type=F32) + bx_ref[...])
    log_a = -LRU_C * gate_r * sp_ref[...]
    a = jnp.exp(log_a)
    th = jnp.tanh(log_a)
    bt = jnp.sqrt(-2.0 * th / (1.0 - th)) * gate_i * xc
    for j in range(2):
        a_scr[j] = a[:, j * 128:(j + 1) * 128]
        b_scr[j] = bt[:, j * 128:(j + 1) * 128]

    def step(t, h):
        r0 = _octet_row(t, nb)
        new = []
        for j in range(2):
            hj = a_scr[j, pl.ds(r0, nb, stride=OCT), :] * h[j] + b_scr[j, pl.ds(r0, nb, stride=OCT), :]
            b_scr[j, pl.ds(r0, nb, stride=OCT), :] = hj
            new.append(hj)
        return tuple(new)

    h = lax.fori_loop(0, tc, step, (h_ref[0], h_ref[1]))
    for j, (g_ref, o_ref) in enumerate(((g0_ref, o0_ref), (g1_ref, o1_ref))):
        h_ref[j] = h[j]
        o_ref[...] = b_scr[j] * jax.nn.gelu(g_ref[...])


def _lru(x0, x1, g0, g1, cw, cb, wa, ba, wx, bx, sp, nb, tc=64):
    rows = x0.shape[0]
    r = tc * nb
    c2 = lambda i: (0, 0)
    slab = pl.BlockSpec((r, 128), lambda i: (i, 0))
    return pl.pallas_call(
        functools.partial(_lru_kernel, nb=nb, tc=tc),
        grid=(rows // r,),
        in_specs=[slab, slab, slab, slab,
                  pl.BlockSpec(cw.shape, c2), pl.BlockSpec(cb.shape, c2),
                  pl.BlockSpec(wa.shape, c2), pl.BlockSpec(ba.shape, c2),
                  pl.BlockSpec(wx.shape, c2), pl.BlockSpec(bx.shape, c2),
                  pl.BlockSpec(sp.shape, c2)],
        out_specs=[slab, slab],
        out_shape=[jax.ShapeDtypeStruct((rows, 128), F32)] * 2,
        scratch_shapes=[pltpu.VMEM((nb * OCT, LRU_WIDTH), F32), pltpu.VMEM((2, nb, 128), F32),
                        pltpu.VMEM((2, r, 128), F32), pltpu.VMEM((2, r, 128), F32)],
        compiler_params=_cparams("arbitrary"),
        name="lru",
    )(x0, x1, g0, g1, cw, cb, wa, ba, wx, bx, sp)


def _block_diag(w):
    h, i, j = w.shape
    eye = jnp.eye(h, dtype=w.dtype)
    return jnp.einsum('hij,hk->hikj', w, eye).reshape(h * i, h * j)


def _mix_ffn_kernel(h_ref, ys0_ref, ys1_ref, yn_ref, yl0_ref, yl1_ref, wglu_ref, wos_ref, won_ref, wol_ref,
                    g_ref, wg_ref, wu_ref, wd_ref, gf_ref, o_ref, *, final):
    tm = h_ref.shape[0]

    def rows(a_ref, b_ref):
        return jnp.concatenate([a_ref[...].reshape(tm, 128), b_ref[...].reshape(tm, 128)], axis=1)

    gl = jnp.dot(rows(ys0_ref, ys1_ref).astype(BF16), wglu_ref[...], preferred_element_type=F32)
    s5 = gl[:, :SSM_WIDTH] * jax.nn.sigmoid(gl[:, SSM_WIDTH:])
    acc = jnp.dot(s5.astype(BF16), wos_ref[...], preferred_element_type=F32)
    acc = acc + jnp.dot(yn_ref[...].astype(BF16), won_ref[...], preferred_element_type=F32)
    acc = acc + jnp.dot(rows(yl0_ref, yl1_ref).astype(BF16), wol_ref[...], preferred_element_type=F32)
    x = h_ref[...] + acc

    y = (x * lax.rsqrt(jnp.mean(x * x, axis=-1, keepdims=True) + RMS_EPS) * g_ref[...]).astype(BF16)
    a = jnp.dot(y, wg_ref[...], preferred_element_type=F32)
    b = jnp.dot(y, wu_ref[...], preferred_element_type=F32)
    m = (jax.nn.silu(a) * b).astype(BF16)
    out = x + jnp.dot(m, wd_ref[...], preferred_element_type=F32)
    if final:
        out = out * lax.rsqrt(jnp.mean(out * out, axis=-1, keepdims=True) + RMS_EPS) * gf_ref[...]
    o_ref[...] = out


def _mix_ffn(h, ys, yn, yl, wglu, wos, won, wol, g, wg, wu, wd, gf, final, b, seq, tm=256):
    d = h.shape[1]
    nt = seq // tm
    row = lambda w: pl.BlockSpec((tm, w), lambda b_, i: (b_ * nt + i, 0))
    const = lambda a: pl.BlockSpec(a.shape, lambda b_, i: (0, 0), pipeline_mode=pl.Buffered(1))
    slab4 = lambda s: s.reshape(seq // OCT, b, OCT, 128)
    return pl.pallas_call(
        functools.partial(_mix_ffn_kernel, final=final),
        grid=(b, nt),
        in_specs=[row(d), _slab_spec(tm, nt), _slab_spec(tm, nt), row(NSA_WIDTH),
                  _slab_spec(tm, nt), _slab_spec(tm, nt),
                  const(wglu), const(wos), const(won), const(wol),
                  const(g), const(wg), const(wu), const(wd), const(gf)],
        out_specs=row(d),
        out_shape=jax.ShapeDtypeStruct((b * seq, d), F32),
        compiler_params=_cparams("parallel", "parallel"),
        name="mix_ffn",
    )(h, slab4(ys[0]), slab4(ys[1]), yn, slab4(yl[0]), slab4(yl[1]), wglu, wos, won, wol, g, wg, wu, wd, gf)


def _head_interleave(w, axis):
    shape = w.shape
    split = shape[:axis] + (NSA_KV_HEADS, NSA_GQA, HEAD_DIM) + shape[axis + 1:]
    return jnp.swapaxes(w.reshape(split), axis, axis + 1).reshape(shape)


def _prep_w_in(w_in):
    o1 = SSM_WIDTH
    o2 = o1 + NSA_WIDTH
    o3 = o2 + 6 * NSA_KV_WIDTH
    o4 = o3 + 3 * NSA_Q_HEADS
    wq = _head_interleave(w_in[:, o1:o2], 1)
    pad = jnp.zeros((w_in.shape[0], 128 - 3 * NSA_Q_HEADS), w_in.dtype)
    w = jnp.concatenate([wq, w_in[:, o2:o3], w_in[:, o3:o4], pad, w_in[:, :o1], w_in[:, o4:]], axis=1)
    return w.astype(BF16)


def _s5_mixer(slabs, b, p):
    we, tz, cp, a16, dg = _s5_tables(p['s5_lam_re'], p['s5_lam_im'], p['s5_log_dt'], p['s5_b_re'],
                                     p['s5_b_im'], p['s5_c_re'], p['s5_c_im'], p['s5_d'])
    return _s5(slabs[0], slabs[1], we, tz, cp, a16, dg, nb=b)


def _nsa_mixer(z, b, seq, bias_tabs, ov, p):
    w1k, w2k, c0k = _compress_weights(p['nsa_pe_k'], p['nsa_w1_k'], p['nsa_w2_k'])
    w1v, w2v, c0v = _compress_weights(p['nsa_pe_v'], p['nsa_w1_v'], p['nsa_w2_v'])
    kc, vct = _compress(z.reshape(b, seq, Z_NSA), w1k, w1v, w2k, w2v, c0k, c0v)
    tab, t2 = bias_tabs
    return _nsa(z, kc, vct, t2, tab, ov, b, seq)


def _lru_mixer(slabs, b, p):
    sp = jax.nn.softplus(-p['lru_lam'].astype(F32)).reshape(1, -1)
    return _lru(slabs[0], slabs[1], slabs[2], slabs[3],
                p['lru_conv_w'].astype(F32), p['lru_conv_b'].astype(F32).reshape(1, -1),
                _block_diag(p['lru_w_a']).astype(BF16), p['lru_b_a'].astype(F32).reshape(1, -1),
                _block_diag(p['lru_w_x']).astype(BF16), p['lru_b_x'].astype(F32).reshape(1, -1),
                sp, nb=b)


def _layer(h2, b, seq, bias_tabs, ov, p, final, norm_final):
    z, slabs = _inproj(h2, p['norm_mix'].reshape(1, -1), _prep_w_in(p['w_in']), b, seq)
    ys = _s5_mixer(slabs[0:2], b, p)
    yn = _nsa_mixer(z, b, seq, bias_tabs, ov, p)
    yl = _lru_mixer(slabs[2:6], b, p)
    w_out = p['w_out']
    wos = w_out[:SSM_WIDTH].astype(BF16)
    won = _head_interleave(w_out[SSM_WIDTH:SSM_WIDTH + NSA_WIDTH], 0).astype(BF16)
    wol = w_out[SSM_WIDTH + NSA_WIDTH:].astype(BF16)
    return _mix_ffn(h2, ys, yn, yl, p['s5_w_glu'].astype(BF16), wos, won, wol,
                    p['norm_ffn'].reshape(1, -1), p['w_gate'].astype(BF16), p['w_up'].astype(BF16),
                    p['w_down'].astype(BF16), norm_final.reshape(1, -1), final, b, seq)


def _overlap_t(seq):
    ncp = seq // CMP_STRIDE
    nsel = seq // SEL_LEN
    cs = np.arange(ncp) * CMP_STRIDE
    ss = np.arange(nsel) * SEL_LEN
    ovl = (cs[None, :] < ss[:, None] + SEL_LEN) & (ss[:, None] < cs[None, :] + CMP_LEN)
    ovl[:, ncp - 1] = False
    return jnp.asarray(ovl.astype(np.float32))


def kernel(x, rel_bias_table, norm_mix, w_in, w_out, s5_lam_re, s5_lam_im, s5_log_dt, s5_b_re, s5_b_im, s5_c_re, s5_c_im, s5_d, s5_w_glu, nsa_pe_k, nsa_w1_k, nsa_w2_k, nsa_pe_v, nsa_w1_v, nsa_w2_v, lru_conv_w, lru_conv_b, lru_w_a, lru_b_a, lru_w_x, lru_b_x, lru_lam, norm_ffn, w_gate, w_up, w_down, norm_final):
    b, seq, d = x.shape
    depth = norm_mix.shape[0]
    per_layer = dict(norm_mix=norm_mix, w_in=w_in, w_out=w_out, s5_lam_re=s5_lam_re, s5_lam_im=s5_lam_im,
                     s5_log_dt=s5_log_dt, s5_b_re=s5_b_re, s5_b_im=s5_b_im, s5_c_re=s5_c_re,
                     s5_c_im=s5_c_im, s5_d=s5_d, s5_w_glu=s5_w_glu, nsa_pe_k=nsa_pe_k, nsa_w1_k=nsa_w1_k,
                     nsa_w2_k=nsa_w2_k, nsa_pe_v=nsa_pe_v, nsa_w1_v=nsa_w1_v, nsa_w2_v=nsa_w2_v,
                     lru_conv_w=lru_conv_w, lru_conv_b=lru_conv_b, lru_w_a=lru_w_a, lru_b_a=lru_b_a,
                     lru_w_x=lru_w_x, lru_b_x=lru_b_x, lru_lam=lru_lam, norm_ffn=norm_ffn, w_gate=w_gate,
                     w_up=w_up, w_down=w_down)
    bias_tabs = _bias_tables(rel_bias_table, seq)
    ov = _overlap_t(seq)
    h2 = x.reshape(b * seq, d)
    for l in range(depth):
        p = {k: v[l] for k, v in per_layer.items()}
        h2 = _layer(h2, b, seq, bias_tabs, ov, p, l == depth - 1, norm_final)
    return h2.reshape(b, seq, d)
```

```python
import functools
import math

import numpy as np
import jax
import jax.numpy as jnp
from jax import lax
from jax.experimental import pallas as pl
from jax.experimental.pallas import tpu as pltpu

F32 = jnp.float32
BF16 = jnp.bfloat16

D_MODEL = 1024
SSM_WIDTH = 256
NSA_WIDTH = 512
LRU_WIDTH = 256
S5_GROUP = 16
S5_GROUPS = 16
S5_STATE = 64
HEAD_DIM = 64
NSA_Q_HEADS = 8
NSA_KV_HEADS = 2
NSA_GQA = 4
NSA_KV_WIDTH = 128
CMP_LEN = 32
CMP_STRIDE = 16
SEL_LEN = 64
SEL_TOPK = 8
WINDOW = 256
Q_BLOCK = 64
LRU_HEADS = 4
LRU_HEAD_DIM = 64
CONV_WIDTH = 4
LRU_C = 8.0
REL_BUCKETS = 32
REL_MAX_DIST = 128
D_FF = 2816
NEG_INF = -1e9
LOG2E = 1.4426950408889634
RMS_EPS = 1e-6

NO_SLOT = -3e38

S5_CHUNK = 16
Z_Q, Z_KV, Z_G, Z_NSA = 0, 512, 1280, 1408
N_SLAB = 6
OCT = 8

VMEM_LIMIT = 56 * 1024 * 1024


def _cparams(*sem):
    return pltpu.CompilerParams(dimension_semantics=sem, vmem_limit_bytes=VMEM_LIMIT)


def _inproj_kernel(x_ref, g_ref, w_ref, o_ref, *slab_refs):
    x = x_ref[...]
    y = x * lax.rsqrt(jnp.mean(x * x, axis=-1, keepdims=True) + RMS_EPS) * g_ref[...]
    res = jnp.dot(y.astype(BF16), w_ref[...], preferred_element_type=F32)
    o_ref[...] = res[:, :Z_NSA]
    for m, s_ref in enumerate(slab_refs):
        s_ref[...] = res[:, Z_NSA + m * 128:Z_NSA + (m + 1) * 128].reshape(s_ref.shape)


def _slab_spec(tm, nt):
    return pl.BlockSpec((tm // OCT, None, OCT, 128), lambda b, i: (i, b, 0, 0))


def _inproj(x2, g, w, b, seq, tm=1024):
    d = x2.shape[1]
    n = w.shape[1]
    nt = seq // tm
    slab = jax.ShapeDtypeStruct((seq // OCT, b, OCT, 128), F32)
    outs = pl.pallas_call(
        _inproj_kernel,
        grid=(b, nt),
        in_specs=[pl.BlockSpec((tm, d), lambda b_, i: (b_ * nt + i, 0)),
                  pl.BlockSpec((1, d), lambda b_, i: (0, 0)),
                  pl.BlockSpec((d, n), lambda b_, i: (0, 0))],
        out_specs=[pl.BlockSpec((tm, Z_NSA), lambda b_, i: (b_ * nt + i, 0))] + [_slab_spec(tm, nt)] * N_SLAB,
        out_shape=[jax.ShapeDtypeStruct((b * seq, Z_NSA), F32)] + [slab] * N_SLAB,
        compiler_params=_cparams("parallel", "parallel"),
        name="inproj",
    )(x2, g, w)
    return outs[0], [s.reshape(b * seq, 128) for s in outs[1:]]


def _lane_regroup(x):
    r = x.shape[0]
    cols = [x[:, v * 128:(v + 1) * 128] for v in range(32)]
    piece = lax.broadcasted_iota(jnp.int32, (r, 128), 1) // 16
    out = [None] * 32
    for ah in range(2):
        for bh in range(2):
            vs = [cols[2 * (8 * ah + i) + bh] for i in range(8)]
            for stage in (4, 2, 1):
                upper = (piece & stage) != 0
                for i in range(8):
                    if i & stage:
                        continue
                    lo, hi = vs[i], vs[i + stage]
                    vs[i] = jnp.where(upper, pltpu.roll(hi, stage * 16, 1), lo)
                    vs[i + stage] = jnp.where(upper, hi, pltpu.roll(lo, 128 - stage * 16, 1))
            for j in range(8):
                out[2 * (8 * bh + j) + ah] = vs[j]
    return jnp.concatenate(out, axis=1)


def _octet_row(t, nb):
    return (t // OCT) * (nb * OCT) + t % OCT


def _s5_kernel(u0_ref, u1_ref, we_ref, tz_ref, cp_ref, a_ref, d_ref, o0_ref, o1_ref, e_ref, carry_ref,
               *, nb, kc):
    @pl.when(pl.program_id(0) == 0)
    def _():
        carry_ref[...] = jnp.zeros_like(carry_ref)

    u_refs = (u0_ref, u1_ref)
    u = jnp.concatenate(
        [jnp.concatenate([u_refs[j][pl.ds(_octet_row(k * S5_CHUNK + s, nb), nb, stride=OCT), :]
                          for s in range(S5_CHUNK) for j in range(2)], axis=1)
         for k in range(kc)], axis=0)
    ug = _lane_regroup(u)
    ugb = ug.astype(BF16)
    npair = S5_GROUPS // 2
    for p in range(npair):
        e_ref[:, p * 256:(p + 1) * 256] = jnp.dot(ugb[:, p * 512:(p + 1) * 512], we_ref[p],
                                                   preferred_element_type=F32)

    a = a_ref[...]

    def step(k, carry):
        r0 = pl.multiple_of(k * nb, nb)
        e = e_ref[pl.ds(r0, nb), :]
        e_ref[pl.ds(r0, nb), :] = carry
        new = []
        for p in range(npair):
            ar = a[:, p * 256:p * 256 + 128]
            ai = a[:, p * 256 + 128:(p + 1) * 256]
            cr = carry[:, p * 256:p * 256 + 128]
            ci = carry[:, p * 256 + 128:(p + 1) * 256]
            new.append(ar * cr - ai * ci + e[:, p * 256:p * 256 + 128])
            new.append(ar * ci + ai * cr + e[:, p * 256 + 128:(p + 1) * 256])
        return jnp.concatenate(new, axis=1)

    carry_ref[...] = lax.fori_loop(0, kc, step, carry_ref[...])

    eb = e_ref[...].astype(BF16)
    ys = []
    for p in range(npair):
        yc = jnp.dot(eb[:, p * 256:(p + 1) * 256], cp_ref[p], preferred_element_type=F32)
        y0 = jnp.dot(ugb[:, (2 * p) * 256:(2 * p + 1) * 256], tz_ref[2 * p], preferred_element_type=F32)
        y1 = jnp.dot(ugb[:, (2 * p + 1) * 256:(2 * p + 2) * 256], tz_ref[2 * p + 1],
                     preferred_element_type=F32)
        ys.append(yc + jnp.concatenate([y0, y1], axis=1))
    y = jnp.concatenate(ys, axis=1) + d_ref[...] * ug
    y = jax.nn.gelu(y)
    y = _lane_regroup(y)
    o_refs = (o0_ref, o1_ref)
    for k in range(kc):
        for s in range(S5_CHUNK):
            for j in range(2):
                lo = s * SSM_WIDTH + j * 128
                o_refs[j][pl.ds(_octet_row(k * S5_CHUNK + s, nb), nb, stride=OCT), :] = \
                    y[k * nb:(k + 1) * nb, lo:lo + 128]


def _s5(u0, u1, we, tz, cp, a16, dg, nb, kc=16):
    rows = u0.shape[0]
    r = kc * S5_CHUNK * nb
    const3 = lambda i: (0, 0, 0)
    slab = pl.BlockSpec((r, 128), lambda i: (i, 0))
    return pl.pallas_call(
        functools.partial(_s5_kernel, nb=nb, kc=kc),
        grid=(rows // r,),
        in_specs=[slab, slab,
                  pl.BlockSpec(we.shape, const3),
                  pl.BlockSpec(tz.shape, const3),
                  pl.BlockSpec(cp.shape, const3),
                  pl.BlockSpec(a16.shape, lambda i: (0, 0)),
                  pl.BlockSpec(dg.shape, lambda i: (0, 0))],
        out_specs=[slab, slab],
        out_shape=[jax.ShapeDtypeStruct((rows, 128), F32)] * 2,
        scratch_shapes=[pltpu.VMEM((kc * nb, 2048), F32), pltpu.VMEM((nb, 2048), F32)],
        compiler_params=_cparams("arbitrary"),
        name="s5",
    )(u0, u1, we, tz, cp, a16, dg)


def _s5_tables(lam_re, lam_im, log_dt, b_re, b_im, c_re, c_im, d_skip):
    L = S5_CHUNK
    G, P, C = S5_GROUPS, S5_STATE, S5_GROUP
    dt = jnp.exp(log_dt.astype(F32))[:, None]
    lr = lam_re.astype(F32)
    li = lam_im.astype(F32)
    mag = jnp.exp(lr * dt)
    ang = li * dt
    ab_re = mag * jnp.cos(ang)
    ab_im = mag * jnp.sin(ang)
    den = lr * lr + li * li
    f_re = ((ab_re - 1.0) * lr + ab_im * li) / den
    f_im = (ab_im * lr - (ab_re - 1.0) * li) / den
    br = b_re.astype(F32)
    bi = b_im.astype(F32)
    bb_re = f_re[..., None] * br - f_im[..., None] * bi
    bb_im = f_re[..., None] * bi + f_im[..., None] * br
    cr = c_re.astype(F32)
    ci = c_im.astype(F32)
    tau = jnp.arange(L + 1, dtype=F32)[:, None, None]
    pr = jnp.exp(lr * dt * tau) * jnp.cos(li * dt * tau)
    pi = jnp.exp(lr * dt * tau) * jnp.sin(li * dt * tau)

    prs = pr[:L][::-1]
    pis = pi[:L][::-1]
    we_re = jnp.einsum('sgp,gpc->gscp', prs, bb_re) - jnp.einsum('sgp,gpc->gscp', pis, bb_im)
    we_im = jnp.einsum('sgp,gpc->gscp', prs, bb_im) + jnp.einsum('sgp,gpc->gscp', pis, bb_re)
    we_re = we_re.reshape(G, L * C, P)
    we_im = we_im.reshape(G, L * C, P)
    z = jnp.zeros_like(we_re[0::2])
    top = jnp.concatenate([we_re[0::2], z, we_im[0::2], z], axis=-1)
    bot = jnp.concatenate([z, we_re[1::2], z, we_im[1::2]], axis=-1)
    we = jnp.concatenate([top, bot], axis=1)

    m_re = pr[:L, :, :, None] * bb_re[None] - pi[:L, :, :, None] * bb_im[None]
    m_im = pr[:L, :, :, None] * bb_im[None] + pi[:L, :, :, None] * bb_re[None]
    kern = jnp.einsum('gop,tgpc->gcto', cr, m_re) - jnp.einsum('gop,tgpc->gcto', ci, m_im)
    padded = jnp.concatenate([jnp.zeros((G, C, L * C), F32), kern.reshape(G, C, L * C)], axis=-1)
    tz = jnp.stack([padded[:, :, (L - s) * C:(2 * L - s) * C] for s in range(L)], axis=1)
    tz = tz.reshape(G, L * C, L * C)

    pr1 = pr[1:]
    pi1 = pi[1:]
    cp_re = jnp.einsum('gop,tgp->gpto', cr, pr1) - jnp.einsum('gop,tgp->gpto', ci, pi1)
    cp_im = -(jnp.einsum('gop,tgp->gpto', cr, pi1) + jnp.einsum('gop,tgp->gpto', ci, pr1))
    cp_re = cp_re.reshape(G, P, L * C)
    cp_im = cp_im.reshape(G, P, L * C)
    zc = jnp.zeros_like(cp_re[0::2])
    cp = jnp.concatenate([
        jnp.concatenate([cp_re[0::2], zc], axis=-1),
        jnp.concatenate([zc, cp_re[1::2]], axis=-1),
        jnp.concatenate([cp_im[0::2], zc], axis=-1),
        jnp.concatenate([zc, cp_im[1::2]], axis=-1)], axis=1)

    a_re = pr[L].reshape(G // 2, 2 * P)
    a_im = pi[L].reshape(G // 2, 2 * P)
    a16 = jnp.concatenate([a_re, a_im], axis=-1).reshape(1, G * 2 * P)
    dg = jnp.broadcast_to(d_skip.astype(F32).reshape(G, 1, C), (G, L, C)).reshape(1, G * L * C)
    return we.astype(BF16), tz.astype(BF16), cp.astype(BF16), a16, dg


def _compress_kernel(ak_ref, av_ref, w1k_ref, w1v_ref, w2k_ref, w2v_ref, c0k_ref, c0v_ref,
                     kc_ref, vct_ref):
    def run(a_ref, w1_ref, w2_ref, c0_ref):
        nc = a_ref.shape[1] // CMP_STRIDE
        a = jnp.concatenate([a_ref[0, pl.ds(l, nc, stride=CMP_STRIDE), :] for l in range(CMP_STRIDE)],
                            axis=1).astype(BF16)
        p1 = jnp.dot(a, w1_ref[0], preferred_element_type=F32)
        p2 = jnp.dot(a, w1_ref[1], preferred_element_type=F32)
        n = p1.shape[0]
        pre = p1 + pltpu.roll(p2, n - 1, 0) + c0_ref[...]
        out = jnp.dot(jax.nn.gelu(pre).astype(BF16), w2_ref[...], preferred_element_type=F32)
        row = lax.broadcasted_iota(jnp.int32, out.shape, 0)
        return jnp.where(row < n - 1, out, 0.0)

    kc_ref[0] = run(ak_ref, w1k_ref, w2k_ref, c0k_ref).astype(BF16)
    vct_ref[0] = run(av_ref, w1v_ref, w2v_ref, c0v_ref).T.astype(BF16)


def _compress(z3, w1k, w1v, w2k, w2v, c0k, c0v):
    b, seq, _ = z3.shape
    nc = seq // CMP_STRIDE
    c3 = lambda i: (0, 0, 0)
    c2 = lambda i: (0, 0)
    return pl.pallas_call(
        _compress_kernel,
        grid=(b,),
        in_specs=[pl.BlockSpec((1, seq, 128), lambda i: (i, 0, Z_KV // 128)),
                  pl.BlockSpec((1, seq, 128), lambda i: (i, 0, Z_KV // 128 + 1)),
                  pl.BlockSpec(w1k.shape, c3), pl.BlockSpec(w1v.shape, c3),
                  pl.BlockSpec(w2k.shape, c2), pl.BlockSpec(w2v.shape, c2),
                  pl.BlockSpec(c0k.shape, c2), pl.BlockSpec(c0v.shape, c2)],
        out_specs=[pl.BlockSpec((1, nc, 128), lambda i: (i, 0, 0)),
                   pl.BlockSpec((1, 128, nc), lambda i: (i, 0, 0))],
        out_shape=[jax.ShapeDtypeStruct((b, nc, 128), BF16),
                   jax.ShapeDtypeStruct((b, 128, nc), BF16)],
        compiler_params=_cparams("parallel"),
        name="compress",
    )(z3, z3, w1k, w1v, w2k, w2v, c0k, c0v)


def _compress_weights(pe, w1, w2):
    w1 = w1.astype(F32)
    half = (CMP_LEN // 2) * HEAD_DIM
    eye = jnp.eye(NSA_KV_HEADS, dtype=F32)

    def bd(w):
        w = w.reshape(CMP_STRIDE, HEAD_DIM, HEAD_DIM)
        return jnp.einsum('ldo,hk->lhdko', w, eye).reshape(NSA_KV_HEADS * half, NSA_KV_HEADS * HEAD_DIM)

    w1s = jnp.stack([bd(w1[:half]), bd(w1[half:])])
    z2 = jnp.zeros((HEAD_DIM, HEAD_DIM), F32)
    w2f = w2.astype(F32)
    w2s = jnp.concatenate([jnp.concatenate([w2f, z2], axis=1),
                           jnp.concatenate([z2, w2f], axis=1)], axis=0)
    c0 = pe.astype(F32).reshape(1, CMP_LEN * HEAD_DIM) @ w1
    c0 = jnp.concatenate([c0, c0], axis=1)
    return w1s.astype(BF16), w2s.astype(BF16), c0


def _t5_bucket_np(dist):
    n = np.maximum(dist, 0)
    max_exact = REL_BUCKETS // 2
    nf = np.maximum(n, 1).astype(np.float32)
    large = max_exact + (np.log(nf / max_exact) / math.log(REL_MAX_DIST / max_exact)
                         * (REL_BUCKETS - max_exact)).astype(np.int32)
    large = np.minimum(large, REL_BUCKETS - 1)
    return np.where(n < max_exact, n, large)


TAB_FAR, TAB_WIN_EDGE, TAB_NONE = 3, 4, 5


def _bias_tables(rel_table, seq):
    tbl = rel_table.astype(F32)
    heads = NSA_Q_HEADS

    def lookup(bucket):
        onehot = jnp.asarray(bucket[..., None] == np.arange(REL_BUCKETS)).astype(F32)
        return jnp.einsum('...k,kh->...h', onehot, tbl, precision=lax.Precision.HIGHEST) * LOG2E

    r = np.arange(Q_BLOCK)[None, :]
    c = np.arange(SEL_LEN)[:, None]
    near = lookup(_t5_bucket_np(64 * np.arange(3)[:, None, None] + (r - c)[None]))
    far = jnp.broadcast_to(tbl[REL_BUCKETS - 1] * LOG2E, (SEL_LEN, Q_BLOCK, heads))
    none = jnp.full((SEL_LEN, Q_BLOCK, heads), NO_SLOT, F32)
    causal = jnp.asarray(r - c >= 0)[:, :, None]
    edge = jnp.asarray(c > r)[:, :, None]
    tab = jnp.stack([jnp.where(causal, near[0], none), near[1], near[2], far,
                     jnp.where(edge, far, none), none])
    tab = tab.transpose(0, 1, 3, 2).reshape(6, SEL_LEN, heads * Q_BLOCK)
    ncp = seq // CMP_STRIDE
    k = np.arange(2 * ncp + 4)[:, None]
    dist = r - CMP_STRIDE * (k - (ncp - 4)) - (CMP_LEN - 1)
    t2 = jnp.where(jnp.asarray(dist >= 0)[:, :, None], lookup(_t5_bucket_np(dist)), NO_SLOT)
    t2 = t2.transpose(0, 2, 1).reshape(2 * ncp + 4, heads * Q_BLOCK)
    t2 = jnp.stack([t2[:2 * ncp], t2[4:]])
    return tab, t2


def _nsa_kernel(q_ref, gl_ref, kc_ref, vct_ref, ks_ref, vs_ref, kw_ref, vw_ref, t2_ref, tab_ref,
                ov_ref, o_ref, s_ref, p_ref, acc_ref, accw_ref, vst_ref, vwt_ref, *, ncp, nw):
    @pl.when(pl.program_id(1) == 0)
    def _():
        for v_ref, vt_ref in ((vs_ref, vst_ref), (vw_ref, vwt_ref)):
            kt = vt_ref.shape[2]
            ones = jnp.ones((16, kt), F32)
            for u in range(vt_ref.shape[0]):
                vt = v_ref[0, u * kt:(u + 1) * kt, :].T
                vt_ref[u] = jnp.concatenate([vt[0:HEAD_DIM], ones, vt[HEAD_DIM:], ones], axis=0).astype(BF16)

    qi0 = pl.program_id(1) * nw
    cw = nw * NSA_GQA * Q_BLOCK
    ncol = NSA_KV_HEADS * cw
    vrows = HEAD_DIM + 16
    lane = lax.broadcasted_iota(jnp.int32, (Q_BLOCK, 128), 1)

    q = q_ref[...] * (HEAD_DIM ** -0.5 * LOG2E)
    pieces = []
    for h in range(NSA_KV_HEADS):
        keep = (lane >= 64) if h == 1 else (lane < 64)
        for w in range(nw):
            for g in range(NSA_GQA):
                pieces.append(jnp.where(keep, q[w * 64:(w + 1) * 64, g * 128:(g + 1) * 128], 0.0))
    qpad = jnp.concatenate(pieces, axis=0).astype(BF16)

    col = lax.broadcasted_iota(jnp.int32, (1, ncol), 1)
    rcol = col % Q_BLOCK
    qcol = qi0 + (col // (NSA_GQA * Q_BLOCK)) % nw
    tvec = Q_BLOCK * qcol + rcol
    nt_dims = (((1,), (1,)), ((), ()))

    def per_head_cols(tiles):
        return jnp.concatenate([tiles[w][:, h * 256:(h + 1) * 256]
                                for h in range(NSA_KV_HEADS) for w in range(nw)], axis=1)

    def slot_terms(m, n_masked):
        m_fin = jnp.where(n_masked > 0, jnp.maximum(m, NEG_INF * LOG2E), m)
        return jnp.exp2(m - m_fin), n_masked * jnp.exp2(NEG_INF * LOG2E - m_fin)

    t2_tiles = []
    for w in range(nw):
        start = (ncp - 4) - 4 * (qi0 + w)
        par = (start // 4) % 2
        a0 = pl.multiple_of(start - 4 * par, 8)
        t2_tiles.append(t2_ref[par, pl.ds(a0, ncp), :])
    s = lax.dot_general(kc_ref[0], qpad, nt_dims, preferred_element_type=F32) + per_head_cols(t2_tiles)
    m = jnp.max(s, axis=0, keepdims=True)
    e = jnp.exp2(s - m)
    n_valid = jnp.clip(jnp.right_shift(tvec - (CMP_STRIDE - 1), 4), 0, ncp - 1)
    scale, extra = slot_terms(m, (ncp - 1 - n_valid).astype(F32))
    p_c = e * (scale / (jnp.sum(e, axis=0, keepdims=True) * scale + extra))
    p_cb = p_c.astype(BF16)
    o_c = [jnp.dot(vct_ref[0, h * 64:(h + 1) * 64, :], p_cb[:, h * cw:(h + 1) * cw],
                   preferred_element_type=F32) for h in range(NSA_KV_HEADS)]

    lane_c = lax.broadcasted_iota(jnp.int32, (ncp, 128), 1)
    parts = []
    for w in range(nw):
        halves = []
        for h in range(NSA_KV_HEADS):
            base = h * cw + w * 256
            ph = p_c[:, base:base + 128] + p_c[:, base + 128:base + 256]
            halves.append(ph + pltpu.roll(ph, 64, 1))
        parts.append(jnp.where(lane_c < 64, halves[0], halves[1]))
    imp = jnp.dot(ov_ref[...], jnp.concatenate(parts, axis=1), preferred_element_type=F32,
                  precision=lax.Precision.HIGHEST)
    nsel = imp.shape[0]
    jidx = lax.broadcasted_iota(jnp.int32, (nsel, 1), 0)
    qv = qi0 + lax.broadcasted_iota(jnp.int32, (1, nw * 128), 1) // 128
    forced = jnp.logical_or(jidx == 0, jnp.logical_or(jidx == qv, jidx == qv - 1))
    avail = jidx <= qv
    cand = jnp.logical_and(avail, jnp.logical_not(forced))
    budget = SEL_TOPK - (1 + jnp.where(qv >= 1, 1, 0) + jnp.where(qv >= 2, 1, 0))
    rank = jnp.zeros((nsel, nw * 128), jnp.int32)
    for jp in range(1, nsel - 2):
        row = imp[jp:jp + 1, :]
        ge = jnp.where(row >= imp, 1, 0)
        gt = jnp.where(row > imp, 1, 0)
        is_cand = jnp.where(qv - 2 >= jp, 1, 0)
        rank = rank + jnp.where(jidx > jp, ge, gt) * is_cand
    sel = jnp.logical_or(jnp.logical_and(forced, avail), jnp.logical_and(cand, rank < budget))
    seladd = jnp.where(sel, 0.0, NO_SLOT)
    pad = jnp.zeros((128 - nsel, 128), F32)
    mask_rows = []
    for w in range(nw):
        mask_rows.append(jnp.concatenate([seladd[:, w * 128:(w + 1) * 128], pad], axis=0).T)
    sel_cols = jnp.concatenate([mask_rows[w][h * 64:(h + 1) * 64, :] for h in range(NSA_KV_HEADS)
                                for w in range(nw) for _ in range(NSA_GQA)], axis=0)
    qsel = jnp.concatenate([qpad, sel_cols.astype(BF16)], axis=1)

    def block_adds(blk, selected):
        tiles = []
        for w in range(nw):
            d = qi0 + w - blk
            if selected:
                e = jnp.where(d < 0, TAB_FAR, jnp.minimum(d, TAB_FAR))
            else:
                outside = jnp.logical_or(blk < 0, jnp.logical_or(d < 0, d > TAB_NONE))
                e = jnp.where(outside, TAB_NONE, d)
            tiles.append(tab_ref[e])
        return per_head_cols(tiles)

    def softmax_tile(m_i, s, adds):
        sm = jnp.concatenate([s[k * 64:(k + 1) * 64] + adds[k] for k in range(len(adds))], axis=0)
        m_new = jnp.maximum(m_i, jnp.max(sm, axis=0, keepdims=True))
        return m_new, jnp.exp2(m_i - m_new), jnp.exp2(sm - m_new).astype(BF16)

    def add_values(ref, alpha, vt_tile, p):
        for h in range(NSA_KV_HEADS):
            ref[h] = alpha[:, h * cw:(h + 1) * cw] * ref[h] + jnp.dot(
                vt_tile[h * vrows:(h + 1) * vrows, :], p[:, h * cw:(h + 1) * cw], preferred_element_type=F32)

    m_init = jnp.full((1, ncol), NO_SLOT, F32)

    accw_ref[...] = jnp.zeros_like(accw_ref)
    m_w = m_init
    for i in range(nw // 2 + 2):
        jj = qi0 // 2 - 2 + i
        jc = jnp.maximum(jj, 0)
        r0 = pl.multiple_of(jc * 128, 128)
        s = lax.dot_general(kw_ref[0, pl.ds(r0, 128), :].astype(BF16), qpad, nt_dims,
                            preferred_element_type=F32)
        m_w, alpha, p = softmax_tile(m_w, s, [block_adds(2 * jj + half, False) for half in range(2)])
        add_values(accw_ref, alpha, vwt_ref[jc], p)

    n_tiles = (qi0 + nw - 1) // 4 + 1

    def masked_scores(u):
        r0 = pl.multiple_of(u * 256, 256)
        block_of_row = 4 * u + lax.broadcasted_iota(jnp.int32, (256, 128), 0) // SEL_LEN
        onehot = jnp.where(lax.broadcasted_iota(jnp.int32, (256, 128), 1) == block_of_row, 1.0, 0.0)
        keys = jnp.concatenate([ks_ref[0, pl.ds(r0, 256), :].astype(BF16), onehot.astype(BF16)], axis=1)
        s = lax.dot_general(keys, qsel, nt_dims, preferred_element_type=F32)
        tmax = None
        for k in range(4):
            sk = s[k * 64:(k + 1) * 64] + block_adds(4 * u + k, True)
            s_ref[k * 64:(k + 1) * 64, :] = sk
            kmax = jnp.max(sk, axis=0, keepdims=True)
            tmax = kmax if tmax is None else jnp.maximum(tmax, kmax)
        return tmax

    tmax0 = masked_scores(0)
    p_ref[...] = jnp.zeros_like(p_ref)
    acc_ref[...] = jnp.zeros_like(acc_ref)

    def sel_body(u, carry):
        m_i, alpha_prev, tmax = carry
        add_values(acc_ref, alpha_prev, vst_ref[jnp.maximum(u - 1, 0)], p_ref[...])
        m_new = jnp.maximum(m_i, tmax)
        p_ref[...] = jnp.exp2(s_ref[...] - m_new).astype(BF16)
        tmax_next = masked_scores(jnp.minimum(u + 1, n_tiles - 1))
        return m_new, jnp.exp2(m_i - m_new), tmax_next

    m_s, alpha_last, _ = lax.fori_loop(0, n_tiles, sel_body, (m_init, jnp.ones((1, ncol), F32), tmax0))
    add_values(acc_ref, alpha_last, vst_ref[n_tiles - 1], p_ref[...])

    nm_s = (SEL_LEN * jnp.maximum(SEL_TOPK - (qcol + 1), 0) + (SEL_LEN - 1 - rcol)).astype(F32)
    nm_w = (WINDOW + Q_BLOCK - jnp.minimum(tvec + 1, WINDOW)).astype(F32)

    def normalised(m, ref, n_masked):
        scale, extra = slot_terms(m, n_masked)
        out = []
        for h in range(NSA_KV_HEADS):
            sl = slice(h * cw, (h + 1) * cw)
            acc = ref[h]
            denom = acc[HEAD_DIM:HEAD_DIM + 1, :] * scale[:, sl] + extra[:, sl]
            out.append((scale[:, sl] / denom, acc[0:HEAD_DIM, :]))
        return out

    o_s = normalised(m_s, acc_ref, nm_s)
    o_w = normalised(m_w, accw_ref, nm_w)

    lane1 = lax.broadcasted_iota(jnp.int32, (1, 128), 1)
    gts = []
    for w in range(nw):
        g = jax.nn.sigmoid(gl_ref[w * 64:(w + 1) * 64, :])
        gts.append(jnp.concatenate([g, g], axis=0).T)

    def gate_vec(h, br):
        tiles = []
        for w in range(nw):
            for gg in range(2):
                c0 = (h * 4 + 2 * gg) * 3 + br
                c1 = (h * 4 + 2 * gg + 1) * 3 + br
                tiles.append(jnp.where(lane1 < 64, gts[w][c0:c0 + 1, :], gts[w][c1:c1 + 1, :]))
        return jnp.concatenate(tiles, axis=1)

    tot = []
    for h in range(NSA_KV_HEADS):
        t = gate_vec(h, 0) * o_c[h]
        t = t + (gate_vec(h, 1) * o_s[h][0]) * o_s[h][1]
        t = t + (gate_vec(h, 2) * o_w[h][0]) * o_w[h][1]
        tot.append(t)
    for w in range(nw):
        ot = jnp.concatenate([t[:, w * 256:(w + 1) * 256] for t in tot], axis=0).T
        o_ref[w * 64:(w + 1) * 64, :] = jnp.concatenate([ot[g * 64:(g + 1) * 64, :]
                                                         for g in range(NSA_GQA)], axis=1)


def _nsa(z, kc, vct, t2, tab, ov, b, seq, nw=4):
    nst = seq // (Q_BLOCK * nw)
    ncp = seq // CMP_STRIDE
    rows = Q_BLOCK * nw
    cw = nw * NSA_GQA * Q_BLOCK
    z3 = z.reshape(b, seq, Z_NSA)
    kv_col = lambda n: pl.BlockSpec((1, seq, 128), lambda i, j: (i, 0, Z_KV // 128 + n))
    return pl.pallas_call(
        functools.partial(_nsa_kernel, ncp=ncp, nw=nw),
        grid=(b, nst),
        in_specs=[pl.BlockSpec((rows, NSA_WIDTH), lambda i, j: (i * nst + j, Z_Q // NSA_WIDTH)),
                  pl.BlockSpec((rows, 128), lambda i, j: (i * nst + j, Z_G // 128)),
                  pl.BlockSpec((1, ncp, 128), lambda i, j: (i, 0, 0)),
                  pl.BlockSpec((1, 128, ncp), lambda i, j: (i, 0, 0)),
                  kv_col(2), kv_col(3), kv_col(4), kv_col(5),
                  pl.BlockSpec(t2.shape, lambda i, j: (0, 0, 0)),
                  pl.BlockSpec(tab.shape, lambda i, j: (0, 0, 0)),
                  pl.BlockSpec(ov.shape, lambda i, j: (0, 0))],
        out_specs=pl.BlockSpec((rows, NSA_WIDTH), lambda i, j: (i * nst + j, 0)),
        out_shape=jax.ShapeDtypeStruct((b * seq, NSA_WIDTH), F32),
        scratch_shapes=[pltpu.VMEM((256, NSA_KV_HEADS * cw), F32),
                        pltpu.VMEM((256, NSA_KV_HEADS * cw), BF16),
                        pltpu.VMEM((NSA_KV_HEADS, HEAD_DIM + 16, cw), F32),
                        pltpu.VMEM((NSA_KV_HEADS, HEAD_DIM + 16, cw), F32),
                        pltpu.VMEM((seq // 256, NSA_KV_HEADS * (HEAD_DIM + 16), 256), BF16),
                        pltpu.VMEM((seq // 128, NSA_KV_HEADS * (HEAD_DIM + 16), 128), BF16)],
        compiler_params=_cparams("parallel", "arbitrary"),
        name="nsa",
    )(z, z, kc, vct, z3, z3, z3, z3, t2, tab, ov)


def _lru_kernel(x0_ref, x1_ref, g0_ref, g1_ref, cw_ref, cb_ref, wa_ref, ba_ref, wx_ref, bx_ref, sp_ref,
                o0_ref, o1_ref, xprev_ref, h_ref, a_scr, b_scr, *, nb, tc):
    @pl.when(pl.program_id(0) == 0)
    def _():
        xprev_ref[...] = jnp.zeros_like(xprev_ref)
        h_ref[...] = jnp.zeros_like(h_ref)

    x = jnp.concatenate([x0_ref[...], x1_ref[...]], axis=1)
    rows, width = x.shape
    grp = nb * OCT
    xm = jnp.concatenate([xprev_ref[...], x[:rows - grp]], axis=0)
    xprev_ref[...] = x[rows - grp:]
    x3 = x.reshape(rows // OCT, OCT, width)
    xm3 = xm.reshape(rows // OCT, OCT, width)
    tlo = lax.broadcasted_iota(jnp.int32, (1, OCT, 1), 1)
    cw = cw_ref[...]
    xc = cb_ref[...].reshape(1, 1, width) + cw[CONV_WIDTH - 1:CONV_WIDTH, :].reshape(1, 1, width) * x3
    for k in range(1, CONV_WIDTH):
        delayed = jnp.where(tlo >= k, pltpu.roll(x3, k, 1), pltpu.roll(xm3, k, 1))
        xc = xc + cw[CONV_WIDTH - 1 - k:CONV_WIDTH - k, :].reshape(1, 1, width) * delayed
    xc = xc.reshape(rows, width)
    xcb = xc.astype(BF16)
    def sigmoid(v):
        return 0.5 * jnp.tanh(0.5 * v) + 0.5

    gate_r = sigmoid(jnp.dot(xcb, wa_ref[...], preferred_element_type=F32) + ba_ref[...])
    gate_i = sigmoid(jnp.dot(xcb, wx_ref[...], preferred_element_type=F32) + bx_ref[...])
    log_a = -LRU_C * gate_r * sp_ref[...]
    a = jnp.exp(log_a)
    th = jnp.tanh(log_a)
    bt = jnp.sqrt(-2.0 * th / (1.0 - th)) * gate_i * xc
    for j in range(2):
        a_scr[j] = a[:, j * 128:(j + 1) * 128]
        b_scr[j] = bt[:, j * 128:(j + 1) * 128]

    def step(t, h):
        r0 = _octet_row(t, nb)
        new = []
        for j in range(2):
            hj = a_scr[j, pl.ds(r0, nb, stride=OCT), :] * h[j] + b_scr[j, pl.ds(r0, nb, stride=OCT), :]
            b_scr[j, pl.ds(r0, nb, stride=OCT), :] = hj
            new.append(hj)
        return tuple(new)

    h = lax.fori_loop(0, tc, step, (h_ref[0], h_ref[1]))
    for j, (g_ref, o_ref) in enumerate(((g0_ref, o0_ref), (g1_ref, o1_ref))):
        h_ref[j] = h[j]
        o_ref[...] = b_scr[j] * jax.nn.gelu(g_ref[...])


def _lru(x0, x1, g0, g1, cw, cb, wa, ba, wx, bx, sp, nb, tc=64):
    rows = x0.shape[0]
    r = tc * nb
    c2 = lambda i: (0, 0)
    slab = pl.BlockSpec((r, 128), lambda i: (i, 0))
    return pl.pallas_call(
        functools.partial(_lru_kernel, nb=nb, tc=tc),
        grid=(rows // r,),
        in_specs=[slab, slab, slab, slab,
                  pl.BlockSpec(cw.shape, c2), pl.BlockSpec(cb.shape, c2),
                  pl.BlockSpec(wa.shape, c2), pl.BlockSpec(ba.shape, c2),
                  pl.BlockSpec(wx.shape, c2), pl.BlockSpec(bx.shape, c2),
                  pl.BlockSpec(sp.shape, c2)],
        out_specs=[slab, slab],
        out_shape=[jax.ShapeDtypeStruct((rows, 128), F32)] * 2,
        scratch_shapes=[pltpu.VMEM((nb * OCT, LRU_WIDTH), F32), pltpu.VMEM((2, nb, 128), F32),
                        pltpu.VMEM((2, r, 128), F32), pltpu.VMEM((2, r, 128), F32)],
        compiler_params=_cparams("arbitrary"),
        name="lru",
    )(x0, x1, g0, g1, cw, cb, wa, ba, wx, bx, sp)


def _block_diag(w):
    h, i, j = w.shape
    eye = jnp.eye(h, dtype=w.dtype)
    return jnp.einsum('hij,hk->hikj', w, eye).reshape(h * i, h * j)


def _mix_ffn_kernel(h_ref, ys0_ref, ys1_ref, yn_ref, yl0_ref, yl1_ref, wglu_ref, wos_ref, won_ref, wol_ref,
                    g_ref, wg_ref, wu_ref, wd_ref, gf_ref, o_ref, *, final):
    tm = h_ref.shape[0]

    def rows(a_ref, b_ref):
        return jnp.concatenate([a_ref[...].reshape(tm, 128), b_ref[...].reshape(tm, 128)], axis=1)

    gl = jnp.dot(rows(ys0_ref, ys1_ref).astype(BF16), wglu_ref[...], preferred_element_type=F32)
    s5 = gl[:, :SSM_WIDTH] * jax.nn.sigmoid(gl[:, SSM_WIDTH:])
    acc = jnp.dot(s5.astype(BF16), wos_ref[...], preferred_element_type=F32)
    acc = acc + jnp.dot(yn_ref[...].astype(BF16), won_ref[...], preferred_element_type=F32)
    acc = acc + jnp.dot(rows(yl0_ref, yl1_ref).astype(BF16), wol_ref[...], preferred_element_type=F32)
    x = h_ref[...] + acc

    y = (x * lax.rsqrt(jnp.mean(x * x, axis=-1, keepdims=True) + RMS_EPS) * g_ref[...]).astype(BF16)
    a = jnp.dot(y, wg_ref[...], preferred_element_type=F32)
    b = jnp.dot(y, wu_ref[...], preferred_element_type=F32)
    m = (jax.nn.silu(a) * b).astype(BF16)
    out = x + jnp.dot(m, wd_ref[...], preferred_element_type=F32)
    if final:
        out = out * lax.rsqrt(jnp.mean(out * out, axis=-1, keepdims=True) + RMS_EPS) * gf_ref[...]
    o_ref[...] = out


def _mix_ffn(h, ys, yn, yl, wglu, wos, won, wol, g, wg, wu, wd, gf, final, b, seq, tm=512):
    d = h.shape[1]
    nt = seq // tm
    row = lambda w: pl.BlockSpec((tm, w), lambda b_, i: (b_ * nt + i, 0))
    const = lambda a: pl.BlockSpec(a.shape, lambda b_, i: (0, 0), pipeline_mode=pl.Buffered(1))
    slab4 = lambda s: s.reshape(seq // OCT, b, OCT, 128)
    return pl.pallas_call(
        functools.partial(_mix_ffn_kernel, final=final),
        grid=(b, nt),
        in_specs=[row(d), _slab_spec(tm, nt), _slab_spec(tm, nt), row(NSA_WIDTH),
                  _slab_spec(tm, nt), _slab_spec(tm, nt),
                  const(wglu), const(wos), const(won), const(wol),
                  const(g), const(wg), const(wu), const(wd), const(gf)],
        out_specs=row(d),
        out_shape=jax.ShapeDtypeStruct((b * seq, d), F32),
        compiler_params=_cparams("parallel", "parallel"),
        name="mix_ffn",
    )(h, slab4(ys[0]), slab4(ys[1]), yn, slab4(yl[0]), slab4(yl[1]), wglu, wos, won, wol, g, wg, wu, wd, gf)


def _head_interleave(w, axis):
    shape = w.shape
    split = shape[:axis] + (NSA_KV_HEADS, NSA_GQA, HEAD_DIM) + shape[axis + 1:]
    return jnp.swapaxes(w.reshape(split), axis, axis + 1).reshape(shape)


def _prep_w_in(w_in):
    o1 = SSM_WIDTH
    o2 = o1 + NSA_WIDTH
    o3 = o2 + 6 * NSA_KV_WIDTH
    o4 = o3 + 3 * NSA_Q_HEADS
    wq = _head_interleave(w_in[:, o1:o2], 1)
    pad = jnp.zeros((w_in.shape[0], 128 - 3 * NSA_Q_HEADS), w_in.dtype)
    w = jnp.concatenate([wq, w_in[:, o2:o3], w_in[:, o3:o4], pad, w_in[:, :o1], w_in[:, o4:]], axis=1)
    return w.astype(BF16)


def _s5_mixer(slabs, b, p):
    we, tz, cp, a16, dg = _s5_tables(p['s5_lam_re'], p['s5_lam_im'], p['s5_log_dt'], p['s5_b_re'],
                                     p['s5_b_im'], p['s5_c_re'], p['s5_c_im'], p['s5_d'])
    return _s5(slabs[0], slabs[1], we, tz, cp, a16, dg, nb=b)


def _nsa_mixer(z, b, seq, bias_tabs, ov, p):
    w1k, w2k, c0k = _compress_weights(p['nsa_pe_k'], p['nsa_w1_k'], p['nsa_w2_k'])
    w1v, w2v, c0v = _compress_weights(p['nsa_pe_v'], p['nsa_w1_v'], p['nsa_w2_v'])
    kc, vct = _compress(z.reshape(b, seq, Z_NSA), w1k, w1v, w2k, w2v, c0k, c0v)
    tab, t2 = bias_tabs
    return _nsa(z, kc, vct, t2, tab, ov, b, seq)


def _lru_mixer(slabs, b, p):
    sp = jax.nn.softplus(-p['lru_lam'].astype(F32)).reshape(1, -1)
    return _lru(slabs[0], slabs[1], slabs[2], slabs[3],
                p['lru_conv_w'].astype(F32), p['lru_conv_b'].astype(F32).reshape(1, -1),
                _block_diag(p['lru_w_a']).astype(BF16), p['lru_b_a'].astype(F32).reshape(1, -1),
                _block_diag(p['lru_w_x']).astype(BF16), p['lru_b_x'].astype(F32).reshape(1, -1),
                sp, nb=b)


def _layer(h2, b, seq, bias_tabs, ov, p, final, norm_final):
    z, slabs = _inproj(h2, p['norm_mix'].reshape(1, -1), _prep_w_in(p['w_in']), b, seq)
    ys = _s5_mixer(slabs[0:2], b, p)
    yn = _nsa_mixer(z, b, seq, bias_tabs, ov, p)
    yl = _lru_mixer(slabs[2:6], b, p)
    w_out = p['w_out']
    wos = w_out[:SSM_WIDTH].astype(BF16)
    won = _head_interleave(w_out[SSM_WIDTH:SSM_WIDTH + NSA_WIDTH], 0).astype(BF16)
    wol = w_out[SSM_WIDTH + NSA_WIDTH:].astype(BF16)
    return _mix_ffn(h2, ys, yn, yl, p['s5_w_glu'].astype(BF16), wos, won, wol,
                    p['norm_ffn'].reshape(1, -1), p['w_gate'].astype(BF16), p['w_up'].astype(BF16),
                    p['w_down'].astype(BF16), norm_final.reshape(1, -1), final, b, seq)


def _overlap_t(seq):
    ncp = seq // CMP_STRIDE
    nsel = seq // SEL_LEN
    cs = np.arange(ncp) * CMP_STRIDE
    ss = np.arange(nsel) * SEL_LEN
    ovl = (cs[None, :] < ss[:, None] + SEL_LEN) & (ss[:, None] < cs[None, :] + CMP_LEN)
    ovl[:, ncp - 1] = False
    return jnp.asarray(ovl.astype(np.float32))


def kernel(x, rel_bias_table, norm_mix, w_in, w_out, s5_lam_re, s5_lam_im, s5_log_dt, s5_b_re, s5_b_im, s5_c_re, s5_c_im, s5_d, s5_w_glu, nsa_pe_k, nsa_w1_k, nsa_w2_k, nsa_pe_v, nsa_w1_v, nsa_w2_v, lru_conv_w, lru_conv_b, lru_w_a, lru_b_a, lru_w_x, lru_b_x, lru_lam, norm_ffn, w_gate, w_up, w_down, norm_final):
    b, seq, d = x.shape
    depth = norm_mix.shape[0]
    per_layer = dict(norm_mix=norm_mix, w_in=w_in, w_out=w_out, s5_lam_re=s5_lam_re, s5_lam_im=s5_lam_im,
                     s5_log_dt=s5_log_dt, s5_b_re=s5_b_re, s5_b_im=s5_b_im, s5_c_re=s5_c_re,
                     s5_c_im=s5_c_im, s5_d=s5_d, s5_w_glu=s5_w_glu, nsa_pe_k=nsa_pe_k, nsa_w1_k=nsa_w1_k,
                     nsa_w2_k=nsa_w2_k, nsa_pe_v=nsa_pe_v, nsa_w1_v=nsa_w1_v, nsa_w2_v=nsa_w2_v,
                     lru_conv_w=lru_conv_w, lru_conv_b=lru_conv_b, lru_w_a=lru_w_a, lru_b_a=lru_b_a,
                     lru_w_x=lru_w_x, lru_b_x=lru_b_x, lru_lam=lru_lam, norm_ffn=norm_ffn, w_gate=w_gate,
                     w_up=w_up, w_down=w_down)
    bias_tabs = _bias_tables(rel_bias_table, seq)
    ov = _overlap_t(seq)
    h2 = x.reshape(b * seq, d)
    for l in range(depth):
        p = {k: v[l] for k, v in per_layer.items()}
        h2 = _layer(h2, b, seq, bias_tabs, ov, p, l == depth - 1, norm_final)
    return h2.reshape(b, seq, d)
```

```python
import functools
import math

import numpy as np
import jax
import jax.numpy as jnp
from jax import lax
from jax.experimental import pallas as pl
from jax.experimental.pallas import tpu as pltpu

F32 = jnp.float32
BF16 = jnp.bfloat16

D_MODEL = 1024
SSM_WIDTH = 256
NSA_WIDTH = 512
LRU_WIDTH = 256
S5_GROUP = 16
S5_GROUPS = 16
S5_STATE = 64
HEAD_DIM = 64
NSA_Q_HEADS = 8
NSA_KV_HEADS = 2
NSA_GQA = 4
NSA_KV_WIDTH = 128
CMP_LEN = 32
CMP_STRIDE = 16
SEL_LEN = 64
SEL_TOPK = 8
WINDOW = 256
Q_BLOCK = 64
LRU_HEADS = 4
LRU_HEAD_DIM = 64
CONV_WIDTH = 4
LRU_C = 8.0
REL_BUCKETS = 32
REL_MAX_DIST = 128
D_FF = 2816
NEG_INF = -1e9
LOG2E = 1.4426950408889634
RMS_EPS = 1e-6

NO_SLOT = -3e38

S5_CHUNK = 16
Z_Q, Z_KV, Z_G, Z_NSA = 0, 512, 1280, 1408
N_SLAB = 6
OCT = 8

VMEM_LIMIT = 56 * 1024 * 1024


def _cparams(*sem):
    return pltpu.CompilerParams(dimension_semantics=sem, vmem_limit_bytes=VMEM_LIMIT)


def _inproj_kernel(x_ref, g_ref, w_ref, o_ref, *slab_refs):
    x = x_ref[...]
    y = x * lax.rsqrt(jnp.mean(x * x, axis=-1, keepdims=True) + RMS_EPS) * g_ref[...]
    res = jnp.dot(y.astype(BF16), w_ref[...], preferred_element_type=F32)
    o_ref[...] = res[:, :Z_NSA]
    for m, s_ref in enumerate(slab_refs):
        s_ref[...] = res[:, Z_NSA + m * 128:Z_NSA + (m + 1) * 128].reshape(s_ref.shape)


def _slab_spec(tm, nt):
    return pl.BlockSpec((tm // OCT, None, OCT, 128), lambda b, i: (i, b, 0, 0))


def _inproj(x2, g, w, b, seq, tm=1024):
    d = x2.shape[1]
    n = w.shape[1]
    nt = seq // tm
    slab = jax.ShapeDtypeStruct((seq // OCT, b, OCT, 128), F32)
    outs = pl.pallas_call(
        _inproj_kernel,
        grid=(b, nt),
        in_specs=[pl.BlockSpec((tm, d), lambda b_, i: (b_ * nt + i, 0)),
                  pl.BlockSpec((1, d), lambda b_, i: (0, 0)),
                  pl.BlockSpec((d, n), lambda b_, i: (0, 0))],
        out_specs=[pl.BlockSpec((tm, Z_NSA), lambda b_, i: (b_ * nt + i, 0))] + [_slab_spec(tm, nt)] * N_SLAB,
        out_shape=[jax.ShapeDtypeStruct((b * seq, Z_NSA), F32)] + [slab] * N_SLAB,
        compiler_params=_cparams("parallel", "parallel"),
        name="inproj",
    )(x2, g, w)
    return outs[0], [s.reshape(b * seq, 128) for s in outs[1:]]


def _lane_regroup(x):
    r = x.shape[0]
    cols = [x[:, v * 128:(v + 1) * 128] for v in range(32)]
    piece = lax.broadcasted_iota(jnp.int32, (r, 128), 1) // 16
    out = [None] * 32
    for ah in range(2):
        for bh in range(2):
            vs = [cols[2 * (8 * ah + i) + bh] for i in range(8)]
            for stage in (4, 2, 1):
                upper = (piece & stage) != 0
                for i in range(8):
                    if i & stage:
                        continue
                    lo, hi = vs[i], vs[i + stage]
                    vs[i] = jnp.where(upper, pltpu.roll(hi, stage * 16, 1), lo)
                    vs[i + stage] = jnp.where(upper, hi, pltpu.roll(lo, 128 - stage * 16, 1))
            for j in range(8):
                out[2 * (8 * bh + j) + ah] = vs[j]
    return jnp.concatenate(out, axis=1)


def _octet_row(t, nb):
    return (t // OCT) * (nb * OCT) + t % OCT


def _s5_kernel(u0_ref, u1_ref, we_ref, tz_ref, cp_ref, a_ref, d_ref, o0_ref, o1_ref, e_ref, carry_ref,
               *, nb, kc):
    @pl.when(pl.program_id(0) == 0)
    def _():
        carry_ref[...] = jnp.zeros_like(carry_ref)

    u_refs = (u0_ref, u1_ref)
    u = jnp.concatenate(
        [jnp.concatenate([u_refs[j][pl.ds(_octet_row(k * S5_CHUNK + s, nb), nb, stride=OCT), :]
                          for s in range(S5_CHUNK) for j in range(2)], axis=1)
         for k in range(kc)], axis=0)
    ug = _lane_regroup(u)
    ugb = ug.astype(BF16)
    npair = S5_GROUPS // 2
    for p in range(npair):
        e_ref[:, p * 256:(p + 1) * 256] = jnp.dot(ugb[:, p * 512:(p + 1) * 512], we_ref[p],
                                                   preferred_element_type=F32)

    a = a_ref[...]

    def step(k, carry):
        r0 = pl.multiple_of(k * nb, nb)
        e = e_ref[pl.ds(r0, nb), :]
        e_ref[pl.ds(r0, nb), :] = carry
        new = []
        for p in range(npair):
            ar = a[:, p * 256:p * 256 + 128]
            ai = a[:, p * 256 + 128:(p + 1) * 256]
            cr = carry[:, p * 256:p * 256 + 128]
            ci = carry[:, p * 256 + 128:(p + 1) * 256]
            new.append(ar * cr - ai * ci + e[:, p * 256:p * 256 + 128])
            new.append(ar * ci + ai * cr + e[:, p * 256 + 128:(p + 1) * 256])
        return jnp.concatenate(new, axis=1)

    carry_ref[...] = lax.fori_loop(0, kc, step, carry_ref[...])

    eb = e_ref[...].astype(BF16)
    ys = []
    for p in range(npair):
        yc = jnp.dot(eb[:, p * 256:(p + 1) * 256], cp_ref[p], preferred_element_type=F32)
        y0 = jnp.dot(ugb[:, (2 * p) * 256:(2 * p + 1) * 256], tz_ref[2 * p], preferred_element_type=F32)
        y1 = jnp.dot(ugb[:, (2 * p + 1) * 256:(2 * p + 2) * 256], tz_ref[2 * p + 1],
                     preferred_element_type=F32)
        ys.append(yc + jnp.concatenate([y0, y1], axis=1))
    y = jnp.concatenate(ys, axis=1) + d_ref[...] * ug
    y = jax.nn.gelu(y)
    y = _lane_regroup(y)
    o_refs = (o0_ref, o1_ref)
    for k in range(kc):
        for s in range(S5_CHUNK):
            for j in range(2):
                lo = s * SSM_WIDTH + j * 128
                o_refs[j][pl.ds(_octet_row(k * S5_CHUNK + s, nb), nb, stride=OCT), :] = \
                    y[k * nb:(k + 1) * nb, lo:lo + 128]


def _s5(u0, u1, we, tz, cp, a16, dg, nb, kc=16):
    rows = u0.shape[0]
    r = kc * S5_CHUNK * nb
    const3 = lambda i: (0, 0, 0)
    slab = pl.BlockSpec((r, 128), lambda i: (i, 0))
    return pl.pallas_call(
        functools.partial(_s5_kernel, nb=nb, kc=kc),
        grid=(rows // r,),
        in_specs=[slab, slab,
                  pl.BlockSpec(we.shape, const3),
                  pl.BlockSpec(tz.shape, const3),
                  pl.BlockSpec(cp.shape, const3),
                  pl.BlockSpec(a16.shape, lambda i: (0, 0)),
                  pl.BlockSpec(dg.shape, lambda i: (0, 0))],
        out_specs=[slab, slab],
        out_shape=[jax.ShapeDtypeStruct((rows, 128), F32)] * 2,
        scratch_shapes=[pltpu.VMEM((kc * nb, 2048), F32), pltpu.VMEM((nb, 2048), F32)],
        compiler_params=_cparams("arbitrary"),
        name="s5",
    )(u0, u1, we, tz, cp, a16, dg)


def _s5_tables(lam_re, lam_im, log_dt, b_re, b_im, c_re, c_im, d_skip):
    L = S5_CHUNK
    G, P, C = S5_GROUPS, S5_STATE, S5_GROUP
    dt = jnp.exp(log_dt.astype(F32))[:, None]
    lr = lam_re.astype(F32)
    li = lam_im.astype(F32)
    mag = jnp.exp(lr * dt)
    ang = li * dt
    ab_re = mag * jnp.cos(ang)
    ab_im = mag * jnp.sin(ang)
    den = lr * lr + li * li
    f_re = ((ab_re - 1.0) * lr + ab_im * li) / den
    f_im = (ab_im * lr - (ab_re - 1.0) * li) / den
    br = b_re.astype(F32)
    bi = b_im.astype(F32)
    bb_re = f_re[..., None] * br - f_im[..., None] * bi
    bb_im = f_re[..., None] * bi + f_im[..., None] * br
    cr = c_re.astype(F32)
    ci = c_im.astype(F32)
    tau = jnp.arange(L + 1, dtype=F32)[:, None, None]
    pr = jnp.exp(lr * dt * tau) * jnp.cos(li * dt * tau)
    pi = jnp.exp(lr * dt * tau) * jnp.sin(li * dt * tau)

    prs = pr[:L][::-1]
    pis = pi[:L][::-1]
    we_re = jnp.einsum('sgp,gpc->gscp', prs, bb_re) - jnp.einsum('sgp,gpc->gscp', pis, bb_im)
    we_im = jnp.einsum('sgp,gpc->gscp', prs, bb_im) + jnp.einsum('sgp,gpc->gscp', pis, bb_re)
    we_re = we_re.reshape(G, L * C, P)
    we_im = we_im.reshape(G, L * C, P)
    z = jnp.zeros_like(we_re[0::2])
    top = jnp.concatenate([we_re[0::2], z, we_im[0::2], z], axis=-1)
    bot = jnp.concatenate([z, we_re[1::2], z, we_im[1::2]], axis=-1)
    we = jnp.concatenate([top, bot], axis=1)

    m_re = pr[:L, :, :, None] * bb_re[None] - pi[:L, :, :, None] * bb_im[None]
    m_im = pr[:L, :, :, None] * bb_im[None] + pi[:L, :, :, None] * bb_re[None]
    kern = jnp.einsum('gop,tgpc->gcto', cr, m_re) - jnp.einsum('gop,tgpc->gcto', ci, m_im)
    padded = jnp.concatenate([jnp.zeros((G, C, L * C), F32), kern.reshape(G, C, L * C)], axis=-1)
    tz = jnp.stack([padded[:, :, (L - s) * C:(2 * L - s) * C] for s in range(L)], axis=1)
    tz = tz.reshape(G, L * C, L * C)

    pr1 = pr[1:]
    pi1 = pi[1:]
    cp_re = jnp.einsum('gop,tgp->gpto', cr, pr1) - jnp.einsum('gop,tgp->gpto', ci, pi1)
    cp_im = -(jnp.einsum('gop,tgp->gpto', cr, pi1) + jnp.einsum('gop,tgp->gpto', ci, pr1))
    cp_re = cp_re.reshape(G, P, L * C)
    cp_im = cp_im.reshape(G, P, L * C)
    zc = jnp.zeros_like(cp_re[0::2])
    cp = jnp.concatenate([
        jnp.concatenate([cp_re[0::2], zc], axis=-1),
        jnp.concatenate([zc, cp_re[1::2]], axis=-1),
        jnp.concatenate([cp_im[0::2], zc], axis=-1),
        jnp.concatenate([zc, cp_im[1::2]], axis=-1)], axis=1)

    a_re = pr[L].reshape(G // 2, 2 * P)
    a_im = pi[L].reshape(G // 2, 2 * P)
    a16 = jnp.concatenate([a_re, a_im], axis=-1).reshape(1, G * 2 * P)
    dg = jnp.broadcast_to(d_skip.astype(F32).reshape(G, 1, C), (G, L, C)).reshape(1, G * L * C)
    return we.astype(BF16), tz.astype(BF16), cp.astype(BF16), a16, dg


def _compress_kernel(ak_ref, av_ref, w1k_ref, w1v_ref, w2k_ref, w2v_ref, c0k_ref, c0v_ref,
                     kc_ref, vct_ref):
    def run(a_ref, w1_ref, w2_ref, c0_ref):
        nc = a_ref.shape[1] // CMP_STRIDE
        a = jnp.concatenate([a_ref[0, pl.ds(l, nc, stride=CMP_STRIDE), :] for l in range(CMP_STRIDE)],
                            axis=1).astype(BF16)
        p1 = jnp.dot(a, w1_ref[0], preferred_element_type=F32)
        p2 = jnp.dot(a, w1_ref[1], preferred_element_type=F32)
        n = p1.shape[0]
        pre = p1 + pltpu.roll(p2, n - 1, 0) + c0_ref[...]
        out = jnp.dot(jax.nn.gelu(pre).astype(BF16), w2_ref[...], preferred_element_type=F32)
        row = lax.broadcasted_iota(jnp.int32, out.shape, 0)
        return jnp.where(row < n - 1, out, 0.0)

    kc_ref[0] = run(ak_ref, w1k_ref, w2k_ref, c0k_ref).astype(BF16)
    vct_ref[0] = run(av_ref, w1v_ref, w2v_ref, c0v_ref).T.astype(BF16)


def _compress(z3, w1k, w1v, w2k, w2v, c0k, c0v):
    b, seq, _ = z3.shape
    nc = seq // CMP_STRIDE
    c3 = lambda i: (0, 0, 0)
    c2 = lambda i: (0, 0)
    return pl.pallas_call(
        _compress_kernel,
        grid=(b,),
        in_specs=[pl.BlockSpec((1, seq, 128), lambda i: (i, 0, Z_KV // 128)),
                  pl.BlockSpec((1, seq, 128), lambda i: (i, 0, Z_KV // 128 + 1)),
                  pl.BlockSpec(w1k.shape, c3), pl.BlockSpec(w1v.shape, c3),
                  pl.BlockSpec(w2k.shape, c2), pl.BlockSpec(w2v.shape, c2),
                  pl.BlockSpec(c0k.shape, c2), pl.BlockSpec(c0v.shape, c2)],
        out_specs=[pl.BlockSpec((1, nc, 128), lambda i: (i, 0, 0)),
                   pl.BlockSpec((1, 128, nc), lambda i: (i, 0, 0))],
        out_shape=[jax.ShapeDtypeStruct((b, nc, 128), BF16),
                   jax.ShapeDtypeStruct((b, 128, nc), BF16)],
        compiler_params=_cparams("parallel"),
        name="compress",
    )(z3, z3, w1k, w1v, w2k, w2v, c0k, c0v)


def _compress_weights(pe, w1, w2):
    w1 = w1.astype(F32)
    half = (CMP_LEN // 2) * HEAD_DIM
    eye = jnp.eye(NSA_KV_HEADS, dtype=F32)

    def bd(w):
        w = w.reshape(CMP_STRIDE, HEAD_DIM, HEAD_DIM)
        return jnp.einsum('ldo,hk->lhdko', w, eye).reshape(NSA_KV_HEADS * half, NSA_KV_HEADS * HEAD_DIM)

    w1s = jnp.stack([bd(w1[:half]), bd(w1[half:])])
    z2 = jnp.zeros((HEAD_DIM, HEAD_DIM), F32)
    w2f = w2.astype(F32)
    w2s = jnp.concatenate([jnp.concatenate([w2f, z2], axis=1),
                           jnp.concatenate([z2, w2f], axis=1)], axis=0)
    c0 = pe.astype(F32).reshape(1, CMP_LEN * HEAD_DIM) @ w1
    c0 = jnp.concatenate([c0, c0], axis=1)
    return w1s.astype(BF16), w2s.astype(BF16), c0


def _t5_bucket_np(dist):
    n = np.maximum(dist, 0)
    max_exact = REL_BUCKETS // 2
    nf = np.maximum(n, 1).astype(np.float32)
    large = max_exact + (np.log(nf / max_exact) / math.log(REL_MAX_DIST / max_exact)
                         * (REL_BUCKETS - max_exact)).astype(np.int32)
    large = np.minimum(large, REL_BUCKETS - 1)
    return np.where(n < max_exact, n, large)


TAB_FAR, TAB_WIN_EDGE, TAB_NONE = 3, 4, 5


def _bias_tables(rel_table, seq):
    tbl = rel_table.astype(F32)
    heads = NSA_Q_HEADS

    def lookup(bucket):
        onehot = jnp.asarray(bucket[..., None] == np.arange(REL_BUCKETS)).astype(F32)
        return jnp.einsum('...k,kh->...h', onehot, tbl, precision=lax.Precision.HIGHEST) * LOG2E

    r = np.arange(Q_BLOCK)[None, :]
    c = np.arange(SEL_LEN)[:, None]
    near = lookup(_t5_bucket_np(64 * np.arange(3)[:, None, None] + (r - c)[None]))
    far = jnp.broadcast_to(tbl[REL_BUCKETS - 1] * LOG2E, (SEL_LEN, Q_BLOCK, heads))
    none = jnp.full((SEL_LEN, Q_BLOCK, heads), NO_SLOT, F32)
    causal = jnp.asarray(r - c >= 0)[:, :, None]
    edge = jnp.asarray(c > r)[:, :, None]
    tab = jnp.stack([jnp.where(causal, near[0], none), near[1], near[2], far,
                     jnp.where(edge, far, none), none])
    tab = tab.transpose(0, 1, 3, 2).reshape(6, SEL_LEN, heads * Q_BLOCK)
    ncp = seq // CMP_STRIDE
    k = np.arange(2 * ncp + 4)[:, None]
    dist = r - CMP_STRIDE * (k - (ncp - 4)) - (CMP_LEN - 1)
    t2 = jnp.where(jnp.asarray(dist >= 0)[:, :, None], lookup(_t5_bucket_np(dist)), NO_SLOT)
    t2 = t2.transpose(0, 2, 1).reshape(2 * ncp + 4, heads * Q_BLOCK)
    t2 = jnp.stack([t2[:2 * ncp], t2[4:]])
    return tab, t2


def _nsa_kernel(q_ref, gl_ref, kc_ref, vct_ref, ks_ref, vs_ref, kw_ref, vw_ref, t2_ref, tab_ref,
                ov_ref, o_ref, s_ref, p_ref, acc_ref, accw_ref, vst_ref, vwt_ref, *, ncp, nw):
    @pl.when(pl.program_id(1) == 0)
    def _():
        for v_ref, vt_ref in ((vs_ref, vst_ref), (vw_ref, vwt_ref)):
            kt = vt_ref.shape[2]
            ones = jnp.ones((16, kt), F32)
            for u in range(vt_ref.shape[0]):
                vt = v_ref[0, u * kt:(u + 1) * kt, :].T
                vt_ref[u] = jnp.concatenate([vt[0:HEAD_DIM], ones, vt[HEAD_DIM:], ones], axis=0).astype(BF16)

    qi0 = pl.program_id(1) * nw
    cw = nw * NSA_GQA * Q_BLOCK
    ncol = NSA_KV_HEADS * cw
    vrows = HEAD_DIM + 16
    lane = lax.broadcasted_iota(jnp.int32, (Q_BLOCK, 128), 1)

    q = q_ref[...] * (HEAD_DIM ** -0.5 * LOG2E)
    pieces = []
    for h in range(NSA_KV_HEADS):
        keep = (lane >= 64) if h == 1 else (lane < 64)
        for w in range(nw):
            for g in range(NSA_GQA):
                pieces.append(jnp.where(keep, q[w * 64:(w + 1) * 64, g * 128:(g + 1) * 128], 0.0))
    qpad = jnp.concatenate(pieces, axis=0).astype(BF16)

    col = lax.broadcasted_iota(jnp.int32, (1, ncol), 1)
    rcol = col % Q_BLOCK
    qcol = qi0 + (col // (NSA_GQA * Q_BLOCK)) % nw
    tvec = Q_BLOCK * qcol + rcol
    nt_dims = (((1,), (1,)), ((), ()))

    def per_head_cols(tiles):
        return jnp.concatenate([tiles[w][:, h * 256:(h + 1) * 256]
                                for h in range(NSA_KV_HEADS) for w in range(nw)], axis=1)

    def slot_terms(m, n_masked):
        m_fin = jnp.where(n_masked > 0, jnp.maximum(m, NEG_INF * LOG2E), m)
        return jnp.exp2(m - m_fin), n_masked * jnp.exp2(NEG_INF * LOG2E - m_fin)

    t2_tiles = []
    for w in range(nw):
        start = (ncp - 4) - 4 * (qi0 + w)
        par = (start // 4) % 2
        a0 = pl.multiple_of(start - 4 * par, 8)
        t2_tiles.append(t2_ref[par, pl.ds(a0, ncp), :])
    s = lax.dot_general(kc_ref[0], qpad, nt_dims, preferred_element_type=F32) + per_head_cols(t2_tiles)
    m = jnp.max(s, axis=0, keepdims=True)
    e = jnp.exp2(s - m)
    n_valid = jnp.clip(jnp.right_shift(tvec - (CMP_STRIDE - 1), 4), 0, ncp - 1)
    scale, extra = slot_terms(m, (ncp - 1 - n_valid).astype(F32))
    p_c = e * (scale / (jnp.sum(e, axis=0, keepdims=True) * scale + extra))
    p_cb = p_c.astype(BF16)
    o_c = [jnp.dot(vct_ref[0, h * 64:(h + 1) * 64, :], p_cb[:, h * cw:(h + 1) * cw],
                   preferred_element_type=F32) for h in range(NSA_KV_HEADS)]

    lane_c = lax.broadcasted_iota(jnp.int32, (ncp, 128), 1)
    parts = []
    for w in range(nw):
        halves = []
        for h in range(NSA_KV_HEADS):
            base = h * cw + w * 256
            ph = p_c[:, base:base + 128] + p_c[:, base + 128:base + 256]
            halves.append(ph + pltpu.roll(ph, 64, 1))
        parts.append(jnp.where(lane_c < 64, halves[0], halves[1]))
    imp = jnp.dot(ov_ref[...], jnp.concatenate(parts, axis=1), preferred_element_type=F32,
                  precision=lax.Precision.HIGHEST)
    nsel = imp.shape[0]
    jidx = lax.broadcasted_iota(jnp.int32, (nsel, 1), 0)
    qv = qi0 + lax.broadcasted_iota(jnp.int32, (1, nw * 128), 1) // 128
    forced = jnp.logical_or(jidx == 0, jnp.logical_or(jidx == qv, jidx == qv - 1))
    avail = jidx <= qv
    cand = jnp.logical_and(avail, jnp.logical_not(forced))
    budget = SEL_TOPK - (1 + jnp.where(qv >= 1, 1, 0) + jnp.where(qv >= 2, 1, 0))
    rank = jnp.zeros((nsel, nw * 128), jnp.int32)
    for jp in range(1, nsel - 2):
        row = imp[jp:jp + 1, :]
        ge = jnp.where(row >= imp, 1, 0)
        gt = jnp.where(row > imp, 1, 0)
        is_cand = jnp.where(qv - 2 >= jp, 1, 0)
        rank = rank + jnp.where(jidx > jp, ge, gt) * is_cand
    sel = jnp.logical_or(jnp.logical_and(forced, avail), jnp.logical_and(cand, rank < budget))
    seladd = jnp.where(sel, 0.0, NO_SLOT)
    pad = jnp.zeros((128 - nsel, 128), F32)
    mask_rows = []
    for w in range(nw):
        mask_rows.append(jnp.concatenate([seladd[:, w * 128:(w + 1) * 128], pad], axis=0).T)
    sel_cols = jnp.concatenate([mask_rows[w][h * 64:(h + 1) * 64, :] for h in range(NSA_KV_HEADS)
                                for w in range(nw) for _ in range(NSA_GQA)], axis=0)
    qsel = jnp.concatenate([qpad, sel_cols.astype(BF16)], axis=1)

    def block_adds(blk, selected):
        tiles = []
        for w in range(nw):
            d = qi0 + w - blk
            if selected:
                e = jnp.where(d < 0, TAB_FAR, jnp.minimum(d, TAB_FAR))
            else:
                outside = jnp.logical_or(blk < 0, jnp.logical_or(d < 0, d > TAB_NONE))
                e = jnp.where(outside, TAB_NONE, d)
            tiles.append(tab_ref[e])
        return per_head_cols(tiles)

    def softmax_tile(m_i, s, adds):
        sm = jnp.concatenate([s[k * 64:(k + 1) * 64] + adds[k] for k in range(len(adds))], axis=0)
        m_new = jnp.maximum(m_i, jnp.max(sm, axis=0, keepdims=True))
        return m_new, jnp.exp2(m_i - m_new), jnp.exp2(sm - m_new).astype(BF16)

    def add_values(ref, alpha, vt_tile, p):
        for h in range(NSA_KV_HEADS):
            ref[h] = alpha[:, h * cw:(h + 1) * cw] * ref[h] + jnp.dot(
                vt_tile[h * vrows:(h + 1) * vrows, :], p[:, h * cw:(h + 1) * cw], preferred_element_type=F32)

    m_init = jnp.full((1, ncol), NO_SLOT, F32)

    accw_ref[...] = jnp.zeros_like(accw_ref)
    m_w = m_init
    for i in range(nw // 4 + 1):
        tj = qi0 // 4 - 1 + i
        tc = jnp.maximum(tj, 0)
        r0 = pl.multiple_of(tc * 256, 256)
        s = lax.dot_general(kw_ref[0, pl.ds(r0, 256), :].astype(BF16), qpad, nt_dims,
                            preferred_element_type=F32)
        m_w, alpha, p = softmax_tile(m_w, s, [block_adds(4 * tj + k, False) for k in range(4)])
        add_values(accw_ref, alpha, vwt_ref[tc], p)

    n_tiles = (qi0 + nw - 1) // 4 + 1

    def masked_scores(u):
        r0 = pl.multiple_of(u * 256, 256)
        block_of_row = 4 * u + lax.broadcasted_iota(jnp.int32, (256, 128), 0) // SEL_LEN
        onehot = jnp.where(lax.broadcasted_iota(jnp.int32, (256, 128), 1) == block_of_row, 1.0, 0.0)
        keys = jnp.concatenate([ks_ref[0, pl.ds(r0, 256), :].astype(BF16), onehot.astype(BF16)], axis=1)
        s = lax.dot_general(keys, qsel, nt_dims, preferred_element_type=F32)
        tmax = None
        for k in range(4):
            sk = s[k * 64:(k + 1) * 64] + block_adds(4 * u + k, True)
            s_ref[k * 64:(k + 1) * 64, :] = sk
            kmax = jnp.max(sk, axis=0, keepdims=True)
            tmax = kmax if tmax is None else jnp.maximum(tmax, kmax)
        return tmax

    tmax0 = masked_scores(0)
    p_ref[...] = jnp.zeros_like(p_ref)
    acc_ref[...] = jnp.zeros_like(acc_ref)

    def sel_body(u, carry):
        m_i, alpha_prev, tmax = carry
        add_values(acc_ref, alpha_prev, vst_ref[jnp.maximum(u - 1, 0)], p_ref[...])
        m_new = jnp.maximum(m_i, tmax)
        p_ref[...] = jnp.exp2(s_ref[...] - m_new).astype(BF16)
        tmax_next = masked_scores(jnp.minimum(u + 1, n_tiles - 1))
        return m_new, jnp.exp2(m_i - m_new), tmax_next

    m_s, alpha_last, _ = lax.fori_loop(0, n_tiles, sel_body, (m_init, jnp.ones((1, ncol), F32), tmax0))
    add_values(acc_ref, alpha_last, vst_ref[n_tiles - 1], p_ref[...])

    nm_s = (SEL_LEN * jnp.maximum(SEL_TOPK - (qcol + 1), 0) + (SEL_LEN - 1 - rcol)).astype(F32)
    nm_w = (WINDOW + Q_BLOCK - jnp.minimum(tvec + 1, WINDOW)).astype(F32)

    def normalised(m, ref, n_masked):
        scale, extra = slot_terms(m, n_masked)
        out = []
        for h in range(NSA_KV_HEADS):
            sl = slice(h * cw, (h + 1) * cw)
            acc = ref[h]
            denom = acc[HEAD_DIM:HEAD_DIM + 1, :] * scale[:, sl] + extra[:, sl]
            out.append((scale[:, sl] / denom, acc[0:HEAD_DIM, :]))
        return out

    o_s = normalised(m_s, acc_ref, nm_s)
    o_w = normalised(m_w, accw_ref, nm_w)

    lane1 = lax.broadcasted_iota(jnp.int32, (1, 128), 1)
    gts = []
    for w in range(nw):
        g = jax.nn.sigmoid(gl_ref[w * 64:(w + 1) * 64, :])
        gts.append(jnp.concatenate([g, g], axis=0).T)

    def gate_vec(h, br):
        tiles = []
        for w in range(nw):
            for gg in range(2):
                c0 = (h * 4 + 2 * gg) * 3 + br
                c1 = (h * 4 + 2 * gg + 1) * 3 + br
                tiles.append(jnp.where(lane1 < 64, gts[w][c0:c0 + 1, :], gts[w][c1:c1 + 1, :]))
        return jnp.concatenate(tiles, axis=1)

    tot = []
    for h in range(NSA_KV_HEADS):
        t = gate_vec(h, 0) * o_c[h]
        t = t + (gate_vec(h, 1) * o_s[h][0]) * o_s[h][1]
        t = t + (gate_vec(h, 2) * o_w[h][0]) * o_w[h][1]
        tot.append(t)
    for w in range(nw):
        ot = jnp.concatenate([t[:, w * 256:(w + 1) * 256] for t in tot], axis=0).T
        o_ref[w * 64:(w + 1) * 64, :] = jnp.concatenate([ot[g * 64:(g + 1) * 64, :]
                                                         for g in range(NSA_GQA)], axis=1)


def _nsa(z, kc, vct, t2, tab, ov, b, seq, nw=4):
    nst = seq // (Q_BLOCK * nw)
    ncp = seq // CMP_STRIDE
    rows = Q_BLOCK * nw
    cw = nw * NSA_GQA * Q_BLOCK
    z3 = z.reshape(b, seq, Z_NSA)
    kv_col = lambda n: pl.BlockSpec((1, seq, 128), lambda i, j: (i, 0, Z_KV // 128 + n))
    return pl.pallas_call(
        functools.partial(_nsa_kernel, ncp=ncp, nw=nw),
        grid=(b, nst),
        in_specs=[pl.BlockSpec((rows, NSA_WIDTH), lambda i, j: (i * nst + j, Z_Q // NSA_WIDTH)),
                  pl.BlockSpec((rows, 128), lambda i, j: (i * nst + j, Z_G // 128)),
                  pl.BlockSpec((1, ncp, 128), lambda i, j: (i, 0, 0)),
                  pl.BlockSpec((1, 128, ncp), lambda i, j: (i, 0, 0)),
                  kv_col(2), kv_col(3), kv_col(4), kv_col(5),
                  pl.BlockSpec(t2.shape, lambda i, j: (0, 0, 0)),
                  pl.BlockSpec(tab.shape, lambda i, j: (0, 0, 0)),
                  pl.BlockSpec(ov.shape, lambda i, j: (0, 0))],
        out_specs=pl.BlockSpec((rows, NSA_WIDTH), lambda i, j: (i * nst + j, 0)),
        out_shape=jax.ShapeDtypeStruct((b * seq, NSA_WIDTH), F32),
        scratch_shapes=[pltpu.VMEM((256, NSA_KV_HEADS * cw), F32),
                        pltpu.VMEM((256, NSA_KV_HEADS * cw), BF16),
                        pltpu.VMEM((NSA_KV_HEADS, HEAD_DIM + 16, cw), F32),
                        pltpu.VMEM((NSA_KV_HEADS, HEAD_DIM + 16, cw), F32),
                        pltpu.VMEM((seq // 256, NSA_KV_HEADS * (HEAD_DIM + 16), 256), BF16),
                        pltpu.VMEM((seq // 256, NSA_KV_HEADS * (HEAD_DIM + 16), 256), BF16)],
        compiler_params=_cparams("parallel", "arbitrary"),
        name="nsa",
    )(z, z, kc, vct, z3, z3, z3, z3, t2, tab, ov)


def _lru_kernel(x0_ref, x1_ref, g0_ref, g1_ref, cw_ref, cb_ref, wa_ref, ba_ref, wx_ref, bx_ref, sp_ref,
                o0_ref, o1_ref, xprev_ref, h_ref, a_scr, b_scr, *, nb, tc):
    @pl.when(pl.program_id(0) == 0)
    def _():
        xprev_ref[...] = jnp.zeros_like(xprev_ref)
        h_ref[...] = jnp.zeros_like(h_ref)

    x = jnp.concatenate([x0_ref[...], x1_ref[...]], axis=1)
    rows, width = x.shape
    grp = nb * OCT
    xm = jnp.concatenate([xprev_ref[...], x[:rows - grp]], axis=0)
    xprev_ref[...] = x[rows - grp:]
    x3 = x.reshape(rows // OCT, OCT, width)
    xm3 = xm.reshape(rows // OCT, OCT, width)
    tlo = lax.broadcasted_iota(jnp.int32, (1, OCT, 1), 1)
    cw = cw_ref[...]
    xc = cb_ref[...].reshape(1, 1, width) + cw[CONV_WIDTH - 1:CONV_WIDTH, :].reshape(1, 1, width) * x3
    for k in range(1, CONV_WIDTH):
        delayed = jnp.where(tlo >= k, pltpu.roll(x3, k, 1), pltpu.roll(xm3, k, 1))
        xc = xc + cw[CONV_WIDTH - 1 - k:CONV_WIDTH - k, :].reshape(1, 1, width) * delayed
    xc = xc.reshape(rows, width)
    xcb = xc.astype(BF16)
    def sigmoid(v):
        return 0.5 * jnp.tanh(0.5 * v) + 0.5

    gate_r = sigmoid(jnp.dot(xcb, wa_ref[...], preferred_element_type=F32) + ba_ref[...])
    gate_i = sigmoid(jnp.dot(xcb, wx_ref[...], preferred_element_type=F32) + bx_ref[...])
    log_a = -LRU_C * gate_r * sp_ref[...]
    a = jnp.exp(log_a)
    th = jnp.tanh(log_a)
    bt = jnp.sqrt(-2.0 * th / (1.0 - th)) * gate_i * xc
    for j in range(2):
        a_scr[j] = a[:, j * 128:(j + 1) * 128]
        b_scr[j] = bt[:, j * 128:(j + 1) * 128]

    def step(t, h):
        r0 = _octet_row(t, nb)
        new = []
        for j in range(2):
            hj = a_scr[j, pl.ds(r0, nb, stride=OCT), :] * h[j] + b_scr[j, pl.ds(r0, nb, stride=OCT), :]
            b_scr[j, pl.ds(r0, nb, stride=OCT), :] = hj
            new.append(hj)
        return tuple(new)

    h = lax.fori_loop(0, tc, step, (h_ref[0], h_ref[1]))
    for j, (g_ref, o_ref) in enumerate(((g0_ref, o0_ref), (g1_ref, o1_ref))):
        h_ref[j] = h[j]
        o_ref[...] = b_scr[j] * jax.nn.gelu(g_ref[...])


def _lru(x0, x1, g0, g1, cw, cb, wa, ba, wx, bx, sp, nb, tc=64):
    rows = x0.shape[0]
    r = tc * nb
    c2 = lambda i: (0, 0)
    slab = pl.BlockSpec((r, 128), lambda i: (i, 0))
    return pl.pallas_call(
        functools.partial(_lru_kernel, nb=nb, tc=tc),
        grid=(rows // r,),
        in_specs=[slab, slab, slab, slab,
                  pl.BlockSpec(cw.shape, c2), pl.BlockSpec(cb.shape, c2),
                  pl.BlockSpec(wa.shape, c2), pl.BlockSpec(ba.shape, c2),
                  pl.BlockSpec(wx.shape, c2), pl.BlockSpec(bx.shape, c2),
                  pl.BlockSpec(sp.shape, c2)],
        out_specs=[slab, slab],
        out_shape=[jax.ShapeDtypeStruct((rows, 128), F32)] * 2,
        scratch_shapes=[pltpu.VMEM((nb * OCT, LRU_WIDTH), F32), pltpu.VMEM((2, nb, 128), F32),
                        pltpu.VMEM((2, r, 128), F32), pltpu.VMEM((2, r, 128), F32)],
        compiler_params=_cparams("arbitrary"),
        name="lru",
    )(x0, x1, g0, g1, cw, cb, wa, ba, wx, bx, sp)


def _block_diag(w):
    h, i, j = w.shape
    eye = jnp.eye(h, dtype=w.dtype)
    return jnp.einsum('hij,hk->hikj', w, eye).reshape(h * i, h * j)


def _mix_ffn_kernel(h_ref, ys0_ref, ys1_ref, yn_ref, yl0_ref, yl1_ref, wglu_ref, wos_ref, won_ref, wol_ref,
                    g_ref, wg_ref, wu_ref, wd_ref, gf_ref, o_ref, *, final):
    tm = h_ref.shape[0]

    def rows(a_ref, b_ref):
        return jnp.concatenate([a_ref[...].reshape(tm, 128), b_ref[...].reshape(tm, 128)], axis=1)

    gl = jnp.dot(rows(ys0_ref, ys1_ref).astype(BF16), wglu_ref[...], preferred_element_type=F32)
    s5 = gl[:, :SSM_WIDTH] * jax.nn.sigmoid(gl[:, SSM_WIDTH:])
    acc = jnp.dot(s5.astype(BF16), wos_ref[...], preferred_element_type=F32)
    acc = acc + jnp.dot(yn_ref[...].astype(BF16), won_ref[...], preferred_element_type=F32)
    acc = acc + jnp.dot(rows(yl0_ref, yl1_ref).astype(BF16), wol_ref[...], preferred_element_type=F32)
    x = h_ref[...] + acc

    y = (x * lax.rsqrt(jnp.mean(x * x, axis=-1, keepdims=True) + RMS_EPS) * g_ref[...]).astype(BF16)
    a = jnp.dot(y, wg_ref[...], preferred_element_type=F32)
    b = jnp.dot(y, wu_ref[...], preferred_element_type=F32)
    m = (jax.nn.silu(a) * b).astype(BF16)
    out = x + jnp.dot(m, wd_ref[...], preferred_element_type=F32)
    if final:
        out = out * lax.rsqrt(jnp.mean(out * out, axis=-1, keepdims=True) + RMS_EPS) * gf_ref[...]
    o_ref[...] = out


def _mix_ffn(h, ys, yn, yl, wglu, wos, won, wol, g, wg, wu, wd, gf, final, b, seq, tm=512):
    d = h.shape[1]
    nt = seq // tm
    row = lambda w: pl.BlockSpec((tm, w), lambda b_, i: (b_ * nt + i, 0))
    const = lambda a: pl.BlockSpec(a.shape, lambda b_, i: (0, 0), pipeline_mode=pl.Buffered(1))
    slab4 = lambda s: s.reshape(seq // OCT, b, OCT, 128)
    return pl.pallas_call(
        functools.partial(_mix_ffn_kernel, final=final),
        grid=(b, nt),
        in_specs=[row(d), _slab_spec(tm, nt), _slab_spec(tm, nt), row(NSA_WIDTH),
                  _slab_spec(tm, nt), _slab_spec(tm, nt),
                  const(wglu), const(wos), const(won), const(wol),
                  const(g), const(wg), const(wu), const(wd), const(gf)],
        out_specs=row(d),
        out_shape=jax.ShapeDtypeStruct((b * seq, d), F32),
        compiler_params=_cparams("parallel", "parallel"),
        name="mix_ffn",
    )(h, slab4(ys[0]), slab4(ys[1]), yn, slab4(yl[0]), slab4(yl[1]), wglu, wos, won, wol, g, wg, wu, wd, gf)


def _head_interleave(w, axis):
    shape = w.shape
    split = shape[:axis] + (NSA_KV_HEADS, NSA_GQA, HEAD_DIM) + shape[axis + 1:]
    return jnp.swapaxes(w.reshape(split), axis, axis + 1).reshape(shape)


def _prep_w_in(w_in):
    o1 = SSM_WIDTH
    o2 = o1 + NSA_WIDTH
    o3 = o2 + 6 * NSA_KV_WIDTH
    o4 = o3 + 3 * NSA_Q_HEADS
    wq = _head_interleave(w_in[:, o1:o2], 1)
    pad = jnp.zeros((w_in.shape[0], 128 - 3 * NSA_Q_HEADS), w_in.dtype)
    w = jnp.concatenate([wq, w_in[:, o2:o3], w_in[:, o3:o4], pad, w_in[:, :o1], w_in[:, o4:]], axis=1)
    return w.astype(BF16)


def _s5_mixer(slabs, b, p):
    we, tz, cp, a16, dg = _s5_tables(p['s5_lam_re'], p['s5_lam_im'], p['s5_log_dt'], p['s5_b_re'],
                                     p['s5_b_im'], p['s5_c_re'], p['s5_c_im'], p['s5_d'])
    return _s5(slabs[0], slabs[1], we, tz, cp, a16, dg, nb=b)


def _nsa_mixer(z, b, seq, bias_tabs, ov, p):
    w1k, w2k, c0k = _compress_weights(p['nsa_pe_k'], p['nsa_w1_k'], p['nsa_w2_k'])
    w1v, w2v, c0v = _compress_weights(p['nsa_pe_v'], p['nsa_w1_v'], p['nsa_w2_v'])
    kc, vct = _compress(z.reshape(b, seq, Z_NSA), w1k, w1v, w2k, w2v, c0k, c0v)
    tab, t2 = bias_tabs
    return _nsa(z, kc, vct, t2, tab, ov, b, seq)


def _lru_mixer(slabs, b, p):
    sp = jax.nn.softplus(-p['lru_lam'].astype(F32)).reshape(1, -1)
    return _lru(slabs[0], slabs[1], slabs[2], slabs[3],
                p['lru_conv_w'].astype(F32), p['lru_conv_b'].astype(F32).reshape(1, -1),
                _block_diag(p['lru_w_a']).astype(BF16), p['lru_b_a'].astype(F32).reshape(1, -1),
                _block_diag(p['lru_w_x']).astype(BF16), p['lru_b_x'].astype(F32).reshape(1, -1),
                sp, nb=b)


def _layer(h2, b, seq, bias_tabs, ov, p, final, norm_final):
    z, slabs = _inproj(h2, p['norm_mix'].reshape(1, -1), _prep_w_in(p['w_in']), b, seq)
    ys = _s5_mixer(slabs[0:2], b, p)
    yn = _nsa_mixer(z, b, seq, bias_tabs, ov, p)
    yl = _lru_mixer(slabs[2:6], b, p)
    w_out = p['w_out']
    wos = w_out[:SSM_WIDTH].astype(BF16)
    won = _head_interleave(w_out[SSM_WIDTH:SSM_WIDTH + NSA_WIDTH], 0).astype(BF16)
    wol = w_out[SSM_WIDTH + NSA_WIDTH:].astype(BF16)
    return _mix_ffn(h2, ys, yn, yl, p['s5_w_glu'].astype(BF16), wos, won, wol,
                    p['norm_ffn'].reshape(1, -1), p['w_gate'].astype(BF16), p['w_up'].astype(BF16),
                    p['w_down'].astype(BF16), norm_final.reshape(1, -1), final, b, seq)


def _overlap_t(seq):
    ncp = seq // CMP_STRIDE
    nsel = seq // SEL_LEN
    cs = np.arange(ncp) * CMP_STRIDE
    ss = np.arange(nsel) * SEL_LEN
    ovl = (cs[None, :] < ss[:, None] + SEL_LEN) & (ss[:, None] < cs[None, :] + CMP_LEN)
    ovl[:, ncp - 1] = False
    return jnp.asarray(ovl.astype(np.float32))


def kernel(x, rel_bias_table, norm_mix, w_in, w_out, s5_lam_re, s5_lam_im, s5_log_dt, s5_b_re, s5_b_im, s5_c_re, s5_c_im, s5_d, s5_w_glu, nsa_pe_k, nsa_w1_k, nsa_w2_k, nsa_pe_v, nsa_w1_v, nsa_w2_v, lru_conv_w, lru_conv_b, lru_w_a, lru_b_a, lru_w_x, lru_b_x, lru_lam, norm_ffn, w_gate, w_up, w_down, norm_final):
    b, seq, d = x.shape
    depth = norm_mix.shape[0]
    per_layer = dict(norm_mix=norm_mix, w_in=w_in, w_out=w_out, s5_lam_re=s5_lam_re, s5_lam_im=s5_lam_im,
                     s5_log_dt=s5_log_dt, s5_b_re=s5_b_re, s5_b_im=s5_b_im, s5_c_re=s5_c_re,
                     s5_c_im=s5_c_im, s5_d=s5_d, s5_w_glu=s5_w_glu, nsa_pe_k=nsa_pe_k, nsa_w1_k=nsa_w1_k,
                     nsa_w2_k=nsa_w2_k, nsa_pe_v=nsa_pe_v, nsa_w1_v=nsa_w1_v, nsa_w2_v=nsa_w2_v,
                     lru_conv_w=lru_conv_w, lru_conv_b=lru_conv_b, lru_w_a=lru_w_a, lru_b_a=lru_b_a,
                     lru_w_x=lru_w_x, lru_b_x=lru_b_x, lru_lam=lru_lam, norm_ffn=norm_ffn, w_gate=w_gate,
                     w_up=w_up, w_down=w_down)
    bias_tabs = _bias_tables(rel_bias_table, seq)
    ov = _overlap_t(seq)
    h2 = x.reshape(b * seq, d)
    for l in range(depth):
        p = {k: v[l] for k, v in per_layer.items()}
        h2 = _layer(h2, b, seq, bias_tabs, ov, p, l == depth - 1, norm_final)
    return h2.reshape(b, seq, d)
```

```python
import functools
import math

import numpy as np
import jax
import jax.numpy as jnp
from jax import lax
from jax.experimental import pallas as pl
from jax.experimental.pallas import tpu as pltpu

F32 = jnp.float32
BF16 = jnp.bfloat16

D_MODEL = 1024
SSM_WIDTH = 256
NSA_WIDTH = 512
LRU_WIDTH = 256
S5_GROUP = 16
S5_GROUPS = 16
S5_STATE = 64
HEAD_DIM = 64
NSA_Q_HEADS = 8
NSA_KV_HEADS = 2
NSA_GQA = 4
NSA_KV_WIDTH = 128
CMP_LEN = 32
CMP_STRIDE = 16
SEL_LEN = 64
SEL_TOPK = 8
WINDOW = 256
Q_BLOCK = 64
LRU_HEADS = 4
LRU_HEAD_DIM = 64
CONV_WIDTH = 4
LRU_C = 8.0
REL_BUCKETS = 32
REL_MAX_DIST = 128
D_FF = 2816
NEG_INF = -1e9
LOG2E = 1.4426950408889634
RMS_EPS = 1e-6

NO_SLOT = -3e38

S5_CHUNK = 16
Z_Q, Z_KV, Z_G, Z_NSA = 0, 512, 1280, 1408
N_SLAB = 6
OCT = 8

VMEM_LIMIT = 56 * 1024 * 1024


def _cparams(*sem):
    return pltpu.CompilerParams(dimension_semantics=sem, vmem_limit_bytes=VMEM_LIMIT)


def _inproj_kernel(x_ref, g_ref, w_ref, o_ref, *slab_refs):
    x = x_ref[...]
    y = x * lax.rsqrt(jnp.mean(x * x, axis=-1, keepdims=True) + RMS_EPS) * g_ref[...]
    res = jnp.dot(y.astype(BF16), w_ref[...], preferred_element_type=F32)
    o_ref[...] = res[:, :Z_NSA]
    for m, s_ref in enumerate(slab_refs):
        s_ref[...] = res[:, Z_NSA + m * 128:Z_NSA + (m + 1) * 128].reshape(s_ref.shape)


def _slab_spec(tm, nt):
    return pl.BlockSpec((tm // OCT, None, OCT, 128), lambda b, i: (i, b, 0, 0))


def _inproj(x2, g, w, b, seq, tm=1024):
    d = x2.shape[1]
    n = w.shape[1]
    nt = seq // tm
    slab = jax.ShapeDtypeStruct((seq // OCT, b, OCT, 128), F32)
    outs = pl.pallas_call(
        _inproj_kernel,
        grid=(b, nt),
        in_specs=[pl.BlockSpec((tm, d), lambda b_, i: (b_ * nt + i, 0)),
                  pl.BlockSpec((1, d), lambda b_, i: (0, 0)),
                  pl.BlockSpec((d, n), lambda b_, i: (0, 0))],
        out_specs=[pl.BlockSpec((tm, Z_NSA), lambda b_, i: (b_ * nt + i, 0))] + [_slab_spec(tm, nt)] * N_SLAB,
        out_shape=[jax.ShapeDtypeStruct((b * seq, Z_NSA), F32)] + [slab] * N_SLAB,
        compiler_params=_cparams("parallel", "parallel"),
        name="inproj",
    )(x2, g, w)
    return outs[0], [s.reshape(b * seq, 128) for s in outs[1:]]


def _lane_regroup(x):
    r = x.shape[0]
    cols = [x[:, v * 128:(v + 1) * 128] for v in range(32)]
    piece = lax.broadcasted_iota(jnp.int32, (r, 128), 1) // 16
    out = [None] * 32
    for ah in range(2):
        for bh in range(2):
            vs = [cols[2 * (8 * ah + i) + bh] for i in range(8)]
            for stage in (4, 2, 1):
                upper = (piece & stage) != 0
                for i in range(8):
                    if i & stage:
                        continue
                    lo, hi = vs[i], vs[i + stage]
                    vs[i] = jnp.where(upper, pltpu.roll(hi, stage * 16, 1), lo)
                    vs[i + stage] = jnp.where(upper, hi, pltpu.roll(lo, 128 - stage * 16, 1))
            for j in range(8):
                out[2 * (8 * bh + j) + ah] = vs[j]
    return jnp.concatenate(out, axis=1)


def _octet_row(t, nb):
    return (t // OCT) * (nb * OCT) + t % OCT


def _s5_kernel(u0_ref, u1_ref, we_ref, tz_ref, cp_ref, a_ref, d_ref, o0_ref, o1_ref, e_ref, carry_ref,
               *, nb, kc):
    @pl.when(pl.program_id(0) == 0)
    def _():
        carry_ref[...] = jnp.zeros_like(carry_ref)

    u_refs = (u0_ref, u1_ref)
    u = jnp.concatenate(
        [jnp.concatenate([u_refs[j][pl.ds(_octet_row(k * S5_CHUNK + s, nb), nb, stride=OCT), :]
                          for s in range(S5_CHUNK) for j in range(2)], axis=1)
         for k in range(kc)], axis=0)
    ug = _lane_regroup(u)
    ugb = ug.astype(BF16)
    npair = S5_GROUPS // 2
    for p in range(npair):
        e_ref[:, p * 256:(p + 1) * 256] = jnp.dot(ugb[:, p * 512:(p + 1) * 512], we_ref[p],
                                                   preferred_element_type=F32)

    a = a_ref[...]

    def step(k, carry):
        r0 = pl.multiple_of(k * nb, nb)
        e = e_ref[pl.ds(r0, nb), :]
        e_ref[pl.ds(r0, nb), :] = carry
        new = []
        for p in range(npair):
            ar = a[:, p * 256:p * 256 + 128]
            ai = a[:, p * 256 + 128:(p + 1) * 256]
            cr = carry[:, p * 256:p * 256 + 128]
            ci = carry[:, p * 256 + 128:(p + 1) * 256]
            new.append(ar * cr - ai * ci + e[:, p * 256:p * 256 + 128])
            new.append(ar * ci + ai * cr + e[:, p * 256 + 128:(p + 1) * 256])
        return jnp.concatenate(new, axis=1)

    carry_ref[...] = lax.fori_loop(0, kc, step, carry_ref[...])

    eb = e_ref[...].astype(BF16)
    ys = []
    for p in range(npair):
        yc = jnp.dot(eb[:, p * 256:(p + 1) * 256], cp_ref[p], preferred_element_type=F32)
        y0 = jnp.dot(ugb[:, (2 * p) * 256:(2 * p + 1) * 256], tz_ref[2 * p], preferred_element_type=F32)
        y1 = jnp.dot(ugb[:, (2 * p + 1) * 256:(2 * p + 2) * 256], tz_ref[2 * p + 1],
                     preferred_element_type=F32)
        ys.append(yc + jnp.concatenate([y0, y1], axis=1))
    y = jnp.concatenate(ys, axis=1) + d_ref[...] * ug
    y = jax.nn.gelu(y)
    y = _lane_regroup(y)
    o_refs = (o0_ref, o1_ref)
    for k in range(kc):
        for s in range(S5_CHUNK):
            for j in range(2):
                lo = s * SSM_WIDTH + j * 128
                o_refs[j][pl.ds(_octet_row(k * S5_CHUNK + s, nb), nb, stride=OCT), :] = \
                    y[k * nb:(k + 1) * nb, lo:lo + 128]


def _s5(u0, u1, we, tz, cp, a16, dg, nb, kc=16):
    rows = u0.shape[0]
    r = kc * S5_CHUNK * nb
    const3 = lambda i: (0, 0, 0)
    slab = pl.BlockSpec((r, 128), lambda i: (i, 0))
    return pl.pallas_call(
        functools.partial(_s5_kernel, nb=nb, kc=kc),
        grid=(rows // r,),
        in_specs=[slab, slab,
                  pl.BlockSpec(we.shape, const3),
                  pl.BlockSpec(tz.shape, const3),
                  pl.BlockSpec(cp.shape, const3),
                  pl.BlockSpec(a16.shape, lambda i: (0, 0)),
                  pl.BlockSpec(dg.shape, lambda i: (0, 0))],
        out_specs=[slab, slab],
        out_shape=[jax.ShapeDtypeStruct((rows, 128), F32)] * 2,
        scratch_shapes=[pltpu.VMEM((kc * nb, 2048), F32), pltpu.VMEM((nb, 2048), F32)],
        compiler_params=_cparams("arbitrary"),
        name="s5",
    )(u0, u1, we, tz, cp, a16, dg)


def _s5_tables(lam_re, lam_im, log_dt, b_re, b_im, c_re, c_im, d_skip):
    L = S5_CHUNK
    G, P, C = S5_GROUPS, S5_STATE, S5_GROUP
    dt = jnp.exp(log_dt.astype(F32))[:, None]
    lr = lam_re.astype(F32)
    li = lam_im.astype(F32)
    mag = jnp.exp(lr * dt)
    ang = li * dt
    ab_re = mag * jnp.cos(ang)
    ab_im = mag * jnp.sin(ang)
    den = lr * lr + li * li
    f_re = ((ab_re - 1.0) * lr + ab_im * li) / den
    f_im = (ab_im * lr - (ab_re - 1.0) * li) / den
    br = b_re.astype(F32)
    bi = b_im.astype(F32)
    bb_re = f_re[..., None] * br - f_im[..., None] * bi
    bb_im = f_re[..., None] * bi + f_im[..., None] * br
    cr = c_re.astype(F32)
    ci = c_im.astype(F32)
    tau = jnp.arange(L + 1, dtype=F32)[:, None, None]
    pr = jnp.exp(lr * dt * tau) * jnp.cos(li * dt * tau)
    pi = jnp.exp(lr * dt * tau) * jnp.sin(li * dt * tau)

    prs = pr[:L][::-1]
    pis = pi[:L][::-1]
    we_re = jnp.einsum('sgp,gpc->gscp', prs, bb_re) - jnp.einsum('sgp,gpc->gscp', pis, bb_im)
    we_im = jnp.einsum('sgp,gpc->gscp', prs, bb_im) + jnp.einsum('sgp,gpc->gscp', pis, bb_re)
    we_re = we_re.reshape(G, L * C, P)
    we_im = we_im.reshape(G, L * C, P)
    z = jnp.zeros_like(we_re[0::2])
    top = jnp.concatenate([we_re[0::2], z, we_im[0::2], z], axis=-1)
    bot = jnp.concatenate([z, we_re[1::2], z, we_im[1::2]], axis=-1)
    we = jnp.concatenate([top, bot], axis=1)

    m_re = pr[:L, :, :, None] * bb_re[None] - pi[:L, :, :, None] * bb_im[None]
    m_im = pr[:L, :, :, None] * bb_im[None] + pi[:L, :, :, None] * bb_re[None]
    kern = jnp.einsum('gop,tgpc->gcto', cr, m_re) - jnp.einsum('gop,tgpc->gcto', ci, m_im)
    padded = jnp.concatenate([jnp.zeros((G, C, L * C), F32), kern.reshape(G, C, L * C)], axis=-1)
    tz = jnp.stack([padded[:, :, (L - s) * C:(2 * L - s) * C] for s in range(L)], axis=1)
    tz = tz.reshape(G, L * C, L * C)

    pr1 = pr[1:]
    pi1 = pi[1:]
    cp_re = jnp.einsum('gop,tgp->gpto', cr, pr1) - jnp.einsum('gop,tgp->gpto', ci, pi1)
    cp_im = -(jnp.einsum('gop,tgp->gpto', cr, pi1) + jnp.einsum('gop,tgp->gpto', ci, pr1))
    cp_re = cp_re.reshape(G, P, L * C)
    cp_im = cp_im.reshape(G, P, L * C)
    zc = jnp.zeros_like(cp_re[0::2])
    cp = jnp.concatenate([
        jnp.concatenate([cp_re[0::2], zc], axis=-1),
        jnp.concatenate([zc, cp_re[1::2]], axis=-1),
        jnp.concatenate([cp_im[0::2], zc], axis=-1),
        jnp.concatenate([zc, cp_im[1::2]], axis=-1)], axis=1)

    a_re = pr[L].reshape(G // 2, 2 * P)
    a_im = pi[L].reshape(G // 2, 2 * P)
    a16 = jnp.concatenate([a_re, a_im], axis=-1).reshape(1, G * 2 * P)
    dg = jnp.broadcast_to(d_skip.astype(F32).reshape(G, 1, C), (G, L, C)).reshape(1, G * L * C)
    return we.astype(BF16), tz.astype(BF16), cp.astype(BF16), a16, dg


def _compress_kernel(ak_ref, av_ref, w1k_ref, w1v_ref, w2k_ref, w2v_ref, c0k_ref, c0v_ref,
                     kc_ref, vct_ref):
    def run(a_ref, w1_ref, w2_ref, c0_ref):
        nc = a_ref.shape[1] // CMP_STRIDE
        a = jnp.concatenate([a_ref[0, pl.ds(l, nc, stride=CMP_STRIDE), :] for l in range(CMP_STRIDE)],
                            axis=1).astype(BF16)
        p1 = jnp.dot(a, w1_ref[0], preferred_element_type=F32)
        p2 = jnp.dot(a, w1_ref[1], preferred_element_type=F32)
        n = p1.shape[0]
        pre = p1 + pltpu.roll(p2, n - 1, 0) + c0_ref[...]
        out = jnp.dot(jax.nn.gelu(pre).astype(BF16), w2_ref[...], preferred_element_type=F32)
        row = lax.broadcasted_iota(jnp.int32, out.shape, 0)
        return jnp.where(row < n - 1, out, 0.0)

    kc_ref[0] = run(ak_ref, w1k_ref, w2k_ref, c0k_ref).astype(BF16)
    vct_ref[0] = run(av_ref, w1v_ref, w2v_ref, c0v_ref).T.astype(BF16)


def _compress(z3, w1k, w1v, w2k, w2v, c0k, c0v):
    b, seq, _ = z3.shape
    nc = seq // CMP_STRIDE
    c3 = lambda i: (0, 0, 0)
    c2 = lambda i: (0, 0)
    return pl.pallas_call(
        _compress_kernel,
        grid=(b,),
        in_specs=[pl.BlockSpec((1, seq, 128), lambda i: (i, 0, Z_KV // 128)),
                  pl.BlockSpec((1, seq, 128), lambda i: (i, 0, Z_KV // 128 + 1)),
                  pl.BlockSpec(w1k.shape, c3), pl.BlockSpec(w1v.shape, c3),
                  pl.BlockSpec(w2k.shape, c2), pl.BlockSpec(w2v.shape, c2),
                  pl.BlockSpec(c0k.shape, c2), pl.BlockSpec(c0v.shape, c2)],
        out_specs=[pl.BlockSpec((1, nc, 128), lambda i: (i, 0, 0)),
                   pl.BlockSpec((1, 128, nc), lambda i: (i, 0, 0))],
        out_shape=[jax.ShapeDtypeStruct((b, nc, 128), BF16),
                   jax.ShapeDtypeStruct((b, 128, nc), BF16)],
        compiler_params=_cparams("parallel"),
        name="compress",
    )(z3, z3, w1k, w1v, w2k, w2v, c0k, c0v)


def _compress_weights(pe, w1, w2):
    w1 = w1.astype(F32)
    half = (CMP_LEN // 2) * HEAD_DIM
    eye = jnp.eye(NSA_KV_HEADS, dtype=F32)

    def bd(w):
        w = w.reshape(CMP_STRIDE, HEAD_DIM, HEAD_DIM)
        return jnp.einsum('ldo,hk->lhdko', w, eye).reshape(NSA_KV_HEADS * half, NSA_KV_HEADS * HEAD_DIM)

    w1s = jnp.stack([bd(w1[:half]), bd(w1[half:])])
    z2 = jnp.zeros((HEAD_DIM, HEAD_DIM), F32)
    w2f = w2.astype(F32)
    w2s = jnp.concatenate([jnp.concatenate([w2f, z2], axis=1),
                           jnp.concatenate([z2, w2f], axis=1)], axis=0)
    c0 = pe.astype(F32).reshape(1, CMP_LEN * HEAD_DIM) @ w1
    c0 = jnp.concatenate([c0, c0], axis=1)
    return w1s.astype(BF16), w2s.astype(BF16), c0


def _t5_bucket_np(dist):
    n = np.maximum(dist, 0)
    max_exact = REL_BUCKETS // 2
    nf = np.maximum(n, 1).astype(np.float32)
    large = max_exact + (np.log(nf / max_exact) / math.log(REL_MAX_DIST / max_exact)
                         * (REL_BUCKETS - max_exact)).astype(np.int32)
    large = np.minimum(large, REL_BUCKETS - 1)
    return np.where(n < max_exact, n, large)


TAB_FAR, TAB_WIN_EDGE, TAB_NONE = 3, 4, 5


def _bias_tables(rel_table, seq):
    tbl = rel_table.astype(F32)
    heads = NSA_Q_HEADS

    def lookup(bucket):
        onehot = jnp.asarray(bucket[..., None] == np.arange(REL_BUCKETS)).astype(F32)
        return jnp.einsum('...k,kh->...h', onehot, tbl, precision=lax.Precision.HIGHEST) * LOG2E

    r = np.arange(Q_BLOCK)[None, :]
    c = np.arange(SEL_LEN)[:, None]
    near = lookup(_t5_bucket_np(64 * np.arange(3)[:, None, None] + (r - c)[None]))
    far = jnp.broadcast_to(tbl[REL_BUCKETS - 1] * LOG2E, (SEL_LEN, Q_BLOCK, heads))
    none = jnp.full((SEL_LEN, Q_BLOCK, heads), NO_SLOT, F32)
    causal = jnp.asarray(r - c >= 0)[:, :, None]
    edge = jnp.asarray(c > r)[:, :, None]
    tab = jnp.stack([jnp.where(causal, near[0], none), near[1], near[2], far,
                     jnp.where(edge, far, none), none])
    tab = tab.transpose(0, 1, 3, 2).reshape(6, SEL_LEN, heads * Q_BLOCK)
    ncp = seq // CMP_STRIDE
    k = np.arange(2 * ncp + 4)[:, None]
    dist = r - CMP_STRIDE * (k - (ncp - 4)) - (CMP_LEN - 1)
    t2 = jnp.where(jnp.asarray(dist >= 0)[:, :, None], lookup(_t5_bucket_np(dist)), NO_SLOT)
    t2 = t2.transpose(0, 2, 1).reshape(2 * ncp + 4, heads * Q_BLOCK)
    t2 = jnp.stack([t2[:2 * ncp], t2[4:]])
    return tab, t2


def _nsa_kernel(q_ref, gl_ref, kc_ref, vct_ref, ks_ref, vs_ref, kw_ref, vw_ref, t2_ref, tab_ref,
                ov_ref, o_ref, s_ref, p_ref, acc_ref, accw_ref, vst_ref, vwt_ref, *, ncp, nw):
    @pl.when(pl.program_id(1) == 0)
    def _():
        for v_ref, vt_ref in ((vs_ref, vst_ref), (vw_ref, vwt_ref)):
            kt = vt_ref.shape[2]
            ones = jnp.ones((16, kt), F32)
            for u in range(vt_ref.shape[0]):
                vt = v_ref[0, u * kt:(u + 1) * kt, :].T
                vt_ref[u] = jnp.concatenate([vt[0:HEAD_DIM], ones, vt[HEAD_DIM:], ones], axis=0).astype(BF16)

    qi0 = pl.program_id(1) * nw
    cw = nw * NSA_GQA * Q_BLOCK
    ncol = NSA_KV_HEADS * cw
    vrows = HEAD_DIM + 16
    lane = lax.broadcasted_iota(jnp.int32, (Q_BLOCK, 128), 1)

    q = q_ref[...] * (HEAD_DIM ** -0.5 * LOG2E)
    pieces = []
    for h in range(NSA_KV_HEADS):
        keep = (lane >= 64) if h == 1 else (lane < 64)
        for w in range(nw):
            for g in range(NSA_GQA):
                pieces.append(jnp.where(keep, q[w * 64:(w + 1) * 64, g * 128:(g + 1) * 128], 0.0))
    qpad = jnp.concatenate(pieces, axis=0).astype(BF16)

    col = lax.broadcasted_iota(jnp.int32, (1, ncol), 1)
    rcol = col % Q_BLOCK
    qcol = qi0 + (col // (NSA_GQA * Q_BLOCK)) % nw
    tvec = Q_BLOCK * qcol + rcol
    nt_dims = (((1,), (1,)), ((), ()))

    def per_head_cols(tiles):
        return jnp.concatenate([tiles[w][:, h * 256:(h + 1) * 256]
                                for h in range(NSA_KV_HEADS) for w in range(nw)], axis=1)

    def slot_terms(m, n_masked):
        m_fin = jnp.where(n_masked > 0, jnp.maximum(m, NEG_INF * LOG2E), m)
        return jnp.exp2(m - m_fin), n_masked * jnp.exp2(NEG_INF * LOG2E - m_fin)

    t2_tiles = []
    for w in range(nw):
        start = (ncp - 4) - 4 * (qi0 + w)
        par = (start // 4) % 2
        a0 = pl.multiple_of(start - 4 * par, 8)
        t2_tiles.append(t2_ref[par, pl.ds(a0, ncp), :])
    s = lax.dot_general(kc_ref[0], qpad, nt_dims, preferred_element_type=F32) + per_head_cols(t2_tiles)
    m = jnp.max(s, axis=0, keepdims=True)
    e = jnp.exp2(s - m)
    n_valid = jnp.clip(jnp.right_shift(tvec - (CMP_STRIDE - 1), 4), 0, ncp - 1)
    scale, extra = slot_terms(m, (ncp - 1 - n_valid).astype(F32))
    p_c = e * (scale / (jnp.sum(e, axis=0, keepdims=True) * scale + extra))
    p_cb = p_c.astype(BF16)
    o_c = [jnp.dot(vct_ref[0, h * 64:(h + 1) * 64, :], p_cb[:, h * cw:(h + 1) * cw],
                   preferred_element_type=F32) for h in range(NSA_KV_HEADS)]

    lane_c = lax.broadcasted_iota(jnp.int32, (ncp, 128), 1)
    parts = []
    for w in range(nw):
        halves = []
        for h in range(NSA_KV_HEADS):
            base = h * cw + w * 256
            ph = p_c[:, base:base + 128] + p_c[:, base + 128:base + 256]
            halves.append(ph + pltpu.roll(ph, 64, 1))
        parts.append(jnp.where(lane_c < 64, halves[0], halves[1]))
    imp = jnp.dot(ov_ref[...], jnp.concatenate(parts, axis=1), preferred_element_type=F32,
                  precision=lax.Precision.HIGHEST)
    nsel = imp.shape[0]
    jidx = lax.broadcasted_iota(jnp.int32, (nsel, 1), 0)
    qv = qi0 + lax.broadcasted_iota(jnp.int32, (1, nw * 128), 1) // 128
    forced = jnp.logical_or(jidx == 0, jnp.logical_or(jidx == qv, jidx == qv - 1))
    avail = jidx <= qv
    cand = jnp.logical_and(avail, jnp.logical_not(forced))
    budget = SEL_TOPK - (1 + jnp.where(qv >= 1, 1, 0) + jnp.where(qv >= 2, 1, 0))
    rank = jnp.zeros((nsel, nw * 128), jnp.int32)
    for jp in range(1, nsel - 2):
        row = imp[jp:jp + 1, :]
        ge = jnp.where(row >= imp, 1, 0)
        gt = jnp.where(row > imp, 1, 0)
        is_cand = jnp.where(qv - 2 >= jp, 1, 0)
        rank = rank + jnp.where(jidx > jp, ge, gt) * is_cand
    sel = jnp.logical_or(jnp.logical_and(forced, avail), jnp.logical_and(cand, rank < budget))
    seladd = jnp.where(sel, 0.0, NO_SLOT)
    pad = jnp.zeros((128 - nsel, 128), F32)
    mask_rows = []
    for w in range(nw):
        mask_rows.append(jnp.concatenate([seladd[:, w * 128:(w + 1) * 128], pad], axis=0).T)
    sel_cols = jnp.concatenate([mask_rows[w][h * 64:(h + 1) * 64, :] for h in range(NSA_KV_HEADS)
                                for w in range(nw) for _ in range(NSA_GQA)], axis=0)
    qsel = jnp.concatenate([qpad, sel_cols.astype(BF16)], axis=1)

    def block_adds(blk, selected):
        tiles = []
        for w in range(nw):
            d = qi0 + w - blk
            if selected:
                e = jnp.where(d < 0, TAB_FAR, jnp.minimum(d, TAB_FAR))
            else:
                outside = jnp.logical_or(blk < 0, jnp.logical_or(d < 0, d > TAB_NONE))
                e = jnp.where(outside, TAB_NONE, d)
            tiles.append(tab_ref[e])
        return per_head_cols(tiles)

    def softmax_tile(m_i, s, adds):
        sm = jnp.concatenate([s[k * 64:(k + 1) * 64] + adds[k] for k in range(len(adds))], axis=0)
        m_new = jnp.maximum(m_i, jnp.max(sm, axis=0, keepdims=True))
        return m_new, jnp.exp2(m_i - m_new), jnp.exp2(sm - m_new).astype(BF16)

    def add_values(ref, alpha, vt_tile, p):
        for h in range(NSA_KV_HEADS):
            ref[h] = alpha[:, h * cw:(h + 1) * cw] * ref[h] + jnp.dot(
                vt_tile[h * vrows:(h + 1) * vrows, :], p[:, h * cw:(h + 1) * cw], preferred_element_type=F32)

    m_init = jnp.full((1, ncol), NO_SLOT, F32)

    accw_ref[...] = jnp.zeros_like(accw_ref)
    m_w = m_init
    for i in range(nw // 4 + 1):
        tj = qi0 // 4 - 1 + i
        tc = jnp.maximum(tj, 0)
        r0 = pl.multiple_of(tc * 256, 256)
        s = lax.dot_general(kw_ref[0, pl.ds(r0, 256), :].astype(BF16), qpad, nt_dims,
                            preferred_element_type=F32)
        m_w, alpha, p = softmax_tile(m_w, s, [block_adds(4 * tj + k, False) for k in range(4)])
        add_values(accw_ref, alpha, vwt_ref[tc], p)

    n_tiles = (qi0 + nw - 1) // 4 + 1

    def masked_scores(u):
        r0 = pl.multiple_of(u * 256, 256)
        block_of_row = 4 * u + lax.broadcasted_iota(jnp.int32, (256, 128), 0) // SEL_LEN
        onehot = jnp.where(lax.broadcasted_iota(jnp.int32, (256, 128), 1) == block_of_row, 1.0, 0.0)
        keys = jnp.concatenate([ks_ref[0, pl.ds(r0, 256), :].astype(BF16), onehot.astype(BF16)], axis=1)
        s = lax.dot_general(keys, qsel, nt_dims, preferred_element_type=F32)
        tmax = None
        for k in range(4):
            sk = s[k * 64:(k + 1) * 64] + block_adds(4 * u + k, True)
            s_ref[k * 64:(k + 1) * 64, :] = sk
            kmax = jnp.max(sk, axis=0, keepdims=True)
            tmax = kmax if tmax is None else jnp.maximum(tmax, kmax)
        return tmax

    tmax0 = masked_scores(0)
    p_ref[...] = jnp.zeros_like(p_ref)
    acc_ref[...] = jnp.zeros_like(acc_ref)

    def sel_body(u, carry):
        m_i, alpha_prev, tmax = carry
        add_values(acc_ref, alpha_prev, vst_ref[jnp.maximum(u - 1, 0)], p_ref[...])
        m_new = jnp.maximum(m_i, tmax)
        p_ref[...] = jnp.exp2(s_ref[...] - m_new).astype(BF16)
        tmax_next = masked_scores(jnp.minimum(u + 1, n_tiles - 1))
        return m_new, jnp.exp2(m_i - m_new), tmax_next

    m_s, alpha_last, _ = lax.fori_loop(0, n_tiles, sel_body, (m_init, jnp.ones((1, ncol), F32), tmax0))
    add_values(acc_ref, alpha_last, vst_ref[n_tiles - 1], p_ref[...])

    nm_s = (SEL_LEN * jnp.maximum(SEL_TOPK - (qcol + 1), 0) + (SEL_LEN - 1 - rcol)).astype(F32)
    nm_w = (WINDOW + Q_BLOCK - jnp.minimum(tvec + 1, WINDOW)).astype(F32)

    def normalised(m, ref, n_masked):
        scale, extra = slot_terms(m, n_masked)
        out = []
        for h in range(NSA_KV_HEADS):
            sl = slice(h * cw, (h + 1) * cw)
            acc = ref[h]
            denom = acc[HEAD_DIM:HEAD_DIM + 1, :] * scale[:, sl] + extra[:, sl]
            out.append((scale[:, sl] / denom, acc[0:HEAD_DIM, :]))
        return out

    o_s = normalised(m_s, acc_ref, nm_s)
    o_w = normalised(m_w, accw_ref, nm_w)

    lane1 = lax.broadcasted_iota(jnp.int32, (1, 128), 1)
    gts = []
    for w in range(nw):
        g = jax.nn.sigmoid(gl_ref[w * 64:(w + 1) * 64, :])
        gts.append(jnp.concatenate([g, g], axis=0).T)

    def gate_vec(h, br):
        tiles = []
        for w in range(nw):
            for gg in range(2):
                c0 = (h * 4 + 2 * gg) * 3 + br
                c1 = (h * 4 + 2 * gg + 1) * 3 + br
                tiles.append(jnp.where(lane1 < 64, gts[w][c0:c0 + 1, :], gts[w][c1:c1 + 1, :]))
        return jnp.concatenate(tiles, axis=1)

    tot = []
    for h in range(NSA_KV_HEADS):
        t = gate_vec(h, 0) * o_c[h]
        t = t + (gate_vec(h, 1) * o_s[h][0]) * o_s[h][1]
        t = t + (gate_vec(h, 2) * o_w[h][0]) * o_w[h][1]
        tot.append(t)
    for w in range(nw):
        ot = jnp.concatenate([t[:, w * 256:(w + 1) * 256] for t in tot], axis=0).T
        o_ref[w * 64:(w + 1) * 64, :] = jnp.concatenate([ot[g * 64:(g + 1) * 64, :]
                                                         for g in range(NSA_GQA)], axis=1)


def _nsa(z, kc, vct, t2, tab, ov, b, seq, nw=4):
    nst = seq // (Q_BLOCK * nw)
    ncp = seq // CMP_STRIDE
    rows = Q_BLOCK * nw
    cw = nw * NSA_GQA * Q_BLOCK
    z3 = z.reshape(b, seq, Z_NSA)
    kv_col = lambda n: pl.BlockSpec((1, seq, 128), lambda i, j: (i, 0, Z_KV // 128 + n))
    return pl.pallas_call(
        functools.partial(_nsa_kernel, ncp=ncp, nw=nw),
        grid=(b, nst),
        in_specs=[pl.BlockSpec((rows, NSA_WIDTH), lambda i, j: (i * nst + j, Z_Q // NSA_WIDTH)),
                  pl.BlockSpec((rows, 128), lambda i, j: (i * nst + j, Z_G // 128)),
                  pl.BlockSpec((1, ncp, 128), lambda i, j: (i, 0, 0)),
                  pl.BlockSpec((1, 128, ncp), lambda i, j: (i, 0, 0)),
                  kv_col(2), kv_col(3), kv_col(4), kv_col(5),
                  pl.BlockSpec(t2.shape, lambda i, j: (0, 0, 0)),
                  pl.BlockSpec(tab.shape, lambda i, j: (0, 0, 0)),
                  pl.BlockSpec(ov.shape, lambda i, j: (0, 0))],
        out_specs=pl.BlockSpec((rows, NSA_WIDTH), lambda i, j: (i * nst + j, 0)),
        out_shape=jax.ShapeDtypeStruct((b * seq, NSA_WIDTH), F32),
        scratch_shapes=[pltpu.VMEM((256, NSA_KV_HEADS * cw), F32),
                        pltpu.VMEM((256, NSA_KV_HEADS * cw), BF16),
                        pltpu.VMEM((NSA_KV_HEADS, HEAD_DIM + 16, cw), F32),
                        pltpu.VMEM((NSA_KV_HEADS, HEAD_DIM + 16, cw), F32),
                        pltpu.VMEM((seq // 256, NSA_KV_HEADS * (HEAD_DIM + 16), 256), BF16),
                        pltpu.VMEM((seq // 256, NSA_KV_HEADS * (HEAD_DIM + 16), 256), BF16)],
        compiler_params=_cparams("parallel", "arbitrary"),
        name="nsa",
    )(z, z, kc, vct, z3, z3, z3, z3, t2, tab, ov)


def _lru_kernel(x0_ref, x1_ref, g0_ref, g1_ref, cw_ref, cb_ref, wa_ref, ba_ref, wx_ref, bx_ref, sp_ref,
                o0_ref, o1_ref, xprev_ref, h_ref, a_scr, b_scr, *, nb, tc):
    @pl.when(pl.program_id(0) == 0)
    def _():
        xprev_ref[...] = jnp.zeros_like(xprev_ref)
        h_ref[...] = jnp.zeros_like(h_ref)

    x = jnp.concatenate([x0_ref[...], x1_ref[...]], axis=1)
    rows, width = x.shape
    grp = nb * OCT
    xm = jnp.concatenate([xprev_ref[...], x[:rows - grp]], axis=0)
    xprev_ref[...] = x[rows - grp:]
    x3 = x.reshape(rows // OCT, OCT, width)
    xm3 = xm.reshape(rows // OCT, OCT, width)
    tlo = lax.broadcasted_iota(jnp.int32, (1, OCT, 1), 1)
    cw = cw_ref[...]
    xc = cb_ref[...].reshape(1, 1, width) + cw[CONV_WIDTH - 1:CONV_WIDTH, :].reshape(1, 1, width) * x3
    for k in range(1, CONV_WIDTH):
        delayed = jnp.where(tlo >= k, pltpu.roll(x3, k, 1), pltpu.roll(xm3, k, 1))
        xc = xc + cw[CONV_WIDTH - 1 - k:CONV_WIDTH - k, :].reshape(1, 1, width) * delayed
    xc = xc.reshape(rows, width)
    xcb = xc.astype(BF16)
    def sigmoid(v):
        return 0.5 * jnp.tanh(0.5 * v) + 0.5

    gate_r = sigmoid(jnp.dot(xcb, wa_ref[...], preferred_element_type=F32) + ba_ref[...])
    gate_i = sigmoid(jnp.dot(xcb, wx_ref[...], preferred_element_type=F32) + bx_ref[...])
    log_a = -LRU_C * gate_r * sp_ref[...]
    a = jnp.exp(log_a)
    th = jnp.tanh(log_a)
    bt = jnp.sqrt(-2.0 * th / (1.0 - th)) * gate_i * xc
    for j in range(2):
        a_scr[j] = a[:, j * 128:(j + 1) * 128]
        b_scr[j] = bt[:, j * 128:(j + 1) * 128]

    def octet(o, h):
        base = pl.multiple_of(o * (nb * OCT), nb * OCT)
        for ti in range(OCT):
            new = []
            for j in range(2):
                rows = pl.ds(base + ti, nb, stride=OCT)
                hj = a_scr[j, rows, :] * h[j] + b_scr[j, rows, :]
                b_scr[j, rows, :] = hj
                new.append(hj)
            h = tuple(new)
        return h

    h = lax.fori_loop(0, tc // OCT, octet, (h_ref[0], h_ref[1]))
    for j, (g_ref, o_ref) in enumerate(((g0_ref, o0_ref), (g1_ref, o1_ref))):
        h_ref[j] = h[j]
        o_ref[...] = b_scr[j] * jax.nn.gelu(g_ref[...])


def _lru(x0, x1, g0, g1, cw, cb, wa, ba, wx, bx, sp, nb, tc=64):
    rows = x0.shape[0]
    r = tc * nb
    c2 = lambda i: (0, 0)
    slab = pl.BlockSpec((r, 128), lambda i: (i, 0))
    return pl.pallas_call(
        functools.partial(_lru_kernel, nb=nb, tc=tc),
        grid=(rows // r,),
        in_specs=[slab, slab, slab, slab,
                  pl.BlockSpec(cw.shape, c2), pl.BlockSpec(cb.shape, c2),
                  pl.BlockSpec(wa.shape, c2), pl.BlockSpec(ba.shape, c2),
                  pl.BlockSpec(wx.shape, c2), pl.BlockSpec(bx.shape, c2),
                  pl.BlockSpec(sp.shape, c2)],
        out_specs=[slab, slab],
        out_shape=[jax.ShapeDtypeStruct((rows, 128), F32)] * 2,
        scratch_shapes=[pltpu.VMEM((nb * OCT, LRU_WIDTH), F32), pltpu.VMEM((2, nb, 128), F32),
                        pltpu.VMEM((2, r, 128), F32), pltpu.VMEM((2, r, 128), F32)],
        compiler_params=_cparams("arbitrary"),
        name="lru",
    )(x0, x1, g0, g1, cw, cb, wa, ba, wx, bx, sp)


def _block_diag(w):
    h, i, j = w.shape
    eye = jnp.eye(h, dtype=w.dtype)
    return jnp.einsum('hij,hk->hikj', w, eye).reshape(h * i, h * j)


def _mix_ffn_kernel(h_ref, ys0_ref, ys1_ref, yn_ref, yl0_ref, yl1_ref, wglu_ref, wos_ref, won_ref, wol_ref,
                    g_ref, wg_ref, wu_ref, wd_ref, gf_ref, o_ref, *, final):
    tm = h_ref.shape[0]

    def rows(a_ref, b_ref):
        return jnp.concatenate([a_ref[...].reshape(tm, 128), b_ref[...].reshape(tm, 128)], axis=1)

    gl = jnp.dot(rows(ys0_ref, ys1_ref).astype(BF16), wglu_ref[...], preferred_element_type=F32)
    s5 = gl[:, :SSM_WIDTH] * jax.nn.sigmoid(gl[:, SSM_WIDTH:])
    acc = jnp.dot(s5.astype(BF16), wos_ref[...], preferred_element_type=F32)
    acc = acc + jnp.dot(yn_ref[...].astype(BF16), won_ref[...], preferred_element_type=F32)
    acc = acc + jnp.dot(rows(yl0_ref, yl1_ref).astype(BF16), wol_ref[...], preferred_element_type=F32)
    x = h_ref[...] + acc

    y = (x * lax.rsqrt(jnp.mean(x * x, axis=-1, keepdims=True) + RMS_EPS) * g_ref[...]).astype(BF16)
    a = jnp.dot(y, wg_ref[...], preferred_element_type=F32)
    b = jnp.dot(y, wu_ref[...], preferred_element_type=F32)
    m = (jax.nn.silu(a) * b).astype(BF16)
    out = x + jnp.dot(m, wd_ref[...], preferred_element_type=F32)
    if final:
        out = out * lax.rsqrt(jnp.mean(out * out, axis=-1, keepdims=True) + RMS_EPS) * gf_ref[...]
    o_ref[...] = out


def _mix_ffn(h, ys, yn, yl, wglu, wos, won, wol, g, wg, wu, wd, gf, final, b, seq, tm=512):
    d = h.shape[1]
    nt = seq // tm
    row = lambda w: pl.BlockSpec((tm, w), lambda b_, i: (b_ * nt + i, 0))
    const = lambda a: pl.BlockSpec(a.shape, lambda b_, i: (0, 0), pipeline_mode=pl.Buffered(1))
    slab4 = lambda s: s.reshape(seq // OCT, b, OCT, 128)
    return pl.pallas_call(
        functools.partial(_mix_ffn_kernel, final=final),
        grid=(b, nt),
        in_specs=[row(d), _slab_spec(tm, nt), _slab_spec(tm, nt), row(NSA_WIDTH),
                  _slab_spec(tm, nt), _slab_spec(tm, nt),
                  const(wglu), const(wos), const(won), const(wol),
                  const(g), const(wg), const(wu), const(wd), const(gf)],
        out_specs=row(d),
        out_shape=jax.ShapeDtypeStruct((b * seq, d), F32),
        compiler_params=_cparams("parallel", "parallel"),
        name="mix_ffn",
    )(h, slab4(ys[0]), slab4(ys[1]), yn, slab4(yl[0]), slab4(yl[1]), wglu, wos, won, wol, g, wg, wu, wd, gf)


def _head_interleave(w, axis):
    shape = w.shape
    split = shape[:axis] + (NSA_KV_HEADS, NSA_GQA, HEAD_DIM) + shape[axis + 1:]
    return jnp.swapaxes(w.reshape(split), axis, axis + 1).reshape(shape)


def _prep_w_in(w_in):
    o1 = SSM_WIDTH
    o2 = o1 + NSA_WIDTH
    o3 = o2 + 6 * NSA_KV_WIDTH
    o4 = o3 + 3 * NSA_Q_HEADS
    wq = _head_interleave(w_in[:, o1:o2], 1)
    pad = jnp.zeros((w_in.shape[0], 128 - 3 * NSA_Q_HEADS), w_in.dtype)
    w = jnp.concatenate([wq, w_in[:, o2:o3], w_in[:, o3:o4], pad, w_in[:, :o1], w_in[:, o4:]], axis=1)
    return w.astype(BF16)


def _s5_mixer(slabs, b, p):
    we, tz, cp, a16, dg = _s5_tables(p['s5_lam_re'], p['s5_lam_im'], p['s5_log_dt'], p['s5_b_re'],
                                     p['s5_b_im'], p['s5_c_re'], p['s5_c_im'], p['s5_d'])
    return _s5(slabs[0], slabs[1], we, tz, cp, a16, dg, nb=b)


def _nsa_mixer(z, b, seq, bias_tabs, ov, p):
    w1k, w2k, c0k = _compress_weights(p['nsa_pe_k'], p['nsa_w1_k'], p['nsa_w2_k'])
    w1v, w2v, c0v = _compress_weights(p['nsa_pe_v'], p['nsa_w1_v'], p['nsa_w2_v'])
    kc, vct = _compress(z.reshape(b, seq, Z_NSA), w1k, w1v, w2k, w2v, c0k, c0v)
    tab, t2 = bias_tabs
    return _nsa(z, kc, vct, t2, tab, ov, b, seq)


def _lru_mixer(slabs, b, p):
    sp = jax.nn.softplus(-p['lru_lam'].astype(F32)).reshape(1, -1)
    return _lru(slabs[0], slabs[1], slabs[2], slabs[3],
                p['lru_conv_w'].astype(F32), p['lru_conv_b'].astype(F32).reshape(1, -1),
                _block_diag(p['lru_w_a']).astype(BF16), p['lru_b_a'].astype(F32).reshape(1, -1),
                _block_diag(p['lru_w_x']).astype(BF16), p['lru_b_x'].astype(F32).reshape(1, -1),
                sp, nb=b)


def _layer(h2, b, seq, bias_tabs, ov, p, final, norm_final):
    z, slabs = _inproj(h2, p['norm_mix'].reshape(1, -1), _prep_w_in(p['w_in']), b, seq)
    ys = _s5_mixer(slabs[0:2], b, p)
    yn = _nsa_mixer(z, b, seq, bias_tabs, ov, p)
    yl = _lru_mixer(slabs[2:6], b, p)
    w_out = p['w_out']
    wos = w_out[:SSM_WIDTH].astype(BF16)
    won = _head_interleave(w_out[SSM_WIDTH:SSM_WIDTH + NSA_WIDTH], 0).astype(BF16)
    wol = w_out[SSM_WIDTH + NSA_WIDTH:].astype(BF16)
    return _mix_ffn(h2, ys, yn, yl, p['s5_w_glu'].astype(BF16), wos, won, wol,
                    p['norm_ffn'].reshape(1, -1), p['w_gate'].astype(BF16), p['w_up'].astype(BF16),
                    p['w_down'].astype(BF16), norm_final.reshape(1, -1), final, b, seq)


def _overlap_t(seq):
    ncp = seq // CMP_STRIDE
    nsel = seq // SEL_LEN
    cs = np.arange(ncp) * CMP_STRIDE
    ss = np.arange(nsel) * SEL_LEN
    ovl = (cs[None, :] < ss[:, None] + SEL_LEN) & (ss[:, None] < cs[None, :] + CMP_LEN)
    ovl[:, ncp - 1] = False
    return jnp.asarray(ovl.astype(np.float32))


def kernel(x, rel_bias_table, norm_mix, w_in, w_out, s5_lam_re, s5_lam_im, s5_log_dt, s5_b_re, s5_b_im, s5_c_re, s5_c_im, s5_d, s5_w_glu, nsa_pe_k, nsa_w1_k, nsa_w2_k, nsa_pe_v, nsa_w1_v, nsa_w2_v, lru_conv_w, lru_conv_b, lru_w_a, lru_b_a, lru_w_x, lru_b_x, lru_lam, norm_ffn, w_gate, w_up, w_down, norm_final):
    b, seq, d = x.shape
    depth = norm_mix.shape[0]
    per_layer = dict(norm_mix=norm_mix, w_in=w_in, w_out=w_out, s5_lam_re=s5_lam_re, s5_lam_im=s5_lam_im,
                     s5_log_dt=s5_log_dt, s5_b_re=s5_b_re, s5_b_im=s5_b_im, s5_c_re=s5_c_re,
                     s5_c_im=s5_c_im, s5_d=s5_d, s5_w_glu=s5_w_glu, nsa_pe_k=nsa_pe_k, nsa_w1_k=nsa_w1_k,
                     nsa_w2_k=nsa_w2_k, nsa_pe_v=nsa_pe_v, nsa_w1_v=nsa_w1_v, nsa_w2_v=nsa_w2_v,
                     lru_conv_w=lru_conv_w, lru_conv_b=lru_conv_b, lru_w_a=lru_w_a, lru_b_a=lru_b_a,
                     lru_w_x=lru_w_x, lru_b_x=lru_b_x, lru_lam=lru_lam, norm_ffn=norm_ffn, w_gate=w_gate,
                     w_up=w_up, w_down=w_down)
    bias_tabs = _bias_tables(rel_bias_table, seq)
    ov = _overlap_t(seq)
    h2 = x.reshape(b * seq, d)
    for l in range(depth):
        p = {k: v[l] for k, v in per_layer.items()}
        h2 = _layer(h2, b, seq, bias_tabs, ov, p, l == depth - 1, norm_final)
    return h2.reshape(b, seq, d)
```

```python
import functools
import math

import numpy as np
import jax
import jax.numpy as jnp
from jax import lax
from jax.experimental import pallas as pl
from jax.experimental.pallas import tpu as pltpu

F32 = jnp.float32
BF16 = jnp.bfloat16

D_MODEL = 1024
SSM_WIDTH = 256
NSA_WIDTH = 512
LRU_WIDTH = 256
S5_GROUP = 16
S5_GROUPS = 16
S5_STATE = 64
HEAD_DIM = 64
NSA_Q_HEADS = 8
NSA_KV_HEADS = 2
NSA_GQA = 4
NSA_KV_WIDTH = 128
CMP_LEN = 32
CMP_STRIDE = 16
SEL_LEN = 64
SEL_TOPK = 8
WINDOW = 256
Q_BLOCK = 64
LRU_HEADS = 4
LRU_HEAD_DIM = 64
CONV_WIDTH = 4
LRU_C = 8.0
REL_BUCKETS = 32
REL_MAX_DIST = 128
D_FF = 2816
NEG_INF = -1e9
LOG2E = 1.4426950408889634
RMS_EPS = 1e-6

NO_SLOT = -3e38

S5_CHUNK = 16
Z_Q, Z_KV, Z_G, Z_NSA = 0, 512, 1280, 1408
N_SLAB = 6
OCT = 8

VMEM_LIMIT = 56 * 1024 * 1024


def _cparams(*sem):
    return pltpu.CompilerParams(dimension_semantics=sem, vmem_limit_bytes=VMEM_LIMIT)


def _inproj_kernel(x_ref, g_ref, w_ref, o_ref, *slab_refs):
    x = x_ref[...]
    y = x * lax.rsqrt(jnp.mean(x * x, axis=-1, keepdims=True) + RMS_EPS) * g_ref[...]
    res = jnp.dot(y.astype(BF16), w_ref[...], preferred_element_type=F32)
    o_ref[...] = res[:, :Z_NSA]
    for m, s_ref in enumerate(slab_refs):
        s_ref[...] = res[:, Z_NSA + m * 128:Z_NSA + (m + 1) * 128].reshape(s_ref.shape)


def _slab_spec(tm, nt):
    return pl.BlockSpec((tm // OCT, None, OCT, 128), lambda b, i: (i, b, 0, 0))


def _inproj(x2, g, w, b, seq, tm=1024):
    d = x2.shape[1]
    n = w.shape[1]
    nt = seq // tm
    slab = jax.ShapeDtypeStruct((seq // OCT, b, OCT, 128), F32)
    outs = pl.pallas_call(
        _inproj_kernel,
        grid=(b, nt),
        in_specs=[pl.BlockSpec((tm, d), lambda b_, i: (b_ * nt + i, 0)),
                  pl.BlockSpec((1, d), lambda b_, i: (0, 0)),
                  pl.BlockSpec((d, n), lambda b_, i: (0, 0))],
        out_specs=[pl.BlockSpec((tm, Z_NSA), lambda b_, i: (b_ * nt + i, 0))] + [_slab_spec(tm, nt)] * N_SLAB,
        out_shape=[jax.ShapeDtypeStruct((b * seq, Z_NSA), F32)] + [slab] * N_SLAB,
        compiler_params=_cparams("parallel", "parallel"),
        name="inproj",
    )(x2, g, w)
    return outs[0], [s.reshape(b * seq, 128) for s in outs[1:]]


def _lane_regroup(x):
    r = x.shape[0]
    cols = [x[:, v * 128:(v + 1) * 128] for v in range(32)]
    piece = lax.broadcasted_iota(jnp.int32, (r, 128), 1) // 16
    out = [None] * 32
    for ah in range(2):
        for bh in range(2):
            vs = [cols[2 * (8 * ah + i) + bh] for i in range(8)]
            for stage in (4, 2, 1):
                upper = (piece & stage) != 0
                for i in range(8):
                    if i & stage:
                        continue
                    lo, hi = vs[i], vs[i + stage]
                    vs[i] = jnp.where(upper, pltpu.roll(hi, stage * 16, 1), lo)
                    vs[i + stage] = jnp.where(upper, hi, pltpu.roll(lo, 128 - stage * 16, 1))
            for j in range(8):
                out[2 * (8 * bh + j) + ah] = vs[j]
    return jnp.concatenate(out, axis=1)


def _octet_row(t, nb):
    return (t // OCT) * (nb * OCT) + t % OCT


def _s5_kernel(u0_ref, u1_ref, we_ref, tz_ref, cp_ref, a_ref, d_ref, o0_ref, o1_ref, e_ref, carry_ref,
               *, nb, kc):
    @pl.when(pl.program_id(0) == 0)
    def _():
        carry_ref[...] = jnp.zeros_like(carry_ref)

    u_refs = (u0_ref, u1_ref)
    u = jnp.concatenate(
        [jnp.concatenate([u_refs[j][pl.ds(_octet_row(k * S5_CHUNK + s, nb), nb, stride=OCT), :]
                          for s in range(S5_CHUNK) for j in range(2)], axis=1)
         for k in range(kc)], axis=0)
    ug = _lane_regroup(u)
    ugb = ug.astype(BF16)
    npair = S5_GROUPS // 2
    for p in range(npair):
        e_ref[:, p * 256:(p + 1) * 256] = jnp.dot(ugb[:, p * 512:(p + 1) * 512], we_ref[p],
                                                   preferred_element_type=F32)

    a = a_ref[...]

    def step(k, carry):
        e = e_ref[k * nb:(k + 1) * nb, :]
        e_ref[k * nb:(k + 1) * nb, :] = carry
        new = []
        for p in range(npair):
            ar = a[:, p * 256:p * 256 + 128]
            ai = a[:, p * 256 + 128:(p + 1) * 256]
            cr = carry[:, p * 256:p * 256 + 128]
            ci = carry[:, p * 256 + 128:(p + 1) * 256]
            new.append(ar * cr - ai * ci + e[:, p * 256:p * 256 + 128])
            new.append(ar * ci + ai * cr + e[:, p * 256 + 128:(p + 1) * 256])
        return jnp.concatenate(new, axis=1)

    carry = carry_ref[...]
    for k in range(kc):
        carry = step(k, carry)
    carry_ref[...] = carry

    eb = e_ref[...].astype(BF16)
    ys = []
    for p in range(npair):
        yc = jnp.dot(eb[:, p * 256:(p + 1) * 256], cp_ref[p], preferred_element_type=F32)
        y0 = jnp.dot(ugb[:, (2 * p) * 256:(2 * p + 1) * 256], tz_ref[2 * p], preferred_element_type=F32)
        y1 = jnp.dot(ugb[:, (2 * p + 1) * 256:(2 * p + 2) * 256], tz_ref[2 * p + 1],
                     preferred_element_type=F32)
        ys.append(yc + jnp.concatenate([y0, y1], axis=1))
    y = jnp.concatenate(ys, axis=1) + d_ref[...] * ug
    y = jax.nn.gelu(y)
    y = _lane_regroup(y)
    o_refs = (o0_ref, o1_ref)
    for k in range(kc):
        for s in range(S5_CHUNK):
            for j in range(2):
                lo = s * SSM_WIDTH + j * 128
                o_refs[j][pl.ds(_octet_row(k * S5_CHUNK + s, nb), nb, stride=OCT), :] = \
                    y[k * nb:(k + 1) * nb, lo:lo + 128]


def _s5(u0, u1, we, tz, cp, a16, dg, nb, kc=16):
    rows = u0.shape[0]
    r = kc * S5_CHUNK * nb
    const3 = lambda i: (0, 0, 0)
    slab = pl.BlockSpec((r, 128), lambda i: (i, 0))
    return pl.pallas_call(
        functools.partial(_s5_kernel, nb=nb, kc=kc),
        grid=(rows // r,),
        in_specs=[slab, slab,
                  pl.BlockSpec(we.shape, const3),
                  pl.BlockSpec(tz.shape, const3),
                  pl.BlockSpec(cp.shape, const3),
                  pl.BlockSpec(a16.shape, lambda i: (0, 0)),
                  pl.BlockSpec(dg.shape, lambda i: (0, 0))],
        out_specs=[slab, slab],
        out_shape=[jax.ShapeDtypeStruct((rows, 128), F32)] * 2,
        scratch_shapes=[pltpu.VMEM((kc * nb, 2048), F32), pltpu.VMEM((nb, 2048), F32)],
        compiler_params=_cparams("arbitrary"),
        name="s5",
    )(u0, u1, we, tz, cp, a16, dg)


def _s5_tables(lam_re, lam_im, log_dt, b_re, b_im, c_re, c_im, d_skip):
    L = S5_CHUNK
    G, P, C = S5_GROUPS, S5_STATE, S5_GROUP
    dt = jnp.exp(log_dt.astype(F32))[:, None]
    lr = lam_re.astype(F32)
    li = lam_im.astype(F32)
    mag = jnp.exp(lr * dt)
    ang = li * dt
    ab_re = mag * jnp.cos(ang)
    ab_im = mag * jnp.sin(ang)
    den = lr * lr + li * li
    f_re = ((ab_re - 1.0) * lr + ab_im * li) / den
    f_im = (ab_im * lr - (ab_re - 1.0) * li) / den
    br = b_re.astype(F32)
    bi = b_im.astype(F32)
    bb_re = f_re[..., None] * br - f_im[..., None] * bi
    bb_im = f_re[..., None] * bi + f_im[..., None] * br
    cr = c_re.astype(F32)
    ci = c_im.astype(F32)
    tau = jnp.arange(L + 1, dtype=F32)[:, None, None]
    pr = jnp.exp(lr * dt * tau) * jnp.cos(li * dt * tau)
    pi = jnp.exp(lr * dt * tau) * jnp.sin(li * dt * tau)

    prs = pr[:L][::-1]
    pis = pi[:L][::-1]
    we_re = jnp.einsum('sgp,gpc->gscp', prs, bb_re) - jnp.einsum('sgp,gpc->gscp', pis, bb_im)
    we_im = jnp.einsum('sgp,gpc->gscp', prs, bb_im) + jnp.einsum('sgp,gpc->gscp', pis, bb_re)
    we_re = we_re.reshape(G, L * C, P)
    we_im = we_im.reshape(G, L * C, P)
    z = jnp.zeros_like(we_re[0::2])
    top = jnp.concatenate([we_re[0::2], z, we_im[0::2], z], axis=-1)
    bot = jnp.concatenate([z, we_re[1::2], z, we_im[1::2]], axis=-1)
    we = jnp.concatenate([top, bot], axis=1)

    m_re = pr[:L, :, :, None] * bb_re[None] - pi[:L, :, :, None] * bb_im[None]
    m_im = pr[:L, :, :, None] * bb_im[None] + pi[:L, :, :, None] * bb_re[None]
    kern = jnp.einsum('gop,tgpc->gcto', cr, m_re) - jnp.einsum('gop,tgpc->gcto', ci, m_im)
    padded = jnp.concatenate([jnp.zeros((G, C, L * C), F32), kern.reshape(G, C, L * C)], axis=-1)
    tz = jnp.stack([padded[:, :, (L - s) * C:(2 * L - s) * C] for s in range(L)], axis=1)
    tz = tz.reshape(G, L * C, L * C)

    pr1 = pr[1:]
    pi1 = pi[1:]
    cp_re = jnp.einsum('gop,tgp->gpto', cr, pr1) - jnp.einsum('gop,tgp->gpto', ci, pi1)
    cp_im = -(jnp.einsum('gop,tgp->gpto', cr, pi1) + jnp.einsum('gop,tgp->gpto', ci, pr1))
    cp_re = cp_re.reshape(G, P, L * C)
    cp_im = cp_im.reshape(G, P, L * C)
    zc = jnp.zeros_like(cp_re[0::2])
    cp = jnp.concatenate([
        jnp.concatenate([cp_re[0::2], zc], axis=-1),
        jnp.concatenate([zc, cp_re[1::2]], axis=-1),
        jnp.concatenate([cp_im[0::2], zc], axis=-1),
        jnp.concatenate([zc, cp_im[1::2]], axis=-1)], axis=1)

    a_re = pr[L].reshape(G // 2, 2 * P)
    a_im = pi[L].reshape(G // 2, 2 * P)
    a16 = jnp.concatenate([a_re, a_im], axis=-1).reshape(1, G * 2 * P)
    dg = jnp.broadcast_to(d_skip.astype(F32).reshape(G, 1, C), (G, L, C)).reshape(1, G * L * C)
    return we.astype(BF16), tz.astype(BF16), cp.astype(BF16), a16, dg


def _compress_kernel(ak_ref, av_ref, w1k_ref, w1v_ref, w2k_ref, w2v_ref, c0k_ref, c0v_ref,
                     kc_ref, vct_ref):
    def run(a_ref, w1_ref, w2_ref, c0_ref):
        nc = a_ref.shape[1] // CMP_STRIDE
        a = jnp.concatenate([a_ref[0, pl.ds(l, nc, stride=CMP_STRIDE), :] for l in range(CMP_STRIDE)],
                            axis=1).astype(BF16)
        p1 = jnp.dot(a, w1_ref[0], preferred_element_type=F32)
        p2 = jnp.dot(a, w1_ref[1], preferred_element_type=F32)
        n = p1.shape[0]
        pre = p1 + pltpu.roll(p2, n - 1, 0) + c0_ref[...]
        out = jnp.dot(jax.nn.gelu(pre).astype(BF16), w2_ref[...], preferred_element_type=F32)
        row = lax.broadcasted_iota(jnp.int32, out.shape, 0)
        return jnp.where(row < n - 1, out, 0.0)

    kc_ref[0] = run(ak_ref, w1k_ref, w2k_ref, c0k_ref).astype(BF16)
    vct_ref[0] = run(av_ref, w1v_ref, w2v_ref, c0v_ref).T.astype(BF16)


def _compress(z3, w1k, w1v, w2k, w2v, c0k, c0v):
    b, seq, _ = z3.shape
    nc = seq // CMP_STRIDE
    c3 = lambda i: (0, 0, 0)
    c2 = lambda i: (0, 0)
    return pl.pallas_call(
        _compress_kernel,
        grid=(b,),
        in_specs=[pl.BlockSpec((1, seq, 128), lambda i: (i, 0, Z_KV // 128)),
                  pl.BlockSpec((1, seq, 128), lambda i: (i, 0, Z_KV // 128 + 1)),
                  pl.BlockSpec(w1k.shape, c3), pl.BlockSpec(w1v.shape, c3),
                  pl.BlockSpec(w2k.shape, c2), pl.BlockSpec(w2v.shape, c2),
                  pl.BlockSpec(c0k.shape, c2), pl.BlockSpec(c0v.shape, c2)],
        out_specs=[pl.BlockSpec((1, nc, 128), lambda i: (i, 0, 0)),
                   pl.BlockSpec((1, 128, nc), lambda i: (i, 0, 0))],
        out_shape=[jax.ShapeDtypeStruct((b, nc, 128), BF16),
                   jax.ShapeDtypeStruct((b, 128, nc), BF16)],
        compiler_params=_cparams("parallel"),
        name="compress",
    )(z3, z3, w1k, w1v, w2k, w2v, c0k, c0v)


def _compress_weights(pe, w1, w2):
    w1 = w1.astype(F32)
    half = (CMP_LEN // 2) * HEAD_DIM
    eye = jnp.eye(NSA_KV_HEADS, dtype=F32)

    def bd(w):
        w = w.reshape(CMP_STRIDE, HEAD_DIM, HEAD_DIM)
        return jnp.einsum('ldo,hk->lhdko', w, eye).reshape(NSA_KV_HEADS * half, NSA_KV_HEADS * HEAD_DIM)

    w1s = jnp.stack([bd(w1[:half]), bd(w1[half:])])
    z2 = jnp.zeros((HEAD_DIM, HEAD_DIM), F32)
    w2f = w2.astype(F32)
    w2s = jnp.concatenate([jnp.concatenate([w2f, z2], axis=1),
                           jnp.concatenate([z2, w2f], axis=1)], axis=0)
    c0 = pe.astype(F32).reshape(1, CMP_LEN * HEAD_DIM) @ w1
    c0 = jnp.concatenate([c0, c0], axis=1)
    return w1s.astype(BF16), w2s.astype(BF16), c0


def _t5_bucket_np(dist):
    n = np.maximum(dist, 0)
    max_exact = REL_BUCKETS // 2
    nf = np.maximum(n, 1).astype(np.float32)
    large = max_exact + (np.log(nf / max_exact) / math.log(REL_MAX_DIST / max_exact)
                         * (REL_BUCKETS - max_exact)).astype(np.int32)
    large = np.minimum(large, REL_BUCKETS - 1)
    return np.where(n < max_exact, n, large)


TAB_FAR, TAB_WIN_EDGE, TAB_NONE = 3, 4, 5


def _bias_tables(rel_table, seq):
    tbl = rel_table.astype(F32)
    heads = NSA_Q_HEADS

    def lookup(bucket):
        onehot = jnp.asarray(bucket[..., None] == np.arange(REL_BUCKETS)).astype(F32)
        return jnp.einsum('...k,kh->...h', onehot, tbl, precision=lax.Precision.HIGHEST) * LOG2E

    r = np.arange(Q_BLOCK)[None, :]
    c = np.arange(SEL_LEN)[:, None]
    near = lookup(_t5_bucket_np(64 * np.arange(3)[:, None, None] + (r - c)[None]))
    far = jnp.broadcast_to(tbl[REL_BUCKETS - 1] * LOG2E, (SEL_LEN, Q_BLOCK, heads))
    none = jnp.full((SEL_LEN, Q_BLOCK, heads), NO_SLOT, F32)
    causal = jnp.asarray(r - c >= 0)[:, :, None]
    edge = jnp.asarray(c > r)[:, :, None]
    tab = jnp.stack([jnp.where(causal, near[0], none), near[1], near[2], far,
                     jnp.where(edge, far, none), none])
    tab = tab.transpose(0, 1, 3, 2).reshape(6, SEL_LEN, heads * Q_BLOCK)
    ncp = seq // CMP_STRIDE
    k = np.arange(2 * ncp + 4)[:, None]
    dist = r - CMP_STRIDE * (k - (ncp - 4)) - (CMP_LEN - 1)
    t2 = jnp.where(jnp.asarray(dist >= 0)[:, :, None], lookup(_t5_bucket_np(dist)), NO_SLOT)
    t2 = t2.transpose(0, 2, 1).reshape(2 * ncp + 4, heads * Q_BLOCK)
    t2 = jnp.stack([t2[:2 * ncp], t2[4:]])
    return tab, t2


def _nsa_kernel(q_ref, gl_ref, kc_ref, vct_ref, ks_ref, vs_ref, kw_ref, vw_ref, t2_ref, tab_ref,
                ov_ref, o_ref, s_ref, p_ref, acc_ref, accw_ref, vst_ref, vwt_ref, *, ncp, nw):
    @pl.when(pl.program_id(1) == 0)
    def _():
        for v_ref, vt_ref in ((vs_ref, vst_ref), (vw_ref, vwt_ref)):
            kt = vt_ref.shape[2]
            ones = jnp.ones((16, kt), F32)
            for u in range(vt_ref.shape[0]):
                vt = v_ref[0, u * kt:(u + 1) * kt, :].T
                vt_ref[u] = jnp.concatenate([vt[0:HEAD_DIM], ones, vt[HEAD_DIM:], ones], axis=0).astype(BF16)

    qi0 = pl.program_id(1) * nw
    cw = nw * NSA_GQA * Q_BLOCK
    ncol = NSA_KV_HEADS * cw
    vrows = HEAD_DIM + 16
    lane = lax.broadcasted_iota(jnp.int32, (Q_BLOCK, 128), 1)

    q = q_ref[...] * (HEAD_DIM ** -0.5 * LOG2E)
    pieces = []
    for h in range(NSA_KV_HEADS):
        keep = (lane >= 64) if h == 1 else (lane < 64)
        for w in range(nw):
            for g in range(NSA_GQA):
                pieces.append(jnp.where(keep, q[w * 64:(w + 1) * 64, g * 128:(g + 1) * 128], 0.0))
    qpad = jnp.concatenate(pieces, axis=0).astype(BF16)

    col = lax.broadcasted_iota(jnp.int32, (1, ncol), 1)
    rcol = col % Q_BLOCK
    qcol = qi0 + (col // (NSA_GQA * Q_BLOCK)) % nw
    tvec = Q_BLOCK * qcol + rcol
    nt_dims = (((1,), (1,)), ((), ()))

    def per_head_cols(tiles):
        return jnp.concatenate([tiles[w][:, h * 256:(h + 1) * 256]
                                for h in range(NSA_KV_HEADS) for w in range(nw)], axis=1)

    def slot_terms(m, n_masked):
        m_fin = jnp.where(n_masked > 0, jnp.maximum(m, NEG_INF * LOG2E), m)
        return jnp.exp2(m - m_fin), n_masked * jnp.exp2(NEG_INF * LOG2E - m_fin)

    t2_tiles = []
    for w in range(nw):
        start = (ncp - 4) - 4 * (qi0 + w)
        par = (start // 4) % 2
        a0 = pl.multiple_of(start - 4 * par, 8)
        t2_tiles.append(t2_ref[par, pl.ds(a0, ncp), :])
    s = lax.dot_general(kc_ref[0], qpad, nt_dims, preferred_element_type=F32) + per_head_cols(t2_tiles)
    m = jnp.max(s, axis=0, keepdims=True)
    e = jnp.exp2(s - m)
    n_valid = jnp.clip(jnp.right_shift(tvec - (CMP_STRIDE - 1), 4), 0, ncp - 1)
    scale, extra = slot_terms(m, (ncp - 1 - n_valid).astype(F32))
    p_c = e * (scale / (jnp.sum(e, axis=0, keepdims=True) * scale + extra))
    p_cb = p_c.astype(BF16)
    o_c = [jnp.dot(vct_ref[0, h * 64:(h + 1) * 64, :], p_cb[:, h * cw:(h + 1) * cw],
                   preferred_element_type=F32) for h in range(NSA_KV_HEADS)]

    lane_c = lax.broadcasted_iota(jnp.int32, (ncp, 128), 1)
    parts = []
    for w in range(nw):
        halves = []
        for h in range(NSA_KV_HEADS):
            base = h * cw + w * 256
            ph = p_c[:, base:base + 128] + p_c[:, base + 128:base + 256]
            halves.append(ph + pltpu.roll(ph, 64, 1))
        parts.append(jnp.where(lane_c < 64, halves[0], halves[1]))
    imp = jnp.dot(ov_ref[...], jnp.concatenate(parts, axis=1), preferred_element_type=F32,
                  precision=lax.Precision.HIGHEST)
    nsel = imp.shape[0]
    jidx = lax.broadcasted_iota(jnp.int32, (nsel, 1), 0)
    qv = qi0 + lax.broadcasted_iota(jnp.int32, (1, nw * 128), 1) // 128
    forced = jnp.logical_or(jidx == 0, jnp.logical_or(jidx == qv, jidx == qv - 1))
    avail = jidx <= qv
    cand = jnp.logical_and(avail, jnp.logical_not(forced))
    budget = SEL_TOPK - (1 + jnp.where(qv >= 1, 1, 0) + jnp.where(qv >= 2, 1, 0))
    rank = jnp.zeros((nsel, nw * 128), jnp.int32)
    for jp in range(1, nsel - 2):
        row = imp[jp:jp + 1, :]
        ge = jnp.where(row >= imp, 1, 0)
        gt = jnp.where(row > imp, 1, 0)
        is_cand = jnp.where(qv - 2 >= jp, 1, 0)
        rank = rank + jnp.where(jidx > jp, ge, gt) * is_cand
    sel = jnp.logical_or(jnp.logical_and(forced, avail), jnp.logical_and(cand, rank < budget))
    seladd = jnp.where(sel, 0.0, NO_SLOT)
    pad = jnp.zeros((128 - nsel, 128), F32)
    mask_rows = []
    for w in range(nw):
        mask_rows.append(jnp.concatenate([seladd[:, w * 128:(w + 1) * 128], pad], axis=0).T)
    sel_cols = jnp.concatenate([mask_rows[w][h * 64:(h + 1) * 64, :] for h in range(NSA_KV_HEADS)
                                for w in range(nw) for _ in range(NSA_GQA)], axis=0)
    qsel = jnp.concatenate([qpad, sel_cols.astype(BF16)], axis=1)

    def block_adds(blk, selected):
        tiles = []
        for w in range(nw):
            d = qi0 + w - blk
            if selected:
                e = jnp.where(d < 0, TAB_FAR, jnp.minimum(d, TAB_FAR))
            else:
                outside = jnp.logical_or(blk < 0, jnp.logical_or(d < 0, d > TAB_NONE))
                e = jnp.where(outside, TAB_NONE, d)
            tiles.append(tab_ref[e])
        return per_head_cols(tiles)

    def softmax_tile(m_i, s, adds):
        sm = jnp.concatenate([s[k * 64:(k + 1) * 64] + adds[k] for k in range(len(adds))], axis=0)
        m_new = jnp.maximum(m_i, jnp.max(sm, axis=0, keepdims=True))
        return m_new, jnp.exp2(m_i - m_new), jnp.exp2(sm - m_new).astype(BF16)

    def add_values(ref, alpha, vt_tile, p):
        for h in range(NSA_KV_HEADS):
            ref[h] = alpha[:, h * cw:(h + 1) * cw] * ref[h] + jnp.dot(
                vt_tile[h * vrows:(h + 1) * vrows, :], p[:, h * cw:(h + 1) * cw], preferred_element_type=F32)

    m_init = jnp.full((1, ncol), NO_SLOT, F32)

    accw_ref[...] = jnp.zeros_like(accw_ref)
    m_w = m_init
    for i in range(nw // 4 + 1):
        tj = qi0 // 4 - 1 + i
        tc = jnp.maximum(tj, 0)
        r0 = pl.multiple_of(tc * 256, 256)
        s = lax.dot_general(kw_ref[0, pl.ds(r0, 256), :].astype(BF16), qpad, nt_dims,
                            preferred_element_type=F32)
        m_w, alpha, p = softmax_tile(m_w, s, [block_adds(4 * tj + k, False) for k in range(4)])
        add_values(accw_ref, alpha, vwt_ref[tc], p)

    n_tiles = (qi0 + nw - 1) // 4 + 1

    def masked_scores(u):
        r0 = pl.multiple_of(u * 256, 256)
        block_of_row = 4 * u + lax.broadcasted_iota(jnp.int32, (256, 128), 0) // SEL_LEN
        onehot = jnp.where(lax.broadcasted_iota(jnp.int32, (256, 128), 1) == block_of_row, 1.0, 0.0)
        keys = jnp.concatenate([ks_ref[0, pl.ds(r0, 256), :].astype(BF16), onehot.astype(BF16)], axis=1)
        s = lax.dot_general(keys, qsel, nt_dims, preferred_element_type=F32)
        tmax = None
        for k in range(4):
            sk = s[k * 64:(k + 1) * 64] + block_adds(4 * u + k, True)
            s_ref[k * 64:(k + 1) * 64, :] = sk
            kmax = jnp.max(sk, axis=0, keepdims=True)
            tmax = kmax if tmax is None else jnp.maximum(tmax, kmax)
        return tmax

    tmax0 = masked_scores(0)
    p_ref[...] = jnp.zeros_like(p_ref)
    acc_ref[...] = jnp.zeros_like(acc_ref)

    def sel_body(u, carry):
        m_i, alpha_prev, tmax = carry
        add_values(acc_ref, alpha_prev, vst_ref[jnp.maximum(u - 1, 0)], p_ref[...])
        m_new = jnp.maximum(m_i, tmax)
        p_ref[...] = jnp.exp2(s_ref[...] - m_new).astype(BF16)
        tmax_next = masked_scores(jnp.minimum(u + 1, n_tiles - 1))
        return m_new, jnp.exp2(m_i - m_new), tmax_next

    m_s, alpha_last, _ = lax.fori_loop(0, n_tiles, sel_body, (m_init, jnp.ones((1, ncol), F32), tmax0))
    add_values(acc_ref, alpha_last, vst_ref[n_tiles - 1], p_ref[...])

    nm_s = (SEL_LEN * jnp.maximum(SEL_TOPK - (qcol + 1), 0) + (SEL_LEN - 1 - rcol)).astype(F32)
    nm_w = (WINDOW + Q_BLOCK - jnp.minimum(tvec + 1, WINDOW)).astype(F32)

    def normalised(m, ref, n_masked):
        scale, extra = slot_terms(m, n_masked)
        out = []
        for h in range(NSA_KV_HEADS):
            sl = slice(h * cw, (h + 1) * cw)
            acc = ref[h]
            denom = acc[HEAD_DIM:HEAD_DIM + 1, :] * scale[:, sl] + extra[:, sl]
            out.append((scale[:, sl] / denom, acc[0:HEAD_DIM, :]))
        return out

    o_s = normalised(m_s, acc_ref, nm_s)
    o_w = normalised(m_w, accw_ref, nm_w)

    lane1 = lax.broadcasted_iota(jnp.int32, (1, 128), 1)
    gts = []
    for w in range(nw):
        g = jax.nn.sigmoid(gl_ref[w * 64:(w + 1) * 64, :])
        gts.append(jnp.concatenate([g, g], axis=0).T)

    def gate_vec(h, br):
        tiles = []
        for w in range(nw):
            for gg in range(2):
                c0 = (h * 4 + 2 * gg) * 3 + br
                c1 = (h * 4 + 2 * gg + 1) * 3 + br
                tiles.append(jnp.where(lane1 < 64, gts[w][c0:c0 + 1, :], gts[w][c1:c1 + 1, :]))
        return jnp.concatenate(tiles, axis=1)

    tot = []
    for h in range(NSA_KV_HEADS):
        t = gate_vec(h, 0) * o_c[h]
        t = t + (gate_vec(h, 1) * o_s[h][0]) * o_s[h][1]
        t = t + (gate_vec(h, 2) * o_w[h][0]) * o_w[h][1]
        tot.append(t)
    for w in range(nw):
        ot = jnp.concatenate([t[:, w * 256:(w + 1) * 256] for t in tot], axis=0).T
        o_ref[w * 64:(w + 1) * 64, :] = jnp.concatenate([ot[g * 64:(g + 1) * 64, :]
                                                         for g in range(NSA_GQA)], axis=1)


def _nsa(z, kc, vct, t2, tab, ov, b, seq, nw=4):
    nst = seq // (Q_BLOCK * nw)
    ncp = seq // CMP_STRIDE
    rows = Q_BLOCK * nw
    cw = nw * NSA_GQA * Q_BLOCK
    z3 = z.reshape(b, seq, Z_NSA)
    kv_col = lambda n: pl.BlockSpec((1, seq, 128), lambda i, j: (i, 0, Z_KV // 128 + n))
    return pl.pallas_call(
        functools.partial(_nsa_kernel, ncp=ncp, nw=nw),
        grid=(b, nst),
        in_specs=[pl.BlockSpec((rows, NSA_WIDTH), lambda i, j: (i * nst + j, Z_Q // NSA_WIDTH)),
                  pl.BlockSpec((rows, 128), lambda i, j: (i * nst + j, Z_G // 128)),
                  pl.BlockSpec((1, ncp, 128), lambda i, j: (i, 0, 0)),
                  pl.BlockSpec((1, 128, ncp), lambda i, j: (i, 0, 0)),
                  kv_col(2), kv_col(3), kv_col(4), kv_col(5),
                  pl.BlockSpec(t2.shape, lambda i, j: (0, 0, 0)),
                  pl.BlockSpec(tab.shape, lambda i, j: (0, 0, 0)),
                  pl.BlockSpec(ov.shape, lambda i, j: (0, 0))],
        out_specs=pl.BlockSpec((rows, NSA_WIDTH), lambda i, j: (i * nst + j, 0)),
        out_shape=jax.ShapeDtypeStruct((b * seq, NSA_WIDTH), F32),
        scratch_shapes=[pltpu.VMEM((256, NSA_KV_HEADS * cw), F32),
                        pltpu.VMEM((256, NSA_KV_HEADS * cw), BF16),
                        pltpu.VMEM((NSA_KV_HEADS, HEAD_DIM + 16, cw), F32),
                        pltpu.VMEM((NSA_KV_HEADS, HEAD_DIM + 16, cw), F32),
                        pltpu.VMEM((seq // 256, NSA_KV_HEADS * (HEAD_DIM + 16), 256), BF16),
                        pltpu.VMEM((seq // 256, NSA_KV_HEADS * (HEAD_DIM + 16), 256), BF16)],
        compiler_params=_cparams("parallel", "arbitrary"),
        name="nsa",
    )(z, z, kc, vct, z3, z3, z3, z3, t2, tab, ov)


def _lru_kernel(x0_ref, x1_ref, g0_ref, g1_ref, cw_ref, cb_ref, wa_ref, ba_ref, wx_ref, bx_ref, sp_ref,
                o0_ref, o1_ref, xprev_ref, h_ref, a_scr, b_scr, *, nb, tc):
    @pl.when(pl.program_id(0) == 0)
    def _():
        xprev_ref[...] = jnp.zeros_like(xprev_ref)
        h_ref[...] = jnp.zeros_like(h_ref)

    x = jnp.concatenate([x0_ref[...], x1_ref[...]], axis=1)
    rows, width = x.shape
    grp = nb * OCT
    xm = jnp.concatenate([xprev_ref[...], x[:rows - grp]], axis=0)
    xprev_ref[...] = x[rows - grp:]
    x3 = x.reshape(rows // OCT, OCT, width)
    xm3 = xm.reshape(rows // OCT, OCT, width)
    tlo = lax.broadcasted_iota(jnp.int32, (1, OCT, 1), 1)
    cw = cw_ref[...]
    xc = cb_ref[...].reshape(1, 1, width) + cw[CONV_WIDTH - 1:CONV_WIDTH, :].reshape(1, 1, width) * x3
    for k in range(1, CONV_WIDTH):
        delayed = jnp.where(tlo >= k, pltpu.roll(x3, k, 1), pltpu.roll(xm3, k, 1))
        xc = xc + cw[CONV_WIDTH - 1 - k:CONV_WIDTH - k, :].reshape(1, 1, width) * delayed
    xc = xc.reshape(rows, width)
    xcb = xc.astype(BF16)
    def sigmoid(v):
        return 0.5 * jnp.tanh(0.5 * v) + 0.5

    gate_r = sigmoid(jnp.dot(xcb, wa_ref[...], preferred_element_type=F32) + ba_ref[...])
    gate_i = sigmoid(jnp.dot(xcb, wx_ref[...], preferred_element_type=F32) + bx_ref[...])
    log_a = -LRU_C * gate_r * sp_ref[...]
    a = jnp.exp(log_a)
    th = jnp.tanh(log_a)
    bt = jnp.sqrt(-2.0 * th / (1.0 - th)) * gate_i * xc
    for j in range(2):
        a_scr[j] = a[:, j * 128:(j + 1) * 128]
        b_scr[j] = bt[:, j * 128:(j + 1) * 128]

    def octet(o, h):
        base = pl.multiple_of(o * (nb * OCT), nb * OCT)
        for ti in range(OCT):
            new = []
            for j in range(2):
                rows = pl.ds(base + ti, nb, stride=OCT)
                hj = a_scr[j, rows, :] * h[j] + b_scr[j, rows, :]
                b_scr[j, rows, :] = hj
                new.append(hj)
            h = tuple(new)
        return h

    h = lax.fori_loop(0, tc // OCT, octet, (h_ref[0], h_ref[1]))
    for j, (g_ref, o_ref) in enumerate(((g0_ref, o0_ref), (g1_ref, o1_ref))):
        h_ref[j] = h[j]
        o_ref[...] = b_scr[j] * jax.nn.gelu(g_ref[...])


def _lru(x0, x1, g0, g1, cw, cb, wa, ba, wx, bx, sp, nb, tc=64):
    rows = x0.shape[0]
    r = tc * nb
    c2 = lambda i: (0, 0)
    slab = pl.BlockSpec((r, 128), lambda i: (i, 0))
    return pl.pallas_call(
        functools.partial(_lru_kernel, nb=nb, tc=tc),
        grid=(rows // r,),
        in_specs=[slab, slab, slab, slab,
                  pl.BlockSpec(cw.shape, c2), pl.BlockSpec(cb.shape, c2),
                  pl.BlockSpec(wa.shape, c2), pl.BlockSpec(ba.shape, c2),
                  pl.BlockSpec(wx.shape, c2), pl.BlockSpec(bx.shape, c2),
                  pl.BlockSpec(sp.shape, c2)],
        out_specs=[slab, slab],
        out_shape=[jax.ShapeDtypeStruct((rows, 128), F32)] * 2,
        scratch_shapes=[pltpu.VMEM((nb * OCT, LRU_WIDTH), F32), pltpu.VMEM((2, nb, 128), F32),
                        pltpu.VMEM((2, r, 128), F32), pltpu.VMEM((2, r, 128), F32)],
        compiler_params=_cparams("arbitrary"),
        name="lru",
    )(x0, x1, g0, g1, cw, cb, wa, ba, wx, bx, sp)


def _block_diag(w):
    h, i, j = w.shape
    eye = jnp.eye(h, dtype=w.dtype)
    return jnp.einsum('hij,hk->hikj', w, eye).reshape(h * i, h * j)


def _mix_ffn_kernel(h_ref, ys0_ref, ys1_ref, yn_ref, yl0_ref, yl1_ref, wglu_ref, wos_ref, won_ref, wol_ref,
                    g_ref, wg_ref, wu_ref, wd_ref, gf_ref, o_ref, *, final):
    tm = h_ref.shape[0]

    def rows(a_ref, b_ref):
        return jnp.concatenate([a_ref[...].reshape(tm, 128), b_ref[...].reshape(tm, 128)], axis=1)

    gl = jnp.dot(rows(ys0_ref, ys1_ref).astype(BF16), wglu_ref[...], preferred_element_type=F32)
    s5 = gl[:, :SSM_WIDTH] * jax.nn.sigmoid(gl[:, SSM_WIDTH:])
    acc = jnp.dot(s5.astype(BF16), wos_ref[...], preferred_element_type=F32)
    acc = acc + jnp.dot(yn_ref[...].astype(BF16), won_ref[...], preferred_element_type=F32)
    acc = acc + jnp.dot(rows(yl0_ref, yl1_ref).astype(BF16), wol_ref[...], preferred_element_type=F32)
    x = h_ref[...] + acc

    y = (x * lax.rsqrt(jnp.mean(x * x, axis=-1, keepdims=True) + RMS_EPS) * g_ref[...]).astype(BF16)
    a = jnp.dot(y, wg_ref[...], preferred_element_type=F32)
    b = jnp.dot(y, wu_ref[...], preferred_element_type=F32)
    m = (jax.nn.silu(a) * b).astype(BF16)
    out = x + jnp.dot(m, wd_ref[...], preferred_element_type=F32)
    if final:
        out = out * lax.rsqrt(jnp.mean(out * out, axis=-1, keepdims=True) + RMS_EPS) * gf_ref[...]
    o_ref[...] = out


def _mix_ffn(h, ys, yn, yl, wglu, wos, won, wol, g, wg, wu, wd, gf, final, b, seq, tm=512):
    d = h.shape[1]
    nt = seq // tm
    row = lambda w: pl.BlockSpec((tm, w), lambda b_, i: (b_ * nt + i, 0))
    const = lambda a: pl.BlockSpec(a.shape, lambda b_, i: (0, 0), pipeline_mode=pl.Buffered(1))
    slab4 = lambda s: s.reshape(seq // OCT, b, OCT, 128)
    return pl.pallas_call(
        functools.partial(_mix_ffn_kernel, final=final),
        grid=(b, nt),
        in_specs=[row(d), _slab_spec(tm, nt), _slab_spec(tm, nt), row(NSA_WIDTH),
                  _slab_spec(tm, nt), _slab_spec(tm, nt),
                  const(wglu), const(wos), const(won), const(wol),
                  const(g), const(wg), const(wu), const(wd), const(gf)],
        out_specs=row(d),
        out_shape=jax.ShapeDtypeStruct((b * seq, d), F32),
        compiler_params=_cparams("parallel", "parallel"),
        name="mix_ffn",
    )(h, slab4(ys[0]), slab4(ys[1]), yn, slab4(yl[0]), slab4(yl[1]), wglu, wos, won, wol, g, wg, wu, wd, gf)


def _head_interleave(w, axis):
    shape = w.shape
    split = shape[:axis] + (NSA_KV_HEADS, NSA_GQA, HEAD_DIM) + shape[axis + 1:]
    return jnp.swapaxes(w.reshape(split), axis, axis + 1).reshape(shape)


def _prep_w_in(w_in):
    o1 = SSM_WIDTH
    o2 = o1 + NSA_WIDTH
    o3 = o2 + 6 * NSA_KV_WIDTH
    o4 = o3 + 3 * NSA_Q_HEADS
    wq = _head_interleave(w_in[:, o1:o2], 1)
    pad = jnp.zeros((w_in.shape[0], 128 - 3 * NSA_Q_HEADS), w_in.dtype)
    w = jnp.concatenate([wq, w_in[:, o2:o3], w_in[:, o3:o4], pad, w_in[:, :o1], w_in[:, o4:]], axis=1)
    return w.astype(BF16)


def _s5_mixer(slabs, b, p):
    we, tz, cp, a16, dg = _s5_tables(p['s5_lam_re'], p['s5_lam_im'], p['s5_log_dt'], p['s5_b_re'],
                                     p['s5_b_im'], p['s5_c_re'], p['s5_c_im'], p['s5_d'])
    return _s5(slabs[0], slabs[1], we, tz, cp, a16, dg, nb=b)


def _nsa_mixer(z, b, seq, bias_tabs, ov, p):
    w1k, w2k, c0k = _compress_weights(p['nsa_pe_k'], p['nsa_w1_k'], p['nsa_w2_k'])
    w1v, w2v, c0v = _compress_weights(p['nsa_pe_v'], p['nsa_w1_v'], p['nsa_w2_v'])
    kc, vct = _compress(z.reshape(b, seq, Z_NSA), w1k, w1v, w2k, w2v, c0k, c0v)
    tab, t2 = bias_tabs
    return _nsa(z, kc, vct, t2, tab, ov, b, seq)


def _lru_mixer(slabs, b, p):
    sp = jax.nn.softplus(-p['lru_lam'].astype(F32)).reshape(1, -1)
    return _lru(slabs[0], slabs[1], slabs[2], slabs[3],
                p['lru_conv_w'].astype(F32), p['lru_conv_b'].astype(F32).reshape(1, -1),
                _block_diag(p['lru_w_a']).astype(BF16), p['lru_b_a'].astype(F32).reshape(1, -1),
                _block_diag(p['lru_w_x']).astype(BF16), p['lru_b_x'].astype(F32).reshape(1, -1),
                sp, nb=b)


def _layer(h2, b, seq, bias_tabs, ov, p, final, norm_final):
    z, slabs = _inproj(h2, p['norm_mix'].reshape(1, -1), _prep_w_in(p['w_in']), b, seq)
    ys = _s5_mixer(slabs[0:2], b, p)
    yn = _nsa_mixer(z, b, seq, bias_tabs, ov, p)
    yl = _lru_mixer(slabs[2:6], b, p)
    w_out = p['w_out']
    wos = w_out[:SSM_WIDTH].astype(BF16)
    won = _head_interleave(w_out[SSM_WIDTH:SSM_WIDTH + NSA_WIDTH], 0).astype(BF16)
    wol = w_out[SSM_WIDTH + NSA_WIDTH:].astype(BF16)
    return _mix_ffn(h2, ys, yn, yl, p['s5_w_glu'].astype(BF16), wos, won, wol,
                    p['norm_ffn'].reshape(1, -1), p['w_gate'].astype(BF16), p['w_up'].astype(BF16),
                    p['w_down'].astype(BF16), norm_final.reshape(1, -1), final, b, seq)


def _overlap_t(seq):
    ncp = seq // CMP_STRIDE
    nsel = seq // SEL_LEN
    cs = np.arange(ncp) * CMP_STRIDE
    ss = np.arange(nsel) * SEL_LEN
    ovl = (cs[None, :] < ss[:, None] + SEL_LEN) & (ss[:, None] < cs[None, :] + CMP_LEN)
    ovl[:, ncp - 1] = False
    return jnp.asarray(ovl.astype(np.float32))


def kernel(x, rel_bias_table, norm_mix, w_in, w_out, s5_lam_re, s5_lam_im, s5_log_dt, s5_b_re, s5_b_im, s5_c_re, s5_c_im, s5_d, s5_w_glu, nsa_pe_k, nsa_w1_k, nsa_w2_k, nsa_pe_v, nsa_w1_v, nsa_w2_v, lru_conv_w, lru_conv_b, lru_w_a, lru_b_a, lru_w_x, lru_b_x, lru_lam, norm_ffn, w_gate, w_up, w_down, norm_final):
    b, seq, d = x.shape
    depth = norm_mix.shape[0]
    per_layer = dict(norm_mix=norm_mix, w_in=w_in, w_out=w_out, s5_lam_re=s5_lam_re, s5_lam_im=s5_lam_im,
                     s5_log_dt=s5_log_dt, s5_b_re=s5_b_re, s5_b_im=s5_b_im, s5_c_re=s5_c_re,
                     s5_c_im=s5_c_im, s5_d=s5_d, s5_w_glu=s5_w_glu, nsa_pe_k=nsa_pe_k, nsa_w1_k=nsa_w1_k,
                     nsa_w2_k=nsa_w2_k, nsa_pe_v=nsa_pe_v, nsa_w1_v=nsa_w1_v, nsa_w2_v=nsa_w2_v,
                     lru_conv_w=lru_conv_w, lru_conv_b=lru_conv_b, lru_w_a=lru_w_a, lru_b_a=lru_b_a,
                     lru_w_x=lru_w_x, lru_b_x=lru_b_x, lru_lam=lru_lam, norm_ffn=norm_ffn, w_gate=w_gate,
                     w_up=w_up, w_down=w_down)
    bias_tabs = _bias_tables(rel_bias_table, seq)
    ov = _overlap_t(seq)
    h2 = x.reshape(b * seq, d)
    for l in range(depth):
        p = {k: v[l] for k, v in per_layer.items()}
        h2 = _layer(h2, b, seq, bias_tabs, ov, p, l == depth - 1, norm_final)
    return h2.reshape(b, seq, d)
```

```python
import functools
import math

import numpy as np
import jax
import jax.numpy as jnp
from jax import lax
from jax.experimental import pallas as pl
from jax.experimental.pallas import tpu as pltpu

F32 = jnp.float32
BF16 = jnp.bfloat16

D_MODEL = 1024
SSM_WIDTH = 256
NSA_WIDTH = 512
LRU_WIDTH = 256
S5_GROUP = 16
S5_GROUPS = 16
S5_STATE = 64
HEAD_DIM = 64
NSA_Q_HEADS = 8
NSA_KV_HEADS = 2
NSA_GQA = 4
NSA_KV_WIDTH = 128
CMP_LEN = 32
CMP_STRIDE = 16
SEL_LEN = 64
SEL_TOPK = 8
WINDOW = 256
Q_BLOCK = 64
LRU_HEADS = 4
LRU_HEAD_DIM = 64
CONV_WIDTH = 4
LRU_C = 8.0
REL_BUCKETS = 32
REL_MAX_DIST = 128
D_FF = 2816
NEG_INF = -1e9
LOG2E = 1.4426950408889634
RMS_EPS = 1e-6

NO_SLOT = -3e38

S5_CHUNK = 16
Z_Q, Z_KV, Z_G, Z_NSA = 0, 0, 768, 896
N_SLAB = 6
OCT = 8

VMEM_LIMIT = 56 * 1024 * 1024


def _cparams(*sem):
    return pltpu.CompilerParams(dimension_semantics=sem, vmem_limit_bytes=VMEM_LIMIT)


def _inproj_kernel(x_ref, g_ref, w_ref, q_ref, o_ref, *slab_refs):
    x = x_ref[...]
    y = x * lax.rsqrt(jnp.mean(x * x, axis=-1, keepdims=True) + RMS_EPS) * g_ref[...]
    res = jnp.dot(y.astype(BF16), w_ref[...], preferred_element_type=F32)
    q_ref[...] = res[:, :NSA_WIDTH].astype(BF16)
    o_ref[...] = res[:, NSA_WIDTH:NSA_WIDTH + Z_NSA]
    base = NSA_WIDTH + Z_NSA
    for m, s_ref in enumerate(slab_refs):
        s_ref[...] = res[:, base + m * 128:base + (m + 1) * 128].reshape(s_ref.shape)


def _slab_spec(tm, nt):
    return pl.BlockSpec((tm // OCT, None, OCT, 128), lambda b, i: (i, b, 0, 0))


def _inproj(x2, g, w, b, seq, tm=1024):
    d = x2.shape[1]
    n = w.shape[1]
    nt = seq // tm
    slab = jax.ShapeDtypeStruct((seq // OCT, b, OCT, 128), F32)
    outs = pl.pallas_call(
        _inproj_kernel,
        grid=(b, nt),
        in_specs=[pl.BlockSpec((tm, d), lambda b_, i: (b_ * nt + i, 0)),
                  pl.BlockSpec((1, d), lambda b_, i: (0, 0)),
                  pl.BlockSpec((d, n), lambda b_, i: (0, 0))],
        out_specs=[pl.BlockSpec((tm, NSA_WIDTH), lambda b_, i: (b_ * nt + i, 0)),
                   pl.BlockSpec((tm, Z_NSA), lambda b_, i: (b_ * nt + i, 0))] + [_slab_spec(tm, nt)] * N_SLAB,
        out_shape=[jax.ShapeDtypeStruct((b * seq, NSA_WIDTH), BF16),
                   jax.ShapeDtypeStruct((b * seq, Z_NSA), F32)] + [slab] * N_SLAB,
        compiler_params=_cparams("parallel", "parallel"),
        name="inproj",
    )(x2, g, w)
    return outs[0], outs[1], [s.reshape(b * seq, 128) for s in outs[2:]]


def _lane_regroup(x):
    r = x.shape[0]
    cols = [x[:, v * 128:(v + 1) * 128] for v in range(32)]
    piece = lax.broadcasted_iota(jnp.int32, (r, 128), 1) // 16
    out = [None] * 32
    for ah in range(2):
        for bh in range(2):
            vs = [cols[2 * (8 * ah + i) + bh] for i in range(8)]
            for stage in (4, 2, 1):
                upper = (piece & stage) != 0
                for i in range(8):
                    if i & stage:
                        continue
                    lo, hi = vs[i], vs[i + stage]
                    vs[i] = jnp.where(upper, pltpu.roll(hi, stage * 16, 1), lo)
                    vs[i + stage] = jnp.where(upper, hi, pltpu.roll(lo, 128 - stage * 16, 1))
            for j in range(8):
                out[2 * (8 * bh + j) + ah] = vs[j]
    return jnp.concatenate(out, axis=1)


def _octet_row(t, nb):
    return (t // OCT) * (nb * OCT) + t % OCT


def _s5_kernel(u0_ref, u1_ref, we_ref, tz_ref, cp_ref, a_ref, d_ref, o0_ref, o1_ref, e_ref, carry_ref,
               *, nb, kc):
    @pl.when(pl.program_id(0) == 0)
    def _():
        carry_ref[...] = jnp.zeros_like(carry_ref)

    u_refs = (u0_ref, u1_ref)
    u = jnp.concatenate(
        [jnp.concatenate([u_refs[j][pl.ds(_octet_row(k * S5_CHUNK + s, nb), nb, stride=OCT), :]
                          for s in range(S5_CHUNK) for j in range(2)], axis=1)
         for k in range(kc)], axis=0)
    ug = _lane_regroup(u)
    ugb = ug.astype(BF16)
    npair = S5_GROUPS // 2
    for p in range(npair):
        e_ref[:, p * 256:(p + 1) * 256] = jnp.dot(ugb[:, p * 512:(p + 1) * 512], we_ref[p],
                                                   preferred_element_type=F32)

    a = a_ref[...]

    def step(k, carry):
        e = e_ref[k * nb:(k + 1) * nb, :]
        e_ref[k * nb:(k + 1) * nb, :] = carry
        new = []
        for p in range(npair):
            ar = a[:, p * 256:p * 256 + 128]
            ai = a[:, p * 256 + 128:(p + 1) * 256]
            cr = carry[:, p * 256:p * 256 + 128]
            ci = carry[:, p * 256 + 128:(p + 1) * 256]
            new.append(ar * cr - ai * ci + e[:, p * 256:p * 256 + 128])
            new.append(ar * ci + ai * cr + e[:, p * 256 + 128:(p + 1) * 256])
        return jnp.concatenate(new, axis=1)

    carry = carry_ref[...]
    for k in range(kc):
        carry = step(k, carry)
    carry_ref[...] = carry

    eb = e_ref[...].astype(BF16)
    ys = []
    for p in range(npair):
        yc = jnp.dot(eb[:, p * 256:(p + 1) * 256], cp_ref[p], preferred_element_type=F32)
        y0 = jnp.dot(ugb[:, (2 * p) * 256:(2 * p + 1) * 256], tz_ref[2 * p], preferred_element_type=F32)
        y1 = jnp.dot(ugb[:, (2 * p + 1) * 256:(2 * p + 2) * 256], tz_ref[2 * p + 1],
                     preferred_element_type=F32)
        ys.append(yc + jnp.concatenate([y0, y1], axis=1))
    y = jnp.concatenate(ys, axis=1) + d_ref[...] * ug
    y = jax.nn.gelu(y)
    y = _lane_regroup(y)
    o_refs = (o0_ref, o1_ref)
    for k in range(kc):
        for s in range(S5_CHUNK):
            for j in range(2):
                lo = s * SSM_WIDTH + j * 128
                o_refs[j][pl.ds(_octet_row(k * S5_CHUNK + s, nb), nb, stride=OCT), :] = \
                    y[k * nb:(k + 1) * nb, lo:lo + 128]


def _s5(u0, u1, we, tz, cp, a16, dg, nb, kc=16):
    rows = u0.shape[0]
    r = kc * S5_CHUNK * nb
    const3 = lambda i: (0, 0, 0)
    slab = pl.BlockSpec((r, 128), lambda i: (i, 0))
    return pl.pallas_call(
        functools.partial(_s5_kernel, nb=nb, kc=kc),
        grid=(rows // r,),
        in_specs=[slab, slab,
                  pl.BlockSpec(we.shape, const3),
                  pl.BlockSpec(tz.shape, const3),
                  pl.BlockSpec(cp.shape, const3),
                  pl.BlockSpec(a16.shape, lambda i: (0, 0)),
                  pl.BlockSpec(dg.shape, lambda i: (0, 0))],
        out_specs=[slab, slab],
        out_shape=[jax.ShapeDtypeStruct((rows, 128), F32)] * 2,
        scratch_shapes=[pltpu.VMEM((kc * nb, 2048), F32), pltpu.VMEM((nb, 2048), F32)],
        compiler_params=_cparams("arbitrary"),
        name="s5",
    )(u0, u1, we, tz, cp, a16, dg)


def _s5_tables(lam_re, lam_im, log_dt, b_re, b_im, c_re, c_im, d_skip):
    L = S5_CHUNK
    G, P, C = S5_GROUPS, S5_STATE, S5_GROUP
    dt = jnp.exp(log_dt.astype(F32))[:, None]
    lr = lam_re.astype(F32)
    li = lam_im.astype(F32)
    mag = jnp.exp(lr * dt)
    ang = li * dt
    ab_re = mag * jnp.cos(ang)
    ab_im = mag * jnp.sin(ang)
    den = lr * lr + li * li
    f_re = ((ab_re - 1.0) * lr + ab_im * li) / den
    f_im = (ab_im * lr - (ab_re - 1.0) * li) / den
    br = b_re.astype(F32)
    bi = b_im.astype(F32)
    bb_re = f_re[..., None] * br - f_im[..., None] * bi
    bb_im = f_re[..., None] * bi + f_im[..., None] * br
    cr = c_re.astype(F32)
    ci = c_im.astype(F32)
    tau = jnp.arange(L + 1, dtype=F32)[:, None, None]
    pr = jnp.exp(lr * dt * tau) * jnp.cos(li * dt * tau)
    pi = jnp.exp(lr * dt * tau) * jnp.sin(li * dt * tau)

    prs = pr[:L][::-1]
    pis = pi[:L][::-1]
    we_re = jnp.einsum('sgp,gpc->gscp', prs, bb_re) - jnp.einsum('sgp,gpc->gscp', pis, bb_im)
    we_im = jnp.einsum('sgp,gpc->gscp', prs, bb_im) + jnp.einsum('sgp,gpc->gscp', pis, bb_re)
    we_re = we_re.reshape(G, L * C, P)
    we_im = we_im.reshape(G, L * C, P)
    z = jnp.zeros_like(we_re[0::2])
    top = jnp.concatenate([we_re[0::2], z, we_im[0::2], z], axis=-1)
    bot = jnp.concatenate([z, we_re[1::2], z, we_im[1::2]], axis=-1)
    we = jnp.concatenate([top, bot], axis=1)

    m_re = pr[:L, :, :, None] * bb_re[None] - pi[:L, :, :, None] * bb_im[None]
    m_im = pr[:L, :, :, None] * bb_im[None] + pi[:L, :, :, None] * bb_re[None]
    kern = jnp.einsum('gop,tgpc->gcto', cr, m_re) - jnp.einsum('gop,tgpc->gcto', ci, m_im)
    padded = jnp.concatenate([jnp.zeros((G, C, L * C), F32), kern.reshape(G, C, L * C)], axis=-1)
    tz = jnp.stack([padded[:, :, (L - s) * C:(2 * L - s) * C] for s in range(L)], axis=1)
    tz = tz.reshape(G, L * C, L * C)

    pr1 = pr[1:]
    pi1 = pi[1:]
    cp_re = jnp.einsum('gop,tgp->gpto', cr, pr1) - jnp.einsum('gop,tgp->gpto', ci, pi1)
    cp_im = -(jnp.einsum('gop,tgp->gpto', cr, pi1) + jnp.einsum('gop,tgp->gpto', ci, pr1))
    cp_re = cp_re.reshape(G, P, L * C)
    cp_im = cp_im.reshape(G, P, L * C)
    zc = jnp.zeros_like(cp_re[0::2])
    cp = jnp.concatenate([
        jnp.concatenate([cp_re[0::2], zc], axis=-1),
        jnp.concatenate([zc, cp_re[1::2]], axis=-1),
        jnp.concatenate([cp_im[0::2], zc], axis=-1),
        jnp.concatenate([zc, cp_im[1::2]], axis=-1)], axis=1)

    a_re = pr[L].reshape(G // 2, 2 * P)
    a_im = pi[L].reshape(G // 2, 2 * P)
    a16 = jnp.concatenate([a_re, a_im], axis=-1).reshape(1, G * 2 * P)
    dg = jnp.broadcast_to(d_skip.astype(F32).reshape(G, 1, C), (G, L, C)).reshape(1, G * L * C)
    return we.astype(BF16), tz.astype(BF16), cp.astype(BF16), a16, dg


def _compress_kernel(ak_ref, av_ref, w1k_ref, w1v_ref, w2k_ref, w2v_ref, c0k_ref, c0v_ref,
                     kc_ref, vct_ref):
    def run(a_ref, w1_ref, w2_ref, c0_ref):
        nc = a_ref.shape[1] // CMP_STRIDE
        a = jnp.concatenate([a_ref[0, pl.ds(l, nc, stride=CMP_STRIDE), :] for l in range(CMP_STRIDE)],
                            axis=1).astype(BF16)
        p1 = jnp.dot(a, w1_ref[0], preferred_element_type=F32)
        p2 = jnp.dot(a, w1_ref[1], preferred_element_type=F32)
        n = p1.shape[0]
        pre = p1 + pltpu.roll(p2, n - 1, 0) + c0_ref[...]
        out = jnp.dot(jax.nn.gelu(pre).astype(BF16), w2_ref[...], preferred_element_type=F32)
        row = lax.broadcasted_iota(jnp.int32, out.shape, 0)
        return jnp.where(row < n - 1, out, 0.0)

    kc_ref[0] = run(ak_ref, w1k_ref, w2k_ref, c0k_ref).astype(BF16)
    vct_ref[0] = run(av_ref, w1v_ref, w2v_ref, c0v_ref).T.astype(BF16)


def _compress(z3, w1k, w1v, w2k, w2v, c0k, c0v):
    b, seq, _ = z3.shape
    nc = seq // CMP_STRIDE
    c3 = lambda i: (0, 0, 0)
    c2 = lambda i: (0, 0)
    return pl.pallas_call(
        _compress_kernel,
        grid=(b,),
        in_specs=[pl.BlockSpec((1, seq, 128), lambda i: (i, 0, Z_KV // 128)),
                  pl.BlockSpec((1, seq, 128), lambda i: (i, 0, Z_KV // 128 + 1)),
                  pl.BlockSpec(w1k.shape, c3), pl.BlockSpec(w1v.shape, c3),
                  pl.BlockSpec(w2k.shape, c2), pl.BlockSpec(w2v.shape, c2),
                  pl.BlockSpec(c0k.shape, c2), pl.BlockSpec(c0v.shape, c2)],
        out_specs=[pl.BlockSpec((1, nc, 128), lambda i: (i, 0, 0)),
                   pl.BlockSpec((1, 128, nc), lambda i: (i, 0, 0))],
        out_shape=[jax.ShapeDtypeStruct((b, nc, 128), BF16),
                   jax.ShapeDtypeStruct((b, 128, nc), BF16)],
        compiler_params=_cparams("parallel"),
        name="compress",
    )(z3, z3, w1k, w1v, w2k, w2v, c0k, c0v)


def _compress_weights(pe, w1, w2):
    w1 = w1.astype(F32)
    half = (CMP_LEN // 2) * HEAD_DIM
    eye = jnp.eye(NSA_KV_HEADS, dtype=F32)

    def bd(w):
        w = w.reshape(CMP_STRIDE, HEAD_DIM, HEAD_DIM)
        return jnp.einsum('ldo,hk->lhdko', w, eye).reshape(NSA_KV_HEADS * half, NSA_KV_HEADS * HEAD_DIM)

    w1s = jnp.stack([bd(w1[:half]), bd(w1[half:])])
    z2 = jnp.zeros((HEAD_DIM, HEAD_DIM), F32)
    w2f = w2.astype(F32)
    w2s = jnp.concatenate([jnp.concatenate([w2f, z2], axis=1),
                           jnp.concatenate([z2, w2f], axis=1)], axis=0)
    c0 = pe.astype(F32).reshape(1, CMP_LEN * HEAD_DIM) @ w1
    c0 = jnp.concatenate([c0, c0], axis=1)
    return w1s.astype(BF16), w2s.astype(BF16), c0


def _t5_bucket_np(dist):
    n = np.maximum(dist, 0)
    max_exact = REL_BUCKETS // 2
    nf = np.maximum(n, 1).astype(np.float32)
    large = max_exact + (np.log(nf / max_exact) / math.log(REL_MAX_DIST / max_exact)
                         * (REL_BUCKETS - max_exact)).astype(np.int32)
    large = np.minimum(large, REL_BUCKETS - 1)
    return np.where(n < max_exact, n, large)


TAB_FAR, TAB_WIN_EDGE, TAB_NONE = 3, 4, 5


def _bias_tables(rel_table, seq):
    tbl = rel_table.astype(F32)
    heads = NSA_Q_HEADS

    def lookup(bucket):
        onehot = jnp.asarray(bucket[..., None] == np.arange(REL_BUCKETS)).astype(F32)
        return jnp.einsum('...k,kh->...h', onehot, tbl, precision=lax.Precision.HIGHEST) * LOG2E

    r = np.arange(Q_BLOCK)[None, :]
    c = np.arange(SEL_LEN)[:, None]
    near = lookup(_t5_bucket_np(64 * np.arange(3)[:, None, None] + (r - c)[None]))
    far = jnp.broadcast_to(tbl[REL_BUCKETS - 1] * LOG2E, (SEL_LEN, Q_BLOCK, heads))
    none = jnp.full((SEL_LEN, Q_BLOCK, heads), NO_SLOT, F32)
    causal = jnp.asarray(r - c >= 0)[:, :, None]
    edge = jnp.asarray(c > r)[:, :, None]
    tab = jnp.stack([jnp.where(causal, near[0], none), near[1], near[2], far,
                     jnp.where(edge, far, none), none])
    tab = tab.transpose(0, 1, 3, 2).reshape(6, SEL_LEN, heads * Q_BLOCK)
    ncp = seq // CMP_STRIDE
    k = np.arange(2 * ncp + 4)[:, None]
    dist = r - CMP_STRIDE * (k - (ncp - 4)) - (CMP_LEN - 1)
    t2 = jnp.where(jnp.asarray(dist >= 0)[:, :, None], lookup(_t5_bucket_np(dist)), NO_SLOT)
    t2 = t2.transpose(0, 2, 1).reshape(2 * ncp + 4, heads * Q_BLOCK)
    t2 = jnp.stack([t2[:2 * ncp], t2[4:]])
    return tab, t2


def _nsa_kernel(q_ref, gl_ref, kc_ref, vct_ref, ks_ref, vs_ref, kw_ref, vw_ref, t2_ref, tab_ref,
                ov_ref, o_ref, s_ref, p_ref, acc_ref, accw_ref, vst_ref, vwt_ref, *, ncp, nw):
    @pl.when(pl.program_id(1) == 0)
    def _():
        for v_ref, vt_ref in ((vs_ref, vst_ref), (vw_ref, vwt_ref)):
            kt = vt_ref.shape[2]
            ones = jnp.ones((16, kt), F32)
            for u in range(vt_ref.shape[0]):
                vt = v_ref[0, u * kt:(u + 1) * kt, :].T
                vt_ref[u] = jnp.concatenate([vt[0:HEAD_DIM], ones, vt[HEAD_DIM:], ones], axis=0).astype(BF16)

    qi0 = pl.program_id(1) * nw
    cw = nw * NSA_GQA * Q_BLOCK
    ncol = NSA_KV_HEADS * cw
    vrows = HEAD_DIM + 16
    lane = lax.broadcasted_iota(jnp.int32, (Q_BLOCK, 128), 1)

    q = q_ref[...].astype(F32) * (HEAD_DIM ** -0.5 * LOG2E)
    pieces = []
    for h in range(NSA_KV_HEADS):
        keep = (lane >= 64) if h == 1 else (lane < 64)
        for w in range(nw):
            for g in range(NSA_GQA):
                pieces.append(jnp.where(keep, q[w * 64:(w + 1) * 64, g * 128:(g + 1) * 128], 0.0))
    qpad = jnp.concatenate(pieces, axis=0).astype(BF16)

    col = lax.broadcasted_iota(jnp.int32, (1, ncol), 1)
    rcol = col % Q_BLOCK
    qcol = qi0 + (col // (NSA_GQA * Q_BLOCK)) % nw
    tvec = Q_BLOCK * qcol + rcol
    nt_dims = (((1,), (1,)), ((), ()))

    def per_head_cols(tiles):
        return jnp.concatenate([tiles[w][:, h * 256:(h + 1) * 256]
                                for h in range(NSA_KV_HEADS) for w in range(nw)], axis=1)

    def slot_terms(m, n_masked):
        m_fin = jnp.where(n_masked > 0, jnp.maximum(m, NEG_INF * LOG2E), m)
        return jnp.exp2(m - m_fin), n_masked * jnp.exp2(NEG_INF * LOG2E - m_fin)

    t2_tiles = []
    for w in range(nw):
        start = (ncp - 4) - 4 * (qi0 + w)
        par = (start // 4) % 2
        a0 = pl.multiple_of(start - 4 * par, 8)
        t2_tiles.append(t2_ref[par, pl.ds(a0, ncp), :])
    s = lax.dot_general(kc_ref[0], qpad, nt_dims, preferred_element_type=F32) + per_head_cols(t2_tiles)
    m = jnp.max(s, axis=0, keepdims=True)
    e = jnp.exp2(s - m)
    n_valid = jnp.clip(jnp.right_shift(tvec - (CMP_STRIDE - 1), 4), 0, ncp - 1)
    scale, extra = slot_terms(m, (ncp - 1 - n_valid).astype(F32))
    p_c = e * (scale / (jnp.sum(e, axis=0, keepdims=True) * scale + extra))
    p_cb = p_c.astype(BF16)
    o_c = [jnp.dot(vct_ref[0, h * 64:(h + 1) * 64, :], p_cb[:, h * cw:(h + 1) * cw],
                   preferred_element_type=F32) for h in range(NSA_KV_HEADS)]

    lane_c = lax.broadcasted_iota(jnp.int32, (ncp, 128), 1)
    parts = []
    for w in range(nw):
        halves = []
        for h in range(NSA_KV_HEADS):
            base = h * cw + w * 256
            ph = p_c[:, base:base + 128] + p_c[:, base + 128:base + 256]
            halves.append(ph + pltpu.roll(ph, 64, 1))
        parts.append(jnp.where(lane_c < 64, halves[0], halves[1]))
    imp = jnp.dot(ov_ref[...], jnp.concatenate(parts, axis=1), preferred_element_type=F32,
                  precision=lax.Precision.HIGHEST)
    nsel = imp.shape[0]
    jidx = lax.broadcasted_iota(jnp.int32, (nsel, 1), 0)
    qv = qi0 + lax.broadcasted_iota(jnp.int32, (1, nw * 128), 1) // 128
    forced = jnp.logical_or(jidx == 0, jnp.logical_or(jidx == qv, jidx == qv - 1))
    avail = jidx <= qv
    cand = jnp.logical_and(avail, jnp.logical_not(forced))
    budget = SEL_TOPK - (1 + jnp.where(qv >= 1, 1, 0) + jnp.where(qv >= 2, 1, 0))
    rank = jnp.zeros((nsel, nw * 128), jnp.int32)
    for jp in range(1, nsel - 2):
        row = imp[jp:jp + 1, :]
        ge = jnp.where(row >= imp, 1, 0)
        gt = jnp.where(row > imp, 1, 0)
        is_cand = jnp.where(qv - 2 >= jp, 1, 0)
        rank = rank + jnp.where(jidx > jp, ge, gt) * is_cand
    sel = jnp.logical_or(jnp.logical_and(forced, avail), jnp.logical_and(cand, rank < budget))
    seladd = jnp.where(sel, 0.0, NO_SLOT)
    pad = jnp.zeros((128 - nsel, 128), F32)
    mask_rows = []
    for w in range(nw):
        mask_rows.append(jnp.concatenate([seladd[:, w * 128:(w + 1) * 128], pad], axis=0).T)
    sel_cols = jnp.concatenate([mask_rows[w][h * 64:(h + 1) * 64, :] for h in range(NSA_KV_HEADS)
                                for w in range(nw) for _ in range(NSA_GQA)], axis=0)
    qsel = jnp.concatenate([qpad, sel_cols.astype(BF16)], axis=1)

    def block_adds(blk, selected):
        tiles = []
        for w in range(nw):
            d = qi0 + w - blk
            if selected:
                e = jnp.where(d < 0, TAB_FAR, jnp.minimum(d, TAB_FAR))
            else:
                outside = jnp.logical_or(blk < 0, jnp.logical_or(d < 0, d > TAB_NONE))
                e = jnp.where(outside, TAB_NONE, d)
            tiles.append(tab_ref[e])
        return per_head_cols(tiles)

    def softmax_tile(m_i, s, adds):
        sm = jnp.concatenate([s[k * 64:(k + 1) * 64] + adds[k] for k in range(len(adds))], axis=0)
        m_new = jnp.maximum(m_i, jnp.max(sm, axis=0, keepdims=True))
        return m_new, jnp.exp2(m_i - m_new), jnp.exp2(sm - m_new).astype(BF16)

    def add_values(ref, alpha, vt_tile, p):
        for h in range(NSA_KV_HEADS):
            ref[h] = alpha[:, h * cw:(h + 1) * cw] * ref[h] + jnp.dot(
                vt_tile[h * vrows:(h + 1) * vrows, :], p[:, h * cw:(h + 1) * cw], preferred_element_type=F32)

    m_init = jnp.full((1, ncol), NO_SLOT, F32)

    accw_ref[...] = jnp.zeros_like(accw_ref)
    m_w = m_init
    for i in range(nw // 4 + 1):
        tj = qi0 // 4 - 1 + i
        tc = jnp.maximum(tj, 0)
        r0 = pl.multiple_of(tc * 256, 256)
        s = lax.dot_general(kw_ref[0, pl.ds(r0, 256), :].astype(BF16), qpad, nt_dims,
                            preferred_element_type=F32)
        m_w, alpha, p = softmax_tile(m_w, s, [block_adds(4 * tj + k, False) for k in range(4)])
        add_values(accw_ref, alpha, vwt_ref[tc], p)

    n_tiles = (qi0 + nw - 1) // 4 + 1

    def masked_scores(u):
        r0 = pl.multiple_of(u * 256, 256)
        block_of_row = 4 * u + lax.broadcasted_iota(jnp.int32, (256, 128), 0) // SEL_LEN
        onehot = jnp.where(lax.broadcasted_iota(jnp.int32, (256, 128), 1) == block_of_row, 1.0, 0.0)
        keys = jnp.concatenate([ks_ref[0, pl.ds(r0, 256), :].astype(BF16), onehot.astype(BF16)], axis=1)
        s = lax.dot_general(keys, qsel, nt_dims, preferred_element_type=F32)
        tmax = None
        for k in range(4):
            sk = s[k * 64:(k + 1) * 64] + block_adds(4 * u + k, True)
            s_ref[k * 64:(k + 1) * 64, :] = sk
            kmax = jnp.max(sk, axis=0, keepdims=True)
            tmax = kmax if tmax is None else jnp.maximum(tmax, kmax)
        return tmax

    tmax0 = masked_scores(0)
    p_ref[...] = jnp.zeros_like(p_ref)
    acc_ref[...] = jnp.zeros_like(acc_ref)

    def sel_body(u, carry):
        m_i, alpha_prev, tmax = carry
        add_values(acc_ref, alpha_prev, vst_ref[jnp.maximum(u - 1, 0)], p_ref[...])
        m_new = jnp.maximum(m_i, tmax)
        p_ref[...] = jnp.exp2(s_ref[...] - m_new).astype(BF16)
        tmax_next = masked_scores(jnp.minimum(u + 1, n_tiles - 1))
        return m_new, jnp.exp2(m_i - m_new), tmax_next

    m_s, alpha_last, _ = lax.fori_loop(0, n_tiles, sel_body, (m_init, jnp.ones((1, ncol), F32), tmax0))
    add_values(acc_ref, alpha_last, vst_ref[n_tiles - 1], p_ref[...])

    nm_s = (SEL_LEN * jnp.maximum(SEL_TOPK - (qcol + 1), 0) + (SEL_LEN - 1 - rcol)).astype(F32)
    nm_w = (WINDOW + Q_BLOCK - jnp.minimum(tvec + 1, WINDOW)).astype(F32)

    def normalised(m, ref, n_masked):
        scale, extra = slot_terms(m, n_masked)
        out = []
        for h in range(NSA_KV_HEADS):
            sl = slice(h * cw, (h + 1) * cw)
            acc = ref[h]
            denom = acc[HEAD_DIM:HEAD_DIM + 1, :] * scale[:, sl] + extra[:, sl]
            out.append((scale[:, sl] / denom, acc[0:HEAD_DIM, :]))
        return out

    o_s = normalised(m_s, acc_ref, nm_s)
    o_w = normalised(m_w, accw_ref, nm_w)

    lane1 = lax.broadcasted_iota(jnp.int32, (1, 128), 1)
    gts = []
    for w in range(nw):
        g = jax.nn.sigmoid(gl_ref[w * 64:(w + 1) * 64, :])
        gts.append(jnp.concatenate([g, g], axis=0).T)

    def gate_vec(h, br):
        tiles = []
        for w in range(nw):
            for gg in range(2):
                c0 = (h * 4 + 2 * gg) * 3 + br
                c1 = (h * 4 + 2 * gg + 1) * 3 + br
                tiles.append(jnp.where(lane1 < 64, gts[w][c0:c0 + 1, :], gts[w][c1:c1 + 1, :]))
        return jnp.concatenate(tiles, axis=1)

    tot = []
    for h in range(NSA_KV_HEADS):
        t = gate_vec(h, 0) * o_c[h]
        t = t + (gate_vec(h, 1) * o_s[h][0]) * o_s[h][1]
        t = t + (gate_vec(h, 2) * o_w[h][0]) * o_w[h][1]
        tot.append(t)
    for w in range(nw):
        ot = jnp.concatenate([t[:, w * 256:(w + 1) * 256] for t in tot], axis=0).T
        o_ref[w * 64:(w + 1) * 64, :] = jnp.concatenate([ot[g * 64:(g + 1) * 64, :]
                                                         for g in range(NSA_GQA)], axis=1)


def _nsa(zq, z, kc, vct, t2, tab, ov, b, seq, nw=4):
    nst = seq // (Q_BLOCK * nw)
    ncp = seq // CMP_STRIDE
    rows = Q_BLOCK * nw
    cw = nw * NSA_GQA * Q_BLOCK
    z3 = z.reshape(b, seq, Z_NSA)
    kv_col = lambda n: pl.BlockSpec((1, seq, 128), lambda i, j: (i, 0, Z_KV // 128 + n))
    return pl.pallas_call(
        functools.partial(_nsa_kernel, ncp=ncp, nw=nw),
        grid=(b, nst),
        in_specs=[pl.BlockSpec((rows, NSA_WIDTH), lambda i, j: (i * nst + j, Z_Q // NSA_WIDTH)),
                  pl.BlockSpec((rows, 128), lambda i, j: (i * nst + j, Z_G // 128)),
                  pl.BlockSpec((1, ncp, 128), lambda i, j: (i, 0, 0)),
                  pl.BlockSpec((1, 128, ncp), lambda i, j: (i, 0, 0)),
                  kv_col(2), kv_col(3), kv_col(4), kv_col(5),
                  pl.BlockSpec(t2.shape, lambda i, j: (0, 0, 0)),
                  pl.BlockSpec(tab.shape, lambda i, j: (0, 0, 0)),
                  pl.BlockSpec(ov.shape, lambda i, j: (0, 0))],
        out_specs=pl.BlockSpec((rows, NSA_WIDTH), lambda i, j: (i * nst + j, 0)),
        out_shape=jax.ShapeDtypeStruct((b * seq, NSA_WIDTH), F32),
        scratch_shapes=[pltpu.VMEM((256, NSA_KV_HEADS * cw), F32),
                        pltpu.VMEM((256, NSA_KV_HEADS * cw), BF16),
                        pltpu.VMEM((NSA_KV_HEADS, HEAD_DIM + 16, cw), F32),
                        pltpu.VMEM((NSA_KV_HEADS, HEAD_DIM + 16, cw), F32),
                        pltpu.VMEM((seq // 256, NSA_KV_HEADS * (HEAD_DIM + 16), 256), BF16),
                        pltpu.VMEM((seq // 256, NSA_KV_HEADS * (HEAD_DIM + 16), 256), BF16)],
        compiler_params=_cparams("parallel", "arbitrary"),
        name="nsa",
    )(zq, z, kc, vct, z3, z3, z3, z3, t2, tab, ov)


def _lru_kernel(x0_ref, x1_ref, g0_ref, g1_ref, cw_ref, cb_ref, wa_ref, ba_ref, wx_ref, bx_ref, sp_ref,
                o0_ref, o1_ref, xprev_ref, h_ref, a_scr, b_scr, *, nb, tc):
    @pl.when(pl.program_id(0) == 0)
    def _():
        xprev_ref[...] = jnp.zeros_like(xprev_ref)
        h_ref[...] = jnp.zeros_like(h_ref)

    x = jnp.concatenate([x0_ref[...], x1_ref[...]], axis=1)
    rows, width = x.shape
    grp = nb * OCT
    xm = jnp.concatenate([xprev_ref[...], x[:rows - grp]], axis=0)
    xprev_ref[...] = x[rows - grp:]
    x3 = x.reshape(rows // OCT, OCT, width)
    xm3 = xm.reshape(rows // OCT, OCT, width)
    tlo = lax.broadcasted_iota(jnp.int32, (1, OCT, 1), 1)
    cw = cw_ref[...]
    xc = cb_ref[...].reshape(1, 1, width) + cw[CONV_WIDTH - 1:CONV_WIDTH, :].reshape(1, 1, width) * x3
    for k in range(1, CONV_WIDTH):
        delayed = jnp.where(tlo >= k, pltpu.roll(x3, k, 1), pltpu.roll(xm3, k, 1))
        xc = xc + cw[CONV_WIDTH - 1 - k:CONV_WIDTH - k, :].reshape(1, 1, width) * delayed
    xc = xc.reshape(rows, width)
    xcb = xc.astype(BF16)
    def sigmoid(v):
        return 0.5 * jnp.tanh(0.5 * v) + 0.5

    gate_r = sigmoid(jnp.dot(xcb, wa_ref[...], preferred_element_type=F32) + ba_ref[...])
    gate_i = sigmoid(jnp.dot(xcb, wx_ref[...], preferred_element_type=F32) + bx_ref[...])
    log_a = -LRU_C * gate_r * sp_ref[...]
    a = jnp.exp(log_a)
    th = jnp.tanh(log_a)
    bt = jnp.sqrt(-2.0 * th / (1.0 - th)) * gate_i * xc
    for j in range(2):
        a_scr[j] = a[:, j * 128:(j + 1) * 128]
        b_scr[j] = bt[:, j * 128:(j + 1) * 128]

    def octet(o, h):
        base = pl.multiple_of(o * (nb * OCT), nb * OCT)
        for ti in range(OCT):
            new = []
            for j in range(2):
                rows = pl.ds(base + ti, nb, stride=OCT)
                hj = a_scr[j, rows, :] * h[j] + b_scr[j, rows, :]
                b_scr[j, rows, :] = hj
                new.append(hj)
            h = tuple(new)
        return h

    h = lax.fori_loop(0, tc // OCT, octet, (h_ref[0], h_ref[1]))
    for j, (g_ref, o_ref) in enumerate(((g0_ref, o0_ref), (g1_ref, o1_ref))):
        h_ref[j] = h[j]
        o_ref[...] = b_scr[j] * jax.nn.gelu(g_ref[...])


def _lru(x0, x1, g0, g1, cw, cb, wa, ba, wx, bx, sp, nb, tc=64):
    rows = x0.shape[0]
    r = tc * nb
    c2 = lambda i: (0, 0)
    slab = pl.BlockSpec((r, 128), lambda i: (i, 0))
    return pl.pallas_call(
        functools.partial(_lru_kernel, nb=nb, tc=tc),
        grid=(rows // r,),
        in_specs=[slab, slab, slab, slab,
                  pl.BlockSpec(cw.shape, c2), pl.BlockSpec(cb.shape, c2),
                  pl.BlockSpec(wa.shape, c2), pl.BlockSpec(ba.shape, c2),
                  pl.BlockSpec(wx.shape, c2), pl.BlockSpec(bx.shape, c2),
                  pl.BlockSpec(sp.shape, c2)],
        out_specs=[slab, slab],
        out_shape=[jax.ShapeDtypeStruct((rows, 128), F32)] * 2,
        scratch_shapes=[pltpu.VMEM((nb * OCT, LRU_WIDTH), F32), pltpu.VMEM((2, nb, 128), F32),
                        pltpu.VMEM((2, r, 128), F32), pltpu.VMEM((2, r, 128), F32)],
        compiler_params=_cparams("arbitrary"),
        name="lru",
    )(x0, x1, g0, g1, cw, cb, wa, ba, wx, bx, sp)


def _block_diag(w):
    h, i, j = w.shape
    eye = jnp.eye(h, dtype=w.dtype)
    return jnp.einsum('hij,hk->hikj', w, eye).reshape(h * i, h * j)


def _mix_ffn_kernel(h_ref, ys0_ref, ys1_ref, yn_ref, yl0_ref, yl1_ref, wglu_ref, wos_ref, won_ref, wol_ref,
                    g_ref, wg_ref, wu_ref, wd_ref, gf_ref, o_ref, *, final):
    tm = h_ref.shape[0]

    def rows(a_ref, b_ref):
        return jnp.concatenate([a_ref[...].reshape(tm, 128), b_ref[...].reshape(tm, 128)], axis=1)

    gl = jnp.dot(rows(ys0_ref, ys1_ref).astype(BF16), wglu_ref[...], preferred_element_type=F32)
    s5 = gl[:, :SSM_WIDTH] * jax.nn.sigmoid(gl[:, SSM_WIDTH:])
    acc = jnp.dot(s5.astype(BF16), wos_ref[...], preferred_element_type=F32)
    acc = acc + jnp.dot(yn_ref[...].astype(BF16), won_ref[...], preferred_element_type=F32)
    acc = acc + jnp.dot(rows(yl0_ref, yl1_ref).astype(BF16), wol_ref[...], preferred_element_type=F32)
    x = h_ref[...] + acc

    y = (x * lax.rsqrt(jnp.mean(x * x, axis=-1, keepdims=True) + RMS_EPS) * g_ref[...]).astype(BF16)
    a = jnp.dot(y, wg_ref[...], preferred_element_type=F32)
    b = jnp.dot(y, wu_ref[...], preferred_element_type=F32)
    m = (jax.nn.silu(a) * b).astype(BF16)
    out = x + jnp.dot(m, wd_ref[...], preferred_element_type=F32)
    if final:
        out = out * lax.rsqrt(jnp.mean(out * out, axis=-1, keepdims=True) + RMS_EPS) * gf_ref[...]
    o_ref[...] = out


def _mix_ffn(h, ys, yn, yl, wglu, wos, won, wol, g, wg, wu, wd, gf, final, b, seq, tm=512):
    d = h.shape[1]
    nt = seq // tm
    row = lambda w: pl.BlockSpec((tm, w), lambda b_, i: (b_ * nt + i, 0))
    const = lambda a: pl.BlockSpec(a.shape, lambda b_, i: (0, 0), pipeline_mode=pl.Buffered(1))
    slab4 = lambda s: s.reshape(seq // OCT, b, OCT, 128)
    return pl.pallas_call(
        functools.partial(_mix_ffn_kernel, final=final),
        grid=(b, nt),
        in_specs=[row(d), _slab_spec(tm, nt), _slab_spec(tm, nt), row(NSA_WIDTH),
                  _slab_spec(tm, nt), _slab_spec(tm, nt),
                  const(wglu), const(wos), const(won), const(wol),
                  const(g), const(wg), const(wu), const(wd), const(gf)],
        out_specs=row(d),
        out_shape=jax.ShapeDtypeStruct((b * seq, d), F32),
        compiler_params=_cparams("parallel", "parallel"),
        name="mix_ffn",
    )(h, slab4(ys[0]), slab4(ys[1]), yn, slab4(yl[0]), slab4(yl[1]), wglu, wos, won, wol, g, wg, wu, wd, gf)


def _head_interleave(w, axis):
    shape = w.shape
    split = shape[:axis] + (NSA_KV_HEADS, NSA_GQA, HEAD_DIM) + shape[axis + 1:]
    return jnp.swapaxes(w.reshape(split), axis, axis + 1).reshape(shape)


def _prep_w_in(w_in):
    o1 = SSM_WIDTH
    o2 = o1 + NSA_WIDTH
    o3 = o2 + 6 * NSA_KV_WIDTH
    o4 = o3 + 3 * NSA_Q_HEADS
    wq = _head_interleave(w_in[:, o1:o2], 1)
    pad = jnp.zeros((w_in.shape[0], 128 - 3 * NSA_Q_HEADS), w_in.dtype)
    w = jnp.concatenate([wq, w_in[:, o2:o3], w_in[:, o3:o4], pad, w_in[:, :o1], w_in[:, o4:]], axis=1)
    return w.astype(BF16)


def _s5_mixer(slabs, b, p):
    we, tz, cp, a16, dg = _s5_tables(p['s5_lam_re'], p['s5_lam_im'], p['s5_log_dt'], p['s5_b_re'],
                                     p['s5_b_im'], p['s5_c_re'], p['s5_c_im'], p['s5_d'])
    return _s5(slabs[0], slabs[1], we, tz, cp, a16, dg, nb=b)


def _nsa_mixer(zq, z, b, seq, bias_tabs, ov, p):
    w1k, w2k, c0k = _compress_weights(p['nsa_pe_k'], p['nsa_w1_k'], p['nsa_w2_k'])
    w1v, w2v, c0v = _compress_weights(p['nsa_pe_v'], p['nsa_w1_v'], p['nsa_w2_v'])
    kc, vct = _compress(z.reshape(b, seq, Z_NSA), w1k, w1v, w2k, w2v, c0k, c0v)
    tab, t2 = bias_tabs
    return _nsa(zq, z, kc, vct, t2, tab, ov, b, seq)


def _lru_mixer(slabs, b, p):
    sp = jax.nn.softplus(-p['lru_lam'].astype(F32)).reshape(1, -1)
    return _lru(slabs[0], slabs[1], slabs[2], slabs[3],
                p['lru_conv_w'].astype(F32), p['lru_conv_b'].astype(F32).reshape(1, -1),
                _block_diag(p['lru_w_a']).astype(BF16), p['lru_b_a'].astype(F32).reshape(1, -1),
                _block_diag(p['lru_w_x']).astype(BF16), p['lru_b_x'].astype(F32).reshape(1, -1),
                sp, nb=b)


def _layer(h2, b, seq, bias_tabs, ov, p, final, norm_final):
    zq, z, slabs = _inproj(h2, p['norm_mix'].reshape(1, -1), _prep_w_in(p['w_in']), b, seq)
    ys = _s5_mixer(slabs[0:2], b, p)
    yn = _nsa_mixer(zq, z, b, seq, bias_tabs, ov, p)
    yl = _lru_mixer(slabs[2:6], b, p)
    w_out = p['w_out']
    wos = w_out[:SSM_WIDTH].astype(BF16)
    won = _head_interleave(w_out[SSM_WIDTH:SSM_WIDTH + NSA_WIDTH], 0).astype(BF16)
    wol = w_out[SSM_WIDTH + NSA_WIDTH:].astype(BF16)
    return _mix_ffn(h2, ys, yn, yl, p['s5_w_glu'].astype(BF16), wos, won, wol,
                    p['norm_ffn'].reshape(1, -1), p['w_gate'].astype(BF16), p['w_up'].astype(BF16),
                    p['w_down'].astype(BF16), norm_final.reshape(1, -1), final, b, seq)


def _overlap_t(seq):
    ncp = seq // CMP_STRIDE
    nsel = seq // SEL_LEN
    cs = np.arange(ncp) * CMP_STRIDE
    ss = np.arange(nsel) * SEL_LEN
    ovl = (cs[None, :] < ss[:, None] + SEL_LEN) & (ss[:, None] < cs[None, :] + CMP_LEN)
    ovl[:, ncp - 1] = False
    return jnp.asarray(ovl.astype(np.float32))


def kernel(x, rel_bias_table, norm_mix, w_in, w_out, s5_lam_re, s5_lam_im, s5_log_dt, s5_b_re, s5_b_im, s5_c_re, s5_c_im, s5_d, s5_w_glu, nsa_pe_k, nsa_w1_k, nsa_w2_k, nsa_pe_v, nsa_w1_v, nsa_w2_v, lru_conv_w, lru_conv_b, lru_w_a, lru_b_a, lru_w_x, lru_b_x, lru_lam, norm_ffn, w_gate, w_up, w_down, norm_final):
    b, seq, d = x.shape
    depth = norm_mix.shape[0]
    per_layer = dict(norm_mix=norm_mix, w_in=w_in, w_out=w_out, s5_lam_re=s5_lam_re, s5_lam_im=s5_lam_im,
                     s5_log_dt=s5_log_dt, s5_b_re=s5_b_re, s5_b_im=s5_b_im, s5_c_re=s5_c_re,
                     s5_c_im=s5_c_im, s5_d=s5_d, s5_w_glu=s5_w_glu, nsa_pe_k=nsa_pe_k, nsa_w1_k=nsa_w1_k,
                     nsa_w2_k=nsa_w2_k, nsa_pe_v=nsa_pe_v, nsa_w1_v=nsa_w1_v, nsa_w2_v=nsa_w2_v,
                     lru_conv_w=lru_conv_w, lru_conv_b=lru_conv_b, lru_w_a=lru_w_a, lru_b_a=lru_b_a,
                     lru_w_x=lru_w_x, lru_b_x=lru_b_x, lru_lam=lru_lam, norm_ffn=norm_ffn, w_gate=w_gate,
                     w_up=w_up, w_down=w_down)
    bias_tabs = _bias_tables(rel_bias_table, seq)
    ov = _overlap_t(seq)
    h2 = x.reshape(b * seq, d)
    for l in range(depth):
        p = {k: v[l] for k, v in per_layer.items()}
        h2 = _layer(h2, b, seq, bias_tabs, ov, p, l == depth - 1, norm_final)
    return h2.reshape(b, seq, d)
```
